```python
import jax, jax.numpy as jnp
from jax import lax
import numpy as np

D_MODEL = 1024
BATCH = 8
SEQ = 2048
DEPTH = 1

HEAD_DIM = 64
ROT_DIM = HEAD_DIM // 4
ROPE_THETA = 500000.0
EPS = 1e-6
NEG = -1e30

A_HEADS = 8
A_PATTERNS = ((128, 1), (512, 4), (2048, 16))
A_BLOCK = 128

B_HEADS = 8
B_KV_HEADS = 2
B_GROUP = B_HEADS // B_KV_HEADS
CMP_LEN = 32
CMP_STRIDE = 16
CMP_HIDDEN = 256
SEL_BLOCK = 64
SEL_TOPK = 8
WIN = 512
NSA_Q_BLOCK = 64
N_BRANCH = 3

A_WIDTH = A_HEADS * HEAD_DIM
B_WIDTH = B_HEADS * HEAD_DIM
KV_WIDTH = B_KV_HEADS * HEAD_DIM
MIX_WIDTH = A_WIDTH + B_WIDTH
IN_SIZES = (A_WIDTH, A_WIDTH, A_WIDTH, B_WIDTH, KV_WIDTH, KV_WIDTH, KV_WIDTH, KV_WIDTH, KV_WIDTH, KV_WIDTH, B_HEADS * N_BRANCH)
IN_WIDTH = 3 * A_WIDTH + B_WIDTH + 6 * KV_WIDTH + B_HEADS * N_BRANCH
D_FF = 4 * D_MODEL

kernel_name = 'hybrid_dilated_nsa_block'


def rms_norm(x, g):
    xf = x.astype(jnp.float32)
    y = xf * lax.rsqrt(jnp.mean(xf * xf, axis=-1, keepdims=True) + EPS)
    return (y * g.astype(jnp.float32)).astype(x.dtype)


def rope_cos_sin(pos):
    inv = ROPE_THETA ** (-jnp.arange(0, ROT_DIM, 2, dtype=jnp.float32) / ROT_DIM)
    ang = pos.astype(jnp.float32)[:, None] * inv[None, :]
    return jnp.cos(ang), jnp.sin(ang)


def apply_rope(x, cos, sin):
    half = ROT_DIM // 2
    xr = x[..., :ROT_DIM].astype(jnp.float32)
    x1, x2 = xr[..., :half], xr[..., half:]
    rot = jnp.concatenate([x1 * cos - x2 * sin, x2 * cos + x1 * sin], axis=-1)
    return jnp.concatenate([rot.astype(x.dtype), x[..., ROT_DIM:]], axis=-1)


def split_columns(proj):
    out, start = [], 0
    for n in IN_SIZES:
        out.append(proj[..., start:start + n])
        start += n
    return out


def to_heads(a, n):
    B, T, _ = a.shape
    return a.reshape(B, T, n, HEAD_DIM).transpose(0, 2, 1, 3)


def masked_softmax(s, mask):
    s = jnp.where(mask, s, NEG)
    m = jnp.max(s, axis=-1, keepdims=True)
    e = jnp.where(mask, jnp.exp(s - m), 0.0)
    den = jnp.sum(e, axis=-1, keepdims=True)
    return e / jnp.maximum(den, 1e-30)


def to_strided(a, dilation, sub_pad):
    B, H, T, hd = a.shape
    a = jnp.pad(a, ((0, 0), (0, 0), (0, sub_pad * dilation - T), (0, 0)))
    return a.reshape(B, H, sub_pad, dilation, hd).transpose(0, 1, 3, 2, 4)


def from_strided(a, T):
    B, H, d, S = a.shape[:4]
    a = jnp.moveaxis(a, 2, 3)
    return a.reshape((B, H, S * d) + a.shape[4:])[:, :, :T]


def dilated_window_attn(q, k, v, window, dilation):
    B, H, T, hd = q.shape
    n_back = window // dilation
    blk = A_BLOCK
    sub = -(-T // dilation)
    nb = -(-sub // blk)
    sub_pad = nb * blk
    qs = to_strided(q, dilation, sub_pad).reshape(B, H, dilation, nb, blk, hd)

    def band(a):
        ab = to_strided(a, dilation, sub_pad).reshape(B, H, dilation, nb, blk, hd)
        prev = jnp.pad(ab[:, :, :, :-1], ((0, 0), (0, 0), (0, 0), (1, 0), (0, 0), (0, 0)))
        return jnp.concatenate([prev, ab], axis=4)

    kb, vb = band(k), band(v)
    s = jnp.einsum('bhrnqd,bhrnkd->bhrnqk', qs, kb, preferred_element_type=jnp.float32) * (hd ** -0.5)
    qi = jnp.arange(blk)[:, None] + blk
    ki = jnp.arange(2 * blk)[None, :]
    dist = qi - ki
    mask = (dist >= 0) & (dist <= n_back) & ((ki >= blk) | (jnp.arange(nb)[:, None, None] > 0))
    s = jnp.where(mask, s, NEG)
    m = jnp.max(s, axis=-1)
    e = jnp.where(mask, jnp.exp(s - m[..., None]), 0.0)
    den = jnp.sum(e, axis=-1)
    o = jnp.einsum('bhrnqk,bhrnkd->bhrnqd', e, vb.astype(jnp.float32)) / den[..., None]
    o = from_strided(o.reshape(B, H, dilation, sub_pad, hd), T)
    m = from_strided(m.reshape(B, H, dilation, sub_pad), T)
    den = from_strided(den.reshape(B, H, dilation, sub_pad), T)
    return o, m, den


def dilated_mixture(q, k, v):
    res = [dilated_window_attn(q, k, v, w, d) for (w, d) in A_PATTERNS]
    m_all = res[0][1]
    for r in res[1:]:
        m_all = jnp.maximum(m_all, r[1])
    num, wsum = 0.0, 0.0
    for o, m, den in res:
        w = den * jnp.exp(m - m_all)
        num = num + w[..., None] * o
        wsum = wsum + w
    return (num / wsum[..., None]).astype(q.dtype)


def compress_tokens(a, pe, w1, w2):
    B, G, T, hd = a.shape
    nc = (T - CMP_LEN) // CMP_STRIDE + 1
    idx = jnp.arange(nc)[:, None] * CMP_STRIDE + jnp.arange(CMP_LEN)[None, :]
    blocks = a[:, :, idx] + pe
    hid = jax.nn.gelu(blocks.reshape(B, G, nc, CMP_LEN * hd) @ w1)
    return hid @ w2


def cmp_sel_overlap(nc, ns):
    c0 = jnp.arange(nc) * CMP_STRIDE
    s0 = jnp.arange(ns) * SEL_BLOCK
    ov = jnp.minimum(c0[:, None] + CMP_LEN, s0[None, :] + SEL_BLOCK) - jnp.maximum(c0[:, None], s0[None, :])
    return jnp.clip(ov, 0, None).astype(jnp.float32) / CMP_LEN


def nsa_attention(q, k_cmp, v_cmp, cmp_end, k_sel, v_sel, k_win, v_win, gates):
    B, G, R, T, hd = q.shape
    scale = hd ** -0.5
    ns = T // SEL_BLOCK
    n_sel = min(SEL_TOPK, ns)
    nqb = T // NSA_Q_BLOCK
    overlap = cmp_sel_overlap(k_cmp.shape[2], ns)
    k_sel_b = k_sel.reshape(B, G, ns, SEL_BLOCK, hd)
    v_sel_b = v_sel.reshape(B, G, ns, SEL_BLOCK, hd)
    pad = ((0, 0), (0, 0), (WIN, 0), (0, 0))
    k_win_p = jnp.pad(k_win, pad)
    v_win_p = jnp.pad(v_win, pad)
    gather_blocks = jax.vmap(jax.vmap(lambda blocks, ids: blocks[ids]))
    sblk = jnp.arange(ns)
    f32 = jnp.float32

    def one_block(b):
        t0 = b * NSA_Q_BLOCK
        qb = lax.dynamic_slice_in_dim(q, t0, NSA_Q_BLOCK, axis=3)
        gb = lax.dynamic_slice_in_dim(gates, t0, NSA_Q_BLOCK, axis=3)
        tpos = t0 + jnp.arange(NSA_Q_BLOCK)
        sc = jnp.einsum('bgrqd,bgcd->bgrqc', qb, k_cmp, preferred_element_type=f32) * scale
        p_cmp = masked_softmax(sc, cmp_end[None, :] <= tpos[:, None])
        o_cmp = jnp.einsum('bgrqc,bgcd->bgrqd', p_cmp, v_cmp.astype(f32))
        imp = jnp.einsum('bgrqc,cs->bgqs', p_cmp, overlap)
        cur = (tpos // SEL_BLOCK)[:, None]
        forced = (sblk == 0) | (sblk == cur) | (sblk == cur - 1)
        score = jnp.where(forced, imp + 2.0, jnp.where(sblk > cur, -1.0, imp))
        _, idx = lax.top_k(score, n_sel)
        kg = gather_blocks(k_sel_b, idx).reshape(B, G, NSA_Q_BLOCK, n_sel * SEL_BLOCK, hd)
        vg = gather_blocks(v_sel_b, idx).reshape(B, G, NSA_Q_BLOCK, n_sel * SEL_BLOCK, hd)
        kpos = (idx[..., None] * SEL_BLOCK + jnp.arange(SEL_BLOCK)).reshape(B, G, NSA_Q_BLOCK, n_sel * SEL_BLOCK)
        ss = jnp.einsum('bgrqd,bgqkd->bgrqk', qb, kg, preferred_element_type=f32) * scale
        p_sel = masked_softmax(ss, (kpos <= tpos[:, None])[:, :, None])
        o_sel = jnp.einsum('bgrqk,bgqkd->bgrqd', p_sel, vg.astype(f32))
        kw = lax.dynamic_slice_in_dim(k_win_p, t0, WIN + NSA_Q_BLOCK, axis=2)
        vw = lax.dynamic_slice_in_dim(v_win_p, t0, WIN + NSA_Q_BLOCK, axis=2)
        kpos_w = t0 - WIN + jnp.arange(WIN + NSA_Q_BLOCK)
        dist = tpos[:, None] - kpos_w[None, :]
        wmask = (dist >= 0) & (dist < WIN) & (kpos_w[None, :] >= 0)
        sw = jnp.einsum('bgrqd,bgkd->bgrqk', qb, kw, preferred_element_type=f32) * scale
        p_win = masked_softmax(sw, wmask)
        o_win = jnp.einsum('bgrqk,bgkd->bgrqd', p_win, vw.astype(f32))
        out = gb[..., 0:1] * o_cmp + gb[..., 1:2] * o_sel + gb[..., 2:3] * o_win
        return out.astype(q.dtype)

    outs = lax.map(one_block, jnp.arange(nqb))
    return outs.transpose(1, 0, 4, 2, 3, 5).reshape(B, T, G * R * hd)


def setup_inputs(seed: int = 0) -> dict:
    key = jax.random.key(seed)
    ks = jax.random.split(key, 16)
    f32 = jnp.float32
    L = DEPTH
    hd = HEAD_DIM

    def nrm(k, shape, fan_in):
        return jax.random.normal(k, shape, f32) * fan_in ** -0.5

    def gain(k, shape):
        return 1.0 + 0.02 * jax.random.normal(k, shape, f32)

    return {
        'x': jax.random.normal(ks[0], (BATCH, SEQ, D_MODEL), f32),
        'norm_mix': gain(ks[1], (L, D_MODEL)),
        'w_in': nrm(ks[2], (L, D_MODEL, IN_WIDTH), D_MODEL),
        'cmp_pe_k': 0.1 * jax.random.normal(ks[3], (L, CMP_LEN, hd), f32),
        'cmp_w1_k': nrm(ks[4], (L, CMP_LEN * hd, CMP_HIDDEN), CMP_LEN * hd),
        'cmp_w2_k': nrm(ks[5], (L, CMP_HIDDEN, hd), CMP_HIDDEN),
        'cmp_pe_v': 0.1 * jax.random.normal(ks[6], (L, CMP_LEN, hd), f32),
        'cmp_w1_v': nrm(ks[7], (L, CMP_LEN * hd, CMP_HIDDEN), CMP_LEN * hd),
        'cmp_w2_v': nrm(ks[8], (L, CMP_HIDDEN, hd), CMP_HIDDEN),
        'g_out_a': gain(ks[9], (L, A_WIDTH)),
        'g_out_b': gain(ks[10], (L, B_WIDTH)),
        'w_out': nrm(ks[11], (L, MIX_WIDTH, D_MODEL), MIX_WIDTH),
        'norm_mlp': gain(ks[12], (L, D_MODEL)),
        'w_up': nrm(ks[13], (L, D_MODEL, D_FF), D_MODEL),
        'w_down': nrm(ks[14], (L, D_FF, D_MODEL), D_FF),
        'norm_final': gain(ks[15], (D_MODEL,)),
    }


def reference(x, norm_mix, w_in, cmp_pe_k, cmp_w1_k, cmp_w2_k, cmp_pe_v, cmp_w1_v, cmp_w2_v, g_out_a, g_out_b, w_out, norm_mlp, w_up, w_down, norm_final):
    B, T, _ = x.shape
    cos, sin = rope_cos_sin(jnp.arange(T))
    nc = (T - CMP_LEN) // CMP_STRIDE + 1
    cmp_end = jnp.arange(nc) * CMP_STRIDE + CMP_LEN - 1
    cos_c, sin_c = rope_cos_sin(cmp_end)
    h_res = x
    for l in range(DEPTH):
        h = rms_norm(h_res, norm_mix[l])
        qa, ka, va, qb, kc, vc, ksl, vsl, kwn, vwn, gl = split_columns(h @ w_in[l])
        qa = apply_rope(to_heads(qa, A_HEADS), cos, sin)
        ka = apply_rope(to_heads(ka, A_HEADS), cos, sin)
        oa = dilated_mixture(qa, ka, to_heads(va, A_HEADS))
        oa = oa.transpose(0, 2, 1, 3).reshape(B, T, A_WIDTH)
        qn = apply_rope(to_heads(qb, B_HEADS), cos, sin).reshape(B, B_KV_HEADS, B_GROUP, T, HEAD_DIM)
        k_cmp = apply_rope(compress_tokens(to_heads(kc, B_KV_HEADS), cmp_pe_k[l], cmp_w1_k[l], cmp_w2_k[l]), cos_c, sin_c)
        v_cmp = compress_tokens(to_heads(vc, B_KV_HEADS), cmp_pe_v[l], cmp_w1_v[l], cmp_w2_v[l])
        k_sel = apply_rope(to_heads(ksl, B_KV_HEADS), cos, sin)
        v_sel = to_heads(vsl, B_KV_HEADS)
        k_win = apply_rope(to_heads(kwn, B_KV_HEADS), cos, sin)
        v_win = to_heads(vwn, B_KV_HEADS)
        gates = jax.nn.sigmoid(gl).reshape(B, T, B_KV_HEADS, B_GROUP, N_BRANCH).transpose(0, 2, 3, 1, 4)
        ob = nsa_attention(qn, k_cmp, v_cmp, cmp_end, k_sel, v_sel, k_win, v_win, gates)
        mixed = jnp.concatenate([rms_norm(oa, g_out_a[l]), rms_norm(ob, g_out_b[l])], axis=-1)
        h_res = h_res + mixed @ w_out[l]
        h = rms_norm(h_res, norm_mlp[l])
        h_res = h_res + jnp.square(jax.nn.relu(h @ w_up[l])) @ w_down[l]
    return rms_norm(h_res, norm_final)
```

```python
import functools

import jax
import jax.numpy as jnp
from jax import lax
from jax.experimental import pallas as pl
from jax.experimental.pallas import tpu as pltpu

F32 = jnp.float32
BF16 = jnp.bfloat16

HEAD_DIM = 64
ROT_DIM = HEAD_DIM // 4
ROPE_THETA = 500000.0
EPS = 1e-6
NEG = -1e30
LANES = 128

A_HEADS = 8
A_PATTERNS = ((128, 1), (512, 4), (2048, 16))
A_BLOCK = 128

B_HEADS = 8
B_KV_HEADS = 2
B_GROUP = B_HEADS // B_KV_HEADS
CMP_LEN = 32
CMP_STRIDE = 16
CMP_HIDDEN = 256
SEL_BLOCK = 64
SEL_SHIFT = 6
SEL_TOPK = 8
WIN = 512
N_BRANCH = 3

A_WIDTH = A_HEADS * HEAD_DIM
B_WIDTH = B_HEADS * HEAD_DIM
KV_WIDTH = B_KV_HEADS * HEAD_DIM

NSA_TQ = 128
NSA_CK = 128
GATE_ROWS = 16
SEL_LANE0 = HEAD_DIM

VMEM_LIMIT = 56 * 1024 * 1024


def _dot(a, b):
    return jnp.dot(a, b, preferred_element_type=F32)


def _dot_nt(a, b):
    return lax.dot_general(a, b, (((1,), (1,)), ((), ())), preferred_element_type=F32)


def _rope_rows(y, cos, sin_a, sin_b):
    outs = []
    for c in range(y.shape[1] // LANES):
        yc = y[:, c * LANES:(c + 1) * LANES]
        outs.append(yc * cos + pltpu.roll(yc, LANES - ROT_DIM // 2, 1) * sin_a
                    + pltpu.roll(yc, ROT_DIM // 2, 1) * sin_b)
    return outs[0] if len(outs) == 1 else jnp.concatenate(outs, axis=1)


def _in_proj_kernel(x_ref, g_ref, wq_ref, wt_ref, cos_ref, sa_ref, sb_ref,
                    qa_ref, ka_ref, va_ref, qb_ref, ks_ref, kw_ref, vst_ref, vwt_ref, kcvc_ref, gt_ref,
                    *, tm, seq):
    tt = pl.program_id(1)
    x = x_ref[...]
    ms = jnp.mean(x * x, axis=-1, keepdims=True)
    h = (x * lax.rsqrt(ms + EPS) * g_ref[...]).astype(BF16)
    cos, sa, sb = cos_ref[...], sa_ref[...], sb_ref[...]
    scale = HEAD_DIM ** -0.5

    def proj(c0, c1):
        return _dot(h, wq_ref[:, c0:c1])

    o = 0
    qa_ref[...] = (_rope_rows(proj(o, o + A_WIDTH), cos, sa, sb) * scale).astype(BF16)
    o += A_WIDTH
    ka_ref[...] = _rope_rows(proj(o, o + A_WIDTH), cos, sa, sb).astype(BF16)
    o += A_WIDTH
    va_ref[...] = proj(o, o + A_WIDTH).astype(BF16)
    o += A_WIDTH
    qb_ref[...] = (_rope_rows(proj(o, o + B_WIDTH), cos, sa, sb) * scale).astype(BF16)
    o += B_WIDTH
    kcvc_ref[...] = proj(o, o + 2 * KV_WIDTH)
    o += 2 * KV_WIDTH
    ksw = _rope_rows(proj(o, o + 2 * KV_WIDTH), cos, sa, sb)

    lane = lax.broadcasted_iota(jnp.int32, (tm, LANES), 1)
    row = lax.broadcasted_iota(jnp.int32, (tm, LANES), 0)
    lo = lane < HEAD_DIM
    blk = (tt * tm + row) >> SEL_SHIFT
    onehot = jnp.where(lane - SEL_LANE0 == blk, 1.0, 0.0)
    for kind, ref in ((0, ks_ref), (1, kw_ref)):
        kk = ksw[:, kind * LANES:(kind + 1) * LANES]
        tail = onehot if kind == 0 else 0.0
        ref[0, 0] = jnp.where(lo, kk, tail).astype(BF16)
        ref[0, 1] = jnp.where(lo, pltpu.roll(kk, HEAD_DIM, 1), tail).astype(BF16)

    tr = _dot_nt(wt_ref[...], h)
    for c in range(tm // LANES):
        vst_ref[0, c] = tr[0:LANES, c * LANES:(c + 1) * LANES].astype(BF16)
        vwt_ref[0, c] = tr[LANES:2 * LANES, c * LANES:(c + 1) * LANES].astype(BF16)
    gt_ref[0] = jax.nn.sigmoid(tr[2 * LANES:2 * LANES + 2 * GATE_ROWS, :])


def _rope_tables(pos):
    half = ROT_DIM // 2
    inv = ROPE_THETA ** (-jnp.arange(0, ROT_DIM, 2, dtype=F32) / ROT_DIM)
    ang = pos.astype(F32)[:, None] * inv[None, :]
    cos, sin = jnp.cos(ang), jnp.sin(ang)
    n = pos.shape[0]
    ones = jnp.ones((n, HEAD_DIM - ROT_DIM), F32)
    zeros = jnp.zeros((n, HEAD_DIM - ROT_DIM), F32)
    zh = jnp.zeros((n, half), F32)
    c_head = jnp.concatenate([cos, cos, ones], axis=1)
    a_head = jnp.concatenate([-sin, zh, zeros], axis=1)
    b_head = jnp.concatenate([zh, sin, zeros], axis=1)
    rep = LANES // HEAD_DIM
    return jnp.tile(c_head, (1, rep)), jnp.tile(a_head, (1, rep)), jnp.tile(b_head, (1, rep))


def _in_proj(x, norm_g, w_in, tables, *, tm=512):
    B, T, D = x.shape
    nt = T // tm
    offs = [0]
    for n in (A_WIDTH, A_WIDTH, A_WIDTH, B_WIDTH, KV_WIDTH, KV_WIDTH, KV_WIDTH, KV_WIDTH, KV_WIDTH, KV_WIDTH,
              B_HEADS * N_BRANCH):
        offs.append(offs[-1] + n)
    col = lambda i: w_in[:, offs[i]:offs[i + 1]]
    wq = jnp.concatenate([col(0), col(1), col(2), col(3), col(4), col(5), col(6), col(8)], axis=1).astype(BF16)
    gl = col(10)
    per_g = B_GROUP * N_BRANCH
    gpad = jnp.zeros((D, GATE_ROWS - per_g), w_in.dtype)
    wt = jnp.concatenate([col(7), col(9), gl[:, :per_g], gpad, gl[:, per_g:], gpad], axis=1).T.astype(BF16)
    cos, sa, sb = tables
    nq = wq.shape[1]
    nr = wt.shape[0]
    tok = lambda w: pl.BlockSpec((None, tm, w), lambda b, t: (b, t, 0))
    const = lambda shape: pl.BlockSpec(shape, lambda b, t: (0,) * len(shape))
    tab = pl.BlockSpec((tm, LANES), lambda b, t: (t, 0))
    out_shapes = (
        jax.ShapeDtypeStruct((B, T, A_WIDTH), BF16),
        jax.ShapeDtypeStruct((B, T, A_WIDTH), BF16),
        jax.ShapeDtypeStruct((B, T, A_WIDTH), BF16),
        jax.ShapeDtypeStruct((B, T, B_WIDTH), BF16),
        jax.ShapeDtypeStruct((B, B_KV_HEADS, T, LANES), BF16),
        jax.ShapeDtypeStruct((B, B_KV_HEADS, T, LANES), BF16),
        jax.ShapeDtypeStruct((B, T // LANES, LANES, LANES), BF16),
        jax.ShapeDtypeStruct((B, T // LANES, LANES, LANES), BF16),
        jax.ShapeDtypeStruct((B, T, 2 * KV_WIDTH), F32),
        jax.ShapeDtypeStruct((B, 2 * GATE_ROWS, T), F32),
    )
    frame = pl.BlockSpec((1, B_KV_HEADS, tm, LANES), lambda b, t: (b, 0, t, 0))
    vt = pl.BlockSpec((1, tm // LANES, LANES, LANES), lambda b, t: (b, t, 0, 0))
    out_specs = (tok(A_WIDTH), tok(A_WIDTH), tok(A_WIDTH), tok(B_WIDTH), frame, frame, vt, vt,
                 tok(2 * KV_WIDTH), pl.BlockSpec((1, 2 * GATE_ROWS, tm), lambda b, t: (b, 0, t)))
    return pl.pallas_call(
        functools.partial(_in_proj_kernel, tm=tm, seq=T),
        grid=(B, nt),
        in_specs=[tok(D), const((1, D)), const((D, nq)), const((nr, D)), tab, tab, tab],
        out_specs=out_specs,
        out_shape=out_shapes,
        compiler_params=pltpu.CompilerParams(dimension_semantics=("parallel", "parallel"),
                                             vmem_limit_bytes=VMEM_LIMIT),
        name="in_proj",
    )(x, norm_g.reshape(1, D), wq, wt, cos, sa, sb)


def _compress_kernel(ak_ref, av_ref, w1k_ref, w1v_ref, pek_ref, pev_ref, w2k_ref, w2vt_ref, cos_ref, sa_ref, sb_ref,
                     kc_ref, vct_ref, *, nc_pad):
    half = CMP_LEN // 2
    hid_w = B_KV_HEADS * CMP_HIDDEN

    def hidden(a_ref, w1_ref, pe_ref):
        acc_u = jnp.zeros((nc_pad, hid_w), F32)
        acc_v = jnp.zeros((nc_pad, hid_w), F32)
        for p in range(half):
            ap = a_ref[0, pl.ds(p, nc_pad, stride=CMP_STRIDE), :]
            acc_u = acc_u + _dot((ap + pe_ref[p:p + 1, :]).astype(BF16), w1_ref[p])
            acc_v = acc_v + _dot((ap + pe_ref[half + p:half + p + 1, :]).astype(BF16), w1_ref[half + p])
        return jax.nn.gelu(acc_u + pltpu.roll(acc_v, nc_pad - 1, 0))

    hk = hidden(ak_ref, w1k_ref, pek_ref).astype(BF16)
    hv = hidden(av_ref, w1v_ref, pev_ref).astype(BF16)
    for g in range(B_KV_HEADS):
        hg = hk[:, g * CMP_HIDDEN:(g + 1) * CMP_HIDDEN]
        kc = _dot(hg, w2k_ref[...])
        kc_ref[0, g] = _rope_rows(kc, cos_ref[...], sa_ref[...], sb_ref[...]).astype(BF16)
        vg = hv[:, g * CMP_HIDDEN:(g + 1) * CMP_HIDDEN]
        vct_ref[0, g] = _dot_nt(w2vt_ref[...], vg).astype(BF16)


def _block_diag_w1(w1):
    w = w1.reshape(CMP_LEN, HEAD_DIM, CMP_HIDDEN)
    z = jnp.zeros_like(w)
    top = jnp.concatenate([w, z], axis=2)
    bot = jnp.concatenate([z, w], axis=2)
    return jnp.concatenate([top, bot], axis=1).astype(BF16)


def _compress(kcvc, pe_k, w1_k, w2_k, pe_v, w1_v, w2_v, cmp_tables):
    B, T, _ = kcvc.shape
    nc_pad = T // CMP_STRIDE
    w2k = jnp.concatenate([w2_k, jnp.zeros_like(w2_k)], axis=1).astype(BF16)
    w2vt = w2_v.T.astype(BF16)
    pek = jnp.tile(pe_k, (1, B_KV_HEADS))
    pev = jnp.tile(pe_v, (1, B_KV_HEADS))
    const = lambda shape: pl.BlockSpec(shape, lambda b: (0,) * len(shape))
    cos, sa, sb = cmp_tables
    return pl.pallas_call(
        functools.partial(_compress_kernel, nc_pad=nc_pad),
        grid=(B,),
        in_specs=[pl.BlockSpec((1, T, KV_WIDTH), lambda b: (b, 0, 0)), pl.BlockSpec((1, T, KV_WIDTH), lambda b: (b, 0, 1)),
                  const((CMP_LEN, LANES, B_KV_HEADS * CMP_HIDDEN)), const((CMP_LEN, LANES, B_KV_HEADS * CMP_HIDDEN)),
                  const((CMP_LEN, LANES)), const((CMP_LEN, LANES)),
                  const((CMP_HIDDEN, LANES)), const((HEAD_DIM, CMP_HIDDEN)),
                  const((nc_pad, LANES)), const((nc_pad, LANES)), const((nc_pad, LANES))],
        out_specs=(pl.BlockSpec((1, B_KV_HEADS, nc_pad, LANES), lambda b: (b, 0, 0, 0)),
                   pl.BlockSpec((1, B_KV_HEADS, HEAD_DIM, nc_pad), lambda b: (b, 0, 0, 0))),
        out_shape=(jax.ShapeDtypeStruct((B, B_KV_HEADS, nc_pad, LANES), BF16),
                   jax.ShapeDtypeStruct((B, B_KV_HEADS, HEAD_DIM, nc_pad), BF16)),
        compiler_params=pltpu.CompilerParams(dimension_semantics=("parallel",), vmem_limit_bytes=VMEM_LIMIT),
        name="compress",
    )(kcvc, kcvc, _block_diag_w1(w1_k), _block_diag_w1(w1_v), pek, pev, w2k, w2vt, cos, sa, sb)


def _mixer_a_kernel(q_ref, k_ref, v_ref, o_ref, qf, kf, vf, u_s, m_s, l_s, *, seq):
    blk = A_BLOCK
    qf[...] = q_ref[0].astype(F32)
    kf[...] = k_ref[0].astype(F32)
    vf[...] = v_ref[0].astype(F32)
    lane = lax.broadcasted_iota(jnp.int32, (blk, LANES), 1)
    lo = lane < HEAD_DIM

    def attend(pi, q_start, k_start, nk, stride, mask):
        sl = lambda start, n: pl.ds(start, n, stride=stride) if stride > 1 else pl.ds(start, n)
        q = qf[sl(q_start, blk), :]
        kb = kf[sl(k_start, nk), :].astype(BF16)
        vb = vf[sl(k_start, nk), :].astype(BF16)
        us, ms, ls = [], [], []
        for hh in range(2):
            qm = jnp.where(lo if hh == 0 else ~lo, q, 0.0).astype(BF16)
            s = jnp.where(mask, _dot_nt(qm, kb), NEG)
            m = jnp.max(s, axis=-1, keepdims=True)
            e = jnp.exp(s - m)
            ls.append(jnp.sum(e, axis=-1, keepdims=True))
            ms.append(m)
            us.append(_dot(e.astype(BF16), vb))
        u_s[pi, sl(q_start, blk), :] = jnp.where(lo, us[0], us[1])
        m_s[pi, sl(q_start, blk), :] = jnp.where(lo, ms[0], ms[1])
        l_s[pi, sl(q_start, blk), :] = jnp.where(lo, ls[0], ls[1])

    qi = lax.broadcasted_iota(jnp.int32, (blk, 2 * blk), 0)
    ki = lax.broadcasted_iota(jnp.int32, (blk, 2 * blk), 1)
    qi1 = lax.broadcasted_iota(jnp.int32, (blk, blk), 0)
    ki1 = lax.broadcasted_iota(jnp.int32, (blk, blk), 1)
    for pi, (window, dil) in enumerate(A_PATTERNS):
        n_back = window // dil
        nb = seq // dil // blk
        band = (ki >= qi + (blk - n_back)) & (ki <= qi + blk)
        first = (ki1 >= qi1 - n_back) & (ki1 <= qi1)

        def residue(r, carry, pi=pi, dil=dil, nb=nb, band=band, first=first):
            attend(pi, r, r, blk, dil, first)

            def inner(n, c):
                q_start = r + dil * blk * n
                attend(pi, q_start, q_start - dil * blk, 2 * blk, dil, band)
                return c

            return lax.fori_loop(1, nb, inner, carry)

        lax.fori_loop(0, dil, residue, 0)

    rows = 256

    def merge(c, carry):
        r0 = pl.multiple_of(c * rows, rows)
        ms = [m_s[p, pl.ds(r0, rows), :] for p in range(len(A_PATTERNS))]
        m_all = functools.reduce(jnp.maximum, ms)
        num = jnp.zeros((rows, LANES), F32)
        den = jnp.zeros((rows, LANES), F32)
        for p in range(len(A_PATTERNS)):
            a = jnp.exp(ms[p] - m_all)
            num = num + a * u_s[p, pl.ds(r0, rows), :]
            den = den + a * l_s[p, pl.ds(r0, rows), :]
        o_ref[0, pl.ds(r0, rows), :] = num / den
        return carry

    lax.fori_loop(0, seq // rows, merge, 0)


def _mixer_a(qa, ka, va):
    B, T, W = qa.shape
    npair = W // LANES
    spec = pl.BlockSpec((1, T, LANES), lambda b, p: (b, 0, p))
    npat = len(A_PATTERNS)
    return pl.pallas_call(
        functools.partial(_mixer_a_kernel, seq=T),
        grid=(B, npair),
        in_specs=[spec, spec, spec],
        out_specs=spec,
        out_shape=jax.ShapeDtypeStruct((B, T, W), F32),
        scratch_shapes=[pltpu.VMEM((T, LANES), F32)] * 3 + [pltpu.VMEM((npat, T, LANES), F32)] * 3,
        compiler_params=pltpu.CompilerParams(dimension_semantics=("parallel", "parallel"),
                                             vmem_limit_bytes=VMEM_LIMIT),
        name="mixer_a",
    )(qa, ka, va)


def _nsa_kernel(q_ref, ks_ref, kw_ref, vst_ref, vwt_ref, kc_ref, vct_ref, gt_ref, ovt_ref, o_ref, *, n_sel_blocks):
    tq, ck = NSA_TQ, NSA_CK
    qi = pl.program_id(2)
    t0 = qi * tq
    nrow = B_GROUP * tq

    q = q_ref[0].astype(F32)
    lane = lax.broadcasted_iota(jnp.int32, (tq, LANES), 1)
    lo = lane < HEAD_DIM
    frames = []
    for r in range(B_GROUP):
        ch = q[:, (r // 2) * LANES:(r // 2 + 1) * LANES]
        if r % 2:
            ch = pltpu.roll(ch, HEAD_DIM, 1)
        frames.append(jnp.where(lo, ch, 0.0))
    qs = jnp.concatenate(frames, axis=0)

    col_i = lax.broadcasted_iota(jnp.int32, (ck, nrow), 1) & (tq - 1)
    row_k = lax.broadcasted_iota(jnp.int32, (ck, nrow), 0)

    sc = _dot_nt(kc_ref[0, 0], qs.astype(BF16))
    cmask = row_k * CMP_STRIDE + (CMP_LEN - 1) <= t0 + col_i
    sc = jnp.where(cmask, sc, NEG)
    m = jnp.max(sc, axis=0, keepdims=True)
    e = jnp.where(cmask, jnp.exp(sc - m), 0.0)
    den = jnp.sum(e, axis=0, keepdims=True)
    p_cmp = e / jnp.maximum(den, 1e-30)
    o_cmp = _dot(vct_ref[0, 0], p_cmp.astype(BF16))

    psum = p_cmp[:, 0:tq]
    for r in range(1, B_GROUP):
        psum = psum + p_cmp[:, r * tq:(r + 1) * tq]
    p_hi = psum.astype(BF16)
    p_lo = (psum - p_hi.astype(F32)).astype(BF16)
    imp = _dot(ovt_ref[...], p_hi) + _dot(ovt_ref[...], p_lo)
    rr = lax.broadcasted_iota(jnp.int32, (LANES, tq), 0)
    j = rr - SEL_LANE0
    cur = (t0 + lax.broadcasted_iota(jnp.int32, (LANES, tq), 1)) >> SEL_SHIFT
    forced = (j == 0) | (j == cur) | (j == cur - 1)
    inband = (j >= 0) & (j < n_sel_blocks)
    low = -3e38
    score = jnp.where(forced, imp + 2.0, jnp.where(j > cur, -1.0, imp))
    score = jnp.where(inband, score, low)
    sel = jnp.zeros((LANES, tq), jnp.bool_)
    rr_f = rr.astype(F32)
    for _ in range(min(SEL_TOPK, n_sel_blocks)):
        mx = jnp.max(score, axis=0, keepdims=True)
        first = jnp.min(jnp.where(score == mx, rr_f, 4.0 * LANES), axis=0, keepdims=True)
        hit = rr_f == first
        sel = sel | hit
        score = jnp.where(hit, low, score)
    selneg = jnp.where(inband & ~sel, NEG, 0.0).T
    q_aug = (qs + jnp.concatenate([selneg] * B_GROUP, axis=0)).astype(BF16)
    q_b = qs.astype(BF16)

    def step(qmat, k_blk, vt_blk, mask, carry):
        m_old, l_old, acc = carry
        s = _dot_nt(k_blk, qmat)
        if mask is not None:
            s = jnp.where(mask, s, NEG)
        m_new = jnp.maximum(m_old, jnp.max(s, axis=0, keepdims=True))
        alpha = jnp.exp(m_old - m_new)
        p = jnp.exp(s - m_new)
        l_new = alpha * l_old + jnp.sum(p, axis=0, keepdims=True)
        acc = alpha * acc + _dot(vt_blk, p.astype(BF16))
        return m_new, l_new, acc

    init = (jnp.full((1, nrow), NEG, F32), jnp.zeros((1, nrow), F32), jnp.zeros((HEAD_DIM, nrow), F32))
    causal = row_k <= col_i
    g = pl.program_id(1)
    v_rows = pl.ds(pl.multiple_of(g * HEAD_DIM, HEAD_DIM), HEAD_DIM)

    def chunk(ref, c):
        return ref[0, 0, pl.ds(pl.multiple_of(c * ck, ck), ck), :]

    carry = step(q_aug, chunk(ks_ref, qi), vst_ref[0, qi, v_rows, :], causal, init)
    carry = lax.fori_loop(
        0, qi, lambda c, cr: step(q_aug, chunk(ks_ref, c), vst_ref[0, c, v_rows, :], None, cr), carry)
    o_sel = carry[2] / carry[1]

    nfull = WIN // ck - 1
    carry = step(q_b, chunk(kw_ref, qi), vwt_ref[0, qi, v_rows, :], causal, init)
    carry = lax.fori_loop(
        jnp.maximum(qi - nfull, 0), qi,
        lambda c, cr: step(q_b, chunk(kw_ref, c), vwt_ref[0, c, v_rows, :], None, cr), carry)
    c_old = jnp.maximum(qi - nfull - 1, 0)
    old_mask = (row_k > col_i) & (qi > nfull)
    carry = step(q_b, chunk(kw_ref, c_old), vwt_ref[0, c_old, v_rows, :], old_mask, carry)
    o_win = carry[2] / carry[1]

    gt = gt_ref[0]
    outs = []
    for r in range(B_GROUP):
        cs = slice(r * tq, (r + 1) * tq)
        outs.append(gt[r * N_BRANCH:r * N_BRANCH + 1, :] * o_cmp[:, cs]
                    + gt[r * N_BRANCH + 1:r * N_BRANCH + 2, :] * o_sel[:, cs]
                    + gt[r * N_BRANCH + 2:r * N_BRANCH + 3, :] * o_win[:, cs])
    for c in range(B_GROUP // 2):
        pair = jnp.concatenate([outs[2 * c], outs[2 * c + 1]], axis=0)
        o_ref[0, :, c * LANES:(c + 1) * LANES] = pair.T


def _overlap_t(nc_pad, ns):
    nc = nc_pad - 1
    c0 = jnp.arange(nc_pad) * CMP_STRIDE
    s0 = jnp.arange(ns) * SEL_BLOCK
    ov = jnp.minimum(c0[None, :] + CMP_LEN, s0[:, None] + SEL_BLOCK) - jnp.maximum(c0[None, :], s0[:, None])
    ov = jnp.clip(ov, 0, None).astype(F32) / CMP_LEN
    ov = jnp.where(jnp.arange(nc_pad)[None, :] < nc, ov, 0.0)
    full = jnp.zeros((LANES, nc_pad), F32).at[SEL_LANE0:SEL_LANE0 + ns].set(ov)
    return full.astype(BF16)


def _nsa(qb, ks, kw, vst, vwt, kc, vct, gt):
    B, T, W = qb.shape
    nq = T // NSA_TQ
    ns = T // SEL_BLOCK
    nc_pad = kc.shape[2]
    gw = W // B_KV_HEADS
    kspec = pl.BlockSpec((1, 1, T, LANES), lambda b, g, i: (b, g, 0, 0))
    vspec = pl.BlockSpec((1, T // LANES, LANES, LANES), lambda b, g, i: (b, 0, 0, 0))
    return pl.pallas_call(
        functools.partial(_nsa_kernel, n_sel_blocks=ns),
        grid=(B, B_KV_HEADS, nq),
        in_specs=[pl.BlockSpec((1, NSA_TQ, gw), lambda b, g, i: (b, i, g)),
                  kspec, kspec, vspec, vspec,
                  pl.BlockSpec((1, 1, nc_pad, LANES), lambda b, g, i: (b, g, 0, 0)),
                  pl.BlockSpec((1, 1, HEAD_DIM, nc_pad), lambda b, g, i: (b, g, 0, 0)),
                  pl.BlockSpec((1, GATE_ROWS, NSA_TQ), lambda b, g, i: (b, g, i)),
                  pl.BlockSpec((LANES, nc_pad), lambda b, g, i: (0, 0))],
        out_specs=pl.BlockSpec((1, NSA_TQ, gw), lambda b, g, i: (b, i, g)),
        out_shape=jax.ShapeDtypeStruct((B, T, W), F32),
        compiler_params=pltpu.CompilerParams(dimension_semantics=("parallel", "parallel", "arbitrary"),
                                             vmem_limit_bytes=VMEM_LIMIT),
        name="nsa",
    )(qb, ks, kw, vst, vwt, kc, vct, gt, _overlap_t(nc_pad, ns))


def _post_kernel(x_ref, oa_ref, ob_ref, ga_ref, gb_ref, wo_ref, gm_ref, wu_ref, wd_ref, gf_ref, o_ref, *, final):
    def norm(v, g):
        return v * lax.rsqrt(jnp.mean(v * v, axis=-1, keepdims=True) + EPS) * g

    na = norm(oa_ref[...], ga_ref[...]).astype(BF16)
    nb = norm(ob_ref[...], gb_ref[...]).astype(BF16)
    aw = na.shape[1]
    h_res = x_ref[...] + _dot(na, wo_ref[0:aw, :]) + _dot(nb, wo_ref[aw:, :])
    h = norm(h_res, gm_ref[...]).astype(BF16)
    u = jnp.square(jnp.maximum(_dot(h, wu_ref[...]), 0.0)).astype(BF16)
    acc = h_res + _dot(u, wd_ref[...])
    o_ref[...] = norm(acc, gf_ref[...]) if final else acc


def _post(x, oa, ob, g_a, g_b, w_out, g_mlp, w_up, w_down, g_final, *, final, tm=256):
    B, T, D = x.shape
    n = B * T
    dff = w_up.shape[1]
    tok = lambda w: pl.BlockSpec((tm, w), lambda i: (i, 0))
    const = lambda shape: pl.BlockSpec(shape, lambda i: (0, 0), pipeline_mode=pl.Buffered(1))
    out = pl.pallas_call(
        functools.partial(_post_kernel, final=final),
        grid=(n // tm,),
        in_specs=[tok(D), tok(A_WIDTH), tok(B_WIDTH), const((1, A_WIDTH)), const((1, B_WIDTH)),
                  const((A_WIDTH + B_WIDTH, D)), const((1, D)), const((D, dff)), const((dff, D)), const((1, D))],
        out_specs=tok(D),
        out_shape=jax.ShapeDtypeStruct((n, D), F32),
        compiler_params=pltpu.CompilerParams(dimension_semantics=("parallel",), vmem_limit_bytes=VMEM_LIMIT),
        name="post",
    )(x.reshape(n, D), oa.reshape(n, A_WIDTH), ob.reshape(n, B_WIDTH), g_a.reshape(1, -1), g_b.reshape(1, -1),
      w_out.astype(BF16), g_mlp.reshape(1, D), w_up.astype(BF16), w_down.astype(BF16), g_final.reshape(1, D))
    return out.reshape(B, T, D)


def kernel(x, norm_mix, w_in, cmp_pe_k, cmp_w1_k, cmp_w2_k, cmp_pe_v, cmp_w1_v, cmp_w2_v, g_out_a, g_out_b,
           w_out, norm_mlp, w_up, w_down, norm_final):
    B, T, D = x.shape
    depth = w_in.shape[0]
    tables = _rope_tables(jnp.arange(T))
    cmp_tables = _rope_tables(jnp.arange(T // CMP_STRIDE) * CMP_STRIDE + CMP_LEN - 1)
    h_res = x
    for l in range(depth):
        qa, ka, va, qb, ks, kw, vst, vwt, kcvc, gt = _in_proj(h_res, norm_mix[l], w_in[l], tables)
        kc, vct = _compress(kcvc, cmp_pe_k[l], cmp_w1_k[l], cmp_w2_k[l], cmp_pe_v[l], cmp_w1_v[l], cmp_w2_v[l],
                            cmp_tables)
        oa = _mixer_a(qa, ka, va)
        ob = _nsa(qb, ks, kw, vst, vwt, kc, vct, gt)
        h_res = _post(h_res, oa, ob, g_out_a[l], g_out_b[l], w_out[l], norm_mlp[l], w_up[l], w_down[l], norm_final,
                      final=(l == depth - 1))
    return h_res
```

```python
import functools

import jax
import jax.numpy as jnp
from jax import lax
from jax.experimental import pallas as pl
from jax.experimental.pallas import tpu as pltpu

F32 = jnp.float32
BF16 = jnp.bfloat16

HEAD_DIM = 64
ROT_DIM = HEAD_DIM // 4
ROPE_THETA = 500000.0
EPS = 1e-6
NEG = -1e30
LANES = 128

A_HEADS = 8
A_PATTERNS = ((128, 1), (512, 4), (2048, 16))
A_BLOCK = 128

B_HEADS = 8
B_KV_HEADS = 2
B_GROUP = B_HEADS // B_KV_HEADS
CMP_LEN = 32
CMP_STRIDE = 16
CMP_HIDDEN = 256
SEL_BLOCK = 64
SEL_SHIFT = 6
SEL_TOPK = 8
WIN = 512
N_BRANCH = 3

A_WIDTH = A_HEADS * HEAD_DIM
B_WIDTH = B_HEADS * HEAD_DIM
KV_WIDTH = B_KV_HEADS * HEAD_DIM

NSA_TQ = 256
NSA_CK = 256
GATE_ROWS = 16
SEL_LANE0 = HEAD_DIM

VMEM_LIMIT = 56 * 1024 * 1024


def _dot(a, b):
    return jnp.dot(a, b, preferred_element_type=F32)


def _dot_nt(a, b):
    return lax.dot_general(a, b, (((1,), (1,)), ((), ())), preferred_element_type=F32)


def _rope_rows(y, cos, sin_a, sin_b):
    outs = []
    for c in range(y.shape[1] // LANES):
        yc = y[:, c * LANES:(c + 1) * LANES]
        outs.append(yc * cos + pltpu.roll(yc, LANES - ROT_DIM // 2, 1) * sin_a
                    + pltpu.roll(yc, ROT_DIM // 2, 1) * sin_b)
    return outs[0] if len(outs) == 1 else jnp.concatenate(outs, axis=1)


def _in_proj_kernel(x_ref, g_ref, wq_ref, wt_ref, cos_ref, sa_ref, sb_ref,
                    qa_ref, ka_ref, va_ref, qb_ref, ks_ref, kw_ref, vst_ref, vwt_ref, kcvc_ref, gt_ref,
                    *, tm, seq):
    tt = pl.program_id(1)
    x = x_ref[...]
    ms = jnp.mean(x * x, axis=-1, keepdims=True)
    h = (x * lax.rsqrt(ms + EPS) * g_ref[...]).astype(BF16)
    cos, sa, sb = cos_ref[...], sa_ref[...], sb_ref[...]
    scale = HEAD_DIM ** -0.5

    def proj(c0, c1):
        return _dot(h, wq_ref[:, c0:c1])

    o = 0
    qa_ref[...] = (_rope_rows(proj(o, o + A_WIDTH), cos, sa, sb) * scale).astype(BF16)
    o += A_WIDTH
    ka_ref[...] = _rope_rows(proj(o, o + A_WIDTH), cos, sa, sb).astype(BF16)
    o += A_WIDTH
    va_ref[...] = proj(o, o + A_WIDTH).astype(BF16)
    o += A_WIDTH
    qb_ref[...] = (_rope_rows(proj(o, o + B_WIDTH), cos, sa, sb) * scale).astype(BF16)
    o += B_WIDTH
    kcvc_ref[...] = proj(o, o + 2 * KV_WIDTH)
    o += 2 * KV_WIDTH
    ksw = _rope_rows(proj(o, o + 2 * KV_WIDTH), cos, sa, sb)

    lane = lax.broadcasted_iota(jnp.int32, (tm, LANES), 1)
    row = lax.broadcasted_iota(jnp.int32, (tm, LANES), 0)
    lo = lane < HEAD_DIM
    blk = (tt * tm + row) >> SEL_SHIFT
    onehot = jnp.where(lane - SEL_LANE0 == blk, 1.0, 0.0)
    for kind, ref in ((0, ks_ref), (1, kw_ref)):
        kk = ksw[:, kind * LANES:(kind + 1) * LANES]
        tail = onehot if kind == 0 else 0.0
        ref[0, 0] = jnp.where(lo, kk, tail).astype(BF16)
        ref[0, 1] = jnp.where(lo, pltpu.roll(kk, HEAD_DIM, 1), tail).astype(BF16)

    tr = _dot_nt(wt_ref[...], h)
    for c in range(tm // NSA_CK):
        vst_ref[0, c] = tr[0:LANES, c * NSA_CK:(c + 1) * NSA_CK].astype(BF16)
        vwt_ref[0, c] = tr[LANES:2 * LANES, c * NSA_CK:(c + 1) * NSA_CK].astype(BF16)
    gt_ref[0] = jax.nn.sigmoid(tr[2 * LANES:2 * LANES + 2 * GATE_ROWS, :])


def _rope_tables(pos):
    half = ROT_DIM // 2
    inv = ROPE_THETA ** (-jnp.arange(0, ROT_DIM, 2, dtype=F32) / ROT_DIM)
    ang = pos.astype(F32)[:, None] * inv[None, :]
    cos, sin = jnp.cos(ang), jnp.sin(ang)
    n = pos.shape[0]
    ones = jnp.ones((n, HEAD_DIM - ROT_DIM), F32)
    zeros = jnp.zeros((n, HEAD_DIM - ROT_DIM), F32)
    zh = jnp.zeros((n, half), F32)
    c_head = jnp.concatenate([cos, cos, ones], axis=1)
    a_head = jnp.concatenate([-sin, zh, zeros], axis=1)
    b_head = jnp.concatenate([zh, sin, zeros], axis=1)
    rep = LANES // HEAD_DIM
    return jnp.tile(c_head, (1, rep)), jnp.tile(a_head, (1, rep)), jnp.tile(b_head, (1, rep))


def _in_proj(x, norm_g, w_in, tables, *, tm=512):
    B, T, D = x.shape
    nt = T // tm
    offs = [0]
    for n in (A_WIDTH, A_WIDTH, A_WIDTH, B_WIDTH, KV_WIDTH, KV_WIDTH, KV_WIDTH, KV_WIDTH, KV_WIDTH, KV_WIDTH,
              B_HEADS * N_BRANCH):
        offs.append(offs[-1] + n)
    col = lambda i: w_in[:, offs[i]:offs[i + 1]]
    wq = jnp.concatenate([col(0), col(1), col(2), col(3), col(4), col(5), col(6), col(8)], axis=1).astype(BF16)
    gl = col(10)
    per_g = B_GROUP * N_BRANCH
    gpad = jnp.zeros((D, GATE_ROWS - per_g), w_in.dtype)
    wt = jnp.concatenate([col(7), col(9), gl[:, :per_g], gpad, gl[:, per_g:], gpad], axis=1).T.astype(BF16)
    cos, sa, sb = tables
    nq = wq.shape[1]
    nr = wt.shape[0]
    tok = lambda w: pl.BlockSpec((None, tm, w), lambda b, t: (b, t, 0))
    const = lambda shape: pl.BlockSpec(shape, lambda b, t: (0,) * len(shape))
    tab = pl.BlockSpec((tm, LANES), lambda b, t: (t, 0))
    out_shapes = (
        jax.ShapeDtypeStruct((B, T, A_WIDTH), BF16),
        jax.ShapeDtypeStruct((B, T, A_WIDTH), BF16),
        jax.ShapeDtypeStruct((B, T, A_WIDTH), BF16),
        jax.ShapeDtypeStruct((B, T, B_WIDTH), BF16),
        jax.ShapeDtypeStruct((B, B_KV_HEADS, T, LANES), BF16),
        jax.ShapeDtypeStruct((B, B_KV_HEADS, T, LANES), BF16),
        jax.ShapeDtypeStruct((B, T // NSA_CK, LANES, NSA_CK), BF16),
        jax.ShapeDtypeStruct((B, T // NSA_CK, LANES, NSA_CK), BF16),
        jax.ShapeDtypeStruct((B, T, 2 * KV_WIDTH), F32),
        jax.ShapeDtypeStruct((B, 2 * GATE_ROWS, T), F32),
    )
    frame = pl.BlockSpec((1, B_KV_HEADS, tm, LANES), lambda b, t: (b, 0, t, 0))
    vt = pl.BlockSpec((1, tm // NSA_CK, LANES, NSA_CK), lambda b, t: (b, t, 0, 0))
    out_specs = (tok(A_WIDTH), tok(A_WIDTH), tok(A_WIDTH), tok(B_WIDTH), frame, frame, vt, vt,
                 tok(2 * KV_WIDTH), pl.BlockSpec((1, 2 * GATE_ROWS, tm), lambda b, t: (b, 0, t)))
    return pl.pallas_call(
        functools.partial(_in_proj_kernel, tm=tm, seq=T),
        grid=(B, nt),
        in_specs=[tok(D), const((1, D)), const((D, nq)), const((nr, D)), tab, tab, tab],
        out_specs=out_specs,
        out_shape=out_shapes,
        compiler_params=pltpu.CompilerParams(dimension_semantics=("parallel", "parallel"),
                                             vmem_limit_bytes=VMEM_LIMIT),
        name="in_proj",
    )(x, norm_g.reshape(1, D), wq, wt, cos, sa, sb)


def _compress_kernel(ak_ref, av_ref, w1k_ref, w1v_ref, pek_ref, pev_ref, w2k_ref, w2vt_ref, cos_ref, sa_ref, sb_ref,
                     kc_ref, vct_ref, *, nc_pad):
    half = CMP_LEN // 2
    hid_w = B_KV_HEADS * CMP_HIDDEN

    def hidden(a_ref, w1_ref, pe_ref):
        acc_u = jnp.zeros((nc_pad, hid_w), F32)
        acc_v = jnp.zeros((nc_pad, hid_w), F32)
        for p in range(half):
            ap = a_ref[0, pl.ds(p, nc_pad, stride=CMP_STRIDE), :]
            acc_u = acc_u + _dot((ap + pe_ref[p:p + 1, :]).astype(BF16), w1_ref[p])
            acc_v = acc_v + _dot((ap + pe_ref[half + p:half + p + 1, :]).astype(BF16), w1_ref[half + p])
        return jax.nn.gelu(acc_u + pltpu.roll(acc_v, nc_pad - 1, 0))

    hk = hidden(ak_ref, w1k_ref, pek_ref).astype(BF16)
    hv = hidden(av_ref, w1v_ref, pev_ref).astype(BF16)
    for g in range(B_KV_HEADS):
        hg = hk[:, g * CMP_HIDDEN:(g + 1) * CMP_HIDDEN]
        kc = _dot(hg, w2k_ref[...])
        kc_ref[0, g] = _rope_rows(kc, cos_ref[...], sa_ref[...], sb_ref[...]).astype(BF16)
        vg = hv[:, g * CMP_HIDDEN:(g + 1) * CMP_HIDDEN]
        vct_ref[0, g] = _dot_nt(w2vt_ref[...], vg).astype(BF16)


def _block_diag_w1(w1):
    w = w1.reshape(CMP_LEN, HEAD_DIM, CMP_HIDDEN)
    z = jnp.zeros_like(w)
    top = jnp.concatenate([w, z], axis=2)
    bot = jnp.concatenate([z, w], axis=2)
    return jnp.concatenate([top, bot], axis=1).astype(BF16)


def _compress(kcvc, pe_k, w1_k, w2_k, pe_v, w1_v, w2_v, cmp_tables):
    B, T, _ = kcvc.shape
    nc_pad = T // CMP_STRIDE
    w2k = jnp.concatenate([w2_k, jnp.zeros_like(w2_k)], axis=1).astype(BF16)
    w2vt = w2_v.T.astype(BF16)
    pek = jnp.tile(pe_k, (1, B_KV_HEADS))
    pev = jnp.tile(pe_v, (1, B_KV_HEADS))
    const = lambda shape: pl.BlockSpec(shape, lambda b: (0,) * len(shape))
    cos, sa, sb = cmp_tables
    return pl.pallas_call(
        functools.partial(_compress_kernel, nc_pad=nc_pad),
        grid=(B,),
        in_specs=[pl.BlockSpec((1, T, KV_WIDTH), lambda b: (b, 0, 0)), pl.BlockSpec((1, T, KV_WIDTH), lambda b: (b, 0, 1)),
                  const((CMP_LEN, LANES, B_KV_HEADS * CMP_HIDDEN)), const((CMP_LEN, LANES, B_KV_HEADS * CMP_HIDDEN)),
                  const((CMP_LEN, LANES)), const((CMP_LEN, LANES)),
                  const((CMP_HIDDEN, LANES)), const((HEAD_DIM, CMP_HIDDEN)),
                  const((nc_pad, LANES)), const((nc_pad, LANES)), const((nc_pad, LANES))],
        out_specs=(pl.BlockSpec((1, B_KV_HEADS, nc_pad, LANES), lambda b: (b, 0, 0, 0)),
                   pl.BlockSpec((1, B_KV_HEADS, HEAD_DIM, nc_pad), lambda b: (b, 0, 0, 0))),
        out_shape=(jax.ShapeDtypeStruct((B, B_KV_HEADS, nc_pad, LANES), BF16),
                   jax.ShapeDtypeStruct((B, B_KV_HEADS, HEAD_DIM, nc_pad), BF16)),
        compiler_params=pltpu.CompilerParams(dimension_semantics=("parallel",), vmem_limit_bytes=VMEM_LIMIT),
        name="compress",
    )(kcvc, kcvc, _block_diag_w1(w1_k), _block_diag_w1(w1_v), pek, pev, w2k, w2vt, cos, sa, sb)


def _mixer_a_kernel(q_ref, k_ref, v_ref, o_ref, qf, kf, vf, u_s, m_s, l_s, *, seq):
    blk = A_BLOCK
    qf[...] = q_ref[0].astype(F32)
    kf[...] = k_ref[0].astype(F32)
    vf[...] = v_ref[0].astype(F32)
    lane = lax.broadcasted_iota(jnp.int32, (blk, LANES), 1)
    lo = lane < HEAD_DIM

    def attend(pi, q_start, k_start, nk, stride, mask):
        sl = lambda start, n: pl.ds(start, n, stride=stride) if stride > 1 else pl.ds(start, n)
        q = qf[sl(q_start, blk), :]
        kb = kf[sl(k_start, nk), :].astype(BF16)
        vb = vf[sl(k_start, nk), :].astype(BF16)
        us, ms, ls = [], [], []
        for hh in range(2):
            qm = jnp.where(lo if hh == 0 else ~lo, q, 0.0).astype(BF16)
            s = jnp.where(mask, _dot_nt(qm, kb), NEG)
            m = jnp.max(s, axis=-1, keepdims=True)
            e = jnp.exp(s - m)
            ls.append(jnp.sum(e, axis=-1, keepdims=True))
            ms.append(m)
            us.append(_dot(e.astype(BF16), vb))
        u_s[pi, sl(q_start, blk), :] = jnp.where(lo, us[0], us[1])
        m_s[pi, sl(q_start, blk), :] = jnp.where(lo, ms[0], ms[1])
        l_s[pi, sl(q_start, blk), :] = jnp.where(lo, ls[0], ls[1])

    qi = lax.broadcasted_iota(jnp.int32, (blk, 2 * blk), 0)
    ki = lax.broadcasted_iota(jnp.int32, (blk, 2 * blk), 1)
    qi1 = lax.broadcasted_iota(jnp.int32, (blk, blk), 0)
    ki1 = lax.broadcasted_iota(jnp.int32, (blk, blk), 1)
    for pi, (window, dil) in enumerate(A_PATTERNS):
        n_back = window // dil
        nb = seq // dil // blk
        band = (ki >= qi + (blk - n_back)) & (ki <= qi + blk)
        first = (ki1 >= qi1 - n_back) & (ki1 <= qi1)

        group = 4

        def later_block(r, n, pi=pi, dil=dil, band=band):
            q_start = r + dil * blk * n
            attend(pi, q_start, q_start - dil * blk, 2 * blk, dil, band)

        if nb == 1:
            def residues(i, carry, pi=pi, dil=dil, first=first):
                for jj in range(group):
                    r = i * group + jj
                    attend(pi, r, r, blk, dil, first)
                return carry

            lax.fori_loop(0, dil // group, residues, 0)
        else:
            def residue(r, carry, pi=pi, dil=dil, nb=nb, first=first, later_block=later_block):
                attend(pi, r, r, blk, dil, first)
                for n in range(1, group):
                    later_block(r, n)

                def inner(i, c):
                    for jj in range(group):
                        later_block(r, i * group + jj)
                    return c

                return lax.fori_loop(1, nb // group, inner, carry)

            lax.fori_loop(0, dil, residue, 0)

    rows = 256

    def merge(c, carry):
        r0 = pl.multiple_of(c * rows, rows)
        ms = [m_s[p, pl.ds(r0, rows), :] for p in range(len(A_PATTERNS))]
        m_all = functools.reduce(jnp.maximum, ms)
        num = jnp.zeros((rows, LANES), F32)
        den = jnp.zeros((rows, LANES), F32)
        for p in range(len(A_PATTERNS)):
            a = jnp.exp(ms[p] - m_all)
            num = num + a * u_s[p, pl.ds(r0, rows), :]
            den = den + a * l_s[p, pl.ds(r0, rows), :]
        o_ref[0, pl.ds(r0, rows), :] = num / den
        return carry

    lax.fori_loop(0, seq // rows, merge, 0)


def _mixer_a(qa, ka, va):
    B, T, W = qa.shape
    npair = W // LANES
    spec = pl.BlockSpec((1, T, LANES), lambda b, p: (b, 0, p))
    npat = len(A_PATTERNS)
    return pl.pallas_call(
        functools.partial(_mixer_a_kernel, seq=T),
        grid=(B, npair),
        in_specs=[spec, spec, spec],
        out_specs=spec,
        out_shape=jax.ShapeDtypeStruct((B, T, W), F32),
        scratch_shapes=[pltpu.VMEM((T, LANES), F32)] * 3 + [pltpu.VMEM((npat, T, LANES), F32)] * 3,
        compiler_params=pltpu.CompilerParams(dimension_semantics=("parallel", "parallel"),
                                             vmem_limit_bytes=VMEM_LIMIT),
        name="mixer_a",
    )(qa, ka, va)


def _nsa_kernel(q_ref, ks_ref, kw_ref, vst_ref, vwt_ref, kc_ref, vct_ref, gt_ref, ovt_ref, o_ref, *, n_sel_blocks):
    tq, ck = NSA_TQ, NSA_CK
    qi = pl.program_id(2)
    t0 = qi * tq
    nrow = B_GROUP * tq

    q = q_ref[0].astype(F32)
    lane = lax.broadcasted_iota(jnp.int32, (tq, LANES), 1)
    lo = lane < HEAD_DIM
    frames = []
    for r in range(B_GROUP):
        ch = q[:, (r // 2) * LANES:(r // 2 + 1) * LANES]
        if r % 2:
            ch = pltpu.roll(ch, HEAD_DIM, 1)
        frames.append(jnp.where(lo, ch, 0.0))
    qs = jnp.concatenate(frames, axis=0)

    col_i = lax.broadcasted_iota(jnp.int32, (ck, nrow), 1) & (tq - 1)
    row_k = lax.broadcasted_iota(jnp.int32, (ck, nrow), 0)

    nc_pad = kc_ref.shape[2]
    sc = _dot_nt(kc_ref[0, 0], qs.astype(BF16))
    c_end = lax.broadcasted_iota(jnp.int32, (nc_pad, nrow), 0) * CMP_STRIDE + (CMP_LEN - 1)
    cmask = c_end <= t0 + (lax.broadcasted_iota(jnp.int32, (nc_pad, nrow), 1) & (tq - 1))
    sc = jnp.where(cmask, sc, NEG)
    m = jnp.max(sc, axis=0, keepdims=True)
    e = jnp.where(cmask, jnp.exp(sc - m), 0.0)
    den = jnp.sum(e, axis=0, keepdims=True)
    p_cmp = e / jnp.maximum(den, 1e-30)
    o_cmp = _dot(vct_ref[0, 0], p_cmp.astype(BF16))

    psum = p_cmp[:, 0:tq]
    for r in range(1, B_GROUP):
        psum = psum + p_cmp[:, r * tq:(r + 1) * tq]
    p_hi = psum.astype(BF16)
    p_lo = (psum - p_hi.astype(F32)).astype(BF16)
    imp = _dot(ovt_ref[...], p_hi) + _dot(ovt_ref[...], p_lo)
    j = lax.broadcasted_iota(jnp.int32, (n_sel_blocks, tq), 0)
    cur = (t0 + lax.broadcasted_iota(jnp.int32, (n_sel_blocks, tq), 1)) >> SEL_SHIFT
    forced = (j == 0) | (j == cur) | (j == cur - 1)
    low = -3e38
    score = jnp.where(forced, imp + 2.0, jnp.where(j > cur, -1.0, imp))
    sel = jnp.zeros((n_sel_blocks, tq), jnp.bool_)
    j_f = j.astype(F32)
    for _ in range(min(SEL_TOPK, n_sel_blocks)):
        mx = jnp.max(score, axis=0, keepdims=True)
        first = jnp.min(jnp.where(score == mx, j_f, 4.0 * LANES), axis=0, keepdims=True)
        hit = j_f == first
        sel = sel | hit
        score = jnp.where(hit, low, score)
    selneg = jnp.concatenate([jnp.zeros((SEL_LANE0, tq), F32), jnp.where(sel, 0.0, NEG),
                              jnp.zeros((LANES - SEL_LANE0 - n_sel_blocks, tq), F32)], axis=0).T
    q_aug = (qs + jnp.concatenate([selneg] * B_GROUP, axis=0)).astype(BF16)
    q_b = qs.astype(BF16)

    g = pl.program_id(1)
    v_rows = pl.ds(pl.multiple_of(g * HEAD_DIM, HEAD_DIM), HEAD_DIM)

    def attend(qmat, k_ref, vt_ref, parts, carry):
        ks = [k_ref[0, 0, pl.ds(pl.multiple_of(c * ck, ck), ck), :] for c, _ in parts]
        vts = [vt_ref[0, c, v_rows, :] for c, _ in parts]
        s = _dot_nt(ks[0] if len(ks) == 1 else jnp.concatenate(ks, axis=0), qmat)
        pieces = []
        for n, (_, mask) in enumerate(parts):
            piece = s[n * ck:(n + 1) * ck]
            pieces.append(piece if mask is None else jnp.where(mask, piece, NEG))
        s = pieces[0] if len(pieces) == 1 else jnp.concatenate(pieces, axis=0)
        m_new = jnp.max(s, axis=0, keepdims=True)
        if carry is not None:
            m_new = jnp.maximum(carry[0], m_new)
        p = jnp.exp(s - m_new)
        l_new = jnp.sum(p, axis=0, keepdims=True)
        pv = _dot(vts[0] if len(vts) == 1 else jnp.concatenate(vts, axis=1), p.astype(BF16))
        if carry is not None:
            alpha = jnp.exp(carry[0] - m_new)
            l_new = alpha * carry[1] + l_new
            pv = alpha * carry[2] + pv
        return m_new, l_new, pv

    causal = row_k <= col_i

    nwin = WIN // ck
    parts = [(jnp.maximum(qi - nwin, 0), (row_k > col_i) & (qi >= nwin))]
    for back in range(nwin - 1, 0, -1):
        parts.append((jnp.maximum(qi - back, 0), qi >= back))
    parts.append((qi, causal))
    carry = attend(q_b, kw_ref, vwt_ref, parts, None)
    o_win = carry[2] / carry[1]

    odd = (qi & 1) == 1
    carry = attend(q_aug, ks_ref, vst_ref, [(jnp.maximum(qi - 1, 0), odd), (qi, causal)], None)
    carry = lax.fori_loop(
        0, qi // 2,
        lambda c, cr: attend(q_aug, ks_ref, vst_ref, [(2 * c, None), (2 * c + 1, None)], cr), carry)
    o_sel = carry[2] / carry[1]

    gt = gt_ref[0]
    outs = []
    for r in range(B_GROUP):
        cs = slice(r * tq, (r + 1) * tq)
        outs.append(gt[r * N_BRANCH:r * N_BRANCH + 1, :] * o_cmp[:, cs]
                    + gt[r * N_BRANCH + 1:r * N_BRANCH + 2, :] * o_sel[:, cs]
                    + gt[r * N_BRANCH + 2:r * N_BRANCH + 3, :] * o_win[:, cs])
    for c in range(B_GROUP // 2):
        pair = jnp.concatenate([outs[2 * c], outs[2 * c + 1]], axis=0)
        o_ref[0, :, c * LANES:(c + 1) * LANES] = pair.T


def _overlap_t(nc_pad, ns):
    nc = nc_pad - 1
    c0 = jnp.arange(nc_pad) * CMP_STRIDE
    s0 = jnp.arange(ns) * SEL_BLOCK
    ov = jnp.minimum(c0[None, :] + CMP_LEN, s0[:, None] + SEL_BLOCK) - jnp.maximum(c0[None, :], s0[:, None])
    ov = jnp.clip(ov, 0, None).astype(F32) / CMP_LEN
    ov = jnp.where(jnp.arange(nc_pad)[None, :] < nc, ov, 0.0)
    return ov.astype(BF16)


def _nsa(qb, ks, kw, vst, vwt, kc, vct, gt):
    B, T, W = qb.shape
    nq = T // NSA_TQ
    ns = T // SEL_BLOCK
    nc_pad = kc.shape[2]
    gw = W // B_KV_HEADS
    kspec = pl.BlockSpec((1, 1, T, LANES), lambda b, g, i: (b, g, 0, 0))
    assert NSA_TQ == NSA_CK and WIN % NSA_CK == 0 and SEL_LANE0 + ns <= LANES
    vspec = pl.BlockSpec((1, T // NSA_CK, LANES, NSA_CK), lambda b, g, i: (b, 0, 0, 0))
    return pl.pallas_call(
        functools.partial(_nsa_kernel, n_sel_blocks=ns),
        grid=(B, B_KV_HEADS, nq),
        in_specs=[pl.BlockSpec((1, NSA_TQ, gw), lambda b, g, i: (b, i, g)),
                  kspec, kspec, vspec, vspec,
                  pl.BlockSpec((1, 1, nc_pad, LANES), lambda b, g, i: (b, g, 0, 0)),
                  pl.BlockSpec((1, 1, HEAD_DIM, nc_pad), lambda b, g, i: (b, g, 0, 0)),
                  pl.BlockSpec((1, GATE_ROWS, NSA_TQ), lambda b, g, i: (b, g, i)),
                  pl.BlockSpec((ns, nc_pad), lambda b, g, i: (0, 0))],
        out_specs=pl.BlockSpec((1, NSA_TQ, gw), lambda b, g, i: (b, i, g)),
        out_shape=jax.ShapeDtypeStruct((B, T, W), F32),
        compiler_params=pltpu.CompilerParams(dimension_semantics=("parallel", "parallel", "arbitrary"),
                                             vmem_limit_bytes=VMEM_LIMIT),
        name="nsa",
    )(qb, ks, kw, vst, vwt, kc, vct, gt, _overlap_t(nc_pad, ns))


def _post_kernel(x_ref, oa_ref, ob_ref, ga_ref, gb_ref, wo_ref, gm_ref, wu_ref, wd_ref, gf_ref, o_ref, *, final):
    def norm(v, g):
        return v * lax.rsqrt(jnp.mean(v * v, axis=-1, keepdims=True) + EPS) * g

    na = norm(oa_ref[...], ga_ref[...]).astype(BF16)
    nb = norm(ob_ref[...], gb_ref[...]).astype(BF16)
    aw = na.shape[1]
    h_res = x_ref[...] + _dot(na, wo_ref[0:aw, :]) + _dot(nb, wo_ref[aw:, :])
    h = norm(h_res, gm_ref[...]).astype(BF16)
    u = jnp.square(jnp.maximum(_dot(h, wu_ref[...]), 0.0)).astype(BF16)
    acc = h_res + _dot(u, wd_ref[...])
    o_ref[...] = norm(acc, gf_ref[...]) if final else acc


def _post(x, oa, ob, g_a, g_b, w_out, g_mlp, w_up, w_down, g_final, *, final, tm=256):
    B, T, D = x.shape
    n = B * T
    dff = w_up.shape[1]
    tok = lambda w: pl.BlockSpec((tm, w), lambda i: (i, 0))
    const = lambda shape: pl.BlockSpec(shape, lambda i: (0, 0), pipeline_mode=pl.Buffered(1))
    out = pl.pallas_call(
        functools.partial(_post_kernel, final=final),
        grid=(n // tm,),
        in_specs=[tok(D), tok(A_WIDTH), tok(B_WIDTH), const((1, A_WIDTH)), const((1, B_WIDTH)),
                  const((A_WIDTH + B_WIDTH, D)), const((1, D)), const((D, dff)), const((dff, D)), const((1, D))],
        out_specs=tok(D),
        out_shape=jax.ShapeDtypeStruct((n, D), F32),
        compiler_params=pltpu.CompilerParams(dimension_semantics=("parallel",), vmem_limit_bytes=VMEM_LIMIT),
        name="post",
    )(x.reshape(n, D), oa.reshape(n, A_WIDTH), ob.reshape(n, B_WIDTH), g_a.reshape(1, -1), g_b.reshape(1, -1),
      w_out.astype(BF16), g_mlp.reshape(1, D), w_up.astype(BF16), w_down.astype(BF16), g_final.reshape(1, D))
    return out.reshape(B, T, D)


def kernel(x, norm_mix, w_in, cmp_pe_k, cmp_w1_k, cmp_w2_k, cmp_pe_v, cmp_w1_v, cmp_w2_v, g_out_a, g_out_b,
           w_out, norm_mlp, w_up, w_down, norm_final):
    B, T, D = x.shape
    depth = w_in.shape[0]
    tables = _rope_tables(jnp.arange(T))
    cmp_tables = _rope_tables(jnp.arange(T // CMP_STRIDE) * CMP_STRIDE + CMP_LEN - 1)
    h_res = x
    for l in range(depth):
        qa, ka, va, qb, ks, kw, vst, vwt, kcvc, gt = _in_proj(h_res, norm_mix[l], w_in[l], tables)
        kc, vct = _compress(kcvc, cmp_pe_k[l], cmp_w1_k[l], cmp_w2_k[l], cmp_pe_v[l], cmp_w1_v[l], cmp_w2_v[l],
                            cmp_tables)
        oa = _mixer_a(qa, ka, va)
        ob = _nsa(qb, ks, kw, vst, vwt, kc, vct, gt)
        h_res = _post(h_res, oa, ob, g_out_a[l], g_out_b[l], w_out[l], norm_mlp[l], w_up[l], w_down[l], norm_final,
                      final=(l == depth - 1))
    return h_res
```

```python
import functools

import jax
import jax.numpy as jnp
import numpy as np
from jax import lax
from jax.experimental import pallas as pl
from jax.experimental.pallas import tpu as pltpu

F32 = jnp.float32
BF16 = jnp.bfloat16

HEAD_DIM = 64
ROT_DIM = HEAD_DIM // 4
ROPE_THETA = 500000.0
EPS = 1e-6
NEG = -1e30
Q_SCALE = HEAD_DIM ** -0.5 * 1.4426950408889634
LANES = 128

A_HEADS = 8
A_PATTERNS = ((128, 1), (512, 4), (2048, 16))
A_BLOCK = 128

B_HEADS = 8
B_KV_HEADS = 2
B_GROUP = B_HEADS // B_KV_HEADS
CMP_LEN = 32
CMP_STRIDE = 16
CMP_HIDDEN = 256
SEL_BLOCK = 64
SEL_SHIFT = 6
SEL_TOPK = 8
WIN = 512
N_BRANCH = 3

A_WIDTH = A_HEADS * HEAD_DIM
B_WIDTH = B_HEADS * HEAD_DIM
KV_WIDTH = B_KV_HEADS * HEAD_DIM

NSA_TQ = 256
NSA_CK = 256
GATE_ROWS = 16
SEL_LANE0 = HEAD_DIM

VMEM_LIMIT = 56 * 1024 * 1024


def _dot(a, b):
    return jnp.dot(a, b, preferred_element_type=F32)


def _dot_nt(a, b):
    return lax.dot_general(a, b, (((1,), (1,)), ((), ())), preferred_element_type=F32)


def _rope_rows(y, cos, sin_a, sin_b):
    outs = []
    for c in range(y.shape[1] // LANES):
        yc = y[:, c * LANES:(c + 1) * LANES]
        outs.append(yc * cos + pltpu.roll(yc, LANES - ROT_DIM // 2, 1) * sin_a
                    + pltpu.roll(yc, ROT_DIM // 2, 1) * sin_b)
    return outs[0] if len(outs) == 1 else jnp.concatenate(outs, axis=1)


def _in_proj_kernel(x_ref, g_ref, wq_ref, wt_ref, cos_ref, sa_ref, sb_ref,
                    qa_ref, ka_ref, va_ref, qb_ref, ks_ref, kw_ref, vst_ref, vwt_ref, kcvc_ref, gt_ref,
                    *, tm, seq):
    tt = pl.program_id(1)
    x = x_ref[...]
    ms = jnp.mean(x * x, axis=-1, keepdims=True)
    h = (x * lax.rsqrt(ms + EPS) * g_ref[...]).astype(BF16)
    cos, sa, sb = cos_ref[...], sa_ref[...], sb_ref[...]
    scale = Q_SCALE

    def proj(c0, c1):
        return _dot(h, wq_ref[:, c0:c1])

    o = 0
    qa_ref[...] = (_rope_rows(proj(o, o + A_WIDTH), cos, sa, sb) * scale).astype(BF16)
    o += A_WIDTH
    ka_ref[...] = _rope_rows(proj(o, o + A_WIDTH), cos, sa, sb).astype(BF16)
    o += A_WIDTH
    va_ref[...] = proj(o, o + A_WIDTH).astype(BF16)
    o += A_WIDTH
    qb_ref[...] = (_rope_rows(proj(o, o + B_WIDTH), cos, sa, sb) * scale).astype(BF16)
    o += B_WIDTH
    kcvc_ref[...] = proj(o, o + 2 * KV_WIDTH)
    o += 2 * KV_WIDTH
    ksw = _rope_rows(proj(o, o + 2 * KV_WIDTH), cos, sa, sb)

    lane = lax.broadcasted_iota(jnp.int32, (tm, LANES), 1)
    row = lax.broadcasted_iota(jnp.int32, (tm, LANES), 0)
    lo = lane < HEAD_DIM
    blk = (tt * tm + row) >> SEL_SHIFT
    onehot = jnp.where(lane - SEL_LANE0 == blk, 1.0, 0.0)
    for kind, ref in ((0, ks_ref), (1, kw_ref)):
        kk = ksw[:, kind * LANES:(kind + 1) * LANES]
        tail = onehot if kind == 0 else 0.0
        ref[0, 0] = jnp.where(lo, kk, tail).astype(BF16)
        ref[0, 1] = jnp.where(lo, pltpu.roll(kk, HEAD_DIM, 1), tail).astype(BF16)

    tr = _dot_nt(wt_ref[...], h)
    for c in range(tm // NSA_CK):
        vst_ref[0, c] = tr[0:LANES, c * NSA_CK:(c + 1) * NSA_CK].astype(BF16)
        vwt_ref[0, c] = tr[LANES:2 * LANES, c * NSA_CK:(c + 1) * NSA_CK].astype(BF16)
    gt_ref[0] = jax.nn.sigmoid(tr[2 * LANES:2 * LANES + 2 * GATE_ROWS, :])


def _rope_tables(pos):
    half = ROT_DIM // 2
    inv = ROPE_THETA ** (-jnp.arange(0, ROT_DIM, 2, dtype=F32) / ROT_DIM)
    ang = pos.astype(F32)[:, None] * inv[None, :]
    cos, sin = jnp.cos(ang), jnp.sin(ang)
    n = pos.shape[0]
    ones = jnp.ones((n, HEAD_DIM - ROT_DIM), F32)
    zeros = jnp.zeros((n, HEAD_DIM - ROT_DIM), F32)
    zh = jnp.zeros((n, half), F32)
    c_head = jnp.concatenate([cos, cos, ones], axis=1)
    a_head = jnp.concatenate([-sin, zh, zeros], axis=1)
    b_head = jnp.concatenate([zh, sin, zeros], axis=1)
    rep = LANES // HEAD_DIM
    return jnp.tile(c_head, (1, rep)), jnp.tile(a_head, (1, rep)), jnp.tile(b_head, (1, rep))


def _in_proj(x, norm_g, w_in, tables, *, tm=512):
    B, T, D = x.shape
    nt = T // tm
    offs = [0]
    for n in (A_WIDTH, A_WIDTH, A_WIDTH, B_WIDTH, KV_WIDTH, KV_WIDTH, KV_WIDTH, KV_WIDTH, KV_WIDTH, KV_WIDTH,
              B_HEADS * N_BRANCH):
        offs.append(offs[-1] + n)
    col = lambda i: w_in[:, offs[i]:offs[i + 1]]
    wq = jnp.concatenate([col(0), col(1), col(2), col(3), col(4), col(5), col(6), col(8)], axis=1).astype(BF16)
    gl = col(10)
    per_g = B_GROUP * N_BRANCH
    gpad = jnp.zeros((D, GATE_ROWS - per_g), w_in.dtype)
    wt = jnp.concatenate([col(7), col(9), gl[:, :per_g], gpad, gl[:, per_g:], gpad], axis=1).T.astype(BF16)
    cos, sa, sb = tables
    nq = wq.shape[1]
    nr = wt.shape[0]
    tok = lambda w: pl.BlockSpec((None, tm, w), lambda b, t: (b, t, 0))
    const = lambda shape: pl.BlockSpec(shape, lambda b, t: (0,) * len(shape))
    tab = pl.BlockSpec((tm, LANES), lambda b, t: (t, 0))
    out_shapes = (
        jax.ShapeDtypeStruct((B, T, A_WIDTH), BF16),
        jax.ShapeDtypeStruct((B, T, A_WIDTH), BF16),
        jax.ShapeDtypeStruct((B, T, A_WIDTH), BF16),
        jax.ShapeDtypeStruct((B, T, B_WIDTH), BF16),
        jax.ShapeDtypeStruct((B, B_KV_HEADS, T, LANES), BF16),
        jax.ShapeDtypeStruct((B, B_KV_HEADS, T, LANES), BF16),
        jax.ShapeDtypeStruct((B, T // NSA_CK, LANES, NSA_CK), BF16),
        jax.ShapeDtypeStruct((B, T // NSA_CK, LANES, NSA_CK), BF16),
        jax.ShapeDtypeStruct((B, T, 2 * KV_WIDTH), F32),
        jax.ShapeDtypeStruct((B, 2 * GATE_ROWS, T), F32),
    )
    frame = pl.BlockSpec((1, B_KV_HEADS, tm, LANES), lambda b, t: (b, 0, t, 0))
    vt = pl.BlockSpec((1, tm // NSA_CK, LANES, NSA_CK), lambda b, t: (b, t, 0, 0))
    out_specs = (tok(A_WIDTH), tok(A_WIDTH), tok(A_WIDTH), tok(B_WIDTH), frame, frame, vt, vt,
                 tok(2 * KV_WIDTH), pl.BlockSpec((1, 2 * GATE_ROWS, tm), lambda b, t: (b, 0, t)))
    return pl.pallas_call(
        functools.partial(_in_proj_kernel, tm=tm, seq=T),
        grid=(B, nt),
        in_specs=[tok(D), const((1, D)), const((D, nq)), const((nr, D)), tab, tab, tab],
        out_specs=out_specs,
        out_shape=out_shapes,
        compiler_params=pltpu.CompilerParams(dimension_semantics=("parallel", "parallel"),
                                             vmem_limit_bytes=VMEM_LIMIT),
        name="in_proj",
    )(x, norm_g.reshape(1, D), wq, wt, cos, sa, sb)


def _compress_kernel(ak_ref, av_ref, w1k_ref, w1v_ref, pek_ref, pev_ref, w2k_ref, w2vt_ref, cos_ref, sa_ref, sb_ref,
                     kc_ref, vct_ref, *, nc_pad):
    half = CMP_LEN // 2
    hid_w = B_KV_HEADS * CMP_HIDDEN

    def hidden(a_ref, w1_ref, pe_ref):
        acc_u = jnp.zeros((nc_pad, hid_w), F32)
        acc_v = jnp.zeros((nc_pad, hid_w), F32)
        for p in range(half):
            ap = a_ref[0, pl.ds(p, nc_pad, stride=CMP_STRIDE), :]
            acc_u = acc_u + _dot((ap + pe_ref[p:p + 1, :]).astype(BF16), w1_ref[p])
            acc_v = acc_v + _dot((ap + pe_ref[half + p:half + p + 1, :]).astype(BF16), w1_ref[half + p])
        return jax.nn.gelu(acc_u + pltpu.roll(acc_v, nc_pad - 1, 0))

    hk = hidden(ak_ref, w1k_ref, pek_ref).astype(BF16)
    hv = hidden(av_ref, w1v_ref, pev_ref).astype(BF16)
    for g in range(B_KV_HEADS):
        hg = hk[:, g * CMP_HIDDEN:(g + 1) * CMP_HIDDEN]
        kc = _dot(hg, w2k_ref[...])
        kc_ref[0, g] = _rope_rows(kc, cos_ref[...], sa_ref[...], sb_ref[...]).astype(BF16)
        vg = hv[:, g * CMP_HIDDEN:(g + 1) * CMP_HIDDEN]
        vct_ref[0, g] = _dot_nt(w2vt_ref[...], vg).astype(BF16)


def _block_diag_w1(w1):
    w = w1.reshape(CMP_LEN, HEAD_DIM, CMP_HIDDEN)
    z = jnp.zeros_like(w)
    top = jnp.concatenate([w, z], axis=2)
    bot = jnp.concatenate([z, w], axis=2)
    return jnp.concatenate([top, bot], axis=1).astype(BF16)


def _compress(kcvc, pe_k, w1_k, w2_k, pe_v, w1_v, w2_v, cmp_tables):
    B, T, _ = kcvc.shape
    nc_pad = T // CMP_STRIDE
    w2k = jnp.concatenate([w2_k, jnp.zeros_like(w2_k)], axis=1).astype(BF16)
    w2vt = w2_v.T.astype(BF16)
    pek = jnp.tile(pe_k, (1, B_KV_HEADS))
    pev = jnp.tile(pe_v, (1, B_KV_HEADS))
    const = lambda shape: pl.BlockSpec(shape, lambda b: (0,) * len(shape))
    cos, sa, sb = cmp_tables
    return pl.pallas_call(
        functools.partial(_compress_kernel, nc_pad=nc_pad),
        grid=(B,),
        in_specs=[pl.BlockSpec((1, T, KV_WIDTH), lambda b: (b, 0, 0)), pl.BlockSpec((1, T, KV_WIDTH), lambda b: (b, 0, 1)),
                  const((CMP_LEN, LANES, B_KV_HEADS * CMP_HIDDEN)), const((CMP_LEN, LANES, B_KV_HEADS * CMP_HIDDEN)),
                  const((CMP_LEN, LANES)), const((CMP_LEN, LANES)),
                  const((CMP_HIDDEN, LANES)), const((HEAD_DIM, CMP_HIDDEN)),
                  const((nc_pad, LANES)), const((nc_pad, LANES)), const((nc_pad, LANES))],
        out_specs=(pl.BlockSpec((1, B_KV_HEADS, nc_pad, LANES), lambda b: (b, 0, 0, 0)),
                   pl.BlockSpec((1, B_KV_HEADS, HEAD_DIM, nc_pad), lambda b: (b, 0, 0, 0))),
        out_shape=(jax.ShapeDtypeStruct((B, B_KV_HEADS, nc_pad, LANES), BF16),
                   jax.ShapeDtypeStruct((B, B_KV_HEADS, HEAD_DIM, nc_pad), BF16)),
        compiler_params=pltpu.CompilerParams(dimension_semantics=("parallel",), vmem_limit_bytes=VMEM_LIMIT),
        name="compress",
    )(kcvc, kcvc, _block_diag_w1(w1_k), _block_diag_w1(w1_v), pek, pev, w2k, w2vt, cos, sa, sb)


def _mixer_a_kernel(q_ref, k_ref, v_ref, bias_ref, o_ref, nat, qd0, qd1, kd, vd0, vd1, u_s, m_s, l_s, *, seq):
    blk = A_BLOCK
    nres = seq // blk
    lane = lax.broadcasted_iota(jnp.int32, (blk, LANES), 1)
    lo = lane < HEAD_DIM
    for src, dsts in ((q_ref, (qd0, qd1)), (k_ref, (kd,)), (v_ref, (vd0, vd1))):
        nat[...] = src[0].astype(F32)
        for r in range(nres):
            rows = nat[pl.ds(r, blk, stride=nres), :]
            if src is k_ref:
                kd[pl.ds(r * blk, blk), :] = rows
            else:
                fill = 0.0 if src is q_ref else 1.0
                dsts[0][pl.ds(r * blk, blk), :] = jnp.where(lo, rows, fill)
                dsts[1][pl.ds(r * blk, blk), :] = jnp.where(lo, fill, rows)

    def pieces(dil, rd, row_off, rows):
        return [pl.ds(pl.multiple_of((rd + dil * jj) * blk + row_off, 8), rows) for jj in range(nres // dil)]

    def gather(ref, idx):
        parts = [ref[i, :] for i in idx]
        return parts[0] if len(parts) == 1 else jnp.concatenate(parts, axis=0)

    def attend(pi, dil, blocks):
        pr = blk // (nres // dil)
        q_idxs, vbs, scores = [], [], []
        for rd, n, first in blocks:
            q_idx = pieces(dil, rd, n * pr, pr)
            k_idx = q_idx if first else pieces(dil, rd, (n - 1) * pr, 2 * pr)
            bias = bias_ref[pi, :, 0:blk] if first else bias_ref[pi, :, blk:3 * blk]
            kb = gather(kd, k_idx).astype(BF16)
            q_idxs.append(q_idx)
            for qd, vd in ((qd0, vd0), (qd1, vd1)):
                vbs.append(gather(vd, k_idx).astype(BF16))
                scores.append(_dot_nt(gather(qd, q_idx).astype(BF16), kb) + bias)
        es, ms = [], []
        for s in scores:
            m = jnp.max(s, axis=-1, keepdims=True)
            ms.append(m)
            es.append(jnp.exp2(s - m).astype(BF16))
        pvs = [_dot(e, vb) for e, vb in zip(es, vbs)]
        for b, q_idx in enumerate(q_idxs):
            u = jnp.where(lo, pvs[2 * b], pvs[2 * b + 1])
            l_swapped = jnp.where(lo, pvs[2 * b + 1], pvs[2 * b])
            m = jnp.where(lo, ms[2 * b], ms[2 * b + 1])
            for jj, idx in enumerate(q_idx):
                u_s[pi, idx, :] = u[jj * pr:(jj + 1) * pr]
                m_s[pi, idx, :] = m[jj * pr:(jj + 1) * pr]
                l_s[pi, idx, :] = l_swapped[jj * pr:(jj + 1) * pr]

    group = 4
    for pi, (_, dil) in enumerate(A_PATTERNS):
        nb = seq // dil // blk
        if nb == 1:
            def residues(i, carry, pi=pi, dil=dil):
                attend(pi, dil, [(i * group + jj, 0, True) for jj in range(group)])
                return carry

            lax.fori_loop(0, dil // group, residues, 0)
        else:
            def residue(rd, carry, pi=pi, dil=dil, nb=nb):
                attend(pi, dil, [(rd, 0, True)] + [(rd, n, False) for n in range(1, group)])

                def inner(i, c):
                    attend(pi, dil, [(rd, i * group + jj, False) for jj in range(group)])
                    return c

                return lax.fori_loop(1, nb // group, inner, carry)

            lax.fori_loop(0, dil, residue, 0)

    rows = 2 * blk

    def merge(c, carry):
        r0 = pl.multiple_of(c * rows, rows)
        ms = [m_s[p, pl.ds(r0, rows), :] for p in range(len(A_PATTERNS))]
        m_all = functools.reduce(jnp.maximum, ms)
        num = jnp.zeros((rows, LANES), F32)
        den = jnp.zeros((rows, LANES), F32)
        for p in range(len(A_PATTERNS)):
            a = jnp.exp2(ms[p] - m_all)
            num = num + a * u_s[p, pl.ds(r0, rows), :]
            den = den + a * pltpu.roll(l_s[p, pl.ds(r0, rows), :], HEAD_DIM, 1)
        out = num / den
        for jj in range(rows // blk):
            o_ref[0, pl.ds(c * (rows // blk) + jj, blk, stride=nres), :] = out[jj * blk:(jj + 1) * blk]
        return carry

    lax.fori_loop(0, seq // rows, merge, 0)


def _mixer_a_bias(nres):
    blk = A_BLOCK

    def sub_pos(i, fold, rows):
        return fold * (i % rows) + i // rows

    out = np.zeros((len(A_PATTERNS), blk, 3 * blk), np.float32)
    for pi, (window, dil) in enumerate(A_PATTERNS):
        n_back = window // dil
        fold = nres // dil
        pr = blk // fold
        sq = sub_pos(np.arange(blk), fold, pr)[:, None]
        d_first = sq - sub_pos(np.arange(blk), fold, pr)[None, :]
        d_band = sq + blk - sub_pos(np.arange(2 * blk), fold, 2 * pr)[None, :]
        dist = np.concatenate([d_first, d_band], axis=1)
        out[pi] = np.where((dist >= 0) & (dist <= n_back), 0.0, NEG)
    return jnp.asarray(out)


def _mixer_a(qa, ka, va):
    B, T, W = qa.shape
    npair = W // LANES
    spec = pl.BlockSpec((1, T, LANES), lambda b, p: (b, 0, p))
    npat = len(A_PATTERNS)
    nres = T // A_BLOCK
    assert all(nres % d == 0 and T % (d * A_BLOCK) == 0 for _, d in A_PATTERNS) and max(d for _, d in A_PATTERNS) == nres
    return pl.pallas_call(
        functools.partial(_mixer_a_kernel, seq=T),
        grid=(B, npair),
        in_specs=[spec, spec, spec, pl.BlockSpec((npat, A_BLOCK, 3 * A_BLOCK), lambda b, p: (0, 0, 0))],
        out_specs=spec,
        out_shape=jax.ShapeDtypeStruct((B, T, W), F32),
        scratch_shapes=[pltpu.VMEM((T, LANES), F32)] * 6 + [pltpu.VMEM((npat, T, LANES), F32)] * 3,
        compiler_params=pltpu.CompilerParams(dimension_semantics=("parallel", "parallel"),
                                             vmem_limit_bytes=VMEM_LIMIT),
        name="mixer_a",
    )(qa, ka, va, _mixer_a_bias(nres))


def _nsa_kernel(q_ref, ks_ref, kw_ref, vst_ref, vwt_ref, kc_ref, vct_ref, gt_ref, ovt_ref, o_ref, *, n_sel_blocks):
    tq, ck = NSA_TQ, NSA_CK
    qi = pl.program_id(2)
    t0 = qi * tq
    nrow = B_GROUP * tq

    q = q_ref[0].astype(F32)
    lane = lax.broadcasted_iota(jnp.int32, (tq, LANES), 1)
    lo = lane < HEAD_DIM
    frames = []
    for r in range(B_GROUP):
        ch = q[:, (r // 2) * LANES:(r // 2 + 1) * LANES]
        if r % 2:
            ch = pltpu.roll(ch, HEAD_DIM, 1)
        frames.append(jnp.where(lo, ch, 0.0))
    qs = jnp.concatenate(frames, axis=0)

    col_i = lax.broadcasted_iota(jnp.int32, (ck, nrow), 1) & (tq - 1)
    row_k = lax.broadcasted_iota(jnp.int32, (ck, nrow), 0)
    causal = row_k <= col_i
    q_b = qs.astype(BF16)
    g = pl.program_id(1)
    v_rows = pl.ds(pl.multiple_of(g * HEAD_DIM, HEAD_DIM), HEAD_DIM)

    def scores(qmat, k_ref, parts):
        ks = [k_ref[0, 0, pl.ds(pl.multiple_of(c * ck, ck), ck), :] for c, _ in parts]
        s = _dot_nt(ks[0] if len(ks) == 1 else jnp.concatenate(ks, axis=0), qmat)
        pieces = []
        for n, (_, mask) in enumerate(parts):
            piece = s[n * ck:(n + 1) * ck]
            pieces.append(piece if mask is None else jnp.where(mask, piece, NEG))
        return pieces[0] if len(pieces) == 1 else jnp.concatenate(pieces, axis=0)

    def softmax_pv(s, vt_ref, parts, carry):
        vts = [vt_ref[0, c, v_rows, :] for c, _ in parts]
        m_new = jnp.max(s, axis=0, keepdims=True)
        if carry is not None:
            m_new = jnp.maximum(carry[0], m_new)
        p = jnp.exp2(s - m_new)
        l_new = jnp.sum(p, axis=0, keepdims=True)
        pv = _dot(vts[0] if len(vts) == 1 else jnp.concatenate(vts, axis=1), p.astype(BF16))
        if carry is not None:
            alpha = jnp.exp2(carry[0] - m_new)
            l_new = alpha * carry[1] + l_new
            pv = alpha * carry[2] + pv
        return m_new, l_new, pv

    nwin = WIN // ck
    win_parts = [(jnp.maximum(qi - nwin, 0), (row_k > col_i) & (qi >= nwin))]
    for back in range(nwin - 1, 0, -1):
        win_parts.append((jnp.maximum(qi - back, 0), qi >= back))
    win_parts.append((qi, causal))
    s_win = scores(q_b, kw_ref, win_parts)

    nc_pad = kc_ref.shape[2]
    sc = _dot_nt(kc_ref[0, 0], q_b)
    c_end = lax.broadcasted_iota(jnp.int32, (nc_pad, nrow), 0) * CMP_STRIDE + (CMP_LEN - 1)
    cmask = c_end <= t0 + (lax.broadcasted_iota(jnp.int32, (nc_pad, nrow), 1) & (tq - 1))
    sc = jnp.where(cmask, sc, NEG)
    m = jnp.max(sc, axis=0, keepdims=True)
    e = jnp.where(cmask, jnp.exp2(sc - m), 0.0)
    den = jnp.sum(e, axis=0, keepdims=True)
    p_cmp = e / jnp.maximum(den, 1e-30)
    o_cmp = _dot(vct_ref[0, 0], p_cmp.astype(BF16))

    psum = p_cmp[:, 0:tq]
    for r in range(1, B_GROUP):
        psum = psum + p_cmp[:, r * tq:(r + 1) * tq]
    p_hi = psum.astype(BF16)
    p_lo = (psum - p_hi.astype(F32)).astype(BF16)
    imp = _dot(ovt_ref[...], p_hi) + _dot(ovt_ref[...], p_lo)
    j = lax.broadcasted_iota(jnp.int32, (n_sel_blocks, tq), 0)
    cur = (t0 + lax.broadcasted_iota(jnp.int32, (n_sel_blocks, tq), 1)) >> SEL_SHIFT
    forced = (j == 0) | (j == cur) | (j == cur - 1)
    low = -3e38
    score = jnp.where(forced, imp + 2.0, jnp.where(j > cur, -1.0, imp))
    sel = jnp.zeros((n_sel_blocks, tq), jnp.bool_)
    j_f = j.astype(F32)
    for _ in range(min(SEL_TOPK, n_sel_blocks)):
        mx = jnp.max(score, axis=0, keepdims=True)
        first = jnp.min(jnp.where(score == mx, j_f, 4.0 * LANES), axis=0, keepdims=True)
        hit = j_f == first
        sel = sel | hit
        score = jnp.where(hit, low, score)
    selneg = jnp.concatenate([jnp.zeros((SEL_LANE0, tq), F32), jnp.where(sel, 0.0, NEG),
                              jnp.zeros((LANES - SEL_LANE0 - n_sel_blocks, tq), F32)], axis=0).T
    q_aug = (qs + jnp.concatenate([selneg] * B_GROUP, axis=0)).astype(BF16)

    odd = (qi & 1) == 1
    diag_parts = [(jnp.maximum(qi - 1, 0), odd), (qi, causal)]
    s_sel = scores(q_aug, ks_ref, diag_parts)
    carry = softmax_pv(s_win, vwt_ref, win_parts, None)
    o_win = carry[2] / carry[1]
    carry = softmax_pv(s_sel, vst_ref, diag_parts, None)

    def earlier(c, cr):
        parts = [(2 * c, None), (2 * c + 1, None)]
        return softmax_pv(scores(q_aug, ks_ref, parts), vst_ref, parts, cr)

    carry = lax.fori_loop(0, qi // 2, earlier, carry)
    o_sel = carry[2] / carry[1]

    gt = gt_ref[0]
    outs = []
    for r in range(B_GROUP):
        cs = slice(r * tq, (r + 1) * tq)
        outs.append(gt[r * N_BRANCH:r * N_BRANCH + 1, :] * o_cmp[:, cs]
                    + gt[r * N_BRANCH + 1:r * N_BRANCH + 2, :] * o_sel[:, cs]
                    + gt[r * N_BRANCH + 2:r * N_BRANCH + 3, :] * o_win[:, cs])
    for c in range(B_GROUP // 2):
        pair = jnp.concatenate([outs[2 * c], outs[2 * c + 1]], axis=0)
        o_ref[0, :, c * LANES:(c + 1) * LANES] = pair.T


def _overlap_t(nc_pad, ns):
    nc = nc_pad - 1
    c0 = jnp.arange(nc_pad) * CMP_STRIDE
    s0 = jnp.arange(ns) * SEL_BLOCK
    ov = jnp.minimum(c0[None, :] + CMP_LEN, s0[:, None] + SEL_BLOCK) - jnp.maximum(c0[None, :], s0[:, None])
    ov = jnp.clip(ov, 0, None).astype(F32) / CMP_LEN
    ov = jnp.where(jnp.arange(nc_pad)[None, :] < nc, ov, 0.0)
    return ov.astype(BF16)


def _nsa(qb, ks, kw, vst, vwt, kc, vct, gt):
    B, T, W = qb.shape
    nq = T // NSA_TQ
    ns = T // SEL_BLOCK
    nc_pad = kc.shape[2]
    gw = W // B_KV_HEADS
    kspec = pl.BlockSpec((1, 1, T, LANES), lambda b, g, i: (b, g, 0, 0))
    assert NSA_TQ == NSA_CK and WIN % NSA_CK == 0 and SEL_LANE0 + ns <= LANES
    vspec = pl.BlockSpec((1, T // NSA_CK, LANES, NSA_CK), lambda b, g, i: (b, 0, 0, 0))
    return pl.pallas_call(
        functools.partial(_nsa_kernel, n_sel_blocks=ns),
        grid=(B, B_KV_HEADS, nq),
        in_specs=[pl.BlockSpec((1, NSA_TQ, gw), lambda b, g, i: (b, i, g)),
                  kspec, kspec, vspec, vspec,
                  pl.BlockSpec((1, 1, nc_pad, LANES), lambda b, g, i: (b, g, 0, 0)),
                  pl.BlockSpec((1, 1, HEAD_DIM, nc_pad), lambda b, g, i: (b, g, 0, 0)),
                  pl.BlockSpec((1, GATE_ROWS, NSA_TQ), lambda b, g, i: (b, g, i)),
                  pl.BlockSpec((ns, nc_pad), lambda b, g, i: (0, 0))],
        out_specs=pl.BlockSpec((1, NSA_TQ, gw), lambda b, g, i: (b, i, g)),
        out_shape=jax.ShapeDtypeStruct((B, T, W), F32),
        compiler_params=pltpu.CompilerParams(dimension_semantics=("parallel", "parallel", "arbitrary"),
                                             vmem_limit_bytes=VMEM_LIMIT),
        name="nsa",
    )(qb, ks, kw, vst, vwt, kc, vct, gt, _overlap_t(nc_pad, ns))


def _post_kernel(x_ref, oa_ref, ob_ref, ga_ref, gb_ref, wo_ref, gm_ref, wu_ref, wd_ref, gf_ref, o_ref, *, final):
    def norm(v, g):
        return v * lax.rsqrt(jnp.mean(v * v, axis=-1, keepdims=True) + EPS) * g

    na = norm(oa_ref[...], ga_ref[...]).astype(BF16)
    nb = norm(ob_ref[...], gb_ref[...]).astype(BF16)
    aw = na.shape[1]
    h_res = x_ref[...] + _dot(na, wo_ref[0:aw, :]) + _dot(nb, wo_ref[aw:, :])
    h = norm(h_res, gm_ref[...]).astype(BF16)
    u = jnp.square(jnp.maximum(_dot(h, wu_ref[...]), 0.0)).astype(BF16)
    acc = h_res + _dot(u, wd_ref[...])
    o_ref[...] = norm(acc, gf_ref[...]) if final else acc


def _post(x, oa, ob, g_a, g_b, w_out, g_mlp, w_up, w_down, g_final, *, final, tm=256):
    B, T, D = x.shape
    n = B * T
    dff = w_up.shape[1]
    tok = lambda w: pl.BlockSpec((tm, w), lambda i: (i, 0))
    const = lambda shape: pl.BlockSpec(shape, lambda i: (0, 0), pipeline_mode=pl.Buffered(1))
    out = pl.pallas_call(
        functools.partial(_post_kernel, final=final),
        grid=(n // tm,),
        in_specs=[tok(D), tok(A_WIDTH), tok(B_WIDTH), const((1, A_WIDTH)), const((1, B_WIDTH)),
                  const((A_WIDTH + B_WIDTH, D)), const((1, D)), const((D, dff)), const((dff, D)), const((1, D))],
        out_specs=tok(D),
        out_shape=jax.ShapeDtypeStruct((n, D), F32),
        compiler_params=pltpu.CompilerParams(dimension_semantics=("parallel",), vmem_limit_bytes=VMEM_LIMIT),
        name="post",
    )(x.reshape(n, D), oa.reshape(n, A_WIDTH), ob.reshape(n, B_WIDTH), g_a.reshape(1, -1), g_b.reshape(1, -1),
      w_out.astype(BF16), g_mlp.reshape(1, D), w_up.astype(BF16), w_down.astype(BF16), g_final.reshape(1, D))
    return out.reshape(B, T, D)


def kernel(x, norm_mix, w_in, cmp_pe_k, cmp_w1_k, cmp_w2_k, cmp_pe_v, cmp_w1_v, cmp_w2_v, g_out_a, g_out_b,
           w_out, norm_mlp, w_up, w_down, norm_final):
    B, T, D = x.shape
    depth = w_in.shape[0]
    tables = _rope_tables(jnp.arange(T))
    cmp_tables = _rope_tables(jnp.arange(T // CMP_STRIDE) * CMP_STRIDE + CMP_LEN - 1)
    h_res = x
    for l in range(depth):
        qa, ka, va, qb, ks, kw, vst, vwt, kcvc, gt = _in_proj(h_res, norm_mix[l], w_in[l], tables)
        kc, vct = _compress(kcvc, cmp_pe_k[l], cmp_w1_k[l], cmp_w2_k[l], cmp_pe_v[l], cmp_w1_v[l], cmp_w2_v[l],
                            cmp_tables)
        oa = _mixer_a(qa, ka, va)
        ob = _nsa(qb, ks, kw, vst, vwt, kc, vct, gt)
        h_res = _post(h_res, oa, ob, g_out_a[l], g_out_b[l], w_out[l], norm_mlp[l], w_up[l], w_down[l], norm_final,
                      final=(l == depth - 1))
    return h_res
```

```python
import functools

import jax
import jax.numpy as jnp
import numpy as np
from jax import lax
from jax.experimental import pallas as pl
from jax.experimental.pallas import tpu as pltpu

F32 = jnp.float32
BF16 = jnp.bfloat16

HEAD_DIM = 64
ROT_DIM = HEAD_DIM // 4
ROPE_THETA = 500000.0
EPS = 1e-6
NEG = -1e30
Q_SCALE = HEAD_DIM ** -0.5 * 1.4426950408889634
LANES = 128

A_HEADS = 8
A_PATTERNS = ((128, 1), (512, 4), (2048, 16))
A_BLOCK = 128

B_HEADS = 8
B_KV_HEADS = 2
B_GROUP = B_HEADS // B_KV_HEADS
CMP_LEN = 32
CMP_STRIDE = 16
CMP_HIDDEN = 256
SEL_BLOCK = 64
SEL_SHIFT = 6
SEL_TOPK = 8
WIN = 512
N_BRANCH = 3

A_WIDTH = A_HEADS * HEAD_DIM
B_WIDTH = B_HEADS * HEAD_DIM
KV_WIDTH = B_KV_HEADS * HEAD_DIM

NSA_TQ = 256
NSA_CK = 256
NSA_VROWS = 80
GATE_ROWS = 16
SEL_LANE0 = HEAD_DIM

VMEM_LIMIT = 56 * 1024 * 1024


def _dot(a, b):
    return jnp.dot(a, b, preferred_element_type=F32)


def _dot_nt(a, b):
    return lax.dot_general(a, b, (((1,), (1,)), ((), ())), preferred_element_type=F32)


def _rope_rows(y, cos, sin_a, sin_b):
    outs = []
    for c in range(y.shape[1] // LANES):
        yc = y[:, c * LANES:(c + 1) * LANES]
        outs.append(yc * cos + pltpu.roll(yc, LANES - ROT_DIM // 2, 1) * sin_a
                    + pltpu.roll(yc, ROT_DIM // 2, 1) * sin_b)
    return outs[0] if len(outs) == 1 else jnp.concatenate(outs, axis=1)


def _in_proj_kernel(x_ref, g_ref, wq_ref, wt_ref, cos_ref, sa_ref, sb_ref,
                    qa_ref, ka_ref, va_ref, qb_ref, ks_ref, kw_ref, vst_ref, vwt_ref, kcvc_ref, gt_ref,
                    *, tm, seq):
    tt = pl.program_id(1)
    x = x_ref[...]
    ms = jnp.mean(x * x, axis=-1, keepdims=True)
    h = (x * lax.rsqrt(ms + EPS) * g_ref[...]).astype(BF16)
    cos, sa, sb = cos_ref[...], sa_ref[...], sb_ref[...]
    scale = Q_SCALE

    def proj(c0, c1):
        return _dot(h, wq_ref[:, c0:c1])

    o = 0
    qa_ref[...] = (_rope_rows(proj(o, o + A_WIDTH), cos, sa, sb) * scale).astype(BF16)
    o += A_WIDTH
    ka_ref[...] = _rope_rows(proj(o, o + A_WIDTH), cos, sa, sb).astype(BF16)
    o += A_WIDTH
    va_ref[...] = proj(o, o + A_WIDTH).astype(BF16)
    o += A_WIDTH
    qb_ref[...] = (_rope_rows(proj(o, o + B_WIDTH), cos, sa, sb) * scale).astype(BF16)
    o += B_WIDTH
    kcvc_ref[...] = proj(o, o + 2 * KV_WIDTH)
    o += 2 * KV_WIDTH
    ksw = _rope_rows(proj(o, o + 2 * KV_WIDTH), cos, sa, sb)

    lane = lax.broadcasted_iota(jnp.int32, (tm, LANES), 1)
    row = lax.broadcasted_iota(jnp.int32, (tm, LANES), 0)
    lo = lane < HEAD_DIM
    blk = (tt * tm + row) >> SEL_SHIFT
    onehot = jnp.where(lane - SEL_LANE0 == blk, 1.0, 0.0)
    for kind, ref in ((0, ks_ref), (1, kw_ref)):
        kk = ksw[:, kind * LANES:(kind + 1) * LANES]
        tail = onehot if kind == 0 else 0.0
        ref[0, 0] = jnp.where(lo, kk, tail).astype(BF16)
        ref[0, 1] = jnp.where(lo, pltpu.roll(kk, HEAD_DIM, 1), tail).astype(BF16)

    tr = _dot_nt(wt_ref[...], h)
    ones_row = jnp.where(lax.broadcasted_iota(jnp.int32, (NSA_VROWS - HEAD_DIM, tm), 0) == 0, 1.0, 0.0)
    for kind, ref in ((0, vst_ref), (1, vwt_ref)):
        rows = [tr[kind * LANES + gg * HEAD_DIM:kind * LANES + (gg + 1) * HEAD_DIM] for gg in range(B_KV_HEADS)]
        slab = jnp.concatenate([rows[0], ones_row, rows[1], ones_row], axis=0).astype(BF16)
        for c in range(tm // NSA_CK):
            ref[0, c] = slab[:, c * NSA_CK:(c + 1) * NSA_CK]
    gt_ref[0] = jax.nn.sigmoid(tr[2 * LANES:2 * LANES + 2 * GATE_ROWS, :])


def _rope_tables(pos):
    half = ROT_DIM // 2
    inv = ROPE_THETA ** (-jnp.arange(0, ROT_DIM, 2, dtype=F32) / ROT_DIM)
    ang = pos.astype(F32)[:, None] * inv[None, :]
    cos, sin = jnp.cos(ang), jnp.sin(ang)
    n = pos.shape[0]
    ones = jnp.ones((n, HEAD_DIM - ROT_DIM), F32)
    zeros = jnp.zeros((n, HEAD_DIM - ROT_DIM), F32)
    zh = jnp.zeros((n, half), F32)
    c_head = jnp.concatenate([cos, cos, ones], axis=1)
    a_head = jnp.concatenate([-sin, zh, zeros], axis=1)
    b_head = jnp.concatenate([zh, sin, zeros], axis=1)
    rep = LANES // HEAD_DIM
    return jnp.tile(c_head, (1, rep)), jnp.tile(a_head, (1, rep)), jnp.tile(b_head, (1, rep))


def _in_proj(x, norm_g, w_in, tables, *, tm=512):
    B, T, D = x.shape
    nt = T // tm
    offs = [0]
    for n in (A_WIDTH, A_WIDTH, A_WIDTH, B_WIDTH, KV_WIDTH, KV_WIDTH, KV_WIDTH, KV_WIDTH, KV_WIDTH, KV_WIDTH,
              B_HEADS * N_BRANCH):
        offs.append(offs[-1] + n)
    col = lambda i: w_in[:, offs[i]:offs[i + 1]]
    wq = jnp.concatenate([col(0), col(1), col(2), col(3), col(4), col(5), col(6), col(8)], axis=1).astype(BF16)
    gl = col(10)
    per_g = B_GROUP * N_BRANCH
    gpad = jnp.zeros((D, GATE_ROWS - per_g), w_in.dtype)
    wt = jnp.concatenate([col(7), col(9), gl[:, :per_g], gpad, gl[:, per_g:], gpad], axis=1).T.astype(BF16)
    cos, sa, sb = tables
    nq = wq.shape[1]
    nr = wt.shape[0]
    tok = lambda w: pl.BlockSpec((None, tm, w), lambda b, t: (b, t, 0))
    const = lambda shape: pl.BlockSpec(shape, lambda b, t: (0,) * len(shape))
    tab = pl.BlockSpec((tm, LANES), lambda b, t: (t, 0))
    out_shapes = (
        jax.ShapeDtypeStruct((B, T, A_WIDTH), BF16),
        jax.ShapeDtypeStruct((B, T, A_WIDTH), BF16),
        jax.ShapeDtypeStruct((B, T, A_WIDTH), BF16),
        jax.ShapeDtypeStruct((B, T, B_WIDTH), BF16),
        jax.ShapeDtypeStruct((B, B_KV_HEADS, T, LANES), BF16),
        jax.ShapeDtypeStruct((B, B_KV_HEADS, T, LANES), BF16),
        jax.ShapeDtypeStruct((B, T // NSA_CK, B_KV_HEADS * NSA_VROWS, NSA_CK), BF16),
        jax.ShapeDtypeStruct((B, T // NSA_CK, B_KV_HEADS * NSA_VROWS, NSA_CK), BF16),
        jax.ShapeDtypeStruct((B, T, 2 * KV_WIDTH), F32),
        jax.ShapeDtypeStruct((B, 2 * GATE_ROWS, T), F32),
    )
    frame = pl.BlockSpec((1, B_KV_HEADS, tm, LANES), lambda b, t: (b, 0, t, 0))
    vt = pl.BlockSpec((1, tm // NSA_CK, B_KV_HEADS * NSA_VROWS, NSA_CK), lambda b, t: (b, t, 0, 0))
    out_specs = (tok(A_WIDTH), tok(A_WIDTH), tok(A_WIDTH), tok(B_WIDTH), frame, frame, vt, vt,
                 tok(2 * KV_WIDTH), pl.BlockSpec((1, 2 * GATE_ROWS, tm), lambda b, t: (b, 0, t)))
    return pl.pallas_call(
        functools.partial(_in_proj_kernel, tm=tm, seq=T),
        grid=(B, nt),
        in_specs=[tok(D), const((1, D)), const((D, nq)), const((nr, D)), tab, tab, tab],
        out_specs=out_specs,
        out_shape=out_shapes,
        compiler_params=pltpu.CompilerParams(dimension_semantics=("parallel", "parallel"),
                                             vmem_limit_bytes=VMEM_LIMIT),
        name="in_proj",
    )(x, norm_g.reshape(1, D), wq, wt, cos, sa, sb)


def _compress_kernel(ak_ref, av_ref, w1k_ref, w1v_ref, pek_ref, pev_ref, w2k_ref, w2vt_ref, cos_ref, sa_ref, sb_ref,
                     kc_ref, vct_ref, *, nc_pad):
    half = CMP_LEN // 2
    hid_w = B_KV_HEADS * CMP_HIDDEN

    def hidden(a_ref, w1_ref, pe_ref):
        acc_u = jnp.zeros((nc_pad, hid_w), F32)
        acc_v = jnp.zeros((nc_pad, hid_w), F32)
        for p in range(half):
            ap = a_ref[0, pl.ds(p, nc_pad, stride=CMP_STRIDE), :]
            acc_u = acc_u + _dot((ap + pe_ref[p:p + 1, :]).astype(BF16), w1_ref[p])
            acc_v = acc_v + _dot((ap + pe_ref[half + p:half + p + 1, :]).astype(BF16), w1_ref[half + p])
        return jax.nn.gelu(acc_u + pltpu.roll(acc_v, nc_pad - 1, 0))

    hk = hidden(ak_ref, w1k_ref, pek_ref).astype(BF16)
    hv = hidden(av_ref, w1v_ref, pev_ref).astype(BF16)
    for g in range(B_KV_HEADS):
        hg = hk[:, g * CMP_HIDDEN:(g + 1) * CMP_HIDDEN]
        kc = _dot(hg, w2k_ref[...])
        kc_ref[0, g] = _rope_rows(kc, cos_ref[...], sa_ref[...], sb_ref[...]).astype(BF16)
        vg = hv[:, g * CMP_HIDDEN:(g + 1) * CMP_HIDDEN]
        vct_ref[0, g] = _dot_nt(w2vt_ref[...], vg).astype(BF16)


def _block_diag_w1(w1):
    w = w1.reshape(CMP_LEN, HEAD_DIM, CMP_HIDDEN)
    z = jnp.zeros_like(w)
    top = jnp.concatenate([w, z], axis=2)
    bot = jnp.concatenate([z, w], axis=2)
    return jnp.concatenate([top, bot], axis=1).astype(BF16)


def _compress(kcvc, pe_k, w1_k, w2_k, pe_v, w1_v, w2_v, cmp_tables):
    B, T, _ = kcvc.shape
    nc_pad = T // CMP_STRIDE
    w2k = jnp.concatenate([w2_k, jnp.zeros_like(w2_k)], axis=1).astype(BF16)
    w2vt = w2_v.T.astype(BF16)
    pek = jnp.tile(pe_k, (1, B_KV_HEADS))
    pev = jnp.tile(pe_v, (1, B_KV_HEADS))
    const = lambda shape: pl.BlockSpec(shape, lambda b: (0,) * len(shape))
    cos, sa, sb = cmp_tables
    return pl.pallas_call(
        functools.partial(_compress_kernel, nc_pad=nc_pad),
        grid=(B,),
        in_specs=[pl.BlockSpec((1, T, KV_WIDTH), lambda b: (b, 0, 0)), pl.BlockSpec((1, T, KV_WIDTH), lambda b: (b, 0, 1)),
                  const((CMP_LEN, LANES, B_KV_HEADS * CMP_HIDDEN)), const((CMP_LEN, LANES, B_KV_HEADS * CMP_HIDDEN)),
                  const((CMP_LEN, LANES)), const((CMP_LEN, LANES)),
                  const((CMP_HIDDEN, LANES)), const((HEAD_DIM, CMP_HIDDEN)),
                  const((nc_pad, LANES)), const((nc_pad, LANES)), const((nc_pad, LANES))],
        out_specs=(pl.BlockSpec((1, B_KV_HEADS, nc_pad, LANES), lambda b: (b, 0, 0, 0)),
                   pl.BlockSpec((1, B_KV_HEADS, HEAD_DIM, nc_pad), lambda b: (b, 0, 0, 0))),
        out_shape=(jax.ShapeDtypeStruct((B, B_KV_HEADS, nc_pad, LANES), BF16),
                   jax.ShapeDtypeStruct((B, B_KV_HEADS, HEAD_DIM, nc_pad), BF16)),
        compiler_params=pltpu.CompilerParams(dimension_semantics=("parallel",), vmem_limit_bytes=VMEM_LIMIT),
        name="compress",
    )(kcvc, kcvc, _block_diag_w1(w1_k), _block_diag_w1(w1_v), pek, pev, w2k, w2vt, cos, sa, sb)


def _mixer_a_kernel(q_ref, k_ref, v_ref, bias_ref, o_ref, nat, qd0, qd1, kd, vd0, vd1, u_s, m_s, l_s, *, seq):
    blk = A_BLOCK
    nres = seq // blk
    lane = lax.broadcasted_iota(jnp.int32, (blk, LANES), 1)
    lo = lane < HEAD_DIM
    for src, dsts in ((q_ref, (qd0, qd1)), (k_ref, (kd,)), (v_ref, (vd0, vd1))):
        nat[...] = src[0].astype(F32)
        for r in range(nres):
            rows = nat[pl.ds(r, blk, stride=nres), :]
            if src is k_ref:
                kd[pl.ds(r * blk, blk), :] = rows
            else:
                fill = 0.0 if src is q_ref else 1.0
                dsts[0][pl.ds(r * blk, blk), :] = jnp.where(lo, rows, fill)
                dsts[1][pl.ds(r * blk, blk), :] = jnp.where(lo, fill, rows)

    def pieces(dil, rd, row_off, rows):
        return [pl.ds(pl.multiple_of((rd + dil * jj) * blk + row_off, 8), rows) for jj in range(nres // dil)]

    def gather(ref, idx):
        parts = [ref[i, :] for i in idx]
        return parts[0] if len(parts) == 1 else jnp.concatenate(parts, axis=0)

    def attend(pi, dil, blocks):
        pr = blk // (nres // dil)
        q_idxs, vbs, scores = [], [], []
        for rd, n, first in blocks:
            q_idx = pieces(dil, rd, n * pr, pr)
            k_idx = q_idx if first else pieces(dil, rd, (n - 1) * pr, 2 * pr)
            bias = bias_ref[pi, :, 0:blk] if first else bias_ref[pi, :, blk:3 * blk]
            kb = gather(kd, k_idx).astype(BF16)
            q_idxs.append(q_idx)
            for qd, vd in ((qd0, vd0), (qd1, vd1)):
                vbs.append(gather(vd, k_idx).astype(BF16))
                scores.append(_dot_nt(gather(qd, q_idx).astype(BF16), kb) + bias)
        es, ms = [], []
        for s in scores:
            m = jnp.max(s, axis=-1, keepdims=True)
            ms.append(m)
            es.append(jnp.exp2(s - m).astype(BF16))
        pvs = [_dot(e, vb) for e, vb in zip(es, vbs)]
        for b, q_idx in enumerate(q_idxs):
            u = jnp.where(lo, pvs[2 * b], pvs[2 * b + 1])
            l_swapped = jnp.where(lo, pvs[2 * b + 1], pvs[2 * b])
            m = jnp.where(lo, ms[2 * b], ms[2 * b + 1])
            for jj, idx in enumerate(q_idx):
                u_s[pi, idx, :] = u[jj * pr:(jj + 1) * pr]
                m_s[pi, idx, :] = m[jj * pr:(jj + 1) * pr]
                l_s[pi, idx, :] = l_swapped[jj * pr:(jj + 1) * pr]

    group = 4
    for pi, (_, dil) in enumerate(A_PATTERNS):
        nb = seq // dil // blk
        if nb == 1:
            def residues(i, carry, pi=pi, dil=dil):
                attend(pi, dil, [(i * group + jj, 0, True) for jj in range(group)])
                return carry

            lax.fori_loop(0, dil // group, residues, 0)
        else:
            def residue(rd, carry, pi=pi, dil=dil, nb=nb):
                attend(pi, dil, [(rd, 0, True)] + [(rd, n, False) for n in range(1, group)])

                def inner(i, c):
                    attend(pi, dil, [(rd, i * group + jj, False) for jj in range(group)])
                    return c

                return lax.fori_loop(1, nb // group, inner, carry)

            lax.fori_loop(0, dil, residue, 0)

    rows = 2 * blk

    def merge(c, carry):
        r0 = pl.multiple_of(c * rows, rows)
        ms = [m_s[p, pl.ds(r0, rows), :] for p in range(len(A_PATTERNS))]
        m_all = functools.reduce(jnp.maximum, ms)
        num = jnp.zeros((rows, LANES), F32)
        den = jnp.zeros((rows, LANES), F32)
        for p in range(len(A_PATTERNS)):
            a = jnp.exp2(ms[p] - m_all)
            num = num + a * u_s[p, pl.ds(r0, rows), :]
            den = den + a * pltpu.roll(l_s[p, pl.ds(r0, rows), :], HEAD_DIM, 1)
        out = num / den
        for jj in range(rows // blk):
            o_ref[0, pl.ds(c * (rows // blk) + jj, blk, stride=nres), :] = out[jj * blk:(jj + 1) * blk]
        return carry

    lax.fori_loop(0, seq // rows, merge, 0)


def _mixer_a_bias(nres):
    blk = A_BLOCK

    def sub_pos(i, fold, rows):
        return fold * (i % rows) + i // rows

    out = np.zeros((len(A_PATTERNS), blk, 3 * blk), np.float32)
    for pi, (window, dil) in enumerate(A_PATTERNS):
        n_back = window // dil
        fold = nres // dil
        pr = blk // fold
        sq = sub_pos(np.arange(blk), fold, pr)[:, None]
        d_first = sq - sub_pos(np.arange(blk), fold, pr)[None, :]
        d_band = sq + blk - sub_pos(np.arange(2 * blk), fold, 2 * pr)[None, :]
        dist = np.concatenate([d_first, d_band], axis=1)
        out[pi] = np.where((dist >= 0) & (dist <= n_back), 0.0, NEG)
    return jnp.asarray(out)


def _mixer_a(qa, ka, va):
    B, T, W = qa.shape
    npair = W // LANES
    spec = pl.BlockSpec((1, T, LANES), lambda b, p: (b, 0, p))
    npat = len(A_PATTERNS)
    nres = T // A_BLOCK
    assert all(nres % d == 0 and T % (d * A_BLOCK) == 0 for _, d in A_PATTERNS) and max(d for _, d in A_PATTERNS) == nres
    return pl.pallas_call(
        functools.partial(_mixer_a_kernel, seq=T),
        grid=(B, npair),
        in_specs=[spec, spec, spec, pl.BlockSpec((npat, A_BLOCK, 3 * A_BLOCK), lambda b, p: (0, 0, 0))],
        out_specs=spec,
        out_shape=jax.ShapeDtypeStruct((B, T, W), F32),
        scratch_shapes=[pltpu.VMEM((T, LANES), F32)] * 6 + [pltpu.VMEM((npat, T, LANES), F32)] * 3,
        compiler_params=pltpu.CompilerParams(dimension_semantics=("parallel", "parallel"),
                                             vmem_limit_bytes=VMEM_LIMIT),
        name="mixer_a",
    )(qa, ka, va, _mixer_a_bias(nres))


def _nsa_kernel(*refs, n_sel_blocks, nq):
    qi = pl.program_id(2)
    for k in range(nq):
        pl.when(qi == k)(functools.partial(_nsa_block, k, *refs, n_sel_blocks=n_sel_blocks))


def _nsa_block(qi, q_ref, ks_ref, kw_ref, vst_ref, vwt_ref, kc_ref, vct_ref, gt_ref, ovt_ref, bias_ref, o_ref,
               *, n_sel_blocks):
    tq, ck = NSA_TQ, NSA_CK
    t0 = qi * tq
    nrow = B_GROUP * tq

    q = q_ref[0].astype(F32)
    lane = lax.broadcasted_iota(jnp.int32, (tq, LANES), 1)
    lo = lane < HEAD_DIM
    frames = []
    for r in range(B_GROUP):
        ch = q[:, (r // 2) * LANES:(r // 2 + 1) * LANES]
        if r % 2:
            ch = pltpu.roll(ch, HEAD_DIM, 1)
        frames.append(jnp.where(lo, ch, 0.0))
    qs = jnp.concatenate(frames, axis=0)

    q_b = qs.astype(BF16)
    g = pl.program_id(1)
    v_rows = pl.ds(pl.multiple_of(g * NSA_VROWS, NSA_VROWS), NSA_VROWS)

    def scores(qmat, k_ref, c0, c1, first_bias):
        s = _dot_nt(k_ref[0, 0, c0 * ck:(c1 + 1) * ck, :], qmat)
        pieces = [s[n * ck:(n + 1) * ck] for n in range(c1 - c0 + 1)]
        pieces[-1] = pieces[-1] + bias_ref[0]
        if first_bias is not None:
            pieces[0] = pieces[0] + first_bias
        return pieces[0] if len(pieces) == 1 else jnp.concatenate(pieces, axis=0)

    def softmax_pv(s, vt_ref, c0, c1):
        vts = [vt_ref[0, c, v_rows, :] for c in range(c0, c1 + 1)]
        p = jnp.exp2(s - jnp.max(s, axis=0, keepdims=True)).astype(BF16)
        pv = _dot(vts[0] if len(vts) == 1 else jnp.concatenate(vts, axis=1), p)
        return pv[0:HEAD_DIM] / pv[HEAD_DIM:HEAD_DIM + 1]

    nwin = WIN // ck
    w0 = max(qi - nwin, 0)
    s_win = scores(q_b, kw_ref, w0, qi, bias_ref[1] if qi >= nwin else None)

    nc_pad = kc_ref.shape[2]
    sc = _dot_nt(kc_ref[0, 0], q_b)
    c_end = lax.broadcasted_iota(jnp.int32, (nc_pad, nrow), 0) * CMP_STRIDE + (CMP_LEN - 1)
    cmask = c_end <= t0 + (lax.broadcasted_iota(jnp.int32, (nc_pad, nrow), 1) & (tq - 1))
    sc = jnp.where(cmask, sc, NEG)
    m = jnp.max(sc, axis=0, keepdims=True)
    e = jnp.where(cmask, jnp.exp2(sc - m), 0.0)
    den = jnp.sum(e, axis=0, keepdims=True)
    p_cmp = e / jnp.maximum(den, 1e-30)
    o_cmp = _dot(vct_ref[0, 0], p_cmp.astype(BF16))

    psum = p_cmp[:, 0:tq]
    for r in range(1, B_GROUP):
        psum = psum + p_cmp[:, r * tq:(r + 1) * tq]
    p_hi = psum.astype(BF16)
    p_lo = (psum - p_hi.astype(F32)).astype(BF16)
    imp = _dot(ovt_ref[...], p_hi) + _dot(ovt_ref[...], p_lo)
    j = lax.broadcasted_iota(jnp.int32, (n_sel_blocks, tq), 0)
    cur = (t0 + lax.broadcasted_iota(jnp.int32, (n_sel_blocks, tq), 1)) >> SEL_SHIFT
    forced = (j == 0) | (j == cur) | (j == cur - 1)
    low = -3e38
    score = jnp.where(forced, imp + 2.0, jnp.where(j > cur, -1.0, imp))
    sel = jnp.zeros((n_sel_blocks, tq), jnp.bool_)
    j_f = j.astype(F32)
    for _ in range(min(SEL_TOPK, n_sel_blocks)):
        mx = jnp.max(score, axis=0, keepdims=True)
        first = jnp.min(jnp.where(score == mx, j_f, 4.0 * LANES), axis=0, keepdims=True)
        hit = j_f == first
        sel = sel | hit
        score = jnp.where(hit, low, score)
    selneg = jnp.concatenate([jnp.zeros((SEL_LANE0, tq), F32), jnp.where(sel, 0.0, NEG),
                              jnp.zeros((LANES - SEL_LANE0 - n_sel_blocks, tq), F32)], axis=0).T
    q_aug = (qs + jnp.concatenate([selneg] * B_GROUP, axis=0)).astype(BF16)

    s_sel = scores(q_aug, ks_ref, 0, qi, None)
    o_win = softmax_pv(s_win, vwt_ref, w0, qi)
    o_sel = softmax_pv(s_sel, vst_ref, 0, qi)

    gt = gt_ref[0]
    outs = []
    for r in range(B_GROUP):
        cs = slice(r * tq, (r + 1) * tq)
        outs.append(gt[r * N_BRANCH:r * N_BRANCH + 1, :] * o_cmp[:, cs]
                    + gt[r * N_BRANCH + 1:r * N_BRANCH + 2, :] * o_sel[:, cs]
                    + gt[r * N_BRANCH + 2:r * N_BRANCH + 3, :] * o_win[:, cs])
    for c in range(B_GROUP // 2):
        pair = jnp.concatenate([outs[2 * c], outs[2 * c + 1]], axis=0)
        o_ref[0, :, c * LANES:(c + 1) * LANES] = pair.T


def _overlap_t(nc_pad, ns):
    nc = nc_pad - 1
    c0 = jnp.arange(nc_pad) * CMP_STRIDE
    s0 = jnp.arange(ns) * SEL_BLOCK
    ov = jnp.minimum(c0[None, :] + CMP_LEN, s0[:, None] + SEL_BLOCK) - jnp.maximum(c0[None, :], s0[:, None])
    ov = jnp.clip(ov, 0, None).astype(F32) / CMP_LEN
    ov = jnp.where(jnp.arange(nc_pad)[None, :] < nc, ov, 0.0)
    return ov.astype(BF16)


def _nsa(qb, ks, kw, vst, vwt, kc, vct, gt):
    B, T, W = qb.shape
    nq = T // NSA_TQ
    ns = T // SEL_BLOCK
    nc_pad = kc.shape[2]
    gw = W // B_KV_HEADS
    kspec = pl.BlockSpec((1, 1, T, LANES), lambda b, g, i: (b, g, 0, 0))
    assert NSA_TQ == NSA_CK and WIN % NSA_CK == 0 and SEL_LANE0 + ns <= LANES
    vspec = pl.BlockSpec((1, T // NSA_CK, B_KV_HEADS * NSA_VROWS, NSA_CK), lambda b, g, i: (b, 0, 0, 0))
    key = np.arange(NSA_CK)[:, None]
    qry = (np.arange(B_GROUP * NSA_TQ) % NSA_TQ)[None, :]
    bias = jnp.asarray(np.stack([np.where(key <= qry, 0.0, NEG), np.where(key > qry, 0.0, NEG)]).astype(np.float32))
    return pl.pallas_call(
        functools.partial(_nsa_kernel, n_sel_blocks=ns, nq=nq),
        grid=(B, B_KV_HEADS, nq),
        in_specs=[pl.BlockSpec((1, NSA_TQ, gw), lambda b, g, i: (b, i, g)),
                  kspec, kspec, vspec, vspec,
                  pl.BlockSpec((1, 1, nc_pad, LANES), lambda b, g, i: (b, g, 0, 0)),
                  pl.BlockSpec((1, 1, HEAD_DIM, nc_pad), lambda b, g, i: (b, g, 0, 0)),
                  pl.BlockSpec((1, GATE_ROWS, NSA_TQ), lambda b, g, i: (b, g, i)),
                  pl.BlockSpec((ns, nc_pad), lambda b, g, i: (0, 0)),
                  pl.BlockSpec((2, NSA_CK, B_GROUP * NSA_TQ), lambda b, g, i: (0, 0, 0))],
        out_specs=pl.BlockSpec((1, NSA_TQ, gw), lambda b, g, i: (b, i, g)),
        out_shape=jax.ShapeDtypeStruct((B, T, W), F32),
        compiler_params=pltpu.CompilerParams(dimension_semantics=("parallel", "parallel", "arbitrary"),
                                             vmem_limit_bytes=VMEM_LIMIT),
        name="nsa",
    )(qb, ks, kw, vst, vwt, kc, vct, gt, _overlap_t(nc_pad, ns), bias)


def _post_kernel(x_ref, oa_ref, ob_ref, ga_ref, gb_ref, wo_ref, gm_ref, wu_ref, wd_ref, gf_ref, o_ref, *, final):
    def norm(v, g):
        return v * lax.rsqrt(jnp.mean(v * v, axis=-1, keepdims=True) + EPS) * g

    na = norm(oa_ref[...], ga_ref[...]).astype(BF16)
    nb = norm(ob_ref[...], gb_ref[...]).astype(BF16)
    aw = na.shape[1]
    h_res = x_ref[...] + _dot(na, wo_ref[0:aw, :]) + _dot(nb, wo_ref[aw:, :])
    h = norm(h_res, gm_ref[...]).astype(BF16)
    u = jnp.square(jnp.maximum(_dot(h, wu_ref[...]), 0.0)).astype(BF16)
    acc = h_res + _dot(u, wd_ref[...])
    o_ref[...] = norm(acc, gf_ref[...]) if final else acc


def _post(x, oa, ob, g_a, g_b, w_out, g_mlp, w_up, w_down, g_final, *, final, tm=256):
    B, T, D = x.shape
    n = B * T
    dff = w_up.shape[1]
    tok = lambda w: pl.BlockSpec((tm, w), lambda i: (i, 0))
    const = lambda shape: pl.BlockSpec(shape, lambda i: (0, 0), pipeline_mode=pl.Buffered(1))
    out = pl.pallas_call(
        functools.partial(_post_kernel, final=final),
        grid=(n // tm,),
        in_specs=[tok(D), tok(A_WIDTH), tok(B_WIDTH), const((1, A_WIDTH)), const((1, B_WIDTH)),
                  const((A_WIDTH + B_WIDTH, D)), const((1, D)), const((D, dff)), const((dff, D)), const((1, D))],
        out_specs=tok(D),
        out_shape=jax.ShapeDtypeStruct((n, D), F32),
        compiler_params=pltpu.CompilerParams(dimension_semantics=("parallel",), vmem_limit_bytes=VMEM_LIMIT),
        name="post",
    )(x.reshape(n, D), oa.reshape(n, A_WIDTH), ob.reshape(n, B_WIDTH), g_a.reshape(1, -1), g_b.reshape(1, -1),
      w_out.astype(BF16), g_mlp.reshape(1, D), w_up.astype(BF16), w_down.astype(BF16), g_final.reshape(1, D))
    return out.reshape(B, T, D)


def kernel(x, norm_mix, w_in, cmp_pe_k, cmp_w1_k, cmp_w2_k, cmp_pe_v, cmp_w1_v, cmp_w2_v, g_out_a, g_out_b,
           w_out, norm_mlp, w_up, w_down, norm_final):
    B, T, D = x.shape
    depth = w_in.shape[0]
    tables = _rope_tables(jnp.arange(T))
    cmp_tables = _rope_tables(jnp.arange(T // CMP_STRIDE) * CMP_STRIDE + CMP_LEN - 1)
    h_res = x
    for l in range(depth):
        qa, ka, va, qb, ks, kw, vst, vwt, kcvc, gt = _in_proj(h_res, norm_mix[l], w_in[l], tables)
        kc, vct = _compress(kcvc, cmp_pe_k[l], cmp_w1_k[l], cmp_w2_k[l], cmp_pe_v[l], cmp_w1_v[l], cmp_w2_v[l],
                            cmp_tables)
        oa = _mixer_a(qa, ka, va)
        ob = _nsa(qb, ks, kw, vst, vwt, kc, vct, gt)
        h_res = _post(h_res, oa, ob, g_out_a[l], g_out_b[l], w_out[l], norm_mlp[l], w_up[l], w_down[l], norm_final,
                      final=(l == depth - 1))
    return h_res
```

```python
import functools

import jax
import jax.numpy as jnp
import numpy as np
from jax import lax
from jax.experimental import pallas as pl
from jax.experimental.pallas import tpu as pltpu

F32 = jnp.float32
BF16 = jnp.bfloat16

HEAD_DIM = 64
ROT_DIM = HEAD_DIM // 4
ROPE_THETA = 500000.0
EPS = 1e-6
NEG = -1e30
Q_SCALE = HEAD_DIM ** -0.5 * 1.4426950408889634
LANES = 128

A_HEADS = 8
A_PATTERNS = ((128, 1), (512, 4), (2048, 16))
A_BLOCK = 128
A_GROUP = 8

B_HEADS = 8
B_KV_HEADS = 2
B_GROUP = B_HEADS // B_KV_HEADS
CMP_LEN = 32
CMP_STRIDE = 16
CMP_HIDDEN = 256
SEL_BLOCK = 64
SEL_SHIFT = 6
SEL_TOPK = 8
WIN = 512
N_BRANCH = 3

A_WIDTH = A_HEADS * HEAD_DIM
B_WIDTH = B_HEADS * HEAD_DIM
KV_WIDTH = B_KV_HEADS * HEAD_DIM

NSA_TQ = 256
NSA_CK = 256
NSA_VROWS = 80
NSA_STEP = 2
GATE_ROWS = 16
SEL_LANE0 = HEAD_DIM

VMEM_LIMIT = 56 * 1024 * 1024


def _dot(a, b):
    return jnp.dot(a, b, preferred_element_type=F32)


def _dot_nt(a, b):
    return lax.dot_general(a, b, (((1,), (1,)), ((), ())), preferred_element_type=F32)


def _rope_rows(y, cos, sin_a, sin_b):
    outs = []
    for c in range(y.shape[1] // LANES):
        yc = y[:, c * LANES:(c + 1) * LANES]
        outs.append(yc * cos + pltpu.roll(yc, LANES - ROT_DIM // 2, 1) * sin_a
                    + pltpu.roll(yc, ROT_DIM // 2, 1) * sin_b)
    return outs[0] if len(outs) == 1 else jnp.concatenate(outs, axis=1)


def _in_proj_kernel(x_ref, g_ref, wq_ref, wt_ref, cos_ref, sa_ref, sb_ref,
                    qa_ref, ka_ref, va_ref, qb_ref, ks_ref, kw_ref, vst_ref, vwt_ref, kcvc_ref, gt_ref,
                    *, tm, seq):
    tt = pl.program_id(1)
    x = x_ref[...]
    ms = jnp.mean(x * x, axis=-1, keepdims=True)
    h = (x * lax.rsqrt(ms + EPS) * g_ref[...]).astype(BF16)
    cos, sa, sb = cos_ref[...], sa_ref[...], sb_ref[...]
    scale = Q_SCALE

    def proj(c0, c1):
        return _dot(h, wq_ref[:, c0:c1])

    o = 0
    qa_ref[...] = (_rope_rows(proj(o, o + A_WIDTH), cos, sa, sb) * scale).astype(BF16)
    o += A_WIDTH
    ka_ref[...] = _rope_rows(proj(o, o + A_WIDTH), cos, sa, sb).astype(BF16)
    o += A_WIDTH
    va_ref[...] = proj(o, o + A_WIDTH).astype(BF16)
    o += A_WIDTH
    qb_ref[...] = (_rope_rows(proj(o, o + B_WIDTH), cos, sa, sb) * scale).astype(BF16)
    o += B_WIDTH
    kcvc_ref[...] = proj(o, o + 2 * KV_WIDTH)
    o += 2 * KV_WIDTH
    ksw = _rope_rows(proj(o, o + 2 * KV_WIDTH), cos, sa, sb)

    lane = lax.broadcasted_iota(jnp.int32, (tm, LANES), 1)
    row = lax.broadcasted_iota(jnp.int32, (tm, LANES), 0)
    lo = lane < HEAD_DIM
    blk = (tt * tm + row) >> SEL_SHIFT
    onehot = jnp.where(lane - SEL_LANE0 == blk, 1.0, 0.0)
    for kind, ref in ((0, ks_ref), (1, kw_ref)):
        kk = ksw[:, kind * LANES:(kind + 1) * LANES]
        tail = onehot if kind == 0 else 0.0
        ref[0, 0] = jnp.where(lo, kk, tail).astype(BF16)
        ref[0, 1] = jnp.where(lo, pltpu.roll(kk, HEAD_DIM, 1), tail).astype(BF16)

    tr = _dot_nt(wt_ref[...], h)
    ones_row = jnp.where(lax.broadcasted_iota(jnp.int32, (NSA_VROWS - HEAD_DIM, tm), 0) == 0, 1.0, 0.0)
    for kind, ref in ((0, vst_ref), (1, vwt_ref)):
        rows = [tr[kind * LANES + gg * HEAD_DIM:kind * LANES + (gg + 1) * HEAD_DIM] for gg in range(B_KV_HEADS)]
        slab = jnp.concatenate([rows[0], ones_row, rows[1], ones_row], axis=0).astype(BF16)
        for c in range(tm // NSA_CK):
            ref[0, c] = slab[:, c * NSA_CK:(c + 1) * NSA_CK]
    gt_ref[0] = jax.nn.sigmoid(tr[2 * LANES:2 * LANES + 2 * GATE_ROWS, :])


def _rope_tables(pos):
    half = ROT_DIM // 2
    inv = ROPE_THETA ** (-jnp.arange(0, ROT_DIM, 2, dtype=F32) / ROT_DIM)
    ang = pos.astype(F32)[:, None] * inv[None, :]
    cos, sin = jnp.cos(ang), jnp.sin(ang)
    n = pos.shape[0]
    ones = jnp.ones((n, HEAD_DIM - ROT_DIM), F32)
    zeros = jnp.zeros((n, HEAD_DIM - ROT_DIM), F32)
    zh = jnp.zeros((n, half), F32)
    c_head = jnp.concatenate([cos, cos, ones], axis=1)
    a_head = jnp.concatenate([-sin, zh, zeros], axis=1)
    b_head = jnp.concatenate([zh, sin, zeros], axis=1)
    rep = LANES // HEAD_DIM
    return jnp.tile(c_head, (1, rep)), jnp.tile(a_head, (1, rep)), jnp.tile(b_head, (1, rep))


def _in_proj(x, norm_g, w_in, tables, *, tm=512):
    B, T, D = x.shape
    nt = T // tm
    offs = [0]
    for n in (A_WIDTH, A_WIDTH, A_WIDTH, B_WIDTH, KV_WIDTH, KV_WIDTH, KV_WIDTH, KV_WIDTH, KV_WIDTH, KV_WIDTH,
              B_HEADS * N_BRANCH):
        offs.append(offs[-1] + n)
    col = lambda i: w_in[:, offs[i]:offs[i + 1]]
    wq = jnp.concatenate([col(0), col(1), col(2), col(3), col(4), col(5), col(6), col(8)], axis=1).astype(BF16)
    gl = col(10)
    per_g = B_GROUP * N_BRANCH
    gpad = jnp.zeros((D, GATE_ROWS - per_g), w_in.dtype)
    wt = jnp.concatenate([col(7), col(9), gl[:, :per_g], gpad, gl[:, per_g:], gpad], axis=1).T.astype(BF16)
    cos, sa, sb = tables
    nq = wq.shape[1]
    nr = wt.shape[0]
    tok = lambda w: pl.BlockSpec((None, tm, w), lambda b, t: (b, t, 0))
    const = lambda shape: pl.BlockSpec(shape, lambda b, t: (0,) * len(shape))
    tab = pl.BlockSpec((tm, LANES), lambda b, t: (t, 0))
    out_shapes = (
        jax.ShapeDtypeStruct((B, T, A_WIDTH), BF16),
        jax.ShapeDtypeStruct((B, T, A_WIDTH), BF16),
        jax.ShapeDtypeStruct((B, T, A_WIDTH), BF16),
        jax.ShapeDtypeStruct((B, T, B_WIDTH), BF16),
        jax.ShapeDtypeStruct((B, B_KV_HEADS, T, LANES), BF16),
        jax.ShapeDtypeStruct((B, B_KV_HEADS, T, LANES), BF16),
        jax.ShapeDtypeStruct((B, T // NSA_CK, B_KV_HEADS * NSA_VROWS, NSA_CK), BF16),
        jax.ShapeDtypeStruct((B, T // NSA_CK, B_KV_HEADS * NSA_VROWS, NSA_CK), BF16),
        jax.ShapeDtypeStruct((B, T, 2 * KV_WIDTH), F32),
        jax.ShapeDtypeStruct((B, 2 * GATE_ROWS, T), F32),
    )
    frame = pl.BlockSpec((1, B_KV_HEADS, tm, LANES), lambda b, t: (b, 0, t, 0))
    vt = pl.BlockSpec((1, tm // NSA_CK, B_KV_HEADS * NSA_VROWS, NSA_CK), lambda b, t: (b, t, 0, 0))
    out_specs = (tok(A_WIDTH), tok(A_WIDTH), tok(A_WIDTH), tok(B_WIDTH), frame, frame, vt, vt,
                 tok(2 * KV_WIDTH), pl.BlockSpec((1, 2 * GATE_ROWS, tm), lambda b, t: (b, 0, t)))
    return pl.pallas_call(
        functools.partial(_in_proj_kernel, tm=tm, seq=T),
        grid=(B, nt),
        in_specs=[tok(D), const((1, D)), const((D, nq)), const((nr, D)), tab, tab, tab],
        out_specs=out_specs,
        out_shape=out_shapes,
        compiler_params=pltpu.CompilerParams(dimension_semantics=("parallel", "parallel"),
                                             vmem_limit_bytes=VMEM_LIMIT),
        name="in_proj",
    )(x, norm_g.reshape(1, D), wq, wt, cos, sa, sb)


def _compress_kernel(ak_ref, av_ref, w1k_ref, w1v_ref, pek_ref, pev_ref, w2k_ref, w2vt_ref, cos_ref, sa_ref, sb_ref,
                     kc_ref, vct_ref, *, nc_pad):
    half = CMP_LEN // 2
    hid_w = B_KV_HEADS * CMP_HIDDEN

    def hidden(a_ref, w1_ref, pe_ref):
        acc_u = jnp.zeros((nc_pad, hid_w), F32)
        acc_v = jnp.zeros((nc_pad, hid_w), F32)
        for p in range(half):
            ap = a_ref[0, pl.ds(p, nc_pad, stride=CMP_STRIDE), :]
            acc_u = acc_u + _dot((ap + pe_ref[p:p + 1, :]).astype(BF16), w1_ref[p])
            acc_v = acc_v + _dot((ap + pe_ref[half + p:half + p + 1, :]).astype(BF16), w1_ref[half + p])
        return jax.nn.gelu(acc_u + pltpu.roll(acc_v, nc_pad - 1, 0))

    hk = hidden(ak_ref, w1k_ref, pek_ref).astype(BF16)
    hv = hidden(av_ref, w1v_ref, pev_ref).astype(BF16)
    for g in range(B_KV_HEADS):
        hg = hk[:, g * CMP_HIDDEN:(g + 1) * CMP_HIDDEN]
        kc = _dot(hg, w2k_ref[...])
        kc_ref[0, g] = _rope_rows(kc, cos_ref[...], sa_ref[...], sb_ref[...]).astype(BF16)
        vg = hv[:, g * CMP_HIDDEN:(g + 1) * CMP_HIDDEN]
        vct_ref[0, g] = _dot_nt(w2vt_ref[...], vg).astype(BF16)


def _block_diag_w1(w1):
    w = w1.reshape(CMP_LEN, HEAD_DIM, CMP_HIDDEN)
    z = jnp.zeros_like(w)
    top = jnp.concatenate([w, z], axis=2)
    bot = jnp.concatenate([z, w], axis=2)
    return jnp.concatenate([top, bot], axis=1).astype(BF16)


def _compress(kcvc, pe_k, w1_k, w2_k, pe_v, w1_v, w2_v, cmp_tables):
    B, T, _ = kcvc.shape
    nc_pad = T // CMP_STRIDE
    w2k = jnp.concatenate([w2_k, jnp.zeros_like(w2_k)], axis=1).astype(BF16)
    w2vt = w2_v.T.astype(BF16)
    pek = jnp.tile(pe_k, (1, B_KV_HEADS))
    pev = jnp.tile(pe_v, (1, B_KV_HEADS))
    const = lambda shape: pl.BlockSpec(shape, lambda b: (0,) * len(shape))
    cos, sa, sb = cmp_tables
    return pl.pallas_call(
        functools.partial(_compress_kernel, nc_pad=nc_pad),
        grid=(B,),
        in_specs=[pl.BlockSpec((1, T, KV_WIDTH), lambda b: (b, 0, 0)), pl.BlockSpec((1, T, KV_WIDTH), lambda b: (b, 0, 1)),
                  const((CMP_LEN, LANES, B_KV_HEADS * CMP_HIDDEN)), const((CMP_LEN, LANES, B_KV_HEADS * CMP_HIDDEN)),
                  const((CMP_LEN, LANES)), const((CMP_LEN, LANES)),
                  const((CMP_HIDDEN, LANES)), const((HEAD_DIM, CMP_HIDDEN)),
                  const((nc_pad, LANES)), const((nc_pad, LANES)), const((nc_pad, LANES))],
        out_specs=(pl.BlockSpec((1, B_KV_HEADS, nc_pad, LANES), lambda b: (b, 0, 0, 0)),
                   pl.BlockSpec((1, B_KV_HEADS, HEAD_DIM, nc_pad), lambda b: (b, 0, 0, 0))),
        out_shape=(jax.ShapeDtypeStruct((B, B_KV_HEADS, nc_pad, LANES), BF16),
                   jax.ShapeDtypeStruct((B, B_KV_HEADS, HEAD_DIM, nc_pad), BF16)),
        compiler_params=pltpu.CompilerParams(dimension_semantics=("parallel",), vmem_limit_bytes=VMEM_LIMIT),
        name="compress",
    )(kcvc, kcvc, _block_diag_w1(w1_k), _block_diag_w1(w1_v), pek, pev, w2k, w2vt, cos, sa, sb)


def _mixer_a_kernel(q_ref, k_ref, v_ref, bias_ref, o_ref, nat, qd0, qd1, kd, vd0, vd1, u_s, m_s, l_s, *, seq):
    blk = A_BLOCK
    nres = seq // blk
    lane = lax.broadcasted_iota(jnp.int32, (blk, LANES), 1)
    lo = lane < HEAD_DIM
    for src, dsts in ((q_ref, (qd0, qd1)), (k_ref, (kd,)), (v_ref, (vd0, vd1))):
        nat[...] = src[0].astype(F32)
        for r in range(nres):
            rows = nat[pl.ds(r, blk, stride=nres), :]
            if src is k_ref:
                kd[pl.ds(r * blk, blk), :] = rows
            else:
                fill = 0.0 if src is q_ref else 1.0
                dsts[0][pl.ds(r * blk, blk), :] = jnp.where(lo, rows, fill)
                dsts[1][pl.ds(r * blk, blk), :] = jnp.where(lo, fill, rows)

    def pieces(dil, rd, row_off, rows):
        return [pl.ds(pl.multiple_of((rd + dil * jj) * blk + row_off, 8), rows) for jj in range(nres // dil)]

    def gather(ref, idx):
        parts = [ref[i, :] for i in idx]
        return parts[0] if len(parts) == 1 else jnp.concatenate(parts, axis=0)

    def attend(pi, dil, blocks):
        pr = blk // (nres // dil)
        q_idxs, vbs, scores = [], [], []
        for rd, n, first in blocks:
            q_idx = pieces(dil, rd, n * pr, pr)
            k_idx = q_idx if first else pieces(dil, rd, (n - 1) * pr, 2 * pr)
            bias = bias_ref[pi, :, 0:blk] if first else bias_ref[pi, :, blk:3 * blk]
            kb = gather(kd, k_idx).astype(BF16)
            q_idxs.append(q_idx)
            for qd, vd in ((qd0, vd0), (qd1, vd1)):
                vbs.append(gather(vd, k_idx).astype(BF16))
                scores.append(_dot_nt(gather(qd, q_idx).astype(BF16), kb) + bias)
        es, ms = [], []
        for s in scores:
            m = jnp.max(s, axis=-1, keepdims=True)
            ms.append(m)
            es.append(jnp.exp2(s - m).astype(BF16))
        pvs = [_dot(e, vb) for e, vb in zip(es, vbs)]
        for b, q_idx in enumerate(q_idxs):
            u = jnp.where(lo, pvs[2 * b], pvs[2 * b + 1])
            l_swapped = jnp.where(lo, pvs[2 * b + 1], pvs[2 * b])
            m = jnp.where(lo, ms[2 * b], ms[2 * b + 1])
            for jj, idx in enumerate(q_idx):
                u_s[pi, idx, :] = u[jj * pr:(jj + 1) * pr]
                m_s[pi, idx, :] = m[jj * pr:(jj + 1) * pr]
                l_s[pi, idx, :] = l_swapped[jj * pr:(jj + 1) * pr]

    for pi, (_, dil) in enumerate(A_PATTERNS):
        nb = seq // dil // blk
        if nb <= A_GROUP:
            per = min(A_GROUP // nb, dil)

            def residues(i, carry, pi=pi, dil=dil, nb=nb, per=per):
                attend(pi, dil, [(i * per + jj, n, n == 0) for jj in range(per) for n in range(nb)])
                return carry

            lax.fori_loop(0, dil // per, residues, 0)
        else:
            group = A_GROUP

            def residue(rd, carry, pi=pi, dil=dil, nb=nb, group=group):
                attend(pi, dil, [(rd, 0, True)] + [(rd, n, False) for n in range(1, group)])

                def inner(i, c):
                    attend(pi, dil, [(rd, i * group + jj, False) for jj in range(group)])
                    return c

                return lax.fori_loop(1, nb // group, inner, carry)

            lax.fori_loop(0, dil, residue, 0)

    rows = 2 * blk

    def merge(c, carry):
        r0 = pl.multiple_of(c * rows, rows)
        ms = [m_s[p, pl.ds(r0, rows), :] for p in range(len(A_PATTERNS))]
        m_all = functools.reduce(jnp.maximum, ms)
        num = jnp.zeros((rows, LANES), F32)
        den = jnp.zeros((rows, LANES), F32)
        for p in range(len(A_PATTERNS)):
            a = jnp.exp2(ms[p] - m_all)
            num = num + a * u_s[p, pl.ds(r0, rows), :]
            den = den + a * pltpu.roll(l_s[p, pl.ds(r0, rows), :], HEAD_DIM, 1)
        out = num / den
        for jj in range(rows // blk):
            o_ref[0, pl.ds(c * (rows // blk) + jj, blk, stride=nres), :] = out[jj * blk:(jj + 1) * blk]
        return carry

    lax.fori_loop(0, seq // rows, merge, 0)


def _mixer_a_bias(nres):
    blk = A_BLOCK

    def sub_pos(i, fold, rows):
        return fold * (i % rows) + i // rows

    out = np.zeros((len(A_PATTERNS), blk, 3 * blk), np.float32)
    for pi, (window, dil) in enumerate(A_PATTERNS):
        n_back = window // dil
        fold = nres // dil
        pr = blk // fold
        sq = sub_pos(np.arange(blk), fold, pr)[:, None]
        d_first = sq - sub_pos(np.arange(blk), fold, pr)[None, :]
        d_band = sq + blk - sub_pos(np.arange(2 * blk), fold, 2 * pr)[None, :]
        dist = np.concatenate([d_first, d_band], axis=1)
        out[pi] = np.where((dist >= 0) & (dist <= n_back), 0.0, NEG)
    return jnp.asarray(out)


def _mixer_a(qa, ka, va):
    B, T, W = qa.shape
    npair = W // LANES
    spec = pl.BlockSpec((1, T, LANES), lambda b, p: (b, 0, p))
    npat = len(A_PATTERNS)
    nres = T // A_BLOCK
    assert all(nres % d == 0 and T % (d * A_BLOCK) == 0 for _, d in A_PATTERNS) and max(d for _, d in A_PATTERNS) == nres
    return pl.pallas_call(
        functools.partial(_mixer_a_kernel, seq=T),
        grid=(B, npair),
        in_specs=[spec, spec, spec, pl.BlockSpec((npat, A_BLOCK, 3 * A_BLOCK), lambda b, p: (0, 0, 0))],
        out_specs=spec,
        out_shape=jax.ShapeDtypeStruct((B, T, W), F32),
        scratch_shapes=[pltpu.VMEM((T, LANES), F32)] * 6 + [pltpu.VMEM((npat, T, LANES), F32)] * 3,
        compiler_params=pltpu.CompilerParams(dimension_semantics=("parallel", "parallel"),
                                             vmem_limit_bytes=VMEM_LIMIT),
        name="mixer_a",
    )(qa, ka, va, _mixer_a_bias(nres))


def _nsa_kernel(*refs, n_sel_blocks, nq):
    qi = pl.program_id(2)
    for k in range(nq):
        pl.when(qi == k)(functools.partial(_nsa_block, k, *refs, n_sel_blocks=n_sel_blocks))


def _nsa_block(qi, q_ref, ks_ref, kw_ref, vst_ref, vwt_ref, kc_ref, vct_ref, gt_ref, ovt_ref, bias_ref, o_ref,
               *, n_sel_blocks):
    tq, ck = NSA_TQ, NSA_CK
    t0 = qi * tq
    nrow = B_GROUP * tq

    q = q_ref[0].astype(F32)
    lane = lax.broadcasted_iota(jnp.int32, (tq, LANES), 1)
    lo = lane < HEAD_DIM
    frames = []
    for r in range(B_GROUP):
        ch = q[:, (r // 2) * LANES:(r // 2 + 1) * LANES]
        if r % 2:
            ch = pltpu.roll(ch, HEAD_DIM, 1)
        frames.append(jnp.where(lo, ch, 0.0))
    qs = jnp.concatenate(frames, axis=0)

    q_b = qs.astype(BF16)
    g = pl.program_id(1)
    v_rows = pl.ds(pl.multiple_of(g * NSA_VROWS, NSA_VROWS), NSA_VROWS)

    def scores(qmat, k_ref, c0, c1, first_bias):
        s = _dot_nt(k_ref[0, 0, c0 * ck:(c1 + 1) * ck, :], qmat)
        pieces = [s[n * ck:(n + 1) * ck] for n in range(c1 - c0 + 1)]
        if c1 == qi:
            pieces[-1] = pieces[-1] + bias_ref[0]
        if first_bias is not None:
            pieces[0] = pieces[0] + first_bias
        return pieces[0] if len(pieces) == 1 else jnp.concatenate(pieces, axis=0)

    def softmax_pv(s, vt_ref, c0, c1, state):
        vts = [vt_ref[0, c, v_rows, :] for c in range(c0, c1 + 1)]
        m_new = jnp.max(s, axis=0, keepdims=True)
        if state is not None:
            m_new = jnp.maximum(state[0], m_new)
        p = jnp.exp2(s - m_new).astype(BF16)
        acc = _dot(vts[0] if len(vts) == 1 else jnp.concatenate(vts, axis=1), p)
        if state is not None:
            acc = jnp.exp2(state[0] - m_new) * state[1] + acc
        return m_new, acc

    def normalised(state):
        return state[1][0:HEAD_DIM] / state[1][HEAD_DIM:HEAD_DIM + 1]

    nwin = WIN // ck
    w0 = max(qi - nwin, 0)
    s_win = scores(q_b, kw_ref, w0, qi, bias_ref[1] if qi >= nwin else None)

    nc_pad = kc_ref.shape[2]
    sc = _dot_nt(kc_ref[0, 0], q_b)
    c_end = lax.broadcasted_iota(jnp.int32, (nc_pad, nrow), 0) * CMP_STRIDE + (CMP_LEN - 1)
    cmask = c_end <= t0 + (lax.broadcasted_iota(jnp.int32, (nc_pad, nrow), 1) & (tq - 1))
    sc = jnp.where(cmask, sc, NEG)
    m = jnp.max(sc, axis=0, keepdims=True)
    e = jnp.where(cmask, jnp.exp2(sc - m), 0.0)
    den = jnp.sum(e, axis=0, keepdims=True)
    p_cmp = e / jnp.maximum(den, 1e-30)
    o_cmp = _dot(vct_ref[0, 0], p_cmp.astype(BF16))

    psum = p_cmp[:, 0:tq]
    for r in range(1, B_GROUP):
        psum = psum + p_cmp[:, r * tq:(r + 1) * tq]
    p_hi = psum.astype(BF16)
    p_lo = (psum - p_hi.astype(F32)).astype(BF16)
    imp = _dot(ovt_ref[...], p_hi) + _dot(ovt_ref[...], p_lo)
    j = lax.broadcasted_iota(jnp.int32, (n_sel_blocks, tq), 0)
    cur = (t0 + lax.broadcasted_iota(jnp.int32, (n_sel_blocks, tq), 1)) >> SEL_SHIFT
    forced = (j == 0) | (j == cur) | (j == cur - 1)
    low = -3e38
    score = jnp.where(forced, imp + 2.0, jnp.where(j > cur, -1.0, imp))
    sel = jnp.zeros((n_sel_blocks, tq), jnp.bool_)
    j_f = j.astype(F32)
    for _ in range(min(SEL_TOPK, n_sel_blocks)):
        mx = jnp.max(score, axis=0, keepdims=True)
        first = jnp.min(jnp.where(score == mx, j_f, 4.0 * LANES), axis=0, keepdims=True)
        hit = j_f == first
        sel = sel | hit
        score = jnp.where(hit, low, score)
    selneg = jnp.concatenate([jnp.zeros((SEL_LANE0, tq), F32), jnp.where(sel, 0.0, NEG),
                              jnp.zeros((LANES - SEL_LANE0 - n_sel_blocks, tq), F32)], axis=0).T
    q_aug = (qs + jnp.concatenate([selneg] * B_GROUP, axis=0)).astype(BF16)

    steps = [(c, min(c + NSA_STEP - 1, qi)) for c in range(0, qi + 1, NSA_STEP)]
    s_next = scores(q_aug, ks_ref, *steps[0], None)
    o_win = normalised(softmax_pv(s_win, vwt_ref, w0, qi, None))
    state = None
    for i, (c0, c1) in enumerate(steps):
        s_cur = s_next
        if i + 1 < len(steps):
            s_next = scores(q_aug, ks_ref, *steps[i + 1], None)
        state = softmax_pv(s_cur, vst_ref, c0, c1, state)
    o_sel = normalised(state)

    gt = gt_ref[0]
    outs = []
    for r in range(B_GROUP):
        cs = slice(r * tq, (r + 1) * tq)
        outs.append(gt[r * N_BRANCH:r * N_BRANCH + 1, :] * o_cmp[:, cs]
                    + gt[r * N_BRANCH + 1:r * N_BRANCH + 2, :] * o_sel[:, cs]
                    + gt[r * N_BRANCH + 2:r * N_BRANCH + 3, :] * o_win[:, cs])
    for c in range(B_GROUP // 2):
        pair = jnp.concatenate([outs[2 * c], outs[2 * c + 1]], axis=0)
        o_ref[0, :, c * LANES:(c + 1) * LANES] = pair.T


def _overlap_t(nc_pad, ns):
    nc = nc_pad - 1
    c0 = jnp.arange(nc_pad) * CMP_STRIDE
    s0 = jnp.arange(ns) * SEL_BLOCK
    ov = jnp.minimum(c0[None, :] + CMP_LEN, s0[:, None] + SEL_BLOCK) - jnp.maximum(c0[None, :], s0[:, None])
    ov = jnp.clip(ov, 0, None).astype(F32) / CMP_LEN
    ov = jnp.where(jnp.arange(nc_pad)[None, :] < nc, ov, 0.0)
    return ov.astype(BF16)


def _nsa(qb, ks, kw, vst, vwt, kc, vct, gt):
    B, T, W = qb.shape
    nq = T // NSA_TQ
    ns = T // SEL_BLOCK
    nc_pad = kc.shape[2]
    gw = W // B_KV_HEADS
    kspec = pl.BlockSpec((1, 1, T, LANES), lambda b, g, i: (b, g, 0, 0))
    assert NSA_TQ == NSA_CK and WIN % NSA_CK == 0 and SEL_LANE0 + ns <= LANES
    vspec = pl.BlockSpec((1, T // NSA_CK, B_KV_HEADS * NSA_VROWS, NSA_CK), lambda b, g, i: (b, 0, 0, 0))
    key = np.arange(NSA_CK)[:, None]
    qry = (np.arange(B_GROUP * NSA_TQ) % NSA_TQ)[None, :]
    bias = jnp.asarray(np.stack([np.where(key <= qry, 0.0, NEG), np.where(key > qry, 0.0, NEG)]).astype(np.float32))
    return pl.pallas_call(
        functools.partial(_nsa_kernel, n_sel_blocks=ns, nq=nq),
        grid=(B, B_KV_HEADS, nq),
        in_specs=[pl.BlockSpec((1, NSA_TQ, gw), lambda b, g, i: (b, i, g)),
                  kspec, kspec, vspec, vspec,
                  pl.BlockSpec((1, 1, nc_pad, LANES), lambda b, g, i: (b, g, 0, 0)),
                  pl.BlockSpec((1, 1, HEAD_DIM, nc_pad), lambda b, g, i: (b, g, 0, 0)),
                  pl.BlockSpec((1, GATE_ROWS, NSA_TQ), lambda b, g, i: (b, g, i)),
                  pl.BlockSpec((ns, nc_pad), lambda b, g, i: (0, 0)),
                  pl.BlockSpec((2, NSA_CK, B_GROUP * NSA_TQ), lambda b, g, i: (0, 0, 0))],
        out_specs=pl.BlockSpec((1, NSA_TQ, gw), lambda b, g, i: (b, i, g)),
        out_shape=jax.ShapeDtypeStruct((B, T, W), F32),
        compiler_params=pltpu.CompilerParams(dimension_semantics=("parallel", "parallel", "arbitrary"),
                                             vmem_limit_bytes=VMEM_LIMIT),
        name="nsa",
    )(qb, ks, kw, vst, vwt, kc, vct, gt, _overlap_t(nc_pad, ns), bias)


def _post_kernel(x_ref, oa_ref, ob_ref, ga_ref, gb_ref, wo_ref, gm_ref, wu_ref, wd_ref, gf_ref, o_ref, *, final):
    def norm(v, g):
        return v * lax.rsqrt(jnp.mean(v * v, axis=-1, keepdims=True) + EPS) * g

    na = norm(oa_ref[...], ga_ref[...]).astype(BF16)
    nb = norm(ob_ref[...], gb_ref[...]).astype(BF16)
    aw = na.shape[1]
    h_res = x_ref[...] + _dot(na, wo_ref[0:aw, :]) + _dot(nb, wo_ref[aw:, :])
    h = norm(h_res, gm_ref[...]).astype(BF16)
    u = jnp.square(jnp.maximum(_dot(h, wu_ref[...]), 0.0)).astype(BF16)
    acc = h_res + _dot(u, wd_ref[...])
    o_ref[...] = norm(acc, gf_ref[...]) if final else acc


def _post(x, oa, ob, g_a, g_b, w_out, g_mlp, w_up, w_down, g_final, *, final, tm=256):
    B, T, D = x.shape
    n = B * T
    dff = w_up.shape[1]
    tok = lambda w: pl.BlockSpec((tm, w), lambda i: (i, 0))
    const = lambda shape: pl.BlockSpec(shape, lambda i: (0, 0), pipeline_mode=pl.Buffered(1))
    out = pl.pallas_call(
        functools.partial(_post_kernel, final=final),
        grid=(n // tm,),
        in_specs=[tok(D), tok(A_WIDTH), tok(B_WIDTH), const((1, A_WIDTH)), const((1, B_WIDTH)),
                  const((A_WIDTH + B_WIDTH, D)), const((1, D)), const((D, dff)), const((dff, D)), const((1, D))],
        out_specs=tok(D),
        out_shape=jax.ShapeDtypeStruct((n, D), F32),
        compiler_params=pltpu.CompilerParams(dimension_semantics=("parallel",), vmem_limit_bytes=VMEM_LIMIT),
        name="post",
    )(x.reshape(n, D), oa.reshape(n, A_WIDTH), ob.reshape(n, B_WIDTH), g_a.reshape(1, -1), g_b.reshape(1, -1),
      w_out.astype(BF16), g_mlp.reshape(1, D), w_up.astype(BF16), w_down.astype(BF16), g_final.reshape(1, D))
    return out.reshape(B, T, D)


def kernel(x, norm_mix, w_in, cmp_pe_k, cmp_w1_k, cmp_w2_k, cmp_pe_v, cmp_w1_v, cmp_w2_v, g_out_a, g_out_b,
           w_out, norm_mlp, w_up, w_down, norm_final):
    B, T, D = x.shape
    depth = w_in.shape[0]
    tables = _rope_tables(jnp.arange(T))
    cmp_tables = _rope_tables(jnp.arange(T // CMP_STRIDE) * CMP_STRIDE + CMP_LEN - 1)
    h_res = x
    for l in range(depth):
        qa, ka, va, qb, ks, kw, vst, vwt, kcvc, gt = _in_proj(h_res, norm_mix[l], w_in[l], tables)
        kc, vct = _compress(kcvc, cmp_pe_k[l], cmp_w1_k[l], cmp_w2_k[l], cmp_pe_v[l], cmp_w1_v[l], cmp_w2_v[l],
                            cmp_tables)
        oa = _mixer_a(qa, ka, va)
        ob = _nsa(qb, ks, kw, vst, vwt, kc, vct, gt)
        h_res = _post(h_res, oa, ob, g_out_a[l], g_out_b[l], w_out[l], norm_mlp[l], w_up[l], w_down[l], norm_final,
                      final=(l == depth - 1))
    return h_res
```

```python
import functools

import jax
import jax.numpy as jnp
import numpy as np
from jax import lax
from jax.experimental import pallas as pl
from jax.experimental.pallas import tpu as pltpu

F32 = jnp.float32
BF16 = jnp.bfloat16

HEAD_DIM = 64
ROT_DIM = HEAD_DIM // 4
ROPE_THETA = 500000.0
EPS = 1e-6
NEG = -1e30
Q_SCALE = HEAD_DIM ** -0.5 * 1.4426950408889634
LANES = 128

A_HEADS = 8
A_PATTERNS = ((128, 1), (512, 4), (2048, 16))
A_BLOCK = 128
A_GROUP = 8

B_HEADS = 8
B_KV_HEADS = 2
B_GROUP = B_HEADS // B_KV_HEADS
CMP_LEN = 32
CMP_STRIDE = 16
CMP_HIDDEN = 256
SEL_BLOCK = 64
SEL_SHIFT = 6
SEL_TOPK = 8
WIN = 512
N_BRANCH = 3

A_WIDTH = A_HEADS * HEAD_DIM
B_WIDTH = B_HEADS * HEAD_DIM
KV_WIDTH = B_KV_HEADS * HEAD_DIM

NSA_TQ = 256
NSA_CK = 256
NSA_VROWS = 80
NSA_STEP = 2
GATE_ROWS = 16
SEL_LANE0 = HEAD_DIM

VMEM_LIMIT = 56 * 1024 * 1024


def _dot(a, b):
    return jnp.dot(a, b, preferred_element_type=F32)


def _dot_nt(a, b):
    return lax.dot_general(a, b, (((1,), (1,)), ((), ())), preferred_element_type=F32)


def _rope_rows(y, cos, sin_a, sin_b):
    outs = []
    for c in range(y.shape[1] // LANES):
        yc = y[:, c * LANES:(c + 1) * LANES]
        outs.append(yc * cos + pltpu.roll(yc, LANES - ROT_DIM // 2, 1) * sin_a
                    + pltpu.roll(yc, ROT_DIM // 2, 1) * sin_b)
    return outs[0] if len(outs) == 1 else jnp.concatenate(outs, axis=1)


def _in_proj_kernel(x_ref, g_ref, wq_ref, wt_ref, cos_ref, sa_ref, sb_ref,
                    qa_ref, ka_ref, va_ref, qb_ref, ks_ref, kw_ref, vst_ref, vwt_ref, kcvc_ref, gt_ref,
                    *, tm, seq):
    tt = pl.program_id(1)
    x = x_ref[...]
    ms = jnp.mean(x * x, axis=-1, keepdims=True)
    h = (x * lax.rsqrt(ms + EPS) * g_ref[...]).astype(BF16)
    cos, sa, sb = cos_ref[...], sa_ref[...], sb_ref[...]
    scale = Q_SCALE

    def proj(c0, c1):
        return _dot(h, wq_ref[:, c0:c1])

    o = 0
    qa_ref[...] = (_rope_rows(proj(o, o + A_WIDTH), cos, sa, sb) * scale).astype(BF16)
    o += A_WIDTH
    ka_ref[...] = _rope_rows(proj(o, o + A_WIDTH), cos, sa, sb).astype(BF16)
    o += A_WIDTH
    va_ref[...] = proj(o, o + A_WIDTH).astype(BF16)
    o += A_WIDTH
    qb_ref[...] = (_rope_rows(proj(o, o + B_WIDTH), cos, sa, sb) * scale).astype(BF16)
    o += B_WIDTH
    kcvc_ref[...] = proj(o, o + 2 * KV_WIDTH)
    o += 2 * KV_WIDTH
    ksw = _rope_rows(proj(o, o + 2 * KV_WIDTH), cos, sa, sb)

    lane = lax.broadcasted_iota(jnp.int32, (tm, LANES), 1)
    row = lax.broadcasted_iota(jnp.int32, (tm, LANES), 0)
    lo = lane < HEAD_DIM
    blk = (tt * tm + row) >> SEL_SHIFT
    onehot = jnp.where(lane - SEL_LANE0 == blk, 1.0, 0.0)
    for kind, ref in ((0, ks_ref), (1, kw_ref)):
        kk = ksw[:, kind * LANES:(kind + 1) * LANES]
        tail = onehot if kind == 0 else 0.0
        ref[0, 0] = jnp.where(lo, kk, tail).astype(BF16)
        ref[0, 1] = jnp.where(lo, pltpu.roll(kk, HEAD_DIM, 1), tail).astype(BF16)

    tr = _dot_nt(wt_ref[...], h)
    ones_row = jnp.where(lax.broadcasted_iota(jnp.int32, (NSA_VROWS - HEAD_DIM, tm), 0) == 0, 1.0, 0.0)
    for kind, ref in ((0, vst_ref), (1, vwt_ref)):
        rows = [tr[kind * LANES + gg * HEAD_DIM:kind * LANES + (gg + 1) * HEAD_DIM] for gg in range(B_KV_HEADS)]
        slab = jnp.concatenate([rows[0], ones_row, rows[1], ones_row], axis=0).astype(BF16)
        for c in range(tm // NSA_CK):
            ref[0, c] = slab[:, c * NSA_CK:(c + 1) * NSA_CK]
    gt_ref[0] = jax.nn.sigmoid(tr[2 * LANES:2 * LANES + 2 * GATE_ROWS, :])


def _rope_tables(pos):
    half = ROT_DIM // 2
    inv = ROPE_THETA ** (-jnp.arange(0, ROT_DIM, 2, dtype=F32) / ROT_DIM)
    ang = pos.astype(F32)[:, None] * inv[None, :]
    cos, sin = jnp.cos(ang), jnp.sin(ang)
    n = pos.shape[0]
    ones = jnp.ones((n, HEAD_DIM - ROT_DIM), F32)
    zeros = jnp.zeros((n, HEAD_DIM - ROT_DIM), F32)
    zh = jnp.zeros((n, half), F32)
    c_head = jnp.concatenate([cos, cos, ones], axis=1)
    a_head = jnp.concatenate([-sin, zh, zeros], axis=1)
    b_head = jnp.concatenate([zh, sin, zeros], axis=1)
    rep = LANES // HEAD_DIM
    return jnp.tile(c_head, (1, rep)), jnp.tile(a_head, (1, rep)), jnp.tile(b_head, (1, rep))


def _in_proj(x, norm_g, w_in, tables, *, tm=512):
    B, T, D = x.shape
    nt = T // tm
    offs = [0]
    for n in (A_WIDTH, A_WIDTH, A_WIDTH, B_WIDTH, KV_WIDTH, KV_WIDTH, KV_WIDTH, KV_WIDTH, KV_WIDTH, KV_WIDTH,
              B_HEADS * N_BRANCH):
        offs.append(offs[-1] + n)
    col = lambda i: w_in[:, offs[i]:offs[i + 1]]
    wq = jnp.concatenate([col(0), col(1), col(2), col(3), col(4), col(5), col(6), col(8)], axis=1).astype(BF16)
    gl = col(10)
    per_g = B_GROUP * N_BRANCH
    gpad = jnp.zeros((D, GATE_ROWS - per_g), w_in.dtype)
    wt = jnp.concatenate([col(7), col(9), gl[:, :per_g], gpad, gl[:, per_g:], gpad], axis=1).T.astype(BF16)
    cos, sa, sb = tables
    nq = wq.shape[1]
    nr = wt.shape[0]
    tok = lambda w: pl.BlockSpec((None, tm, w), lambda b, t: (b, t, 0))
    const = lambda shape: pl.BlockSpec(shape, lambda b, t: (0,) * len(shape))
    tab = pl.BlockSpec((tm, LANES), lambda b, t: (t, 0))
    out_shapes = (
        jax.ShapeDtypeStruct((B, T, A_WIDTH), BF16),
        jax.ShapeDtypeStruct((B, T, A_WIDTH), BF16),
        jax.ShapeDtypeStruct((B, T, A_WIDTH), BF16),
        jax.ShapeDtypeStruct((B, T, B_WIDTH), BF16),
        jax.ShapeDtypeStruct((B, B_KV_HEADS, T, LANES), BF16),
        jax.ShapeDtypeStruct((B, B_KV_HEADS, T, LANES), BF16),
        jax.ShapeDtypeStruct((B, T // NSA_CK, B_KV_HEADS * NSA_VROWS, NSA_CK), BF16),
        jax.ShapeDtypeStruct((B, T // NSA_CK, B_KV_HEADS * NSA_VROWS, NSA_CK), BF16),
        jax.ShapeDtypeStruct((B, T, 2 * KV_WIDTH), F32),
        jax.ShapeDtypeStruct((B, 2 * GATE_ROWS, T), F32),
    )
    frame = pl.BlockSpec((1, B_KV_HEADS, tm, LANES), lambda b, t: (b, 0, t, 0))
    vt = pl.BlockSpec((1, tm // NSA_CK, B_KV_HEADS * NSA_VROWS, NSA_CK), lambda b, t: (b, t, 0, 0))
    out_specs = (tok(A_WIDTH), tok(A_WIDTH), tok(A_WIDTH), tok(B_WIDTH), frame, frame, vt, vt,
                 tok(2 * KV_WIDTH), pl.BlockSpec((1, 2 * GATE_ROWS, tm), lambda b, t: (b, 0, t)))
    return pl.pallas_call(
        functools.partial(_in_proj_kernel, tm=tm, seq=T),
        grid=(B, nt),
        in_specs=[tok(D), const((1, D)), const((D, nq)), const((nr, D)), tab, tab, tab],
        out_specs=out_specs,
        out_shape=out_shapes,
        compiler_params=pltpu.CompilerParams(dimension_semantics=("parallel", "parallel"),
                                             vmem_limit_bytes=VMEM_LIMIT),
        name="in_proj",
    )(x, norm_g.reshape(1, D), wq, wt, cos, sa, sb)


def _compress_kernel(ak_ref, av_ref, w1k_ref, w1v_ref, pek_ref, pev_ref, w2k_ref, w2vt_ref, cos_ref, sa_ref, sb_ref,
                     kc_ref, vct_ref, *, nc_pad):
    half = CMP_LEN // 2
    hid_w = B_KV_HEADS * CMP_HIDDEN

    def hidden(a_ref, w1_ref, pe_ref):
        acc_u = jnp.zeros((nc_pad, hid_w), F32)
        acc_v = jnp.zeros((nc_pad, hid_w), F32)
        for p in range(half):
            ap = a_ref[0, pl.ds(p, nc_pad, stride=CMP_STRIDE), :]
            acc_u = acc_u + _dot((ap + pe_ref[p:p + 1, :]).astype(BF16), w1_ref[p])
            acc_v = acc_v + _dot((ap + pe_ref[half + p:half + p + 1, :]).astype(BF16), w1_ref[half + p])
        return jax.nn.gelu(acc_u + pltpu.roll(acc_v, nc_pad - 1, 0))

    hk = hidden(ak_ref, w1k_ref, pek_ref).astype(BF16)
    hv = hidden(av_ref, w1v_ref, pev_ref).astype(BF16)
    for g in range(B_KV_HEADS):
        hg = hk[:, g * CMP_HIDDEN:(g + 1) * CMP_HIDDEN]
        kc = _dot(hg, w2k_ref[...])
        kc_ref[0, g] = _rope_rows(kc, cos_ref[...], sa_ref[...], sb_ref[...]).astype(BF16)
        vg = hv[:, g * CMP_HIDDEN:(g + 1) * CMP_HIDDEN]
        vct_ref[0, g] = _dot_nt(w2vt_ref[...], vg).astype(BF16)


def _block_diag_w1(w1):
    w = w1.reshape(CMP_LEN, HEAD_DIM, CMP_HIDDEN)
    z = jnp.zeros_like(w)
    top = jnp.concatenate([w, z], axis=2)
    bot = jnp.concatenate([z, w], axis=2)
    return jnp.concatenate([top, bot], axis=1).astype(BF16)


def _compress(kcvc, pe_k, w1_k, w2_k, pe_v, w1_v, w2_v, cmp_tables):
    B, T, _ = kcvc.shape
    nc_pad = T // CMP_STRIDE
    w2k = jnp.concatenate([w2_k, jnp.zeros_like(w2_k)], axis=1).astype(BF16)
    w2vt = w2_v.T.astype(BF16)
    pek = jnp.tile(pe_k, (1, B_KV_HEADS))
    pev = jnp.tile(pe_v, (1, B_KV_HEADS))
    const = lambda shape: pl.BlockSpec(shape, lambda b: (0,) * len(shape))
    cos, sa, sb = cmp_tables
    return pl.pallas_call(
        functools.partial(_compress_kernel, nc_pad=nc_pad),
        grid=(B,),
        in_specs=[pl.BlockSpec((1, T, KV_WIDTH), lambda b: (b, 0, 0)), pl.BlockSpec((1, T, KV_WIDTH), lambda b: (b, 0, 1)),
                  const((CMP_LEN, LANES, B_KV_HEADS * CMP_HIDDEN)), const((CMP_LEN, LANES, B_KV_HEADS * CMP_HIDDEN)),
                  const((CMP_LEN, LANES)), const((CMP_LEN, LANES)),
                  const((CMP_HIDDEN, LANES)), const((HEAD_DIM, CMP_HIDDEN)),
                  const((nc_pad, LANES)), const((nc_pad, LANES)), const((nc_pad, LANES))],
        out_specs=(pl.BlockSpec((1, B_KV_HEADS, nc_pad, LANES), lambda b: (b, 0, 0, 0)),
                   pl.BlockSpec((1, B_KV_HEADS, HEAD_DIM, nc_pad), lambda b: (b, 0, 0, 0))),
        out_shape=(jax.ShapeDtypeStruct((B, B_KV_HEADS, nc_pad, LANES), BF16),
                   jax.ShapeDtypeStruct((B, B_KV_HEADS, HEAD_DIM, nc_pad), BF16)),
        compiler_params=pltpu.CompilerParams(dimension_semantics=("parallel",), vmem_limit_bytes=VMEM_LIMIT),
        name="compress",
    )(kcvc, kcvc, _block_diag_w1(w1_k), _block_diag_w1(w1_v), pek, pev, w2k, w2vt, cos, sa, sb)


def _mixer_a_kernel(q_ref, k_ref, v_ref, bias_ref, o_ref, nat, qd0, qd1, kd, vd0, vd1, u_s, m_s, l_s, *, seq):
    blk = A_BLOCK
    nres = seq // blk
    lane = lax.broadcasted_iota(jnp.int32, (blk, LANES), 1)
    lo = lane < HEAD_DIM
    for src, dsts in ((q_ref, (qd0, qd1)), (k_ref, (kd,)), (v_ref, (vd0, vd1))):
        nat[...] = src[0].astype(F32)
        for r in range(nres):
            rows = nat[pl.ds(r, blk, stride=nres), :]
            if src is k_ref:
                kd[pl.ds(r * blk, blk), :] = rows
            else:
                fill = 0.0 if src is q_ref else 1.0
                dsts[0][pl.ds(r * blk, blk), :] = jnp.where(lo, rows, fill)
                dsts[1][pl.ds(r * blk, blk), :] = jnp.where(lo, fill, rows)

    def pieces(dil, rd, row_off, rows):
        return [pl.ds(pl.multiple_of((rd + dil * jj) * blk + row_off, 8), rows) for jj in range(nres // dil)]

    def gather(ref, idx):
        parts = [ref[i, :] for i in idx]
        return parts[0] if len(parts) == 1 else jnp.concatenate(parts, axis=0)

    def attend(pi, dil, blocks):
        pr = blk // (nres // dil)
        q_idxs, vbs, scores = [], [], []
        for rd, n, first in blocks:
            q_idx = pieces(dil, rd, n * pr, pr)
            k_idx = q_idx if first else pieces(dil, rd, (n - 1) * pr, 2 * pr)
            bias = bias_ref[pi, :, 0:blk] if first else bias_ref[pi, :, blk:3 * blk]
            kb = gather(kd, k_idx).astype(BF16)
            q_idxs.append(q_idx)
            for qd, vd in ((qd0, vd0), (qd1, vd1)):
                vbs.append(gather(vd, k_idx).astype(BF16))
                scores.append(_dot_nt(gather(qd, q_idx).astype(BF16), kb) + bias)
        es, ms = [], []
        for s in scores:
            m = jnp.max(s, axis=-1, keepdims=True)
            ms.append(m)
            es.append(jnp.exp2((s - m).astype(BF16)))
        pvs = [_dot(e, vb) for e, vb in zip(es, vbs)]
        for b, q_idx in enumerate(q_idxs):
            u = jnp.where(lo, pvs[2 * b], pvs[2 * b + 1])
            l_swapped = jnp.where(lo, pvs[2 * b + 1], pvs[2 * b])
            m = jnp.where(lo, ms[2 * b], ms[2 * b + 1])
            for jj, idx in enumerate(q_idx):
                u_s[pi, idx, :] = u[jj * pr:(jj + 1) * pr]
                m_s[pi, idx, :] = m[jj * pr:(jj + 1) * pr]
                l_s[pi, idx, :] = l_swapped[jj * pr:(jj + 1) * pr]

    for pi, (_, dil) in enumerate(A_PATTERNS):
        nb = seq // dil // blk
        if nb <= A_GROUP:
            per = min(A_GROUP // nb, dil)

            def residues(i, carry, pi=pi, dil=dil, nb=nb, per=per):
                attend(pi, dil, [(i * per + jj, n, n == 0) for jj in range(per) for n in range(nb)])
                return carry

            lax.fori_loop(0, dil // per, residues, 0)
        else:
            group = A_GROUP

            def residue(rd, carry, pi=pi, dil=dil, nb=nb, group=group):
                attend(pi, dil, [(rd, 0, True)] + [(rd, n, False) for n in range(1, group)])

                def inner(i, c):
                    attend(pi, dil, [(rd, i * group + jj, False) for jj in range(group)])
                    return c

                return lax.fori_loop(1, nb // group, inner, carry)

            lax.fori_loop(0, dil, residue, 0)

    rows = 2 * blk

    def merge(c, carry):
        r0 = pl.multiple_of(c * rows, rows)
        ms = [m_s[p, pl.ds(r0, rows), :] for p in range(len(A_PATTERNS))]
        m_all = functools.reduce(jnp.maximum, ms)
        num = jnp.zeros((rows, LANES), F32)
        den = jnp.zeros((rows, LANES), F32)
        for p in range(len(A_PATTERNS)):
            a = jnp.exp2(ms[p] - m_all)
            num = num + a * u_s[p, pl.ds(r0, rows), :]
            den = den + a * pltpu.roll(l_s[p, pl.ds(r0, rows), :], HEAD_DIM, 1)
        out = num / den
        for jj in range(rows // blk):
            o_ref[0, pl.ds(c * (rows // blk) + jj, blk, stride=nres), :] = out[jj * blk:(jj + 1) * blk]
        return carry

    lax.fori_loop(0, seq // rows, merge, 0)


def _mixer_a_bias(nres):
    blk = A_BLOCK

    def sub_pos(i, fold, rows):
        return fold * (i % rows) + i // rows

    out = np.zeros((len(A_PATTERNS), blk, 3 * blk), np.float32)
    for pi, (window, dil) in enumerate(A_PATTERNS):
        n_back = window // dil
        fold = nres // dil
        pr = blk // fold
        sq = sub_pos(np.arange(blk), fold, pr)[:, None]
        d_first = sq - sub_pos(np.arange(blk), fold, pr)[None, :]
        d_band = sq + blk - sub_pos(np.arange(2 * blk), fold, 2 * pr)[None, :]
        dist = np.concatenate([d_first, d_band], axis=1)
        out[pi] = np.where((dist >= 0) & (dist <= n_back), 0.0, NEG)
    return jnp.asarray(out)


def _mixer_a(qa, ka, va):
    B, T, W = qa.shape
    npair = W // LANES
    spec = pl.BlockSpec((1, T, LANES), lambda b, p: (b, 0, p))
    npat = len(A_PATTERNS)
    nres = T // A_BLOCK
    assert all(nres % d == 0 and T % (d * A_BLOCK) == 0 for _, d in A_PATTERNS) and max(d for _, d in A_PATTERNS) == nres
    return pl.pallas_call(
        functools.partial(_mixer_a_kernel, seq=T),
        grid=(B, npair),
        in_specs=[spec, spec, spec, pl.BlockSpec((npat, A_BLOCK, 3 * A_BLOCK), lambda b, p: (0, 0, 0))],
        out_specs=spec,
        out_shape=jax.ShapeDtypeStruct((B, T, W), F32),
        scratch_shapes=[pltpu.VMEM((T, LANES), F32)] * 6 + [pltpu.VMEM((npat, T, LANES), F32)] * 3,
        compiler_params=pltpu.CompilerParams(dimension_semantics=("parallel", "parallel"),
                                             vmem_limit_bytes=VMEM_LIMIT),
        name="mixer_a",
    )(qa, ka, va, _mixer_a_bias(nres))


def _nsa_kernel(*refs, n_sel_blocks, nq):
    qi = pl.program_id(2)
    for k in range(nq):
        pl.when(qi == k)(functools.partial(_nsa_block, k, *refs, n_sel_blocks=n_sel_blocks))


def _nsa_block(qi, q_ref, ks_ref, kw_ref, vst_ref, vwt_ref, kc_ref, vct_ref, gt_ref, ovt_ref, bias_ref, o_ref,
               *, n_sel_blocks):
    tq, ck = NSA_TQ, NSA_CK
    t0 = qi * tq
    nrow = B_GROUP * tq

    q = q_ref[0].astype(F32)
    lane = lax.broadcasted_iota(jnp.int32, (tq, LANES), 1)
    lo = lane < HEAD_DIM
    frames = []
    for r in range(B_GROUP):
        ch = q[:, (r // 2) * LANES:(r // 2 + 1) * LANES]
        if r % 2:
            ch = pltpu.roll(ch, HEAD_DIM, 1)
        frames.append(jnp.where(lo, ch, 0.0))
    qs = jnp.concatenate(frames, axis=0)

    q_b = qs.astype(BF16)
    g = pl.program_id(1)
    v_rows = pl.ds(pl.multiple_of(g * NSA_VROWS, NSA_VROWS), NSA_VROWS)

    def scores(qmat, k_ref, c0, c1, first_bias):
        s = _dot_nt(k_ref[0, 0, c0 * ck:(c1 + 1) * ck, :], qmat)
        pieces = [s[n * ck:(n + 1) * ck] for n in range(c1 - c0 + 1)]
        if c1 == qi:
            pieces[-1] = pieces[-1] + bias_ref[0]
        if first_bias is not None:
            pieces[0] = pieces[0] + first_bias
        return pieces[0] if len(pieces) == 1 else jnp.concatenate(pieces, axis=0)

    def softmax_pv(s, vt_ref, c0, c1, state):
        vts = [vt_ref[0, c, v_rows, :] for c in range(c0, c1 + 1)]
        m_new = jnp.max(s, axis=0, keepdims=True)
        if state is not None:
            m_new = jnp.maximum(state[0], m_new)
        p = jnp.exp2(s - m_new).astype(BF16)
        acc = _dot(vts[0] if len(vts) == 1 else jnp.concatenate(vts, axis=1), p)
        if state is not None:
            acc = jnp.exp2(state[0] - m_new) * state[1] + acc
        return m_new, acc

    def normalised(state):
        return state[1][0:HEAD_DIM] * (1.0 / state[1][HEAD_DIM:HEAD_DIM + 1])

    nwin = WIN // ck
    w0 = max(qi - nwin, 0)
    s_win = scores(q_b, kw_ref, w0, qi, bias_ref[1] if qi >= nwin else None)

    nc_pad = kc_ref.shape[2]
    n_vis = min(nc_pad, -(-((t0 + tq - CMP_LEN) // CMP_STRIDE + 1) // 16) * 16)
    n_all = max(0, ((t0 - CMP_LEN + 1) // CMP_STRIDE + 1) // 8 * 8) if t0 >= CMP_LEN else 0
    sc = _dot_nt(kc_ref[0, 0, 0:n_vis, :], q_b)
    c_end = (lax.broadcasted_iota(jnp.int32, (n_vis - n_all, nrow), 0) + n_all) * CMP_STRIDE + (CMP_LEN - 1)
    t_col = t0 + (lax.broadcasted_iota(jnp.int32, (n_vis - n_all, nrow), 1) & (tq - 1))
    tail = jnp.where(c_end <= t_col, sc[n_all:], NEG)
    sc = tail if n_all == 0 else jnp.concatenate([sc[:n_all], tail], axis=0)
    m = jnp.max(sc, axis=0, keepdims=True)
    e = jnp.exp2(sc - m)
    if t0 < CMP_LEN - 1:
        e = jnp.where(t_col[0:1] >= CMP_LEN - 1, e, 0.0)
    den = jnp.sum(e, axis=0, keepdims=True)
    p_cmp = e * (1.0 / jnp.maximum(den, 1e-30))
    if n_vis < nc_pad:
        p_cmp = jnp.concatenate([p_cmp, jnp.zeros((nc_pad - n_vis, nrow), F32)], axis=0)
    o_cmp = _dot(vct_ref[0, 0], p_cmp.astype(BF16))

    psum = p_cmp[:, 0:tq]
    for r in range(1, B_GROUP):
        psum = psum + p_cmp[:, r * tq:(r + 1) * tq]
    p_hi = psum.astype(BF16)
    p_lo = (psum - p_hi.astype(F32)).astype(BF16)
    imp = _dot(ovt_ref[...], p_hi) + _dot(ovt_ref[...], p_lo)
    j = lax.broadcasted_iota(jnp.int32, (n_sel_blocks, tq), 0)
    cur = (t0 + lax.broadcasted_iota(jnp.int32, (n_sel_blocks, tq), 1)) >> SEL_SHIFT
    forced = (j == 0) | (j == cur) | (j == cur - 1)
    low = -3e38
    score = jnp.where(forced, imp + 2.0, jnp.where(j > cur, -1.0, imp))
    sel = jnp.zeros((n_sel_blocks, tq), jnp.bool_)
    j_f = j.astype(F32)
    for _ in range(min(SEL_TOPK, n_sel_blocks)):
        mx = jnp.max(score, axis=0, keepdims=True)
        first = jnp.min(jnp.where(score == mx, j_f, 4.0 * LANES), axis=0, keepdims=True)
        hit = j_f == first
        sel = sel | hit
        score = jnp.where(hit, low, score)
    selneg = jnp.concatenate([jnp.zeros((SEL_LANE0, tq), F32), jnp.where(sel, 0.0, NEG),
                              jnp.zeros((LANES - SEL_LANE0 - n_sel_blocks, tq), F32)], axis=0).T
    q_aug = (qs + jnp.concatenate([selneg] * B_GROUP, axis=0)).astype(BF16)

    steps = [(c, min(c + NSA_STEP - 1, qi)) for c in range(0, qi + 1, NSA_STEP)]
    s_next = scores(q_aug, ks_ref, *steps[0], None)
    o_win = normalised(softmax_pv(s_win, vwt_ref, w0, qi, None))
    state = None
    for i, (c0, c1) in enumerate(steps):
        s_cur = s_next
        if i + 1 < len(steps):
            s_next = scores(q_aug, ks_ref, *steps[i + 1], None)
        state = softmax_pv(s_cur, vst_ref, c0, c1, state)
    o_sel = normalised(state)

    gt = gt_ref[0]
    outs = []
    for r in range(B_GROUP):
        cs = slice(r * tq, (r + 1) * tq)
        outs.append(gt[r * N_BRANCH:r * N_BRANCH + 1, :] * o_cmp[:, cs]
                    + gt[r * N_BRANCH + 1:r * N_BRANCH + 2, :] * o_sel[:, cs]
                    + gt[r * N_BRANCH + 2:r * N_BRANCH + 3, :] * o_win[:, cs])
    for c in range(B_GROUP // 2):
        pair = jnp.concatenate([outs[2 * c], outs[2 * c + 1]], axis=0)
        o_ref[0, :, c * LANES:(c + 1) * LANES] = pair.T


def _overlap_t(nc_pad, ns):
    nc = nc_pad - 1
    c0 = jnp.arange(nc_pad) * CMP_STRIDE
    s0 = jnp.arange(ns) * SEL_BLOCK
    ov = jnp.minimum(c0[None, :] + CMP_LEN, s0[:, None] + SEL_BLOCK) - jnp.maximum(c0[None, :], s0[:, None])
    ov = jnp.clip(ov, 0, None).astype(F32) / CMP_LEN
    ov = jnp.where(jnp.arange(nc_pad)[None, :] < nc, ov, 0.0)
    return ov.astype(BF16)


def _nsa(qb, ks, kw, vst, vwt, kc, vct, gt):
    B, T, W = qb.shape
    nq = T // NSA_TQ
    ns = T // SEL_BLOCK
    nc_pad = kc.shape[2]
    gw = W // B_KV_HEADS
    kspec = pl.BlockSpec((1, 1, T, LANES), lambda b, g, i: (b, g, 0, 0))
    assert NSA_TQ == NSA_CK and WIN % NSA_CK == 0 and SEL_LANE0 + ns <= LANES
    vspec = pl.BlockSpec((1, T // NSA_CK, B_KV_HEADS * NSA_VROWS, NSA_CK), lambda b, g, i: (b, 0, 0, 0))
    key = np.arange(NSA_CK)[:, None]
    qry = (np.arange(B_GROUP * NSA_TQ) % NSA_TQ)[None, :]
    bias = jnp.asarray(np.stack([np.where(key <= qry, 0.0, NEG), np.where(key > qry, 0.0, NEG)]).astype(np.float32))
    return pl.pallas_call(
        functools.partial(_nsa_kernel, n_sel_blocks=ns, nq=nq),
        grid=(B, B_KV_HEADS, nq),
        in_specs=[pl.BlockSpec((1, NSA_TQ, gw), lambda b, g, i: (b, i, g)),
                  kspec, kspec, vspec, vspec,
                  pl.BlockSpec((1, 1, nc_pad, LANES), lambda b, g, i: (b, g, 0, 0)),
                  pl.BlockSpec((1, 1, HEAD_DIM, nc_pad), lambda b, g, i: (b, g, 0, 0)),
                  pl.BlockSpec((1, GATE_ROWS, NSA_TQ), lambda b, g, i: (b, g, i)),
                  pl.BlockSpec((ns, nc_pad), lambda b, g, i: (0, 0)),
                  pl.BlockSpec((2, NSA_CK, B_GROUP * NSA_TQ), lambda b, g, i: (0, 0, 0))],
        out_specs=pl.BlockSpec((1, NSA_TQ, gw), lambda b, g, i: (b, i, g)),
        out_shape=jax.ShapeDtypeStruct((B, T, W), F32),
        compiler_params=pltpu.CompilerParams(dimension_semantics=("parallel", "parallel", "arbitrary"),
                                             vmem_limit_bytes=VMEM_LIMIT),
        name="nsa",
    )(qb, ks, kw, vst, vwt, kc, vct, gt, _overlap_t(nc_pad, ns), bias)


def _post_kernel(x_ref, oa_ref, ob_ref, ga_ref, gb_ref, wo_ref, gm_ref, wu_ref, wd_ref, gf_ref, o_ref, *, final):
    def norm(v, g):
        return v * lax.rsqrt(jnp.mean(v * v, axis=-1, keepdims=True) + EPS) * g

    na = norm(oa_ref[...], ga_ref[...]).astype(BF16)
    nb = norm(ob_ref[...], gb_ref[...]).astype(BF16)
    aw = na.shape[1]
    h_res = x_ref[...] + _dot(na, wo_ref[0:aw, :]) + _dot(nb, wo_ref[aw:, :])
    h = norm(h_res, gm_ref[...]).astype(BF16)
    u = jnp.square(jnp.maximum(_dot(h, wu_ref[...]), 0.0)).astype(BF16)
    acc = h_res + _dot(u, wd_ref[...])
    o_ref[...] = norm(acc, gf_ref[...]) if final else acc


def _post(x, oa, ob, g_a, g_b, w_out, g_mlp, w_up, w_down, g_final, *, final, tm=256):
    B, T, D = x.shape
    n = B * T
    dff = w_up.shape[1]
    tok = lambda w: pl.BlockSpec((tm, w), lambda i: (i, 0))
    const = lambda shape: pl.BlockSpec(shape, lambda i: (0, 0), pipeline_mode=pl.Buffered(1))
    out = pl.pallas_call(
        functools.partial(_post_kernel, final=final),
        grid=(n // tm,),
        in_specs=[tok(D), tok(A_WIDTH), tok(B_WIDTH), const((1, A_WIDTH)), const((1, B_WIDTH)),
                  const((A_WIDTH + B_WIDTH, D)), const((1, D)), const((D, dff)), const((dff, D)), const((1, D))],
        out_specs=tok(D),
        out_shape=jax.ShapeDtypeStruct((n, D), F32),
        compiler_params=pltpu.CompilerParams(dimension_semantics=("parallel",), vmem_limit_bytes=VMEM_LIMIT),
        name="post",
    )(x.reshape(n, D), oa.reshape(n, A_WIDTH), ob.reshape(n, B_WIDTH), g_a.reshape(1, -1), g_b.reshape(1, -1),
      w_out.astype(BF16), g_mlp.reshape(1, D), w_up.astype(BF16), w_down.astype(BF16), g_final.reshape(1, D))
    return out.reshape(B, T, D)


def kernel(x, norm_mix, w_in, cmp_pe_k, cmp_w1_k, cmp_w2_k, cmp_pe_v, cmp_w1_v, cmp_w2_v, g_out_a, g_out_b,
           w_out, norm_mlp, w_up, w_down, norm_final):
    B, T, D = x.shape
    depth = w_in.shape[0]
    tables = _rope_tables(jnp.arange(T))
    cmp_tables = _rope_tables(jnp.arange(T // CMP_STRIDE) * CMP_STRIDE + CMP_LEN - 1)
    h_res = x
    for l in range(depth):
        qa, ka, va, qb, ks, kw, vst, vwt, kcvc, gt = _in_proj(h_res, norm_mix[l], w_in[l], tables)
        kc, vct = _compress(kcvc, cmp_pe_k[l], cmp_w1_k[l], cmp_w2_k[l], cmp_pe_v[l], cmp_w1_v[l], cmp_w2_v[l],
                            cmp_tables)
        oa = _mixer_a(qa, ka, va)
        ob = _nsa(qb, ks, kw, vst, vwt, kc, vct, gt)
        h_res = _post(h_res, oa, ob, g_out_a[l], g_out_b[l], w_out[l], norm_mlp[l], w_up[l], w_down[l], norm_final,
                      final=(l == depth - 1))
    return h_res
```

```python
import functools

import jax
import jax.numpy as jnp
import numpy as np
from jax import lax
from jax.experimental import pallas as pl
from jax.experimental.pallas import tpu as pltpu

F32 = jnp.float32
BF16 = jnp.bfloat16

HEAD_DIM = 64
ROT_DIM = HEAD_DIM // 4
ROPE_THETA = 500000.0
EPS = 1e-6
NEG = -1e30
Q_SCALE = HEAD_DIM ** -0.5 * 1.4426950408889634
LANES = 128

A_HEADS = 8
A_PATTERNS = ((128, 1), (512, 4), (2048, 16))
A_BLOCK = 128
A_GROUP = 8

B_HEADS = 8
B_KV_HEADS = 2
B_GROUP = B_HEADS // B_KV_HEADS
CMP_LEN = 32
CMP_STRIDE = 16
CMP_HIDDEN = 256
SEL_BLOCK = 64
SEL_SHIFT = 6
SEL_TOPK = 8
WIN = 512
N_BRANCH = 3

A_WIDTH = A_HEADS * HEAD_DIM
B_WIDTH = B_HEADS * HEAD_DIM
KV_WIDTH = B_KV_HEADS * HEAD_DIM

NSA_TQ = 256
NSA_CK = 256
NSA_VROWS = 80
NSA_STEP = 2
GATE_ROWS = 16
SEL_LANE0 = HEAD_DIM

VMEM_LIMIT = 56 * 1024 * 1024


def _dot(a, b):
    return jnp.dot(a, b, preferred_element_type=F32)


def _dot_nt(a, b):
    return lax.dot_general(a, b, (((1,), (1,)), ((), ())), preferred_element_type=F32)


def _rope_rows(y, cos, sin_a, sin_b):
    outs = []
    for c in range(y.shape[1] // LANES):
        yc = y[:, c * LANES:(c + 1) * LANES]
        outs.append(yc * cos + pltpu.roll(yc, LANES - ROT_DIM // 2, 1) * sin_a
                    + pltpu.roll(yc, ROT_DIM // 2, 1) * sin_b)
    return outs[0] if len(outs) == 1 else jnp.concatenate(outs, axis=1)


def _in_proj_kernel(x_ref, g_ref, wq_ref, wt_ref, cos_ref, sa_ref, sb_ref,
                    qa_ref, ka_ref, va_ref, qb_ref, ks_ref, kw_ref, vst_ref, vwt_ref, kcvc_ref, gt_ref,
                    *, tm, seq):
    tt = pl.program_id(1)
    x = x_ref[...]
    ms = jnp.mean(x * x, axis=-1, keepdims=True)
    h = (x * lax.rsqrt(ms + EPS) * g_ref[...]).astype(BF16)
    cos, sa, sb = cos_ref[...], sa_ref[...], sb_ref[...]
    scale = Q_SCALE

    def proj(c0, c1):
        return _dot(h, wq_ref[:, c0:c1])

    o = 0
    qa_ref[...] = (_rope_rows(proj(o, o + A_WIDTH), cos, sa, sb) * scale).astype(BF16)
    o += A_WIDTH
    ka_ref[...] = _rope_rows(proj(o, o + A_WIDTH), cos, sa, sb).astype(BF16)
    o += A_WIDTH
    va_ref[...] = proj(o, o + A_WIDTH).astype(BF16)
    o += A_WIDTH
    qb_ref[...] = (_rope_rows(proj(o, o + B_WIDTH), cos, sa, sb) * scale).astype(BF16)
    o += B_WIDTH
    kcvc_ref[...] = proj(o, o + 2 * KV_WIDTH)
    o += 2 * KV_WIDTH
    ksw = _rope_rows(proj(o, o + 2 * KV_WIDTH), cos, sa, sb)

    lane = lax.broadcasted_iota(jnp.int32, (tm, LANES), 1)
    row = lax.broadcasted_iota(jnp.int32, (tm, LANES), 0)
    lo = lane < HEAD_DIM
    blk = (tt * tm + row) >> SEL_SHIFT
    onehot = jnp.where(lane - SEL_LANE0 == blk, 1.0, 0.0)
    for kind, ref in ((0, ks_ref), (1, kw_ref)):
        kk = ksw[:, kind * LANES:(kind + 1) * LANES]
        tail = onehot if kind == 0 else 0.0
        ref[0, 0] = jnp.where(lo, kk, tail).astype(BF16)
        ref[0, 1] = jnp.where(lo, pltpu.roll(kk, HEAD_DIM, 1), tail).astype(BF16)

    tr = _dot_nt(wt_ref[...], h)
    ones_row = jnp.where(lax.broadcasted_iota(jnp.int32, (NSA_VROWS - HEAD_DIM, tm), 0) == 0, 1.0, 0.0)
    for kind, ref in ((0, vst_ref), (1, vwt_ref)):
        rows = [tr[kind * LANES + gg * HEAD_DIM:kind * LANES + (gg + 1) * HEAD_DIM] for gg in range(B_KV_HEADS)]
        slab = jnp.concatenate([rows[0], ones_row, rows[1], ones_row], axis=0).astype(BF16)
        for c in range(tm // NSA_CK):
            ref[0, c] = slab[:, c * NSA_CK:(c + 1) * NSA_CK]
    gt_ref[0] = jax.nn.sigmoid(tr[2 * LANES:2 * LANES + 2 * GATE_ROWS, :])


def _rope_tables(pos):
    half = ROT_DIM // 2
    inv = ROPE_THETA ** (-jnp.arange(0, ROT_DIM, 2, dtype=F32) / ROT_DIM)
    ang = pos.astype(F32)[:, None] * inv[None, :]
    cos, sin = jnp.cos(ang), jnp.sin(ang)
    n = pos.shape[0]
    ones = jnp.ones((n, HEAD_DIM - ROT_DIM), F32)
    zeros = jnp.zeros((n, HEAD_DIM - ROT_DIM), F32)
    zh = jnp.zeros((n, half), F32)
    c_head = jnp.concatenate([cos, cos, ones], axis=1)
    a_head = jnp.concatenate([-sin, zh, zeros], axis=1)
    b_head = jnp.concatenate([zh, sin, zeros], axis=1)
    rep = LANES // HEAD_DIM
    return jnp.tile(c_head, (1, rep)), jnp.tile(a_head, (1, rep)), jnp.tile(b_head, (1, rep))


def _in_proj(x, norm_g, w_in, tables, *, tm=512):
    B, T, D = x.shape
    nt = T // tm
    offs = [0]
    for n in (A_WIDTH, A_WIDTH, A_WIDTH, B_WIDTH, KV_WIDTH, KV_WIDTH, KV_WIDTH, KV_WIDTH, KV_WIDTH, KV_WIDTH,
              B_HEADS * N_BRANCH):
        offs.append(offs[-1] + n)
    col = lambda i: w_in[:, offs[i]:offs[i + 1]]
    wq = jnp.concatenate([col(0), col(1), col(2), col(3), col(4), col(5), col(6), col(8)], axis=1).astype(BF16)
    gl = col(10)
    per_g = B_GROUP * N_BRANCH
    gpad = jnp.zeros((D, GATE_ROWS - per_g), w_in.dtype)
    wt = jnp.concatenate([col(7), col(9), gl[:, :per_g], gpad, gl[:, per_g:], gpad], axis=1).T.astype(BF16)
    cos, sa, sb = tables
    nq = wq.shape[1]
    nr = wt.shape[0]
    tok = lambda w: pl.BlockSpec((None, tm, w), lambda b, t: (b, t, 0))
    const = lambda shape: pl.BlockSpec(shape, lambda b, t: (0,) * len(shape))
    tab = pl.BlockSpec((tm, LANES), lambda b, t: (t, 0))
    out_shapes = (
        jax.ShapeDtypeStruct((B, T, A_WIDTH), BF16),
        jax.ShapeDtypeStruct((B, T, A_WIDTH), BF16),
        jax.ShapeDtypeStruct((B, T, A_WIDTH), BF16),
        jax.ShapeDtypeStruct((B, T, B_WIDTH), BF16),
        jax.ShapeDtypeStruct((B, B_KV_HEADS, T, LANES), BF16),
        jax.ShapeDtypeStruct((B, B_KV_HEADS, T, LANES), BF16),
        jax.ShapeDtypeStruct((B, T // NSA_CK, B_KV_HEADS * NSA_VROWS, NSA_CK), BF16),
        jax.ShapeDtypeStruct((B, T // NSA_CK, B_KV_HEADS * NSA_VROWS, NSA_CK), BF16),
        jax.ShapeDtypeStruct((B, T, 2 * KV_WIDTH), F32),
        jax.ShapeDtypeStruct((B, 2 * GATE_ROWS, T), F32),
    )
    frame = pl.BlockSpec((1, B_KV_HEADS, tm, LANES), lambda b, t: (b, 0, t, 0))
    vt = pl.BlockSpec((1, tm // NSA_CK, B_KV_HEADS * NSA_VROWS, NSA_CK), lambda b, t: (b, t, 0, 0))
    out_specs = (tok(A_WIDTH), tok(A_WIDTH), tok(A_WIDTH), tok(B_WIDTH), frame, frame, vt, vt,
                 tok(2 * KV_WIDTH), pl.BlockSpec((1, 2 * GATE_ROWS, tm), lambda b, t: (b, 0, t)))
    return pl.pallas_call(
        functools.partial(_in_proj_kernel, tm=tm, seq=T),
        grid=(B, nt),
        in_specs=[tok(D), const((1, D)), const((D, nq)), const((nr, D)), tab, tab, tab],
        out_specs=out_specs,
        out_shape=out_shapes,
        compiler_params=pltpu.CompilerParams(dimension_semantics=("parallel", "parallel"),
                                             vmem_limit_bytes=VMEM_LIMIT),
        name="in_proj",
    )(x, norm_g.reshape(1, D), wq, wt, cos, sa, sb)


def _compress_kernel(ak_ref, av_ref, w1k_ref, w1v_ref, pek_ref, pev_ref, w2k_ref, w2vt_ref, cos_ref, sa_ref, sb_ref,
                     kc_ref, vct_ref, *, nc_pad):
    half = CMP_LEN // 2
    hid_w = B_KV_HEADS * CMP_HIDDEN

    def hidden(a_ref, w1_ref, pe_ref):
        acc_u = jnp.zeros((nc_pad, hid_w), F32)
        acc_v = jnp.zeros((nc_pad, hid_w), F32)
        for p in range(half):
            ap = a_ref[0, pl.ds(p, nc_pad, stride=CMP_STRIDE), :]
            acc_u = acc_u + _dot((ap + pe_ref[p:p + 1, :]).astype(BF16), w1_ref[p])
            acc_v = acc_v + _dot((ap + pe_ref[half + p:half + p + 1, :]).astype(BF16), w1_ref[half + p])
        return jax.nn.gelu(acc_u + pltpu.roll(acc_v, nc_pad - 1, 0))

    hk = hidden(ak_ref, w1k_ref, pek_ref).astype(BF16)
    hv = hidden(av_ref, w1v_ref, pev_ref).astype(BF16)
    for g in range(B_KV_HEADS):
        hg = hk[:, g * CMP_HIDDEN:(g + 1) * CMP_HIDDEN]
        kc = _dot(hg, w2k_ref[...])
        kc_ref[0, g] = _rope_rows(kc, cos_ref[...], sa_ref[...], sb_ref[...]).astype(BF16)
        vg = hv[:, g * CMP_HIDDEN:(g + 1) * CMP_HIDDEN]
        vct_ref[0, g] = _dot_nt(w2vt_ref[...], vg).astype(BF16)


def _block_diag_w1(w1):
    w = w1.reshape(CMP_LEN, HEAD_DIM, CMP_HIDDEN)
    z = jnp.zeros_like(w)
    top = jnp.concatenate([w, z], axis=2)
    bot = jnp.concatenate([z, w], axis=2)
    return jnp.concatenate([top, bot], axis=1).astype(BF16)


def _compress(kcvc, pe_k, w1_k, w2_k, pe_v, w1_v, w2_v, cmp_tables):
    B, T, _ = kcvc.shape
    nc_pad = T // CMP_STRIDE
    w2k = jnp.concatenate([w2_k, jnp.zeros_like(w2_k)], axis=1).astype(BF16)
    w2vt = w2_v.T.astype(BF16)
    pek = jnp.tile(pe_k, (1, B_KV_HEADS))
    pev = jnp.tile(pe_v, (1, B_KV_HEADS))
    const = lambda shape: pl.BlockSpec(shape, lambda b: (0,) * len(shape))
    cos, sa, sb = cmp_tables
    return pl.pallas_call(
        functools.partial(_compress_kernel, nc_pad=nc_pad),
        grid=(B,),
        in_specs=[pl.BlockSpec((1, T, KV_WIDTH), lambda b: (b, 0, 0)), pl.BlockSpec((1, T, KV_WIDTH), lambda b: (b, 0, 1)),
                  const((CMP_LEN, LANES, B_KV_HEADS * CMP_HIDDEN)), const((CMP_LEN, LANES, B_KV_HEADS * CMP_HIDDEN)),
                  const((CMP_LEN, LANES)), const((CMP_LEN, LANES)),
                  const((CMP_HIDDEN, LANES)), const((HEAD_DIM, CMP_HIDDEN)),
                  const((nc_pad, LANES)), const((nc_pad, LANES)), const((nc_pad, LANES))],
        out_specs=(pl.BlockSpec((1, B_KV_HEADS, nc_pad, LANES), lambda b: (b, 0, 0, 0)),
                   pl.BlockSpec((1, B_KV_HEADS, HEAD_DIM, nc_pad), lambda b: (b, 0, 0, 0))),
        out_shape=(jax.ShapeDtypeStruct((B, B_KV_HEADS, nc_pad, LANES), BF16),
                   jax.ShapeDtypeStruct((B, B_KV_HEADS, HEAD_DIM, nc_pad), BF16)),
        compiler_params=pltpu.CompilerParams(dimension_semantics=("parallel",), vmem_limit_bytes=VMEM_LIMIT),
        name="compress",
    )(kcvc, kcvc, _block_diag_w1(w1_k), _block_diag_w1(w1_v), pek, pev, w2k, w2vt, cos, sa, sb)


def _mixer_a_kernel(q_ref, k_ref, v_ref, bias_ref, o_ref, nat, qd0, qd1, kd, vd0, vd1, u_s, m_s, l_s, *, seq):
    blk = A_BLOCK
    nres = seq // blk
    lane = lax.broadcasted_iota(jnp.int32, (blk, LANES), 1)
    lo = lane < HEAD_DIM
    for src, dsts in ((q_ref, (qd0, qd1)), (k_ref, (kd,)), (v_ref, (vd0, vd1))):
        nat[...] = src[0].astype(F32)
        for r in range(nres):
            rows = nat[pl.ds(r, blk, stride=nres), :]
            if src is k_ref:
                kd[pl.ds(r * blk, blk), :] = rows
            else:
                fill = 0.0 if src is q_ref else 1.0
                dsts[0][pl.ds(r * blk, blk), :] = jnp.where(lo, rows, fill)
                dsts[1][pl.ds(r * blk, blk), :] = jnp.where(lo, fill, rows)

    def pieces(dil, rd, row_off, rows):
        return [pl.ds(pl.multiple_of((rd + dil * jj) * blk + row_off, 8), rows) for jj in range(nres // dil)]

    def gather(ref, idx):
        parts = [ref[i, :] for i in idx]
        return parts[0] if len(parts) == 1 else jnp.concatenate(parts, axis=0)

    def attend(pi, dil, blocks):
        pr = blk // (nres // dil)
        q_idxs, vbs, scores = [], [], []
        for rd, n, first in blocks:
            q_idx = pieces(dil, rd, n * pr, pr)
            k_idx = q_idx if first else pieces(dil, rd, (n - 1) * pr, 2 * pr)
            bias = bias_ref[pi, :, 0:blk] if first else bias_ref[pi, :, blk:3 * blk]
            kb = gather(kd, k_idx).astype(BF16)
            q_idxs.append(q_idx)
            for qd, vd in ((qd0, vd0), (qd1, vd1)):
                vbs.append(gather(vd, k_idx).astype(BF16))
                scores.append(_dot_nt(gather(qd, q_idx).astype(BF16), kb) + bias)
        es, ms = [], []
        for s in scores:
            m = jnp.max(s, axis=-1, keepdims=True)
            ms.append(m)
            es.append(jnp.exp2((s - m).astype(BF16)))
        pvs = [_dot(e, vb) for e, vb in zip(es, vbs)]
        for b, q_idx in enumerate(q_idxs):
            u = jnp.where(lo, pvs[2 * b], pvs[2 * b + 1])
            l_swapped = jnp.where(lo, pvs[2 * b + 1], pvs[2 * b])
            m = jnp.where(lo, ms[2 * b], ms[2 * b + 1])
            for jj, idx in enumerate(q_idx):
                u_s[pi, idx, :] = u[jj * pr:(jj + 1) * pr]
                m_s[pi, idx, :] = m[jj * pr:(jj + 1) * pr]
                l_s[pi, idx, :] = l_swapped[jj * pr:(jj + 1) * pr]

    for pi, (_, dil) in enumerate(A_PATTERNS):
        nb = seq // dil // blk
        if nb <= A_GROUP:
            per = min(A_GROUP // nb, dil)

            def residues(i, carry, pi=pi, dil=dil, nb=nb, per=per):
                attend(pi, dil, [(i * per + jj, n, n == 0) for jj in range(per) for n in range(nb)])
                return carry

            lax.fori_loop(0, dil // per, residues, 0)
        else:
            group = A_GROUP

            def residue(rd, carry, pi=pi, dil=dil, nb=nb, group=group):
                attend(pi, dil, [(rd, 0, True)] + [(rd, n, False) for n in range(1, group)])

                def inner(i, c):
                    attend(pi, dil, [(rd, i * group + jj, False) for jj in range(group)])
                    return c

                return lax.fori_loop(1, nb // group, inner, carry)

            lax.fori_loop(0, dil, residue, 0)

    rows = 2 * blk

    def merge(c, carry):
        r0 = pl.multiple_of(c * rows, rows)
        ms = [m_s[p, pl.ds(r0, rows), :] for p in range(len(A_PATTERNS))]
        m_all = functools.reduce(jnp.maximum, ms)
        num = jnp.zeros((rows, LANES), F32)
        den = jnp.zeros((rows, LANES), F32)
        for p in range(len(A_PATTERNS)):
            a = jnp.exp2(ms[p] - m_all)
            num = num + a * u_s[p, pl.ds(r0, rows), :]
            den = den + a * pltpu.roll(l_s[p, pl.ds(r0, rows), :], HEAD_DIM, 1)
        out = num / den
        for jj in range(rows // blk):
            o_ref[0, pl.ds(c * (rows // blk) + jj, blk, stride=nres), :] = out[jj * blk:(jj + 1) * blk]
        return carry

    lax.fori_loop(0, seq // rows, merge, 0)


def _mixer_a_bias(nres):
    blk = A_BLOCK

    def sub_pos(i, fold, rows):
        return fold * (i % rows) + i // rows

    out = np.zeros((len(A_PATTERNS), blk, 3 * blk), np.float32)
    for pi, (window, dil) in enumerate(A_PATTERNS):
        n_back = window // dil
        fold = nres // dil
        pr = blk // fold
        sq = sub_pos(np.arange(blk), fold, pr)[:, None]
        d_first = sq - sub_pos(np.arange(blk), fold, pr)[None, :]
        d_band = sq + blk - sub_pos(np.arange(2 * blk), fold, 2 * pr)[None, :]
        dist = np.concatenate([d_first, d_band], axis=1)
        out[pi] = np.where((dist >= 0) & (dist <= n_back), 0.0, NEG)
    return jnp.asarray(out)


def _mixer_a(qa, ka, va):
    B, T, W = qa.shape
    npair = W // LANES
    spec = pl.BlockSpec((1, T, LANES), lambda b, p: (b, 0, p))
    npat = len(A_PATTERNS)
    nres = T // A_BLOCK
    assert all(nres % d == 0 and T % (d * A_BLOCK) == 0 for _, d in A_PATTERNS) and max(d for _, d in A_PATTERNS) == nres
    return pl.pallas_call(
        functools.partial(_mixer_a_kernel, seq=T),
        grid=(B, npair),
        in_specs=[spec, spec, spec, pl.BlockSpec((npat, A_BLOCK, 3 * A_BLOCK), lambda b, p: (0, 0, 0))],
        out_specs=spec,
        out_shape=jax.ShapeDtypeStruct((B, T, W), F32),
        scratch_shapes=[pltpu.VMEM((T, LANES), F32)] * 6 + [pltpu.VMEM((npat, T, LANES), F32)] * 3,
        compiler_params=pltpu.CompilerParams(dimension_semantics=("parallel", "parallel"),
                                             vmem_limit_bytes=VMEM_LIMIT),
        name="mixer_a",
    )(qa, ka, va, _mixer_a_bias(nres))


def _nsa_kernel(*refs, n_sel_blocks, nq):
    qi = pl.program_id(1)
    for k in range(nq):
        pl.when(qi == k)(functools.partial(_nsa_block, k, *refs, n_sel_blocks=n_sel_blocks))


def _nsa_block(qi, *refs, n_sel_blocks):
    groups = [_nsa_group(qi, g, *refs, n_sel_blocks=n_sel_blocks) for g in range(B_KV_HEADS)]
    while groups:
        groups = [gen for gen in groups if next(gen, True) is None]


def _nsa_group(qi, g, q_ref, ks_ref, kw_ref, vst_ref, vwt_ref, kc_ref, vct_ref, gt_ref, ovt_ref, bias_ref, o_ref,
               *, n_sel_blocks):
    tq, ck = NSA_TQ, NSA_CK
    t0 = qi * tq
    nrow = B_GROUP * tq
    gw = B_GROUP * HEAD_DIM

    q = q_ref[0, :, g * gw:(g + 1) * gw].astype(F32)
    lane = lax.broadcasted_iota(jnp.int32, (tq, LANES), 1)
    lo = lane < HEAD_DIM
    frames = []
    for r in range(B_GROUP):
        ch = q[:, (r // 2) * LANES:(r // 2 + 1) * LANES]
        if r % 2:
            ch = pltpu.roll(ch, HEAD_DIM, 1)
        frames.append(jnp.where(lo, ch, 0.0))
    qs = jnp.concatenate(frames, axis=0)

    q_b = qs.astype(BF16)
    v_rows = slice(g * NSA_VROWS, (g + 1) * NSA_VROWS)

    def scores(qmat, k_ref, c0, c1, first_bias):
        s = _dot_nt(k_ref[0, g, c0 * ck:(c1 + 1) * ck, :], qmat)
        pieces = [s[n * ck:(n + 1) * ck] for n in range(c1 - c0 + 1)]
        if c1 == qi:
            pieces[-1] = pieces[-1] + bias_ref[0]
        if first_bias is not None:
            pieces[0] = pieces[0] + first_bias
        return pieces[0] if len(pieces) == 1 else jnp.concatenate(pieces, axis=0)

    def softmax_pv(s, vt_ref, c0, c1, state):
        vts = [vt_ref[0, c, v_rows, :] for c in range(c0, c1 + 1)]
        m_new = jnp.max(s, axis=0, keepdims=True)
        if state is not None:
            m_new = jnp.maximum(state[0], m_new)
        p = jnp.exp2(s - m_new).astype(BF16)
        acc = _dot(vts[0] if len(vts) == 1 else jnp.concatenate(vts, axis=1), p)
        if state is not None:
            acc = jnp.exp2(state[0] - m_new) * state[1] + acc
        return m_new, acc

    def normalised(state):
        return state[1][0:HEAD_DIM] * (1.0 / state[1][HEAD_DIM:HEAD_DIM + 1])

    nwin = WIN // ck
    w0 = max(qi - nwin, 0)
    s_win = scores(q_b, kw_ref, w0, qi, bias_ref[1] if qi >= nwin else None)
    yield

    nc_pad = kc_ref.shape[2]
    n_vis = min(nc_pad, -(-((t0 + tq - CMP_LEN) // CMP_STRIDE + 1) // 16) * 16)
    n_all = max(0, ((t0 - CMP_LEN + 1) // CMP_STRIDE + 1) // 8 * 8) if t0 >= CMP_LEN else 0
    sc = _dot_nt(kc_ref[0, g, 0:n_vis, :], q_b)
    yield
    c_end = (lax.broadcasted_iota(jnp.int32, (n_vis - n_all, nrow), 0) + n_all) * CMP_STRIDE + (CMP_LEN - 1)
    t_col = t0 + (lax.broadcasted_iota(jnp.int32, (n_vis - n_all, nrow), 1) & (tq - 1))
    tail = jnp.where(c_end <= t_col, sc[n_all:], NEG)
    sc = tail if n_all == 0 else jnp.concatenate([sc[:n_all], tail], axis=0)
    m = jnp.max(sc, axis=0, keepdims=True)
    e = jnp.exp2(sc - m)
    if t0 < CMP_LEN - 1:
        e = jnp.where(t_col[0:1] >= CMP_LEN - 1, e, 0.0)
    den = jnp.sum(e, axis=0, keepdims=True)
    p_cmp = e * (1.0 / jnp.maximum(den, 1e-30))
    if n_vis < nc_pad:
        p_cmp = jnp.concatenate([p_cmp, jnp.zeros((nc_pad - n_vis, nrow), F32)], axis=0)
    o_cmp = _dot(vct_ref[0, g], p_cmp.astype(BF16))
    yield

    psum = p_cmp[:, 0:tq]
    for r in range(1, B_GROUP):
        psum = psum + p_cmp[:, r * tq:(r + 1) * tq]
    p_hi = psum.astype(BF16)
    p_lo = (psum - p_hi.astype(F32)).astype(BF16)
    imp = _dot(ovt_ref[...], p_hi) + _dot(ovt_ref[...], p_lo)
    yield
    j = lax.broadcasted_iota(jnp.int32, (n_sel_blocks, tq), 0)
    cur = (t0 + lax.broadcasted_iota(jnp.int32, (n_sel_blocks, tq), 1)) >> SEL_SHIFT
    forced = (j == 0) | (j == cur) | (j == cur - 1)
    low = -3e38
    score = jnp.where(forced, imp + 2.0, jnp.where(j > cur, -1.0, imp))
    sel = jnp.zeros((n_sel_blocks, tq), jnp.bool_)
    j_f = j.astype(F32)
    for _ in range(min(SEL_TOPK, n_sel_blocks)):
        mx = jnp.max(score, axis=0, keepdims=True)
        first = jnp.min(jnp.where(score == mx, j_f, 4.0 * LANES), axis=0, keepdims=True)
        hit = j_f == first
        sel = sel | hit
        score = jnp.where(hit, low, score)
    selneg = jnp.concatenate([jnp.zeros((SEL_LANE0, tq), F32), jnp.where(sel, 0.0, NEG),
                              jnp.zeros((LANES - SEL_LANE0 - n_sel_blocks, tq), F32)], axis=0).T
    q_aug = (qs + jnp.concatenate([selneg] * B_GROUP, axis=0)).astype(BF16)

    steps = [(c, min(c + NSA_STEP - 1, qi)) for c in range(0, qi + 1, NSA_STEP)]
    s_next = scores(q_aug, ks_ref, *steps[0], None)
    yield
    o_win = normalised(softmax_pv(s_win, vwt_ref, w0, qi, None))
    yield
    state = None
    for i, (c0, c1) in enumerate(steps):
        s_cur = s_next
        if i + 1 < len(steps):
            s_next = scores(q_aug, ks_ref, *steps[i + 1], None)
            yield
        state = softmax_pv(s_cur, vst_ref, c0, c1, state)
        yield
    o_sel = normalised(state)

    gt = gt_ref[0, g * GATE_ROWS:(g + 1) * GATE_ROWS, :]
    outs = []
    for r in range(B_GROUP):
        cs = slice(r * tq, (r + 1) * tq)
        outs.append(gt[r * N_BRANCH:r * N_BRANCH + 1, :] * o_cmp[:, cs]
                    + gt[r * N_BRANCH + 1:r * N_BRANCH + 2, :] * o_sel[:, cs]
                    + gt[r * N_BRANCH + 2:r * N_BRANCH + 3, :] * o_win[:, cs])
    for c in range(B_GROUP // 2):
        pair = jnp.concatenate([outs[2 * c], outs[2 * c + 1]], axis=0)
        o_ref[0, :, g * gw + c * LANES:g * gw + (c + 1) * LANES] = pair.T


def _overlap_t(nc_pad, ns):
    nc = nc_pad - 1
    c0 = jnp.arange(nc_pad) * CMP_STRIDE
    s0 = jnp.arange(ns) * SEL_BLOCK
    ov = jnp.minimum(c0[None, :] + CMP_LEN, s0[:, None] + SEL_BLOCK) - jnp.maximum(c0[None, :], s0[:, None])
    ov = jnp.clip(ov, 0, None).astype(F32) / CMP_LEN
    ov = jnp.where(jnp.arange(nc_pad)[None, :] < nc, ov, 0.0)
    return ov.astype(BF16)


def _nsa(qb, ks, kw, vst, vwt, kc, vct, gt):
    B, T, W = qb.shape
    nq = T // NSA_TQ
    ns = T // SEL_BLOCK
    nc_pad = kc.shape[2]
    kspec = pl.BlockSpec((1, B_KV_HEADS, T, LANES), lambda b, i: (b, 0, 0, 0))
    assert NSA_TQ == NSA_CK and WIN % NSA_CK == 0 and SEL_LANE0 + ns <= LANES
    vspec = pl.BlockSpec((1, T // NSA_CK, B_KV_HEADS * NSA_VROWS, NSA_CK), lambda b, i: (b, 0, 0, 0))
    key = np.arange(NSA_CK)[:, None]
    qry = (np.arange(B_GROUP * NSA_TQ) % NSA_TQ)[None, :]
    bias = jnp.asarray(np.stack([np.where(key <= qry, 0.0, NEG), np.where(key > qry, 0.0, NEG)]).astype(np.float32))
    return pl.pallas_call(
        functools.partial(_nsa_kernel, n_sel_blocks=ns, nq=nq),
        grid=(B, nq),
        in_specs=[pl.BlockSpec((1, NSA_TQ, W), lambda b, i: (b, i, 0)),
                  kspec, kspec, vspec, vspec,
                  pl.BlockSpec((1, B_KV_HEADS, nc_pad, LANES), lambda b, i: (b, 0, 0, 0)),
                  pl.BlockSpec((1, B_KV_HEADS, HEAD_DIM, nc_pad), lambda b, i: (b, 0, 0, 0)),
                  pl.BlockSpec((1, B_KV_HEADS * GATE_ROWS, NSA_TQ), lambda b, i: (b, 0, i)),
                  pl.BlockSpec((ns, nc_pad), lambda b, i: (0, 0)),
                  pl.BlockSpec((2, NSA_CK, B_GROUP * NSA_TQ), lambda b, i: (0, 0, 0))],
        out_specs=pl.BlockSpec((1, NSA_TQ, W), lambda b, i: (b, i, 0)),
        out_shape=jax.ShapeDtypeStruct((B, T, W), F32),
        compiler_params=pltpu.CompilerParams(dimension_semantics=("parallel", "arbitrary"),
                                             vmem_limit_bytes=VMEM_LIMIT),
        name="nsa",
    )(qb, ks, kw, vst, vwt, kc, vct, gt, _overlap_t(nc_pad, ns), bias)


def _post_kernel(x_ref, oa_ref, ob_ref, ga_ref, gb_ref, wo_ref, gm_ref, wu_ref, wd_ref, gf_ref, o_ref, *, final):
    def norm(v, g):
        return v * lax.rsqrt(jnp.mean(v * v, axis=-1, keepdims=True) + EPS) * g

    na = norm(oa_ref[...], ga_ref[...]).astype(BF16)
    nb = norm(ob_ref[...], gb_ref[...]).astype(BF16)
    aw = na.shape[1]
    h_res = x_ref[...] + _dot(na, wo_ref[0:aw, :]) + _dot(nb, wo_ref[aw:, :])
    h = norm(h_res, gm_ref[...]).astype(BF16)
    u = jnp.square(jnp.maximum(_dot(h, wu_ref[...]), 0.0)).astype(BF16)
    acc = h_res + _dot(u, wd_ref[...])
    o_ref[...] = norm(acc, gf_ref[...]) if final else acc


def _post(x, oa, ob, g_a, g_b, w_out, g_mlp, w_up, w_down, g_final, *, final, tm=256):
    B, T, D = x.shape
    n = B * T
    dff = w_up.shape[1]
    tok = lambda w: pl.BlockSpec((tm, w), lambda i: (i, 0))
    const = lambda shape: pl.BlockSpec(shape, lambda i: (0, 0), pipeline_mode=pl.Buffered(1))
    out = pl.pallas_call(
        functools.partial(_post_kernel, final=final),
        grid=(n // tm,),
        in_specs=[tok(D), tok(A_WIDTH), tok(B_WIDTH), const((1, A_WIDTH)), const((1, B_WIDTH)),
                  const((A_WIDTH + B_WIDTH, D)), const((1, D)), const((D, dff)), const((dff, D)), const((1, D))],
        out_specs=tok(D),
        out_shape=jax.ShapeDtypeStruct((n, D), F32),
        compiler_params=pltpu.CompilerParams(dimension_semantics=("parallel",), vmem_limit_bytes=VMEM_LIMIT),
        name="post",
    )(x.reshape(n, D), oa.reshape(n, A_WIDTH), ob.reshape(n, B_WIDTH), g_a.reshape(1, -1), g_b.reshape(1, -1),
      w_out.astype(BF16), g_mlp.reshape(1, D), w_up.astype(BF16), w_down.astype(BF16), g_final.reshape(1, D))
    return out.reshape(B, T, D)


def kernel(x, norm_mix, w_in, cmp_pe_k, cmp_w1_k, cmp_w2_k, cmp_pe_v, cmp_w1_v, cmp_w2_v, g_out_a, g_out_b,
           w_out, norm_mlp, w_up, w_down, norm_final):
    B, T, D = x.shape
    depth = w_in.shape[0]
    tables = _rope_tables(jnp.arange(T))
    cmp_tables = _rope_tables(jnp.arange(T // CMP_STRIDE) * CMP_STRIDE + CMP_LEN - 1)
    h_res = x
    for l in range(depth):
        qa, ka, va, qb, ks, kw, vst, vwt, kcvc, gt = _in_proj(h_res, norm_mix[l], w_in[l], tables)
        kc, vct = _compress(kcvc, cmp_pe_k[l], cmp_w1_k[l], cmp_w2_k[l], cmp_pe_v[l], cmp_w1_v[l], cmp_w2_v[l],
                            cmp_tables)
        oa = _mixer_a(qa, ka, va)
        ob = _nsa(qb, ks, kw, vst, vwt, kc, vct, gt)
        h_res = _post(h_res, oa, ob, g_out_a[l], g_out_b[l], w_out[l], norm_mlp[l], w_up[l], w_down[l], norm_final,
                      final=(l == depth - 1))
    return h_res
```

```python
import functools

import jax
import jax.numpy as jnp
import numpy as np
from jax import lax
from jax.experimental import pallas as pl
from jax.experimental.pallas import tpu as pltpu

F32 = jnp.float32
BF16 = jnp.bfloat16

HEAD_DIM = 64
ROT_DIM = HEAD_DIM // 4
ROPE_THETA = 500000.0
EPS = 1e-6
NEG = -1e30
Q_SCALE = HEAD_DIM ** -0.5 * 1.4426950408889634
LANES = 128

A_HEADS = 8
A_PATTERNS = ((128, 1), (512, 4), (2048, 16))
A_BLOCK = 128
A_GROUP = 8

B_HEADS = 8
B_KV_HEADS = 2
B_GROUP = B_HEADS // B_KV_HEADS
CMP_LEN = 32
CMP_STRIDE = 16
CMP_HIDDEN = 256
SEL_BLOCK = 64
SEL_SHIFT = 6
SEL_TOPK = 8
WIN = 512
N_BRANCH = 3

A_WIDTH = A_HEADS * HEAD_DIM
B_WIDTH = B_HEADS * HEAD_DIM
KV_WIDTH = B_KV_HEADS * HEAD_DIM

NSA_TQ = 256
NSA_CK = 256
NSA_VROWS = 80
GATE_ROWS = 16
SEL_LANE0 = HEAD_DIM

VMEM_LIMIT = 56 * 1024 * 1024


def _dot(a, b):
    return jnp.dot(a, b, preferred_element_type=F32)


def _dot_nt(a, b):
    return lax.dot_general(a, b, (((1,), (1,)), ((), ())), preferred_element_type=F32)


def _rope_rows(y, cos, sin_a, sin_b):
    outs = []
    for c in range(y.shape[1] // LANES):
        yc = y[:, c * LANES:(c + 1) * LANES]
        outs.append(yc * cos + pltpu.roll(yc, LANES - ROT_DIM // 2, 1) * sin_a
                    + pltpu.roll(yc, ROT_DIM // 2, 1) * sin_b)
    return outs[0] if len(outs) == 1 else jnp.concatenate(outs, axis=1)


def _in_proj_kernel(x_ref, g_ref, wq_ref, wt_ref, cos_ref, sa_ref, sb_ref,
                    qa_ref, ka_ref, va_ref, qb_ref, ks_ref, kw_ref, vst_ref, vwt_ref, kcvc_ref, gt_ref,
                    *, tm, seq):
    tt = pl.program_id(1)
    x = x_ref[...]
    ms = jnp.mean(x * x, axis=-1, keepdims=True)
    h = (x * lax.rsqrt(ms + EPS) * g_ref[...]).astype(BF16)
    cos, sa, sb = cos_ref[...], sa_ref[...], sb_ref[...]
    scale = Q_SCALE

    def proj(c0, c1):
        return _dot(h, wq_ref[:, c0:c1])

    o = 0
    qa_ref[...] = (_rope_rows(proj(o, o + A_WIDTH), cos, sa, sb) * scale).astype(BF16)
    o += A_WIDTH
    ka_ref[...] = _rope_rows(proj(o, o + A_WIDTH), cos, sa, sb).astype(BF16)
    o += A_WIDTH
    va_ref[...] = proj(o, o + A_WIDTH).astype(BF16)
    o += A_WIDTH
    qb_ref[...] = (_rope_rows(proj(o, o + B_WIDTH), cos, sa, sb) * scale).astype(BF16)
    o += B_WIDTH
    kcvc_ref[...] = proj(o, o + 2 * KV_WIDTH)
    o += 2 * KV_WIDTH
    ksw = _rope_rows(proj(o, o + 2 * KV_WIDTH), cos, sa, sb)

    lane = lax.broadcasted_iota(jnp.int32, (tm, LANES), 1)
    row = lax.broadcasted_iota(jnp.int32, (tm, LANES), 0)
    lo = lane < HEAD_DIM
    blk = (tt * tm + row) >> SEL_SHIFT
    onehot = jnp.where(lane - SEL_LANE0 == blk, 1.0, 0.0)
    for kind, ref in ((0, ks_ref), (1, kw_ref)):
        kk = ksw[:, kind * LANES:(kind + 1) * LANES]
        tail = onehot if kind == 0 else 0.0
        ref[0, 0] = jnp.where(lo, kk, tail).astype(BF16)
        ref[0, 1] = jnp.where(lo, pltpu.roll(kk, HEAD_DIM, 1), tail).astype(BF16)

    tr = _dot_nt(wt_ref[...], h)
    ones_row = jnp.where(lax.broadcasted_iota(jnp.int32, (NSA_VROWS - HEAD_DIM, tm), 0) == 0, 1.0, 0.0)
    for kind, ref in ((0, vst_ref), (1, vwt_ref)):
        rows = [tr[kind * LANES + gg * HEAD_DIM:kind * LANES + (gg + 1) * HEAD_DIM] for gg in range(B_KV_HEADS)]
        slab = jnp.concatenate([rows[0], ones_row, rows[1], ones_row], axis=0).astype(BF16)
        for c in range(tm // NSA_CK):
            ref[0, c] = slab[:, c * NSA_CK:(c + 1) * NSA_CK]
    gt_ref[0] = jax.nn.sigmoid(tr[2 * LANES:2 * LANES + 2 * GATE_ROWS, :])


def _rope_tables(pos):
    half = ROT_DIM // 2
    inv = ROPE_THETA ** (-jnp.arange(0, ROT_DIM, 2, dtype=F32) / ROT_DIM)
    ang = pos.astype(F32)[:, None] * inv[None, :]
    cos, sin = jnp.cos(ang), jnp.sin(ang)
    n = pos.shape[0]
    ones = jnp.ones((n, HEAD_DIM - ROT_DIM), F32)
    zeros = jnp.zeros((n, HEAD_DIM - ROT_DIM), F32)
    zh = jnp.zeros((n, half), F32)
    c_head = jnp.concatenate([cos, cos, ones], axis=1)
    a_head = jnp.concatenate([-sin, zh, zeros], axis=1)
    b_head = jnp.concatenate([zh, sin, zeros], axis=1)
    rep = LANES // HEAD_DIM
    return jnp.tile(c_head, (1, rep)), jnp.tile(a_head, (1, rep)), jnp.tile(b_head, (1, rep))


def _in_proj(x, norm_g, w_in, tables, *, tm=512):
    B, T, D = x.shape
    nt = T // tm
    offs = [0]
    for n in (A_WIDTH, A_WIDTH, A_WIDTH, B_WIDTH, KV_WIDTH, KV_WIDTH, KV_WIDTH, KV_WIDTH, KV_WIDTH, KV_WIDTH,
              B_HEADS * N_BRANCH):
        offs.append(offs[-1] + n)
    col = lambda i: w_in[:, offs[i]:offs[i + 1]]
    wq = jnp.concatenate([col(0), col(1), col(2), col(3), col(4), col(5), col(6), col(8)], axis=1).astype(BF16)
    gl = col(10)
    per_g = B_GROUP * N_BRANCH
    gpad = jnp.zeros((D, GATE_ROWS - per_g), w_in.dtype)
    wt = jnp.concatenate([col(7), col(9), gl[:, :per_g], gpad, gl[:, per_g:], gpad], axis=1).T.astype(BF16)
    cos, sa, sb = tables
    nq = wq.shape[1]
    nr = wt.shape[0]
    tok = lambda w: pl.BlockSpec((None, tm, w), lambda b, t: (b, t, 0))
    const = lambda shape: pl.BlockSpec(shape, lambda b, t: (0,) * len(shape))
    tab = pl.BlockSpec((tm, LANES), lambda b, t: (t, 0))
    out_shapes = (
        jax.ShapeDtypeStruct((B, T, A_WIDTH), BF16),
        jax.ShapeDtypeStruct((B, T, A_WIDTH), BF16),
        jax.ShapeDtypeStruct((B, T, A_WIDTH), BF16),
        jax.ShapeDtypeStruct((B, T, B_WIDTH), BF16),
        jax.ShapeDtypeStruct((B, B_KV_HEADS, T, LANES), BF16),
        jax.ShapeDtypeStruct((B, B_KV_HEADS, T, LANES), BF16),
        jax.ShapeDtypeStruct((B, T // NSA_CK, B_KV_HEADS * NSA_VROWS, NSA_CK), BF16),
        jax.ShapeDtypeStruct((B, T // NSA_CK, B_KV_HEADS * NSA_VROWS, NSA_CK), BF16),
        jax.ShapeDtypeStruct((B, T, 2 * KV_WIDTH), F32),
        jax.ShapeDtypeStruct((B, 2 * GATE_ROWS, T), F32),
    )
    frame = pl.BlockSpec((1, B_KV_HEADS, tm, LANES), lambda b, t: (b, 0, t, 0))
    vt = pl.BlockSpec((1, tm // NSA_CK, B_KV_HEADS * NSA_VROWS, NSA_CK), lambda b, t: (b, t, 0, 0))
    out_specs = (tok(A_WIDTH), tok(A_WIDTH), tok(A_WIDTH), tok(B_WIDTH), frame, frame, vt, vt,
                 tok(2 * KV_WIDTH), pl.BlockSpec((1, 2 * GATE_ROWS, tm), lambda b, t: (b, 0, t)))
    return pl.pallas_call(
        functools.partial(_in_proj_kernel, tm=tm, seq=T),
        grid=(B, nt),
        in_specs=[tok(D), const((1, D)), const((D, nq)), const((nr, D)), tab, tab, tab],
        out_specs=out_specs,
        out_shape=out_shapes,
        compiler_params=pltpu.CompilerParams(dimension_semantics=("parallel", "parallel"),
                                             vmem_limit_bytes=VMEM_LIMIT),
        name="in_proj",
    )(x, norm_g.reshape(1, D), wq, wt, cos, sa, sb)


def _compress_kernel(ak_ref, av_ref, w1k_ref, w1v_ref, pek_ref, pev_ref, w2k_ref, w2vt_ref, cos_ref, sa_ref, sb_ref,
                     kc_ref, vct_ref, *, nc_pad):
    half = CMP_LEN // 2
    hid_w = B_KV_HEADS * CMP_HIDDEN

    def hidden(a_ref, w1_ref, pe_ref):
        acc_u = jnp.zeros((nc_pad, hid_w), F32)
        acc_v = jnp.zeros((nc_pad, hid_w), F32)
        for p in range(half):
            ap = a_ref[0, pl.ds(p, nc_pad, stride=CMP_STRIDE), :]
            acc_u = acc_u + _dot((ap + pe_ref[p:p + 1, :]).astype(BF16), w1_ref[p])
            acc_v = acc_v + _dot((ap + pe_ref[half + p:half + p + 1, :]).astype(BF16), w1_ref[half + p])
        return jax.nn.gelu(acc_u + pltpu.roll(acc_v, nc_pad - 1, 0))

    hk = hidden(ak_ref, w1k_ref, pek_ref).astype(BF16)
    hv = hidden(av_ref, w1v_ref, pev_ref).astype(BF16)
    for g in range(B_KV_HEADS):
        hg = hk[:, g * CMP_HIDDEN:(g + 1) * CMP_HIDDEN]
        kc = _dot(hg, w2k_ref[...])
        kc_ref[0, g] = _rope_rows(kc, cos_ref[...], sa_ref[...], sb_ref[...]).astype(BF16)
        vg = hv[:, g * CMP_HIDDEN:(g + 1) * CMP_HIDDEN]
        vct_ref[0, g] = _dot_nt(w2vt_ref[...], vg).astype(BF16)


def _block_diag_w1(w1):
    w = w1.reshape(CMP_LEN, HEAD_DIM, CMP_HIDDEN)
    z = jnp.zeros_like(w)
    top = jnp.concatenate([w, z], axis=2)
    bot = jnp.concatenate([z, w], axis=2)
    return jnp.concatenate([top, bot], axis=1).astype(BF16)


def _compress(kcvc, pe_k, w1_k, w2_k, pe_v, w1_v, w2_v, cmp_tables):
    B, T, _ = kcvc.shape
    nc_pad = T // CMP_STRIDE
    w2k = jnp.concatenate([w2_k, jnp.zeros_like(w2_k)], axis=1).astype(BF16)
    w2vt = w2_v.T.astype(BF16)
    pek = jnp.tile(pe_k, (1, B_KV_HEADS))
    pev = jnp.tile(pe_v, (1, B_KV_HEADS))
    const = lambda shape: pl.BlockSpec(shape, lambda b: (0,) * len(shape))
    cos, sa, sb = cmp_tables
    return pl.pallas_call(
        functools.partial(_compress_kernel, nc_pad=nc_pad),
        grid=(B,),
        in_specs=[pl.BlockSpec((1, T, KV_WIDTH), lambda b: (b, 0, 0)), pl.BlockSpec((1, T, KV_WIDTH), lambda b: (b, 0, 1)),
                  const((CMP_LEN, LANES, B_KV_HEADS * CMP_HIDDEN)), const((CMP_LEN, LANES, B_KV_HEADS * CMP_HIDDEN)),
                  const((CMP_LEN, LANES)), const((CMP_LEN, LANES)),
                  const((CMP_HIDDEN, LANES)), const((HEAD_DIM, CMP_HIDDEN)),
                  const((nc_pad, LANES)), const((nc_pad, LANES)), const((nc_pad, LANES))],
        out_specs=(pl.BlockSpec((1, B_KV_HEADS, nc_pad, LANES), lambda b: (b, 0, 0, 0)),
                   pl.BlockSpec((1, B_KV_HEADS, HEAD_DIM, nc_pad), lambda b: (b, 0, 0, 0))),
        out_shape=(jax.ShapeDtypeStruct((B, B_KV_HEADS, nc_pad, LANES), BF16),
                   jax.ShapeDtypeStruct((B, B_KV_HEADS, HEAD_DIM, nc_pad), BF16)),
        compiler_params=pltpu.CompilerParams(dimension_semantics=("parallel",), vmem_limit_bytes=VMEM_LIMIT),
        name="compress",
    )(kcvc, kcvc, _block_diag_w1(w1_k), _block_diag_w1(w1_v), pek, pev, w2k, w2vt, cos, sa, sb)


def _mixer_a_kernel(q_ref, k_ref, v_ref, bias_ref, o_ref, nat, qd0, qd1, kd, vd0, vd1, u_s, m_s, l_s, *, seq):
    blk = A_BLOCK
    nres = seq // blk
    lane = lax.broadcasted_iota(jnp.int32, (blk, LANES), 1)
    lo = lane < HEAD_DIM
    for src, dsts in ((q_ref, (qd0, qd1)), (k_ref, (kd,)), (v_ref, (vd0, vd1))):
        nat[...] = src[0].astype(F32)
        for r in range(nres):
            rows = nat[pl.ds(r, blk, stride=nres), :]
            if src is k_ref:
                kd[pl.ds(r * blk, blk), :] = rows
            else:
                fill = 0.0 if src is q_ref else 1.0
                dsts[0][pl.ds(r * blk, blk), :] = jnp.where(lo, rows, fill)
                dsts[1][pl.ds(r * blk, blk), :] = jnp.where(lo, fill, rows)

    def pieces(dil, rd, row_off, rows):
        return [pl.ds(pl.multiple_of((rd + dil * jj) * blk + row_off, 8), rows) for jj in range(nres // dil)]

    def gather(ref, idx):
        parts = [ref[i, :] for i in idx]
        return parts[0] if len(parts) == 1 else jnp.concatenate(parts, axis=0)

    def attend(pi, dil, blocks):
        pr = blk // (nres // dil)
        q_idxs, vbs, scores = [], [], []
        for rd, n, first in blocks:
            q_idx = pieces(dil, rd, n * pr, pr)
            k_idx = q_idx if first else pieces(dil, rd, (n - 1) * pr, 2 * pr)
            bias = bias_ref[pi, :, 0:blk] if first else bias_ref[pi, :, blk:3 * blk]
            kb = gather(kd, k_idx).astype(BF16)
            q_idxs.append(q_idx)
            for qd, vd in ((qd0, vd0), (qd1, vd1)):
                vbs.append(gather(vd, k_idx).astype(BF16))
                scores.append(_dot_nt(gather(qd, q_idx).astype(BF16), kb) + bias)
        es, ms = [], []
        for s in scores:
            m = jnp.max(s, axis=-1, keepdims=True)
            ms.append(m)
            es.append(jnp.exp2((s - m).astype(BF16)))
        pvs = [_dot(e, vb) for e, vb in zip(es, vbs)]
        for b, q_idx in enumerate(q_idxs):
            u = jnp.where(lo, pvs[2 * b], pvs[2 * b + 1])
            l_swapped = jnp.where(lo, pvs[2 * b + 1], pvs[2 * b])
            m = jnp.where(lo, ms[2 * b], ms[2 * b + 1])
            for jj, idx in enumerate(q_idx):
                u_s[pi, idx, :] = u[jj * pr:(jj + 1) * pr]
                m_s[pi, idx, :] = m[jj * pr:(jj + 1) * pr]
                l_s[pi, idx, :] = l_swapped[jj * pr:(jj + 1) * pr]

    for pi, (_, dil) in enumerate(A_PATTERNS):
        nb = seq // dil // blk
        if nb <= A_GROUP:
            per = min(A_GROUP // nb, dil)

            def residues(i, carry, pi=pi, dil=dil, nb=nb, per=per):
                attend(pi, dil, [(i * per + jj, n, n == 0) for jj in range(per) for n in range(nb)])
                return carry

            lax.fori_loop(0, dil // per, residues, 0)
        else:
            group = A_GROUP

            def residue(rd, carry, pi=pi, dil=dil, nb=nb, group=group):
                attend(pi, dil, [(rd, 0, True)] + [(rd, n, False) for n in range(1, group)])

                def inner(i, c):
                    attend(pi, dil, [(rd, i * group + jj, False) for jj in range(group)])
                    return c

                return lax.fori_loop(1, nb // group, inner, carry)

            lax.fori_loop(0, dil, residue, 0)

    rows = 2 * blk

    def merge(c, carry):
        r0 = pl.multiple_of(c * rows, rows)
        ms = [m_s[p, pl.ds(r0, rows), :] for p in range(len(A_PATTERNS))]
        m_all = functools.reduce(jnp.maximum, ms)
        num = jnp.zeros((rows, LANES), F32)
        den = jnp.zeros((rows, LANES), F32)
        for p in range(len(A_PATTERNS)):
            a = jnp.exp2(ms[p] - m_all)
            num = num + a * u_s[p, pl.ds(r0, rows), :]
            den = den + a * pltpu.roll(l_s[p, pl.ds(r0, rows), :], HEAD_DIM, 1)
        out = num / den
        for jj in range(rows // blk):
            o_ref[0, pl.ds(c * (rows // blk) + jj, blk, stride=nres), :] = out[jj * blk:(jj + 1) * blk]
        return carry

    lax.fori_loop(0, seq // rows, merge, 0)


def _mixer_a_bias(nres):
    blk = A_BLOCK

    def sub_pos(i, fold, rows):
        return fold * (i % rows) + i // rows

    out = np.zeros((len(A_PATTERNS), blk, 3 * blk), np.float32)
    for pi, (window, dil) in enumerate(A_PATTERNS):
        n_back = window // dil
        fold = nres // dil
        pr = blk // fold
        sq = sub_pos(np.arange(blk), fold, pr)[:, None]
        d_first = sq - sub_pos(np.arange(blk), fold, pr)[None, :]
        d_band = sq + blk - sub_pos(np.arange(2 * blk), fold, 2 * pr)[None, :]
        dist = np.concatenate([d_first, d_band], axis=1)
        out[pi] = np.where((dist >= 0) & (dist <= n_back), 0.0, NEG)
    return jnp.asarray(out)


def _mixer_a(qa, ka, va):
    B, T, W = qa.shape
    npair = W // LANES
    spec = pl.BlockSpec((1, T, LANES), lambda b, p: (b, 0, p))
    npat = len(A_PATTERNS)
    nres = T // A_BLOCK
    assert all(nres % d == 0 and T % (d * A_BLOCK) == 0 for _, d in A_PATTERNS) and max(d for _, d in A_PATTERNS) == nres
    return pl.pallas_call(
        functools.partial(_mixer_a_kernel, seq=T),
        grid=(B, npair),
        in_specs=[spec, spec, spec, pl.BlockSpec((npat, A_BLOCK, 3 * A_BLOCK), lambda b, p: (0, 0, 0))],
        out_specs=spec,
        out_shape=jax.ShapeDtypeStruct((B, T, W), F32),
        scratch_shapes=[pltpu.VMEM((T, LANES), F32)] * 6 + [pltpu.VMEM((npat, T, LANES), F32)] * 3,
        compiler_params=pltpu.CompilerParams(dimension_semantics=("parallel", "parallel"),
                                             vmem_limit_bytes=VMEM_LIMIT),
        name="mixer_a",
    )(qa, ka, va, _mixer_a_bias(nres))


def _nsa_kernel(q_ref, ks_ref, kw_ref, vst_ref, vwt_ref, kc_ref, vct_ref, gt_ref, ovt_ref, bias_ref, cbias_ref, o_ref,
                *, n_sel_blocks):
    tq, ck = NSA_TQ, NSA_CK
    qi = pl.program_id(1)
    t0 = qi * tq
    nrow = B_GROUP * tq
    gw = B_GROUP * HEAD_DIM
    groups = range(B_KV_HEADS)
    lane = lax.broadcasted_iota(jnp.int32, (tq, LANES), 1)
    lo = lane < HEAD_DIM
    t_row = t0 + (lax.broadcasted_iota(jnp.int32, (1, nrow), 1) & (tq - 1))

    def chunk(ref, g, c):
        return ref[0, g, pl.ds(pl.multiple_of(c * ck, ck), ck), :]

    def vt_chunk(ref, g, c):
        return ref[0, c, g * NSA_VROWS:(g + 1) * NSA_VROWS, :]

    def softmax_pv(s, vt, state):
        m_new = jnp.max(s, axis=0, keepdims=True)
        if state is not None:
            m_new = jnp.maximum(state[0], m_new)
        p = jnp.exp2(s - m_new).astype(BF16)
        acc = _dot(vt, p)
        if state is not None:
            acc = jnp.exp2(state[0] - m_new) * state[1] + acc
        return m_new, acc

    def normalised(state):
        return state[1][0:HEAD_DIM] * (1.0 / state[1][HEAD_DIM:HEAD_DIM + 1])

    qs = []
    for g in groups:
        q = q_ref[0, :, g * gw:(g + 1) * gw].astype(F32)
        frames = []
        for r in range(B_GROUP):
            ch = q[:, (r // 2) * LANES:(r // 2 + 1) * LANES]
            if r % 2:
                ch = pltpu.roll(ch, HEAD_DIM, 1)
            frames.append(jnp.where(lo, ch, 0.0))
        qs.append(jnp.concatenate(frames, axis=0))
    q_b = [q.astype(BF16) for q in qs]

    nwin = WIN // ck
    win_chunks = [jnp.maximum(qi - back, 0) for back in range(nwin, -1, -1)]
    s_win = []
    for g in groups:
        s = _dot_nt(jnp.concatenate([chunk(kw_ref, g, c) for c in win_chunks], axis=0), q_b[g])
        pieces = [jnp.where(qi >= nwin, s[0:ck] + bias_ref[1], NEG)]
        for n in range(1, nwin):
            pieces.append(jnp.where(qi >= nwin - n, s[n * ck:(n + 1) * ck], NEG))
        pieces.append(s[nwin * ck:] + bias_ref[0])
        s_win.append(jnp.concatenate(pieces, axis=0))

    s_cmp = [_dot_nt(kc_ref[0, g], q_b[g]) + cbias_ref[0] for g in groups]
    p_cmp, o_cmp = [], []
    for g in groups:
        m = jnp.max(s_cmp[g], axis=0, keepdims=True)
        e = jnp.where(t_row >= CMP_LEN - 1, jnp.exp2(s_cmp[g] - m), 0.0)
        den = jnp.sum(e, axis=0, keepdims=True)
        p_cmp.append(e * (1.0 / jnp.maximum(den, 1e-30)))
        o_cmp.append(_dot(vct_ref[0, g], p_cmp[g].astype(BF16)))

    imp = []
    for g in groups:
        psum = p_cmp[g][:, 0:tq]
        for r in range(1, B_GROUP):
            psum = psum + p_cmp[g][:, r * tq:(r + 1) * tq]
        p_hi = psum.astype(BF16)
        p_lo = (psum - p_hi.astype(F32)).astype(BF16)
        imp.append(_dot(ovt_ref[...], p_hi) + _dot(ovt_ref[...], p_lo))
    j = lax.broadcasted_iota(jnp.int32, (n_sel_blocks, tq), 0)
    j_f = j.astype(F32)
    cur = (t0 + lax.broadcasted_iota(jnp.int32, (n_sel_blocks, tq), 1)) >> SEL_SHIFT
    forced = (j == 0) | (j == cur) | (j == cur - 1)
    low = -3e38
    q_aug = []
    for g in groups:
        score = jnp.where(forced, imp[g] + 2.0, jnp.where(j > cur, -1.0, imp[g]))
        sel = jnp.zeros((n_sel_blocks, tq), jnp.bool_)
        for _ in range(min(SEL_TOPK, n_sel_blocks)):
            mx = jnp.max(score, axis=0, keepdims=True)
            first = jnp.min(jnp.where(score == mx, j_f, 4.0 * LANES), axis=0, keepdims=True)
            hit = j_f == first
            sel = sel | hit
            score = jnp.where(hit, low, score)
        selneg = jnp.concatenate([jnp.zeros((SEL_LANE0, tq), F32), jnp.where(sel, 0.0, NEG),
                                  jnp.zeros((LANES - SEL_LANE0 - n_sel_blocks, tq), F32)], axis=0).T
        q_aug.append((qs[g] + jnp.concatenate([selneg] * B_GROUP, axis=0)).astype(BF16))

    s_diag = [_dot_nt(chunk(ks_ref, g, qi), q_aug[g]) + bias_ref[0] for g in groups]
    win_vt = [jnp.concatenate([vt_chunk(vwt_ref, g, c) for c in win_chunks], axis=1) for g in groups]
    o_win = [normalised(softmax_pv(s_win[g], win_vt[g], None)) for g in groups]
    states = tuple(softmax_pv(s_diag[g], vt_chunk(vst_ref, g, qi), None) for g in groups)

    def earlier(c, st):
        s = [_dot_nt(chunk(ks_ref, g, c), q_aug[g]) for g in groups]
        return tuple(softmax_pv(s[g], vt_chunk(vst_ref, g, c), st[g]) for g in groups)

    states = lax.fori_loop(0, qi, earlier, states)

    for g in groups:
        o_sel = normalised(states[g])
        gt = gt_ref[0, g * GATE_ROWS:(g + 1) * GATE_ROWS, :]
        outs = []
        for r in range(B_GROUP):
            cs = slice(r * tq, (r + 1) * tq)
            outs.append(gt[r * N_BRANCH:r * N_BRANCH + 1, :] * o_cmp[g][:, cs]
                        + gt[r * N_BRANCH + 1:r * N_BRANCH + 2, :] * o_sel[:, cs]
                        + gt[r * N_BRANCH + 2:r * N_BRANCH + 3, :] * o_win[g][:, cs])
        for c in range(B_GROUP // 2):
            pair = jnp.concatenate([outs[2 * c], outs[2 * c + 1]], axis=0)
            o_ref[0, :, g * gw + c * LANES:g * gw + (c + 1) * LANES] = pair.T


def _overlap_t(nc_pad, ns):
    nc = nc_pad - 1
    c0 = jnp.arange(nc_pad) * CMP_STRIDE
    s0 = jnp.arange(ns) * SEL_BLOCK
    ov = jnp.minimum(c0[None, :] + CMP_LEN, s0[:, None] + SEL_BLOCK) - jnp.maximum(c0[None, :], s0[:, None])
    ov = jnp.clip(ov, 0, None).astype(F32) / CMP_LEN
    ov = jnp.where(jnp.arange(nc_pad)[None, :] < nc, ov, 0.0)
    return ov.astype(BF16)


def _nsa(qb, ks, kw, vst, vwt, kc, vct, gt):
    B, T, W = qb.shape
    nq = T // NSA_TQ
    ns = T // SEL_BLOCK
    nc_pad = kc.shape[2]
    nrow = B_GROUP * NSA_TQ
    assert NSA_TQ == NSA_CK and WIN % NSA_CK == 0 and SEL_LANE0 + ns <= LANES
    kspec = pl.BlockSpec((1, B_KV_HEADS, T, LANES), lambda b, i: (b, 0, 0, 0))
    vspec = pl.BlockSpec((1, T // NSA_CK, B_KV_HEADS * NSA_VROWS, NSA_CK), lambda b, i: (b, 0, 0, 0))
    key = np.arange(NSA_CK)[:, None]
    qry = (np.arange(nrow) % NSA_TQ)[None, :]
    bias = jnp.asarray(np.stack([np.where(key <= qry, 0.0, NEG), np.where(key > qry, 0.0, NEG)]).astype(np.float32))
    cmp_end = (np.arange(nc_pad) * CMP_STRIDE + CMP_LEN - 1)[None, :, None]
    t_query = (np.arange(nq) * NSA_TQ)[:, None, None] + qry[None]
    cbias = jnp.asarray(np.where((cmp_end <= t_query) & (np.arange(nc_pad) < nc_pad - 1)[None, :, None], 0.0, NEG)
                        .astype(np.float32))
    return pl.pallas_call(
        functools.partial(_nsa_kernel, n_sel_blocks=ns),
        grid=(B, nq),
        in_specs=[pl.BlockSpec((1, NSA_TQ, W), lambda b, i: (b, i, 0)),
                  kspec, kspec, vspec, vspec,
                  pl.BlockSpec((1, B_KV_HEADS, nc_pad, LANES), lambda b, i: (b, 0, 0, 0)),
                  pl.BlockSpec((1, B_KV_HEADS, HEAD_DIM, nc_pad), lambda b, i: (b, 0, 0, 0)),
                  pl.BlockSpec((1, B_KV_HEADS * GATE_ROWS, NSA_TQ), lambda b, i: (b, 0, i)),
                  pl.BlockSpec((ns, nc_pad), lambda b, i: (0, 0)),
                  pl.BlockSpec((2, NSA_CK, nrow), lambda b, i: (0, 0, 0)),
                  pl.BlockSpec((1, nc_pad, nrow), lambda b, i: (i, 0, 0))],
        out_specs=pl.BlockSpec((1, NSA_TQ, W), lambda b, i: (b, i, 0)),
        out_shape=jax.ShapeDtypeStruct((B, T, W), F32),
        compiler_params=pltpu.CompilerParams(dimension_semantics=("parallel", "arbitrary"),
                                             vmem_limit_bytes=VMEM_LIMIT),
        name="nsa",
    )(qb, ks, kw, vst, vwt, kc, vct, gt, _overlap_t(nc_pad, ns), bias, cbias)


def _post_kernel(x_ref, oa_ref, ob_ref, ga_ref, gb_ref, wo_ref, gm_ref, wu_ref, wd_ref, gf_ref, o_ref, *, final):
    def norm(v, g):
        return v * lax.rsqrt(jnp.mean(v * v, axis=-1, keepdims=True) + EPS) * g

    na = norm(oa_ref[...], ga_ref[...]).astype(BF16)
    nb = norm(ob_ref[...], gb_ref[...]).astype(BF16)
    aw = na.shape[1]
    h_res = x_ref[...] + _dot(na, wo_ref[0:aw, :]) + _dot(nb, wo_ref[aw:, :])
    h = norm(h_res, gm_ref[...]).astype(BF16)
    u = jnp.square(jnp.maximum(_dot(h, wu_ref[...]), 0.0)).astype(BF16)
    acc = h_res + _dot(u, wd_ref[...])
    o_ref[...] = norm(acc, gf_ref[...]) if final else acc


def _post(x, oa, ob, g_a, g_b, w_out, g_mlp, w_up, w_down, g_final, *, final, tm=256):
    B, T, D = x.shape
    n = B * T
    dff = w_up.shape[1]
    tok = lambda w: pl.BlockSpec((tm, w), lambda i: (i, 0))
    const = lambda shape: pl.BlockSpec(shape, lambda i: (0, 0), pipeline_mode=pl.Buffered(1))
    out = pl.pallas_call(
        functools.partial(_post_kernel, final=final),
        grid=(n // tm,),
        in_specs=[tok(D), tok(A_WIDTH), tok(B_WIDTH), const((1, A_WIDTH)), const((1, B_WIDTH)),
                  const((A_WIDTH + B_WIDTH, D)), const((1, D)), const((D, dff)), const((dff, D)), const((1, D))],
        out_specs=tok(D),
        out_shape=jax.ShapeDtypeStruct((n, D), F32),
        compiler_params=pltpu.CompilerParams(dimension_semantics=("parallel",), vmem_limit_bytes=VMEM_LIMIT),
        name="post",
    )(x.reshape(n, D), oa.reshape(n, A_WIDTH), ob.reshape(n, B_WIDTH), g_a.reshape(1, -1), g_b.reshape(1, -1),
      w_out.astype(BF16), g_mlp.reshape(1, D), w_up.astype(BF16), w_down.astype(BF16), g_final.reshape(1, D))
    return out.reshape(B, T, D)


def kernel(x, norm_mix, w_in, cmp_pe_k, cmp_w1_k, cmp_w2_k, cmp_pe_v, cmp_w1_v, cmp_w2_v, g_out_a, g_out_b,
           w_out, norm_mlp, w_up, w_down, norm_final):
    B, T, D = x.shape
    depth = w_in.shape[0]
    tables = _rope_tables(jnp.arange(T))
    cmp_tables = _rope_tables(jnp.arange(T // CMP_STRIDE) * CMP_STRIDE + CMP_LEN - 1)
    h_res = x
    for l in range(depth):
        qa, ka, va, qb, ks, kw, vst, vwt, kcvc, gt = _in_proj(h_res, norm_mix[l], w_in[l], tables)
        kc, vct = _compress(kcvc, cmp_pe_k[l], cmp_w1_k[l], cmp_w2_k[l], cmp_pe_v[l], cmp_w1_v[l], cmp_w2_v[l],
                            cmp_tables)
        oa = _mixer_a(qa, ka, va)
        ob = _nsa(qb, ks, kw, vst, vwt, kc, vct, gt)
        h_res = _post(h_res, oa, ob, g_out_a[l], g_out_b[l], w_out[l], norm_mlp[l], w_up[l], w_down[l], norm_final,
                      final=(l == depth - 1))
    return h_res
```

```python
import functools

import jax
import jax.numpy as jnp
import numpy as np
from jax import lax
from jax.experimental import pallas as pl
from jax.experimental.pallas import tpu as pltpu

F32 = jnp.float32
BF16 = jnp.bfloat16

HEAD_DIM = 64
ROT_DIM = HEAD_DIM // 4
ROPE_THETA = 500000.0
EPS = 1e-6
NEG = -1e30
Q_SCALE = HEAD_DIM ** -0.5 * 1.4426950408889634
LANES = 128

A_HEADS = 8
A_PATTERNS = ((128, 1), (512, 4), (2048, 16))
A_BLOCK = 128
A_GROUP = 16

B_HEADS = 8
B_KV_HEADS = 2
B_GROUP = B_HEADS // B_KV_HEADS
CMP_LEN = 32
CMP_STRIDE = 16
CMP_HIDDEN = 256
SEL_BLOCK = 64
SEL_SHIFT = 6
SEL_TOPK = 8
WIN = 512
N_BRANCH = 3

A_WIDTH = A_HEADS * HEAD_DIM
B_WIDTH = B_HEADS * HEAD_DIM
KV_WIDTH = B_KV_HEADS * HEAD_DIM

NSA_TQ = 256
NSA_CK = 256
NSA_LOOP_CHUNKS = 2
NSA_VROWS = 80
GATE_ROWS = 16
SEL_LANE0 = HEAD_DIM

VMEM_LIMIT = 56 * 1024 * 1024


def _dot(a, b):
    return jnp.dot(a, b, preferred_element_type=F32)


def _dot_nt(a, b):
    return lax.dot_general(a, b, (((1,), (1,)), ((), ())), preferred_element_type=F32)


def _rope_rows(y, cos, sin_a, sin_b):
    outs = []
    for c in range(y.shape[1] // LANES):
        yc = y[:, c * LANES:(c + 1) * LANES]
        outs.append(yc * cos + pltpu.roll(yc, LANES - ROT_DIM // 2, 1) * sin_a
                    + pltpu.roll(yc, ROT_DIM // 2, 1) * sin_b)
    return outs[0] if len(outs) == 1 else jnp.concatenate(outs, axis=1)


def _in_proj_kernel(x_ref, g_ref, wq_ref, wt_ref, cos_ref, sa_ref, sb_ref,
                    qa_ref, ka_ref, va_ref, qb_ref, ks_ref, kw_ref, vst_ref, vwt_ref, kcvc_ref, gt_ref,
                    *, tm):
    tt = pl.program_id(1)
    x = x_ref[...]
    ms = jnp.mean(x * x, axis=-1, keepdims=True)
    h = (x * lax.rsqrt(ms + EPS) * g_ref[...]).astype(BF16)
    cos, sa, sb = cos_ref[...], sa_ref[...], sb_ref[...]
    scale = Q_SCALE

    def proj(c0, c1):
        return _dot(h, wq_ref[:, c0:c1])

    o = 0
    qa_ref[...] = (_rope_rows(proj(o, o + A_WIDTH), cos, sa, sb) * scale).astype(BF16)
    o += A_WIDTH
    ka_ref[...] = _rope_rows(proj(o, o + A_WIDTH), cos, sa, sb).astype(BF16)
    o += A_WIDTH
    va_ref[...] = proj(o, o + A_WIDTH).astype(BF16)
    o += A_WIDTH
    lane = lax.broadcasted_iota(jnp.int32, (tm, LANES), 1)
    row = lax.broadcasted_iota(jnp.int32, (tm, LANES), 0)
    lo = lane < HEAD_DIM
    qb = _rope_rows(proj(o, o + B_WIDTH), cos, sa, sb) * scale
    for hh in range(B_HEADS):
        ch = qb[:, (hh // 2) * LANES:(hh // 2 + 1) * LANES]
        if hh % 2:
            ch = pltpu.roll(ch, HEAD_DIM, 1)
        qb_ref[0, hh // B_GROUP, hh % B_GROUP] = jnp.where(lo, ch, 0.0).astype(BF16)
    o += B_WIDTH
    kcvc_ref[...] = proj(o, o + 2 * KV_WIDTH)
    o += 2 * KV_WIDTH
    ksw = _rope_rows(proj(o, o + 2 * KV_WIDTH), cos, sa, sb)

    blk = (tt * tm + row) >> SEL_SHIFT
    onehot = jnp.where(lane - SEL_LANE0 == blk, 1.0, 0.0)
    for kind, ref in ((0, ks_ref), (1, kw_ref)):
        kk = ksw[:, kind * LANES:(kind + 1) * LANES]
        tail = onehot if kind == 0 else 0.0
        ref[0, 0] = jnp.where(lo, kk, tail).astype(BF16)
        ref[0, 1] = jnp.where(lo, pltpu.roll(kk, HEAD_DIM, 1), tail).astype(BF16)

    tr = _dot_nt(wt_ref[...], h)
    ones_row = jnp.where(lax.broadcasted_iota(jnp.int32, (NSA_VROWS - HEAD_DIM, tm), 0) == 0, 1.0, 0.0)
    for kind, ref in ((0, vst_ref), (1, vwt_ref)):
        rows = [tr[kind * LANES + gg * HEAD_DIM:kind * LANES + (gg + 1) * HEAD_DIM] for gg in range(B_KV_HEADS)]
        slab = jnp.concatenate([rows[0], ones_row, rows[1], ones_row], axis=0).astype(BF16)
        for c in range(tm // NSA_CK):
            ref[0, c] = slab[:, c * NSA_CK:(c + 1) * NSA_CK]
    gt_ref[0] = jax.nn.sigmoid(tr[2 * LANES:2 * LANES + 2 * GATE_ROWS, :])


def _rope_tables(pos):
    half = ROT_DIM // 2
    inv = ROPE_THETA ** (-jnp.arange(0, ROT_DIM, 2, dtype=F32) / ROT_DIM)
    ang = pos.astype(F32)[:, None] * inv[None, :]
    cos, sin = jnp.cos(ang), jnp.sin(ang)
    n = pos.shape[0]
    ones = jnp.ones((n, HEAD_DIM - ROT_DIM), F32)
    zeros = jnp.zeros((n, HEAD_DIM - ROT_DIM), F32)
    zh = jnp.zeros((n, half), F32)
    c_head = jnp.concatenate([cos, cos, ones], axis=1)
    a_head = jnp.concatenate([-sin, zh, zeros], axis=1)
    b_head = jnp.concatenate([zh, sin, zeros], axis=1)
    rep = LANES // HEAD_DIM
    return jnp.tile(c_head, (1, rep)), jnp.tile(a_head, (1, rep)), jnp.tile(b_head, (1, rep))


def _in_proj(x, norm_g, w_in, tables, *, tm=1024):
    B, T, D = x.shape
    nt = T // tm
    offs = [0]
    for n in (A_WIDTH, A_WIDTH, A_WIDTH, B_WIDTH, KV_WIDTH, KV_WIDTH, KV_WIDTH, KV_WIDTH, KV_WIDTH, KV_WIDTH,
              B_HEADS * N_BRANCH):
        offs.append(offs[-1] + n)
    col = lambda i: w_in[:, offs[i]:offs[i + 1]]
    wq = jnp.concatenate([col(0), col(1), col(2), col(3), col(4), col(5), col(6), col(8)], axis=1).astype(BF16)
    gl = col(10)
    per_g = B_GROUP * N_BRANCH
    gpad = jnp.zeros((D, GATE_ROWS - per_g), w_in.dtype)
    wt = jnp.concatenate([col(7), col(9), gl[:, :per_g], gpad, gl[:, per_g:], gpad], axis=1).T.astype(BF16)
    cos, sa, sb = tables
    nq = wq.shape[1]
    nr = wt.shape[0]
    tok = lambda w: pl.BlockSpec((None, tm, w), lambda b, t: (b, t, 0))
    const = lambda shape: pl.BlockSpec(shape, lambda b, t: (0,) * len(shape))
    tab = pl.BlockSpec((tm, LANES), lambda b, t: (t, 0))
    out_shapes = (
        jax.ShapeDtypeStruct((B, T, A_WIDTH), BF16),
        jax.ShapeDtypeStruct((B, T, A_WIDTH), BF16),
        jax.ShapeDtypeStruct((B, T, A_WIDTH), BF16),
        jax.ShapeDtypeStruct((B, B_KV_HEADS, B_GROUP, T, LANES), BF16),
        jax.ShapeDtypeStruct((B, B_KV_HEADS, T, LANES), BF16),
        jax.ShapeDtypeStruct((B, B_KV_HEADS, T, LANES), BF16),
        jax.ShapeDtypeStruct((B, T // NSA_CK, B_KV_HEADS * NSA_VROWS, NSA_CK), BF16),
        jax.ShapeDtypeStruct((B, T // NSA_CK, B_KV_HEADS * NSA_VROWS, NSA_CK), BF16),
        jax.ShapeDtypeStruct((B, T, 2 * KV_WIDTH), F32),
        jax.ShapeDtypeStruct((B, 2 * GATE_ROWS, T), F32),
    )
    frame = pl.BlockSpec((1, B_KV_HEADS, tm, LANES), lambda b, t: (b, 0, t, 0))
    vt = pl.BlockSpec((1, tm // NSA_CK, B_KV_HEADS * NSA_VROWS, NSA_CK), lambda b, t: (b, t, 0, 0))
    qframe = pl.BlockSpec((1, B_KV_HEADS, B_GROUP, tm, LANES), lambda b, t: (b, 0, 0, t, 0))
    out_specs = (tok(A_WIDTH), tok(A_WIDTH), tok(A_WIDTH), qframe, frame, frame, vt, vt,
                 tok(2 * KV_WIDTH), pl.BlockSpec((1, 2 * GATE_ROWS, tm), lambda b, t: (b, 0, t)))
    return pl.pallas_call(
        functools.partial(_in_proj_kernel, tm=tm),
        grid=(B, nt),
        in_specs=[tok(D), const((1, D)), const((D, nq)), const((nr, D)), tab, tab, tab],
        out_specs=out_specs,
        out_shape=out_shapes,
        compiler_params=pltpu.CompilerParams(dimension_semantics=("parallel", "parallel"),
                                             vmem_limit_bytes=VMEM_LIMIT),
        name="in_proj",
    )(x, norm_g.reshape(1, D), wq, wt, cos, sa, sb)


def _compress_kernel(ak_ref, av_ref, w1k_ref, w1v_ref, pek_ref, pev_ref, w2k_ref, w2vt_ref, cos_ref, sa_ref, sb_ref,
                     kc_ref, vct_ref, *, nc_pad):
    half = CMP_LEN // 2
    hid_w = B_KV_HEADS * CMP_HIDDEN

    def hidden(a_ref, w1_ref, pe_ref):
        acc_u = jnp.zeros((nc_pad, hid_w), F32)
        acc_v = jnp.zeros((nc_pad, hid_w), F32)
        for p in range(half):
            ap = a_ref[0, pl.ds(p, nc_pad, stride=CMP_STRIDE), :]
            acc_u = acc_u + _dot((ap + pe_ref[p:p + 1, :]).astype(BF16), w1_ref[p])
            acc_v = acc_v + _dot((ap + pe_ref[half + p:half + p + 1, :]).astype(BF16), w1_ref[half + p])
        return jax.nn.gelu(acc_u + pltpu.roll(acc_v, nc_pad - 1, 0))

    hk = hidden(ak_ref, w1k_ref, pek_ref).astype(BF16)
    hv = hidden(av_ref, w1v_ref, pev_ref).astype(BF16)
    for g in range(B_KV_HEADS):
        hg = hk[:, g * CMP_HIDDEN:(g + 1) * CMP_HIDDEN]
        kc = _dot(hg, w2k_ref[...])
        kc_ref[0, g] = _rope_rows(kc, cos_ref[...], sa_ref[...], sb_ref[...]).astype(BF16)
        vg = hv[:, g * CMP_HIDDEN:(g + 1) * CMP_HIDDEN]
        vct_ref[0, g] = _dot_nt(w2vt_ref[...], vg).astype(BF16)


def _block_diag_w1(w1):
    w = w1.reshape(CMP_LEN, HEAD_DIM, CMP_HIDDEN)
    z = jnp.zeros_like(w)
    top = jnp.concatenate([w, z], axis=2)
    bot = jnp.concatenate([z, w], axis=2)
    return jnp.concatenate([top, bot], axis=1).astype(BF16)


def _compress(kcvc, pe_k, w1_k, w2_k, pe_v, w1_v, w2_v, cmp_tables):
    B, T, _ = kcvc.shape
    nc_pad = T // CMP_STRIDE
    w2k = jnp.concatenate([w2_k, jnp.zeros_like(w2_k)], axis=1).astype(BF16)
    w2vt = w2_v.T.astype(BF16)
    pek = jnp.tile(pe_k, (1, B_KV_HEADS))
    pev = jnp.tile(pe_v, (1, B_KV_HEADS))
    const = lambda shape: pl.BlockSpec(shape, lambda b: (0,) * len(shape))
    cos, sa, sb = cmp_tables
    return pl.pallas_call(
        functools.partial(_compress_kernel, nc_pad=nc_pad),
        grid=(B,),
        in_specs=[pl.BlockSpec((1, T, KV_WIDTH), lambda b: (b, 0, 0)), pl.BlockSpec((1, T, KV_WIDTH), lambda b: (b, 0, 1)),
                  const((CMP_LEN, LANES, B_KV_HEADS * CMP_HIDDEN)), const((CMP_LEN, LANES, B_KV_HEADS * CMP_HIDDEN)),
                  const((CMP_LEN, LANES)), const((CMP_LEN, LANES)),
                  const((CMP_HIDDEN, LANES)), const((HEAD_DIM, CMP_HIDDEN)),
                  const((nc_pad, LANES)), const((nc_pad, LANES)), const((nc_pad, LANES))],
        out_specs=(pl.BlockSpec((1, B_KV_HEADS, nc_pad, LANES), lambda b: (b, 0, 0, 0)),
                   pl.BlockSpec((1, B_KV_HEADS, HEAD_DIM, nc_pad), lambda b: (b, 0, 0, 0))),
        out_shape=(jax.ShapeDtypeStruct((B, B_KV_HEADS, nc_pad, LANES), BF16),
                   jax.ShapeDtypeStruct((B, B_KV_HEADS, HEAD_DIM, nc_pad), BF16)),
        compiler_params=pltpu.CompilerParams(dimension_semantics=("parallel",), vmem_limit_bytes=VMEM_LIMIT),
        name="compress",
    )(kcvc, kcvc, _block_diag_w1(w1_k), _block_diag_w1(w1_v), pek, pev, w2k, w2vt, cos, sa, sb)


def _mixer_a_kernel(q_ref, k_ref, v_ref, bias_ref, o_ref, nat, qd0, qd1, kd, vd0, vd1, u_s, m_s, l_s, *, seq):
    blk = A_BLOCK
    nres = seq // blk
    lane = lax.broadcasted_iota(jnp.int32, (blk, LANES), 1)
    lo = lane < HEAD_DIM
    for src, dsts in ((q_ref, (qd0, qd1)), (k_ref, (kd,)), (v_ref, (vd0, vd1))):
        nat[...] = src[0].astype(F32)
        for r in range(nres):
            rows = nat[pl.ds(r, blk, stride=nres), :]
            if src is k_ref:
                kd[pl.ds(r * blk, blk), :] = rows
            else:
                fill = 0.0 if src is q_ref else 1.0
                dsts[0][pl.ds(r * blk, blk), :] = jnp.where(lo, rows, fill)
                dsts[1][pl.ds(r * blk, blk), :] = jnp.where(lo, fill, rows)

    def pieces(dil, rd, row_off, rows):
        return [pl.ds(pl.multiple_of((rd + dil * jj) * blk + row_off, 8), rows) for jj in range(nres // dil)]

    def gather(ref, idx):
        parts = [ref[i, :] for i in idx]
        return parts[0] if len(parts) == 1 else jnp.concatenate(parts, axis=0)

    def attend(pi, dil, blocks):
        pr = blk // (nres // dil)
        q_idxs, vbs, scores = [], [], []
        for rd, n, first in blocks:
            q_idx = pieces(dil, rd, n * pr, pr)
            k_idx = q_idx if first else pieces(dil, rd, (n - 1) * pr, 2 * pr)
            bias = bias_ref[pi, :, 0:blk] if first else bias_ref[pi, :, blk:3 * blk]
            kb = gather(kd, k_idx).astype(BF16)
            q_idxs.append(q_idx)
            for qd, vd in ((qd0, vd0), (qd1, vd1)):
                vbs.append(gather(vd, k_idx).astype(BF16))
                scores.append(_dot_nt(gather(qd, q_idx).astype(BF16), kb) + bias)
        es, ms = [], []
        for s in scores:
            m = jnp.max(s, axis=-1, keepdims=True)
            ms.append(m)
            es.append(jnp.exp2((s - m).astype(BF16)))
        pvs = [_dot(e, vb) for e, vb in zip(es, vbs)]
        for b, q_idx in enumerate(q_idxs):
            u = jnp.where(lo, pvs[2 * b], pvs[2 * b + 1])
            l_swapped = jnp.where(lo, pvs[2 * b + 1], pvs[2 * b])
            m = jnp.where(lo, ms[2 * b], ms[2 * b + 1])
            for jj, idx in enumerate(q_idx):
                u_s[pi, idx, :] = u[jj * pr:(jj + 1) * pr]
                m_s[pi, idx, :] = m[jj * pr:(jj + 1) * pr]
                l_s[pi, idx, :] = l_swapped[jj * pr:(jj + 1) * pr]

    for pi, (_, dil) in enumerate(A_PATTERNS):
        nb = seq // dil // blk
        if nb <= A_GROUP:
            per = min(A_GROUP // nb, dil)

            def residues(i, carry, pi=pi, dil=dil, nb=nb, per=per):
                attend(pi, dil, [(i * per + jj, n, n == 0) for jj in range(per) for n in range(nb)])
                return carry

            lax.fori_loop(0, dil // per, residues, 0)
        else:
            group = A_GROUP

            def residue(rd, carry, pi=pi, dil=dil, nb=nb, group=group):
                attend(pi, dil, [(rd, 0, True)] + [(rd, n, False) for n in range(1, group)])

                def inner(i, c):
                    attend(pi, dil, [(rd, i * group + jj, False) for jj in range(group)])
                    return c

                return lax.fori_loop(1, nb // group, inner, carry)

            lax.fori_loop(0, dil, residue, 0)

    rows = 2 * blk

    def merge(c, carry):
        r0 = pl.multiple_of(c * rows, rows)
        ms = [m_s[p, pl.ds(r0, rows), :] for p in range(len(A_PATTERNS))]
        m_all = functools.reduce(jnp.maximum, ms)
        num = jnp.zeros((rows, LANES), F32)
        den = jnp.zeros((rows, LANES), F32)
        for p in range(len(A_PATTERNS)):
            a = jnp.exp2(ms[p] - m_all)
            num = num + a * u_s[p, pl.ds(r0, rows), :]
            den = den + a * pltpu.roll(l_s[p, pl.ds(r0, rows), :], HEAD_DIM, 1)
        out = num / den
        for jj in range(rows // blk):
            o_ref[0, pl.ds(c * (rows // blk) + jj, blk, stride=nres), :] = out[jj * blk:(jj + 1) * blk]
        return carry

    lax.fori_loop(0, seq // rows, merge, 0)


def _mixer_a_bias(nres):
    blk = A_BLOCK

    def sub_pos(i, fold, rows):
        return fold * (i % rows) + i // rows

    out = np.zeros((len(A_PATTERNS), blk, 3 * blk), np.float32)
    for pi, (window, dil) in enumerate(A_PATTERNS):
        n_back = window // dil
        fold = nres // dil
        pr = blk // fold
        sq = sub_pos(np.arange(blk), fold, pr)[:, None]
        d_first = sq - sub_pos(np.arange(blk), fold, pr)[None, :]
        d_band = sq + blk - sub_pos(np.arange(2 * blk), fold, 2 * pr)[None, :]
        dist = np.concatenate([d_first, d_band], axis=1)
        out[pi] = np.where((dist >= 0) & (dist <= n_back), 0.0, NEG)
    return jnp.asarray(out)


def _mixer_a(qa, ka, va):
    B, T, W = qa.shape
    npair = W // LANES
    spec = pl.BlockSpec((1, T, LANES), lambda b, p: (b, 0, p))
    npat = len(A_PATTERNS)
    nres = T // A_BLOCK
    assert all(nres % d == 0 and T % (d * A_BLOCK) == 0 for _, d in A_PATTERNS) and max(d for _, d in A_PATTERNS) == nres
    return pl.pallas_call(
        functools.partial(_mixer_a_kernel, seq=T),
        grid=(B, npair),
        in_specs=[spec, spec, spec, pl.BlockSpec((npat, A_BLOCK, 3 * A_BLOCK), lambda b, p: (0, 0, 0))],
        out_specs=spec,
        out_shape=jax.ShapeDtypeStruct((B, T, W), F32),
        scratch_shapes=[pltpu.VMEM((T, LANES), F32)] * 6 + [pltpu.VMEM((npat, T, LANES), F32)] * 3,
        compiler_params=pltpu.CompilerParams(dimension_semantics=("parallel", "parallel"),
                                             vmem_limit_bytes=VMEM_LIMIT),
        name="mixer_a",
    )(qa, ka, va, _mixer_a_bias(nres))


def _nsa_kernel(q_ref, ks_ref, kw_ref, vst_ref, vwt_ref, kc_ref, vct_ref, gt_ref, ovt_ref, bias_ref, cbias_ref, o_ref,
                *, n_sel_blocks):
    tq, ck = NSA_TQ, NSA_CK
    qi = pl.program_id(1)
    t0 = qi * tq
    nrow = B_GROUP * tq
    gw = B_GROUP * HEAD_DIM
    groups = range(B_KV_HEADS)
    t_row = t0 + (lax.broadcasted_iota(jnp.int32, (1, nrow), 1) & (tq - 1))

    def chunk(ref, g, c):
        return ref[0, g, pl.ds(pl.multiple_of(c * ck, ck), ck), :]

    def vt_chunk(ref, g, c):
        return ref[0, c, g * NSA_VROWS:(g + 1) * NSA_VROWS, :]

    def softmax_pv(s, vt, state):
        m_new = jnp.max(s, axis=0, keepdims=True)
        if state is not None:
            m_new = jnp.maximum(state[0], m_new)
        p = jnp.exp2(s - m_new).astype(BF16)
        acc = _dot(vt, p)
        if state is not None:
            acc = jnp.exp2(state[0] - m_new) * state[1] + acc
        return m_new, acc

    def normalised(state):
        return state[1][0:HEAD_DIM] * (1.0 / state[1][HEAD_DIM:HEAD_DIM + 1])

    q_b = [jnp.concatenate([q_ref[0, g, r] for r in range(B_GROUP)], axis=0) for g in groups]

    nwin = WIN // ck
    win_chunks = [jnp.maximum(qi - back, 0) for back in range(nwin, -1, -1)]
    s_win = []
    for g in groups:
        s = _dot_nt(jnp.concatenate([chunk(kw_ref, g, c) for c in win_chunks], axis=0), q_b[g])
        pieces = [jnp.where(qi >= nwin, s[0:ck] + bias_ref[1], NEG)]
        for n in range(1, nwin):
            pieces.append(jnp.where(qi >= nwin - n, s[n * ck:(n + 1) * ck], NEG))
        pieces.append(s[nwin * ck:] + bias_ref[0])
        s_win.append(pieces)

    s_cmp = [_dot_nt(kc_ref[0, g], q_b[g]) + cbias_ref[0] for g in groups]
    p_cmp, o_cmp = [], []
    for g in groups:
        m = jnp.max(s_cmp[g], axis=0, keepdims=True)
        e = jnp.exp2(s_cmp[g] - m)
        den = jnp.sum(e, axis=0, keepdims=True)
        p_cmp.append(e * jnp.where(t_row >= CMP_LEN - 1, 1.0 / jnp.maximum(den, 1e-30), 0.0))
        o_cmp.append(_dot(vct_ref[0, g], p_cmp[g].astype(BF16)))

    imp = []
    for g in groups:
        psum = p_cmp[g][:, 0:tq]
        for r in range(1, B_GROUP):
            psum = psum + p_cmp[g][:, r * tq:(r + 1) * tq]
        p_hi = psum.astype(BF16)
        p_lo = (psum - p_hi.astype(F32)).astype(BF16)
        imp.append(_dot(ovt_ref[...], p_hi) + _dot(ovt_ref[...], p_lo))
    j = lax.broadcasted_iota(jnp.int32, (n_sel_blocks, tq), 0)
    j_f = j.astype(F32)
    cur = (t0 + lax.broadcasted_iota(jnp.int32, (n_sel_blocks, tq), 1)) >> SEL_SHIFT
    forced = (j == 0) | (j == cur) | (j == cur - 1)
    low = -3e38
    q_aug = []
    for g in groups:
        score = jnp.where(forced, imp[g] + 2.0, jnp.where(j > cur, -1.0, imp[g]))
        sel = jnp.zeros((n_sel_blocks, tq), jnp.bool_)
        for _ in range(min(SEL_TOPK, n_sel_blocks)):
            mx = jnp.max(score, axis=0, keepdims=True)
            first = jnp.min(jnp.where(score == mx, j_f, 4.0 * LANES), axis=0, keepdims=True)
            hit = j_f == first
            sel = sel | hit
            score = jnp.where(hit, low, score)
        selneg = jnp.concatenate([jnp.zeros((SEL_LANE0, tq), F32), jnp.where(sel, 0.0, NEG),
                                  jnp.zeros((LANES - SEL_LANE0 - n_sel_blocks, tq), F32)], axis=0).T
        q_aug.append(q_b[g] + jnp.concatenate([selneg.astype(BF16)] * B_GROUP, axis=0))

    s_diag = [_dot_nt(chunk(ks_ref, g, qi), q_aug[g]) + bias_ref[0] for g in groups]
    def merged(parts):
        m_new = functools.reduce(jnp.maximum, [m for m, _ in parts])
        acc = sum(jnp.exp2(m - m_new) * a for m, a in parts)
        return m_new, acc

    o_win = [normalised(merged([softmax_pv(s_win[g][n], vt_chunk(vwt_ref, g, c), None)
                                for n, c in enumerate(win_chunks)])) for g in groups]
    states = tuple(softmax_pv(s_diag[g], vt_chunk(vst_ref, g, qi), None) for g in groups)

    def earlier(cs, st):
        s = [[_dot_nt(chunk(ks_ref, g, c), q_aug[g]) for c in cs] for g in groups]
        return tuple(merged([st[g]] + [softmax_pv(s[g][n], vt_chunk(vst_ref, g, c), None) for n, c in enumerate(cs)])
                     for g in groups)

    step = NSA_LOOP_CHUNKS
    states = lax.fori_loop(0, qi // step, lambda i, st: earlier(tuple(step * i + n for n in range(step)), st), states)
    while step > 1:
        step //= 2
        first = qi // (2 * step) * (2 * step)
        states = lax.cond((qi & step) != 0,
                          lambda st, first=first, step=step: earlier(tuple(first + n for n in range(step)), st),
                          lambda st: st, states)

    for g in groups:
        o_sel = normalised(states[g])
        gt = gt_ref[0, g * GATE_ROWS:(g + 1) * GATE_ROWS, :]
        outs = []
        for r in range(B_GROUP):
            cs = slice(r * tq, (r + 1) * tq)
            outs.append(gt[r * N_BRANCH:r * N_BRANCH + 1, :] * o_cmp[g][:, cs]
                        + gt[r * N_BRANCH + 1:r * N_BRANCH + 2, :] * o_sel[:, cs]
                        + gt[r * N_BRANCH + 2:r * N_BRANCH + 3, :] * o_win[g][:, cs])
        for c in range(B_GROUP // 2):
            pair = jnp.concatenate([outs[2 * c], outs[2 * c + 1]], axis=0)
            o_ref[0, :, g * gw + c * LANES:g * gw + (c + 1) * LANES] = pair.T


def _overlap_t(nc_pad, ns):
    nc = nc_pad - 1
    c0 = jnp.arange(nc_pad) * CMP_STRIDE
    s0 = jnp.arange(ns) * SEL_BLOCK
    ov = jnp.minimum(c0[None, :] + CMP_LEN, s0[:, None] + SEL_BLOCK) - jnp.maximum(c0[None, :], s0[:, None])
    ov = jnp.clip(ov, 0, None).astype(F32) / CMP_LEN
    ov = jnp.where(jnp.arange(nc_pad)[None, :] < nc, ov, 0.0)
    return ov.astype(BF16)


def _nsa(qb, ks, kw, vst, vwt, kc, vct, gt):
    B, _, _, T, _ = qb.shape
    W = B_WIDTH
    nq = T // NSA_TQ
    ns = T // SEL_BLOCK
    nc_pad = kc.shape[2]
    nrow = B_GROUP * NSA_TQ
    assert NSA_TQ == NSA_CK and WIN % NSA_CK == 0 and SEL_LANE0 + ns <= LANES
    kspec = pl.BlockSpec((1, B_KV_HEADS, T, LANES), lambda b, i: (b, 0, 0, 0))
    vspec = pl.BlockSpec((1, T // NSA_CK, B_KV_HEADS * NSA_VROWS, NSA_CK), lambda b, i: (b, 0, 0, 0))
    key = np.arange(NSA_CK)[:, None]
    qry = (np.arange(nrow) % NSA_TQ)[None, :]
    bias = jnp.asarray(np.stack([np.where(key <= qry, 0.0, NEG), np.where(key > qry, 0.0, NEG)]).astype(np.float32))
    cmp_end = (np.arange(nc_pad) * CMP_STRIDE + CMP_LEN - 1)[None, :, None]
    t_query = (np.arange(nq) * NSA_TQ)[:, None, None] + qry[None]
    cbias = jnp.asarray(np.where((cmp_end <= t_query) & (np.arange(nc_pad) < nc_pad - 1)[None, :, None], 0.0, NEG)
                        .astype(np.float32))
    return pl.pallas_call(
        functools.partial(_nsa_kernel, n_sel_blocks=ns),
        grid=(B, nq),
        in_specs=[pl.BlockSpec((1, B_KV_HEADS, B_GROUP, NSA_TQ, LANES), lambda b, i: (b, 0, 0, i, 0)),
                  kspec, kspec, vspec, vspec,
                  pl.BlockSpec((1, B_KV_HEADS, nc_pad, LANES), lambda b, i: (b, 0, 0, 0)),
                  pl.BlockSpec((1, B_KV_HEADS, HEAD_DIM, nc_pad), lambda b, i: (b, 0, 0, 0)),
                  pl.BlockSpec((1, B_KV_HEADS * GATE_ROWS, NSA_TQ), lambda b, i: (b, 0, i)),
                  pl.BlockSpec((ns, nc_pad), lambda b, i: (0, 0)),
                  pl.BlockSpec((2, NSA_CK, nrow), lambda b, i: (0, 0, 0)),
                  pl.BlockSpec((1, nc_pad, nrow), lambda b, i: (i, 0, 0))],
        out_specs=pl.BlockSpec((1, NSA_TQ, W), lambda b, i: (b, i, 0)),
        out_shape=jax.ShapeDtypeStruct((B, T, W), F32),
        compiler_params=pltpu.CompilerParams(dimension_semantics=("parallel", "arbitrary"),
                                             vmem_limit_bytes=VMEM_LIMIT),
        name="nsa",
    )(qb, ks, kw, vst, vwt, kc, vct, gt, _overlap_t(nc_pad, ns), bias, cbias)


def _post_kernel(x_ref, oa_ref, ob_ref, ga_ref, gb_ref, wo_ref, gm_ref, wu_ref, wd_ref, gf_ref, o_ref, *, final):
    def norm(v, g):
        return v * lax.rsqrt(jnp.mean(v * v, axis=-1, keepdims=True) + EPS) * g

    na = norm(oa_ref[...], ga_ref[...]).astype(BF16)
    nb = norm(ob_ref[...], gb_ref[...]).astype(BF16)
    aw = na.shape[1]
    h_res = x_ref[...] + _dot(na, wo_ref[0:aw, :]) + _dot(nb, wo_ref[aw:, :])
    h = norm(h_res, gm_ref[...]).astype(BF16)
    u = jnp.square(jnp.maximum(_dot(h, wu_ref[...]), 0.0)).astype(BF16)
    acc = h_res + _dot(u, wd_ref[...])
    o_ref[...] = norm(acc, gf_ref[...]) if final else acc


def _post(x, oa, ob, g_a, g_b, w_out, g_mlp, w_up, w_down, g_final, *, final, tm=512):
    B, T, D = x.shape
    n = B * T
    dff = w_up.shape[1]
    tok = lambda w: pl.BlockSpec((tm, w), lambda i: (i, 0))
    const = lambda shape: pl.BlockSpec(shape, lambda i: (0, 0), pipeline_mode=pl.Buffered(1))
    out = pl.pallas_call(
        functools.partial(_post_kernel, final=final),
        grid=(n // tm,),
        in_specs=[tok(D), tok(A_WIDTH), tok(B_WIDTH), const((1, A_WIDTH)), const((1, B_WIDTH)),
                  const((A_WIDTH + B_WIDTH, D)), const((1, D)), const((D, dff)), const((dff, D)), const((1, D))],
        out_specs=tok(D),
        out_shape=jax.ShapeDtypeStruct((n, D), F32),
        compiler_params=pltpu.CompilerParams(dimension_semantics=("parallel",), vmem_limit_bytes=VMEM_LIMIT),
        name="post",
    )(x.reshape(n, D), oa.reshape(n, A_WIDTH), ob.reshape(n, B_WIDTH), g_a.reshape(1, -1), g_b.reshape(1, -1),
      w_out.astype(BF16), g_mlp.reshape(1, D), w_up.astype(BF16), w_down.astype(BF16), g_final.reshape(1, D))
    return out.reshape(B, T, D)


def kernel(x, norm_mix, w_in, cmp_pe_k, cmp_w1_k, cmp_w2_k, cmp_pe_v, cmp_w1_v, cmp_w2_v, g_out_a, g_out_b,
           w_out, norm_mlp, w_up, w_down, norm_final):
    B, T, D = x.shape
    depth = w_in.shape[0]
    tables = _rope_tables(jnp.arange(T))
    cmp_tables = _rope_tables(jnp.arange(T // CMP_STRIDE) * CMP_STRIDE + CMP_LEN - 1)
    h_res = x
    for l in range(depth):
        qa, ka, va, qb, ks, kw, vst, vwt, kcvc, gt = _in_proj(h_res, norm_mix[l], w_in[l], tables)
        kc, vct = _compress(kcvc, cmp_pe_k[l], cmp_w1_k[l], cmp_w2_k[l], cmp_pe_v[l], cmp_w1_v[l], cmp_w2_v[l],
                            cmp_tables)
        oa = _mixer_a(qa, ka, va)
        ob = _nsa(qb, ks, kw, vst, vwt, kc, vct, gt)
        h_res = _post(h_res, oa, ob, g_out_a[l], g_out_b[l], w_out[l], norm_mlp[l], w_up[l], w_down[l], norm_final,
                      final=(l == depth - 1))
    return h_res
```

```python
import functools

import jax
import jax.numpy as jnp
import numpy as np
from jax import lax
from jax.experimental import pallas as pl
from jax.experimental.pallas import tpu as pltpu

F32 = jnp.float32
BF16 = jnp.bfloat16

HEAD_DIM = 64
ROT_DIM = HEAD_DIM // 4
ROPE_THETA = 500000.0
EPS = 1e-6
NEG = -1e30
Q_SCALE = HEAD_DIM ** -0.5 * 1.4426950408889634
LANES = 128

A_HEADS = 8
A_PATTERNS = ((128, 1), (512, 4), (2048, 16))
A_BLOCK = 128
A_GROUP = 16

B_HEADS = 8
B_KV_HEADS = 2
B_GROUP = B_HEADS // B_KV_HEADS
CMP_LEN = 32
CMP_STRIDE = 16
CMP_HIDDEN = 256
SEL_BLOCK = 64
SEL_SHIFT = 6
SEL_TOPK = 8
WIN = 512
N_BRANCH = 3

A_WIDTH = A_HEADS * HEAD_DIM
B_WIDTH = B_HEADS * HEAD_DIM
KV_WIDTH = B_KV_HEADS * HEAD_DIM

NSA_TQ = 256
NSA_CK = 256
NSA_SPLIT = 2
NSA_VROWS = 80
GATE_ROWS = 16
SEL_LANE0 = HEAD_DIM

VMEM_LIMIT = 56 * 1024 * 1024


def _dot(a, b):
    return jnp.dot(a, b, preferred_element_type=F32)


def _dot_nt(a, b):
    return lax.dot_general(a, b, (((1,), (1,)), ((), ())), preferred_element_type=F32)


def _rope_rows(y, cos, sin_a, sin_b):
    outs = []
    for c in range(y.shape[1] // LANES):
        yc = y[:, c * LANES:(c + 1) * LANES]
        outs.append(yc * cos + pltpu.roll(yc, LANES - ROT_DIM // 2, 1) * sin_a
                    + pltpu.roll(yc, ROT_DIM // 2, 1) * sin_b)
    return outs[0] if len(outs) == 1 else jnp.concatenate(outs, axis=1)


def _in_proj_kernel(x_ref, g_ref, wq_ref, wt_ref, cos_ref, sa_ref, sb_ref,
                    qa_ref, ka_ref, va_ref, qb_ref, ks_ref, kw_ref, vst_ref, vwt_ref, kcvc_ref, gt_ref,
                    *, tm):
    tt = pl.program_id(1)
    x = x_ref[...]
    ms = jnp.mean(x * x, axis=-1, keepdims=True)
    h = (x * lax.rsqrt(ms + EPS) * g_ref[...]).astype(BF16)
    cos, sa, sb = cos_ref[...], sa_ref[...], sb_ref[...]
    scale = Q_SCALE

    def proj(c0, c1):
        return _dot(h, wq_ref[:, c0:c1])

    o = 0
    qa_ref[...] = (_rope_rows(proj(o, o + A_WIDTH), cos, sa, sb) * scale).astype(BF16)
    o += A_WIDTH
    ka_ref[...] = _rope_rows(proj(o, o + A_WIDTH), cos, sa, sb).astype(BF16)
    o += A_WIDTH
    va_ref[...] = proj(o, o + A_WIDTH).astype(BF16)
    o += A_WIDTH
    lane = lax.broadcasted_iota(jnp.int32, (tm, LANES), 1)
    row = lax.broadcasted_iota(jnp.int32, (tm, LANES), 0)
    lo = lane < HEAD_DIM
    qb = _rope_rows(proj(o, o + B_WIDTH), cos, sa, sb) * scale
    for hh in range(B_HEADS):
        ch = qb[:, (hh // 2) * LANES:(hh // 2 + 1) * LANES]
        if hh % 2:
            ch = pltpu.roll(ch, HEAD_DIM, 1)
        qb_ref[0, hh // B_GROUP, hh % B_GROUP] = jnp.where(lo, ch, 0.0).astype(BF16)
    o += B_WIDTH
    kcvc_ref[...] = proj(o, o + 2 * KV_WIDTH)
    o += 2 * KV_WIDTH
    ksw = _rope_rows(proj(o, o + 2 * KV_WIDTH), cos, sa, sb)

    blk = (tt * tm + row) >> SEL_SHIFT
    onehot = jnp.where(lane - SEL_LANE0 == blk, 1.0, 0.0)
    for kind, ref in ((0, ks_ref), (1, kw_ref)):
        kk = ksw[:, kind * LANES:(kind + 1) * LANES]
        tail = onehot if kind == 0 else 0.0
        ref[0, 0] = jnp.where(lo, kk, tail).astype(BF16)
        ref[0, 1] = jnp.where(lo, pltpu.roll(kk, HEAD_DIM, 1), tail).astype(BF16)

    tr = _dot_nt(wt_ref[...], h)
    ones_row = jnp.where(lax.broadcasted_iota(jnp.int32, (NSA_VROWS - HEAD_DIM, tm), 0) == 0, 1.0, 0.0)
    for kind, ref in ((0, vst_ref), (1, vwt_ref)):
        rows = [tr[kind * LANES + gg * HEAD_DIM:kind * LANES + (gg + 1) * HEAD_DIM] for gg in range(B_KV_HEADS)]
        slab = jnp.concatenate([rows[0], ones_row, rows[1], ones_row], axis=0).astype(BF16)
        for c in range(tm // NSA_CK):
            ref[0, c] = slab[:, c * NSA_CK:(c + 1) * NSA_CK]
    gt_ref[0] = jax.nn.sigmoid(tr[2 * LANES:2 * LANES + 2 * GATE_ROWS, :])


def _rope_tables(pos):
    half = ROT_DIM // 2
    inv = ROPE_THETA ** (-jnp.arange(0, ROT_DIM, 2, dtype=F32) / ROT_DIM)
    ang = pos.astype(F32)[:, None] * inv[None, :]
    cos, sin = jnp.cos(ang), jnp.sin(ang)
    n = pos.shape[0]
    ones = jnp.ones((n, HEAD_DIM - ROT_DIM), F32)
    zeros = jnp.zeros((n, HEAD_DIM - ROT_DIM), F32)
    zh = jnp.zeros((n, half), F32)
    c_head = jnp.concatenate([cos, cos, ones], axis=1)
    a_head = jnp.concatenate([-sin, zh, zeros], axis=1)
    b_head = jnp.concatenate([zh, sin, zeros], axis=1)
    rep = LANES // HEAD_DIM
    return jnp.tile(c_head, (1, rep)), jnp.tile(a_head, (1, rep)), jnp.tile(b_head, (1, rep))


def _in_proj(x, norm_g, w_in, tables, *, tm=1024):
    B, T, D = x.shape
    nt = T // tm
    offs = [0]
    for n in (A_WIDTH, A_WIDTH, A_WIDTH, B_WIDTH, KV_WIDTH, KV_WIDTH, KV_WIDTH, KV_WIDTH, KV_WIDTH, KV_WIDTH,
              B_HEADS * N_BRANCH):
        offs.append(offs[-1] + n)
    col = lambda i: w_in[:, offs[i]:offs[i + 1]]
    wq = jnp.concatenate([col(0), col(1), col(2), col(3), col(4), col(5), col(6), col(8)], axis=1).astype(BF16)
    gl = col(10)
    per_g = B_GROUP * N_BRANCH
    gpad = jnp.zeros((D, GATE_ROWS - per_g), w_in.dtype)
    wt = jnp.concatenate([col(7), col(9), gl[:, :per_g], gpad, gl[:, per_g:], gpad], axis=1).T.astype(BF16)
    cos, sa, sb = tables
    nq = wq.shape[1]
    nr = wt.shape[0]
    tok = lambda w: pl.BlockSpec((None, tm, w), lambda b, t: (b, t, 0))
    const = lambda shape: pl.BlockSpec(shape, lambda b, t: (0,) * len(shape))
    tab = pl.BlockSpec((tm, LANES), lambda b, t: (t, 0))
    out_shapes = (
        jax.ShapeDtypeStruct((B, T, A_WIDTH), BF16),
        jax.ShapeDtypeStruct((B, T, A_WIDTH), BF16),
        jax.ShapeDtypeStruct((B, T, A_WIDTH), BF16),
        jax.ShapeDtypeStruct((B, B_KV_HEADS, B_GROUP, T, LANES), BF16),
        jax.ShapeDtypeStruct((B, B_KV_HEADS, T, LANES), BF16),
        jax.ShapeDtypeStruct((B, B_KV_HEADS, T, LANES), BF16),
        jax.ShapeDtypeStruct((B, T // NSA_CK, B_KV_HEADS * NSA_VROWS, NSA_CK), BF16),
        jax.ShapeDtypeStruct((B, T // NSA_CK, B_KV_HEADS * NSA_VROWS, NSA_CK), BF16),
        jax.ShapeDtypeStruct((B, T, 2 * KV_WIDTH), F32),
        jax.ShapeDtypeStruct((B, 2 * GATE_ROWS, T), F32),
    )
    frame = pl.BlockSpec((1, B_KV_HEADS, tm, LANES), lambda b, t: (b, 0, t, 0))
    vt = pl.BlockSpec((1, tm // NSA_CK, B_KV_HEADS * NSA_VROWS, NSA_CK), lambda b, t: (b, t, 0, 0))
    qframe = pl.BlockSpec((1, B_KV_HEADS, B_GROUP, tm, LANES), lambda b, t: (b, 0, 0, t, 0))
    out_specs = (tok(A_WIDTH), tok(A_WIDTH), tok(A_WIDTH), qframe, frame, frame, vt, vt,
                 tok(2 * KV_WIDTH), pl.BlockSpec((1, 2 * GATE_ROWS, tm), lambda b, t: (b, 0, t)))
    return pl.pallas_call(
        functools.partial(_in_proj_kernel, tm=tm),
        grid=(B, nt),
        in_specs=[tok(D), const((1, D)), const((D, nq)), const((nr, D)), tab, tab, tab],
        out_specs=out_specs,
        out_shape=out_shapes,
        compiler_params=pltpu.CompilerParams(dimension_semantics=("parallel", "parallel"),
                                             vmem_limit_bytes=VMEM_LIMIT),
        name="in_proj",
    )(x, norm_g.reshape(1, D), wq, wt, cos, sa, sb)


def _compress_kernel(ak_ref, av_ref, w1k_ref, w1v_ref, pek_ref, pev_ref, w2k_ref, w2vt_ref, cos_ref, sa_ref, sb_ref,
                     kc_ref, vct_ref, *, nc_pad):
    half = CMP_LEN // 2
    hid_w = B_KV_HEADS * CMP_HIDDEN

    def hidden(a_ref, w1_ref, pe_ref):
        acc_u = jnp.zeros((nc_pad, hid_w), F32)
        acc_v = jnp.zeros((nc_pad, hid_w), F32)
        for p in range(half):
            ap = a_ref[0, pl.ds(p, nc_pad, stride=CMP_STRIDE), :]
            acc_u = acc_u + _dot((ap + pe_ref[p:p + 1, :]).astype(BF16), w1_ref[p])
            acc_v = acc_v + _dot((ap + pe_ref[half + p:half + p + 1, :]).astype(BF16), w1_ref[half + p])
        return jax.nn.gelu(acc_u + pltpu.roll(acc_v, nc_pad - 1, 0))

    hk = hidden(ak_ref, w1k_ref, pek_ref).astype(BF16)
    hv = hidden(av_ref, w1v_ref, pev_ref).astype(BF16)
    for g in range(B_KV_HEADS):
        hg = hk[:, g * CMP_HIDDEN:(g + 1) * CMP_HIDDEN]
        kc = _dot(hg, w2k_ref[...])
        kc_ref[0, g] = _rope_rows(kc, cos_ref[...], sa_ref[...], sb_ref[...]).astype(BF16)
        vg = hv[:, g * CMP_HIDDEN:(g + 1) * CMP_HIDDEN]
        vct_ref[0, g] = _dot_nt(w2vt_ref[...], vg).astype(BF16)


def _block_diag_w1(w1):
    w = w1.reshape(CMP_LEN, HEAD_DIM, CMP_HIDDEN)
    z = jnp.zeros_like(w)
    top = jnp.concatenate([w, z], axis=2)
    bot = jnp.concatenate([z, w], axis=2)
    return jnp.concatenate([top, bot], axis=1).astype(BF16)


def _compress(kcvc, pe_k, w1_k, w2_k, pe_v, w1_v, w2_v, cmp_tables):
    B, T, _ = kcvc.shape
    nc_pad = T // CMP_STRIDE
    w2k = jnp.concatenate([w2_k, jnp.zeros_like(w2_k)], axis=1).astype(BF16)
    w2vt = w2_v.T.astype(BF16)
    pek = jnp.tile(pe_k, (1, B_KV_HEADS))
    pev = jnp.tile(pe_v, (1, B_KV_HEADS))
    const = lambda shape: pl.BlockSpec(shape, lambda b: (0,) * len(shape))
    cos, sa, sb = cmp_tables
    return pl.pallas_call(
        functools.partial(_compress_kernel, nc_pad=nc_pad),
        grid=(B,),
        in_specs=[pl.BlockSpec((1, T, KV_WIDTH), lambda b: (b, 0, 0)), pl.BlockSpec((1, T, KV_WIDTH), lambda b: (b, 0, 1)),
                  const((CMP_LEN, LANES, B_KV_HEADS * CMP_HIDDEN)), const((CMP_LEN, LANES, B_KV_HEADS * CMP_HIDDEN)),
                  const((CMP_LEN, LANES)), const((CMP_LEN, LANES)),
                  const((CMP_HIDDEN, LANES)), const((HEAD_DIM, CMP_HIDDEN)),
                  const((nc_pad, LANES)), const((nc_pad, LANES)), const((nc_pad, LANES))],
        out_specs=(pl.BlockSpec((1, B_KV_HEADS, nc_pad, LANES), lambda b: (b, 0, 0, 0)),
                   pl.BlockSpec((1, B_KV_HEADS, HEAD_DIM, nc_pad), lambda b: (b, 0, 0, 0))),
        out_shape=(jax.ShapeDtypeStruct((B, B_KV_HEADS, nc_pad, LANES), BF16),
                   jax.ShapeDtypeStruct((B, B_KV_HEADS, HEAD_DIM, nc_pad), BF16)),
        compiler_params=pltpu.CompilerParams(dimension_semantics=("parallel",), vmem_limit_bytes=VMEM_LIMIT),
        name="compress",
    )(kcvc, kcvc, _block_diag_w1(w1_k), _block_diag_w1(w1_v), pek, pev, w2k, w2vt, cos, sa, sb)


def _mixer_a_kernel(q_ref, k_ref, v_ref, bias_ref, o_ref, nat, qd0, qd1, kd, vd0, vd1, u_s, m_s, l_s, *, seq):
    blk = A_BLOCK
    nres = seq // blk
    lane = lax.broadcasted_iota(jnp.int32, (blk, LANES), 1)
    lo = lane < HEAD_DIM
    for src, dsts in ((q_ref, (qd0, qd1)), (k_ref, (kd,)), (v_ref, (vd0, vd1))):
        nat[...] = src[0].astype(F32)
        for r in range(nres):
            rows = nat[pl.ds(r, blk, stride=nres), :]
            if src is k_ref:
                kd[pl.ds(r * blk, blk), :] = rows
            else:
                fill = 0.0 if src is q_ref else 1.0
                dsts[0][pl.ds(r * blk, blk), :] = jnp.where(lo, rows, fill)
                dsts[1][pl.ds(r * blk, blk), :] = jnp.where(lo, fill, rows)

    def pieces(dil, rd, row_off, rows):
        return [pl.ds(pl.multiple_of((rd + dil * jj) * blk + row_off, 8), rows) for jj in range(nres // dil)]

    def gather(ref, idx):
        parts = [ref[i, :] for i in idx]
        return parts[0] if len(parts) == 1 else jnp.concatenate(parts, axis=0)

    def attend(pi, dil, blocks):
        pr = blk // (nres // dil)
        q_idxs, vbs, scores = [], [], []
        for rd, n, first in blocks:
            q_idx = pieces(dil, rd, n * pr, pr)
            k_idx = q_idx if first else pieces(dil, rd, (n - 1) * pr, 2 * pr)
            bias = bias_ref[pi, :, 0:blk] if first else bias_ref[pi, :, blk:3 * blk]
            kb = gather(kd, k_idx).astype(BF16)
            q_idxs.append(q_idx)
            for qd, vd in ((qd0, vd0), (qd1, vd1)):
                vbs.append(gather(vd, k_idx).astype(BF16))
                scores.append(_dot_nt(gather(qd, q_idx).astype(BF16), kb) + bias)
        es, ms = [], []
        for s in scores:
            m = jnp.max(s, axis=-1, keepdims=True)
            ms.append(m)
            es.append(jnp.exp2((s - m).astype(BF16)))
        pvs = [_dot(e, vb) for e, vb in zip(es, vbs)]
        for b, q_idx in enumerate(q_idxs):
            u = jnp.where(lo, pvs[2 * b], pvs[2 * b + 1])
            l_swapped = jnp.where(lo, pvs[2 * b + 1], pvs[2 * b])
            m = jnp.where(lo, ms[2 * b], ms[2 * b + 1])
            for jj, idx in enumerate(q_idx):
                u_s[pi, idx, :] = u[jj * pr:(jj + 1) * pr]
                m_s[pi, idx, :] = m[jj * pr:(jj + 1) * pr]
                l_s[pi, idx, :] = l_swapped[jj * pr:(jj + 1) * pr]

    for pi, (_, dil) in enumerate(A_PATTERNS):
        nb = seq // dil // blk
        if nb <= A_GROUP:
            per = min(A_GROUP // nb, dil)

            def residues(i, carry, pi=pi, dil=dil, nb=nb, per=per):
                attend(pi, dil, [(i * per + jj, n, n == 0) for jj in range(per) for n in range(nb)])
                return carry

            lax.fori_loop(0, dil // per, residues, 0)
        else:
            group = A_GROUP

            def residue(rd, carry, pi=pi, dil=dil, nb=nb, group=group):
                attend(pi, dil, [(rd, 0, True)] + [(rd, n, False) for n in range(1, group)])

                def inner(i, c):
                    attend(pi, dil, [(rd, i * group + jj, False) for jj in range(group)])
                    return c

                return lax.fori_loop(1, nb // group, inner, carry)

            lax.fori_loop(0, dil, residue, 0)

    rows = 2 * blk

    def merge(c, carry):
        r0 = pl.multiple_of(c * rows, rows)
        ms = [m_s[p, pl.ds(r0, rows), :] for p in range(len(A_PATTERNS))]
        m_all = functools.reduce(jnp.maximum, ms)
        num = jnp.zeros((rows, LANES), F32)
        den = jnp.zeros((rows, LANES), F32)
        for p in range(len(A_PATTERNS)):
            a = jnp.exp2(ms[p] - m_all)
            num = num + a * u_s[p, pl.ds(r0, rows), :]
            den = den + a * pltpu.roll(l_s[p, pl.ds(r0, rows), :], HEAD_DIM, 1)
        out = num / den
        for jj in range(rows // blk):
            o_ref[0, pl.ds(c * (rows // blk) + jj, blk, stride=nres), :] = out[jj * blk:(jj + 1) * blk]
        return carry

    lax.fori_loop(0, seq // rows, merge, 0)


def _mixer_a_bias(nres):
    blk = A_BLOCK

    def sub_pos(i, fold, rows):
        return fold * (i % rows) + i // rows

    out = np.zeros((len(A_PATTERNS), blk, 3 * blk), np.float32)
    for pi, (window, dil) in enumerate(A_PATTERNS):
        n_back = window // dil
        fold = nres // dil
        pr = blk // fold
        sq = sub_pos(np.arange(blk), fold, pr)[:, None]
        d_first = sq - sub_pos(np.arange(blk), fold, pr)[None, :]
        d_band = sq + blk - sub_pos(np.arange(2 * blk), fold, 2 * pr)[None, :]
        dist = np.concatenate([d_first, d_band], axis=1)
        out[pi] = np.where((dist >= 0) & (dist <= n_back), 0.0, NEG)
    return jnp.asarray(out)


def _mixer_a(qa, ka, va):
    B, T, W = qa.shape
    npair = W // LANES
    spec = pl.BlockSpec((1, T, LANES), lambda b, p: (b, 0, p))
    npat = len(A_PATTERNS)
    nres = T // A_BLOCK
    assert all(nres % d == 0 and T % (d * A_BLOCK) == 0 for _, d in A_PATTERNS) and max(d for _, d in A_PATTERNS) == nres
    return pl.pallas_call(
        functools.partial(_mixer_a_kernel, seq=T),
        grid=(B, npair),
        in_specs=[spec, spec, spec, pl.BlockSpec((npat, A_BLOCK, 3 * A_BLOCK), lambda b, p: (0, 0, 0))],
        out_specs=spec,
        out_shape=jax.ShapeDtypeStruct((B, T, W), F32),
        scratch_shapes=[pltpu.VMEM((T, LANES), F32)] * 6 + [pltpu.VMEM((npat, T, LANES), F32)] * 3,
        compiler_params=pltpu.CompilerParams(dimension_semantics=("parallel", "parallel"),
                                             vmem_limit_bytes=VMEM_LIMIT),
        name="mixer_a",
    )(qa, ka, va, _mixer_a_bias(nres))


def _nsa_kernel(q_ref, ks_ref, kw_ref, vst_ref, vwt_ref, kc_ref, vct_ref, gt_ref, ovt_ref, bias_ref, cbias_ref, o_ref,
                *, n_sel_blocks, nq):
    tq, ck = NSA_TQ, NSA_CK
    nrow = B_GROUP * tq
    gw = B_GROUP * HEAD_DIM
    step = pl.program_id(1)
    per_part = nq // NSA_SPLIT
    lanes = [(h, g) for h in range(NSA_SPLIT) for g in range(B_KV_HEADS)]
    qi = [step + h * per_part for h in range(NSA_SPLIT)]
    col_i = lax.broadcasted_iota(jnp.int32, (1, nrow), 1) & (tq - 1)

    def chunk(ref, g, c):
        return ref[0, g, pl.ds(pl.multiple_of(c * ck, ck), ck), :]

    def vt_chunk(ref, g, c):
        return ref[0, c, g * NSA_VROWS:(g + 1) * NSA_VROWS, :]

    def softmax_pv(s, vt):
        m = jnp.max(s, axis=0, keepdims=True)
        return m, _dot(vt, jnp.exp2(s - m).astype(BF16))

    def merged(parts):
        m_new = functools.reduce(jnp.maximum, [m for m, _ in parts])
        acc = sum(jnp.exp2(m - m_new) * a for m, a in parts)
        return m_new, acc

    def normalised(state):
        return state[1][0:HEAD_DIM] * (1.0 / state[1][HEAD_DIM:HEAD_DIM + 1])

    q_b = {(h, g): jnp.concatenate([q_ref[0, g, r, h] for r in range(B_GROUP)], axis=0) for h, g in lanes}

    nwin = WIN // ck
    win_chunks = [[jnp.maximum(qi[h] - back, 0) for back in range(nwin, -1, -1)] for h in range(NSA_SPLIT)]
    s_win = {}
    for h, g in lanes:
        s = _dot_nt(jnp.concatenate([chunk(kw_ref, g, c) for c in win_chunks[h]], axis=0), q_b[h, g])
        pieces = [jnp.where(qi[h] >= nwin, s[0:ck] + bias_ref[1], NEG)]
        for n in range(1, nwin):
            pieces.append(jnp.where(qi[h] >= nwin - n, s[n * ck:(n + 1) * ck], NEG))
        pieces.append(s[nwin * ck:] + bias_ref[0])
        s_win[h, g] = pieces

    s_cmp = {(h, g): _dot_nt(kc_ref[0, g], q_b[h, g]) + cbias_ref[h, 0] for h, g in lanes}
    p_cmp, o_cmp = {}, {}
    for h, g in lanes:
        m = jnp.max(s_cmp[h, g], axis=0, keepdims=True)
        e = jnp.exp2(s_cmp[h, g] - m)
        den = jnp.sum(e, axis=0, keepdims=True)
        sees_block = qi[h] * tq + col_i >= CMP_LEN - 1
        p_cmp[h, g] = e * jnp.where(sees_block, 1.0 / jnp.maximum(den, 1e-30), 0.0)
        o_cmp[h, g] = _dot(vct_ref[0, g], p_cmp[h, g].astype(BF16))

    imp = {}
    for h, g in lanes:
        psum = p_cmp[h, g][:, 0:tq]
        for r in range(1, B_GROUP):
            psum = psum + p_cmp[h, g][:, r * tq:(r + 1) * tq]
        p_hi = psum.astype(BF16)
        p_lo = (psum - p_hi.astype(F32)).astype(BF16)
        imp[h, g] = _dot(ovt_ref[...], p_hi) + _dot(ovt_ref[...], p_lo)
    j = lax.broadcasted_iota(jnp.int32, (n_sel_blocks, tq), 0)
    j_f = j.astype(F32)
    low = -3e38
    q_aug = {}
    for h, g in lanes:
        cur = (qi[h] * tq + lax.broadcasted_iota(jnp.int32, (n_sel_blocks, tq), 1)) >> SEL_SHIFT
        forced = (j == 0) | (j == cur) | (j == cur - 1)
        score = jnp.where(forced, imp[h, g] + 2.0, jnp.where(j > cur, -1.0, imp[h, g]))
        sel = jnp.zeros((n_sel_blocks, tq), jnp.bool_)
        for _ in range(min(SEL_TOPK, n_sel_blocks)):
            mx = jnp.max(score, axis=0, keepdims=True)
            first = jnp.min(jnp.where(score == mx, j_f, 4.0 * LANES), axis=0, keepdims=True)
            hit = j_f == first
            sel = sel | hit
            score = jnp.where(hit, low, score)
        selneg = jnp.concatenate([jnp.zeros((SEL_LANE0, tq), F32), jnp.where(sel, 0.0, NEG),
                                  jnp.zeros((LANES - SEL_LANE0 - n_sel_blocks, tq), F32)], axis=0).T
        q_aug[h, g] = q_b[h, g] + jnp.concatenate([selneg.astype(BF16)] * B_GROUP, axis=0)

    s_diag = {(h, g): _dot_nt(chunk(ks_ref, g, qi[h]), q_aug[h, g]) + bias_ref[0] for h, g in lanes}
    o_win = {(h, g): normalised(merged([softmax_pv(s_win[h, g][n], vt_chunk(vwt_ref, g, c))
                                        for n, c in enumerate(win_chunks[h])])) for h, g in lanes}
    states = tuple(softmax_pv(s_diag[h, g], vt_chunk(vst_ref, g, qi[h])) for h, g in lanes)

    def earlier(work, st):
        s = [_dot_nt(chunk(ks_ref, lanes[n][1], c), q_aug[lanes[n]]) for n, c in work]
        parts = [[st[n]] for n in range(len(lanes))]
        for (n, c), sc in zip(work, s):
            parts[n].append(softmax_pv(sc, vt_chunk(vst_ref, lanes[n][1], c)))
        return tuple(merged(p) if len(p) > 1 else p[0] for p in parts)

    states = lax.fori_loop(0, step, lambda c, st: earlier([(n, c) for n in range(len(lanes))], st), states)
    for h in range(1, NSA_SPLIT):
        mine = [n for n, (hh, _) in enumerate(lanes) if hh >= h]
        pairs = per_part // 2
        states = lax.fori_loop(
            0, pairs,
            lambda i, st, mine=mine, h=h: earlier(
                [(n, step + (h - 1) * per_part + 2 * i + k) for n in mine for k in range(2)], st), states)

    for n, (h, g) in enumerate(lanes):
        o_sel = normalised(states[n])
        gt = gt_ref[0, g * GATE_ROWS:(g + 1) * GATE_ROWS, h, :]
        outs = []
        for r in range(B_GROUP):
            cs = slice(r * tq, (r + 1) * tq)
            outs.append(gt[r * N_BRANCH:r * N_BRANCH + 1, :] * o_cmp[h, g][:, cs]
                        + gt[r * N_BRANCH + 1:r * N_BRANCH + 2, :] * o_sel[:, cs]
                        + gt[r * N_BRANCH + 2:r * N_BRANCH + 3, :] * o_win[h, g][:, cs])
        for c in range(B_GROUP // 2):
            pair = jnp.concatenate([outs[2 * c], outs[2 * c + 1]], axis=0)
            o_ref[0, h, :, g * gw + c * LANES:g * gw + (c + 1) * LANES] = pair.T


def _overlap_t(nc_pad, ns):
    nc = nc_pad - 1
    c0 = jnp.arange(nc_pad) * CMP_STRIDE
    s0 = jnp.arange(ns) * SEL_BLOCK
    ov = jnp.minimum(c0[None, :] + CMP_LEN, s0[:, None] + SEL_BLOCK) - jnp.maximum(c0[None, :], s0[:, None])
    ov = jnp.clip(ov, 0, None).astype(F32) / CMP_LEN
    ov = jnp.where(jnp.arange(nc_pad)[None, :] < nc, ov, 0.0)
    return ov.astype(BF16)


def _nsa(qb, ks, kw, vst, vwt, kc, vct, gt):
    B, _, _, T, _ = qb.shape
    W = B_WIDTH
    nq = T // NSA_TQ
    ns = T // SEL_BLOCK
    nc_pad = kc.shape[2]
    nrow = B_GROUP * NSA_TQ
    part = T // NSA_SPLIT
    assert NSA_TQ == NSA_CK and WIN % NSA_CK == 0 and SEL_LANE0 + ns <= LANES and (nq // NSA_SPLIT) % 2 == 0
    kspec = pl.BlockSpec((1, B_KV_HEADS, T, LANES), lambda b, i: (b, 0, 0, 0))
    vspec = pl.BlockSpec((1, T // NSA_CK, B_KV_HEADS * NSA_VROWS, NSA_CK), lambda b, i: (b, 0, 0, 0))
    key = np.arange(NSA_CK)[:, None]
    qry = (np.arange(nrow) % NSA_TQ)[None, :]
    bias = jnp.asarray(np.stack([np.where(key <= qry, 0.0, NEG), np.where(key > qry, 0.0, NEG)]).astype(np.float32))
    cmp_end = (np.arange(nc_pad) * CMP_STRIDE + CMP_LEN - 1)[None, :, None]
    t_query = (np.arange(nq) * NSA_TQ)[:, None, None] + qry[None]
    cbias = jnp.asarray(np.where((cmp_end <= t_query) & (np.arange(nc_pad) < nc_pad - 1)[None, :, None], 0.0, NEG)
                        .astype(np.float32)).reshape(NSA_SPLIT, nq // NSA_SPLIT, nc_pad, nrow)
    out = pl.pallas_call(
        functools.partial(_nsa_kernel, n_sel_blocks=ns, nq=nq),
        grid=(B, nq // NSA_SPLIT),
        in_specs=[pl.BlockSpec((1, B_KV_HEADS, B_GROUP, NSA_SPLIT, NSA_TQ, LANES), lambda b, i: (b, 0, 0, 0, i, 0)),
                  kspec, kspec, vspec, vspec,
                  pl.BlockSpec((1, B_KV_HEADS, nc_pad, LANES), lambda b, i: (b, 0, 0, 0)),
                  pl.BlockSpec((1, B_KV_HEADS, HEAD_DIM, nc_pad), lambda b, i: (b, 0, 0, 0)),
                  pl.BlockSpec((1, B_KV_HEADS * GATE_ROWS, NSA_SPLIT, NSA_TQ), lambda b, i: (b, 0, 0, i)),
                  pl.BlockSpec((ns, nc_pad), lambda b, i: (0, 0)),
                  pl.BlockSpec((2, NSA_CK, nrow), lambda b, i: (0, 0, 0)),
                  pl.BlockSpec((NSA_SPLIT, 1, nc_pad, nrow), lambda b, i: (0, i, 0, 0))],
        out_specs=pl.BlockSpec((1, NSA_SPLIT, NSA_TQ, W), lambda b, i: (b, 0, i, 0)),
        out_shape=jax.ShapeDtypeStruct((B, NSA_SPLIT, part, W), F32),
        compiler_params=pltpu.CompilerParams(dimension_semantics=("parallel", "arbitrary"),
                                             vmem_limit_bytes=VMEM_LIMIT),
        name="nsa",
    )(qb.reshape(B, B_KV_HEADS, B_GROUP, NSA_SPLIT, part, LANES), ks, kw, vst, vwt, kc, vct,
      gt.reshape(B, B_KV_HEADS * GATE_ROWS, NSA_SPLIT, part), _overlap_t(nc_pad, ns), bias, cbias)
    return out.reshape(B, T, W)


def _post_kernel(x_ref, oa_ref, ob_ref, ga_ref, gb_ref, wo_ref, gm_ref, wu_ref, wd_ref, gf_ref, o_ref, *, final):
    def norm(v, g):
        return v * lax.rsqrt(jnp.mean(v * v, axis=-1, keepdims=True) + EPS) * g

    na = norm(oa_ref[...], ga_ref[...]).astype(BF16)
    nb = norm(ob_ref[...], gb_ref[...]).astype(BF16)
    aw = na.shape[1]
    h_res = x_ref[...] + _dot(na, wo_ref[0:aw, :]) + _dot(nb, wo_ref[aw:, :])
    h = norm(h_res, gm_ref[...]).astype(BF16)
    u = jnp.square(jnp.maximum(_dot(h, wu_ref[...]), 0.0)).astype(BF16)
    acc = h_res + _dot(u, wd_ref[...])
    o_ref[...] = norm(acc, gf_ref[...]) if final else acc


def _post(x, oa, ob, g_a, g_b, w_out, g_mlp, w_up, w_down, g_final, *, final, tm=512):
    B, T, D = x.shape
    n = B * T
    dff = w_up.shape[1]
    tok = lambda w: pl.BlockSpec((tm, w), lambda i: (i, 0))
    const = lambda shape: pl.BlockSpec(shape, lambda i: (0, 0), pipeline_mode=pl.Buffered(1))
    out = pl.pallas_call(
        functools.partial(_post_kernel, final=final),
        grid=(n // tm,),
        in_specs=[tok(D), tok(A_WIDTH), tok(B_WIDTH), const((1, A_WIDTH)), const((1, B_WIDTH)),
                  const((A_WIDTH + B_WIDTH, D)), const((1, D)), const((D, dff)), const((dff, D)), const((1, D))],
        out_specs=tok(D),
        out_shape=jax.ShapeDtypeStruct((n, D), F32),
        compiler_params=pltpu.CompilerParams(dimension_semantics=("parallel",), vmem_limit_bytes=VMEM_LIMIT),
        name="post",
    )(x.reshape(n, D), oa.reshape(n, A_WIDTH), ob.reshape(n, B_WIDTH), g_a.reshape(1, -1), g_b.reshape(1, -1),
      w_out.astype(BF16), g_mlp.reshape(1, D), w_up.astype(BF16), w_down.astype(BF16), g_final.reshape(1, D))
    return out.reshape(B, T, D)


def kernel(x, norm_mix, w_in, cmp_pe_k, cmp_w1_k, cmp_w2_k, cmp_pe_v, cmp_w1_v, cmp_w2_v, g_out_a, g_out_b,
           w_out, norm_mlp, w_up, w_down, norm_final):
    B, T, D = x.shape
    depth = w_in.shape[0]
    tables = _rope_tables(jnp.arange(T))
    cmp_tables = _rope_tables(jnp.arange(T // CMP_STRIDE) * CMP_STRIDE + CMP_LEN - 1)
    h_res = x
    for l in range(depth):
        qa, ka, va, qb, ks, kw, vst, vwt, kcvc, gt = _in_proj(h_res, norm_mix[l], w_in[l], tables)
        kc, vct = _compress(kcvc, cmp_pe_k[l], cmp_w1_k[l], cmp_w2_k[l], cmp_pe_v[l], cmp_w1_v[l], cmp_w2_v[l],
                            cmp_tables)
        oa = _mixer_a(qa, ka, va)
        ob = _nsa(qb, ks, kw, vst, vwt, kc, vct, gt)
        h_res = _post(h_res, oa, ob, g_out_a[l], g_out_b[l], w_out[l], norm_mlp[l], w_up[l], w_down[l], norm_final,
                      final=(l == depth - 1))
    return h_res
```

```python
import functools

import jax
import jax.numpy as jnp
import numpy as np
from jax import lax
from jax.experimental import pallas as pl
from jax.experimental.pallas import tpu as pltpu

F32 = jnp.float32
BF16 = jnp.bfloat16

HEAD_DIM = 64
ROT_DIM = HEAD_DIM // 4
ROPE_THETA = 500000.0
EPS = 1e-6
NEG = -1e30
Q_SCALE = HEAD_DIM ** -0.5 * 1.4426950408889634
LANES = 128

A_HEADS = 8
A_PATTERNS = ((128, 1), (512, 4), (2048, 16))
A_BLOCK = 128
A_GROUP = 16

B_HEADS = 8
B_KV_HEADS = 2
B_GROUP = B_HEADS // B_KV_HEADS
CMP_LEN = 32
CMP_STRIDE = 16
CMP_HIDDEN = 256
SEL_BLOCK = 64
SEL_SHIFT = 6
SEL_TOPK = 8
WIN = 512
N_BRANCH = 3

A_WIDTH = A_HEADS * HEAD_DIM
B_WIDTH = B_HEADS * HEAD_DIM
KV_WIDTH = B_KV_HEADS * HEAD_DIM

NSA_TQ = 256
NSA_CK = 256
NSA_SPLIT = 2
NSA_VROWS = 80
GATE_ROWS = 16
SEL_LANE0 = HEAD_DIM

VMEM_LIMIT = 56 * 1024 * 1024


def _dot(a, b):
    return jnp.dot(a, b, preferred_element_type=F32)


def _dot_nt(a, b):
    return lax.dot_general(a, b, (((1,), (1,)), ((), ())), preferred_element_type=F32)


def _rope_rows(y, cos, sin_a, sin_b):
    outs = []
    for c in range(y.shape[1] // LANES):
        yc = y[:, c * LANES:(c + 1) * LANES]
        outs.append(yc * cos + pltpu.roll(yc, LANES - ROT_DIM // 2, 1) * sin_a
                    + pltpu.roll(yc, ROT_DIM // 2, 1) * sin_b)
    return outs[0] if len(outs) == 1 else jnp.concatenate(outs, axis=1)


def _in_proj_kernel(x_ref, g_ref, wq_ref, wkk_ref, wt_ref, cos_ref, sa_ref, sb_ref,
                    qa_ref, ka_ref, va_ref, qb_ref, ks_ref, kw_ref, vst_ref, vwt_ref, kcvc_ref, gt_ref,
                    *, tm):
    tt = pl.program_id(1)
    x = x_ref[...]
    ms = jnp.mean(x * x, axis=-1, keepdims=True)
    h = (x * lax.rsqrt(ms + EPS) * g_ref[...]).astype(BF16)
    cos, sa, sb = cos_ref[...], sa_ref[...], sb_ref[...]
    scale = Q_SCALE

    def proj(c0, c1):
        return _dot(h, wq_ref[:, c0:c1])

    o = 0
    qa_ref[...] = (_rope_rows(proj(o, o + A_WIDTH), cos, sa, sb) * scale).astype(BF16)
    o += A_WIDTH
    ka_ref[...] = _rope_rows(proj(o, o + A_WIDTH), cos, sa, sb).astype(BF16)
    o += A_WIDTH
    va_ref[...] = proj(o, o + A_WIDTH).astype(BF16)
    o += A_WIDTH
    lane = lax.broadcasted_iota(jnp.int32, (tm, LANES), 1)
    row = lax.broadcasted_iota(jnp.int32, (tm, LANES), 0)
    lo = lane < HEAD_DIM
    qb = _rope_rows(proj(o, o + B_WIDTH), cos, sa, sb) * scale
    for hh in range(B_HEADS):
        ch = qb[:, (hh // 2) * LANES:(hh // 2 + 1) * LANES]
        if hh % 2:
            ch = pltpu.roll(ch, HEAD_DIM, 1)
        qb_ref[0, hh // B_GROUP, hh % B_GROUP] = jnp.where(lo, ch, 0.0).astype(BF16)
    o += B_WIDTH
    kcvc_ref[...] = proj(o, o + 2 * KV_WIDTH)
    o += 2 * KV_WIDTH
    ksw = _rope_rows(_dot(h, wkk_ref[...]), cos, sa, sb)

    blk = (tt * tm + row) >> SEL_SHIFT
    onehot = jnp.where(lane - SEL_LANE0 == blk, 1.0, 0.0)
    for kind, ref in ((0, ks_ref), (1, kw_ref)):
        kk = ksw[:, kind * LANES:(kind + 1) * LANES]
        tail = onehot if kind == 0 else 0.0
        ref[0, 0] = jnp.where(lo, kk, tail).astype(BF16)
        ref[0, 1] = jnp.where(lo, pltpu.roll(kk, HEAD_DIM, 1), tail).astype(BF16)

    tr = _dot_nt(wt_ref[...], h)
    ones_row = jnp.where(lax.broadcasted_iota(jnp.int32, (NSA_VROWS - HEAD_DIM, tm), 0) == 0, 1.0, 0.0)
    for kind, ref in ((0, vst_ref), (1, vwt_ref)):
        rows = [tr[kind * LANES + gg * HEAD_DIM:kind * LANES + (gg + 1) * HEAD_DIM] for gg in range(B_KV_HEADS)]
        slab = jnp.concatenate([rows[0], ones_row, rows[1], ones_row], axis=0).astype(BF16)
        for c in range(tm // NSA_CK):
            ref[0, c] = slab[:, c * NSA_CK:(c + 1) * NSA_CK]
    gt_ref[0] = jax.nn.sigmoid(tr[2 * LANES:2 * LANES + 2 * GATE_ROWS, :])


def _rope_tables(pos):
    half = ROT_DIM // 2
    inv = ROPE_THETA ** (-np.arange(0, ROT_DIM, 2, dtype=np.float64) / ROT_DIM)
    ang = np.asarray(pos, np.float64)[:, None] * inv[None, :]
    cos, sin = np.cos(ang), np.sin(ang)
    n = len(pos)
    ones = np.ones((n, HEAD_DIM - ROT_DIM))
    zeros = np.zeros((n, HEAD_DIM - ROT_DIM))
    zh = np.zeros((n, half))
    c_head = np.concatenate([cos, cos, ones], axis=1)
    a_head = np.concatenate([-sin, zh, zeros], axis=1)
    b_head = np.concatenate([zh, sin, zeros], axis=1)
    rep = LANES // HEAD_DIM
    return tuple(jnp.asarray(np.tile(t, (1, rep)).astype(np.float32)) for t in (c_head, a_head, b_head))


def _in_proj(x, norm_g, w_in, tables, *, tm=1024):
    B, T, D = x.shape
    nt = T // tm
    offs = [0]
    for n in (A_WIDTH, A_WIDTH, A_WIDTH, B_WIDTH, KV_WIDTH, KV_WIDTH, KV_WIDTH, KV_WIDTH, KV_WIDTH, KV_WIDTH,
              B_HEADS * N_BRANCH):
        offs.append(offs[-1] + n)
    col = lambda i: w_in[:, offs[i]:offs[i + 1]]
    wq = w_in[:, :offs[6]].astype(BF16)
    wkk = jnp.concatenate([col(6), col(8)], axis=1).astype(BF16)
    gl = col(10)
    per_g = B_GROUP * N_BRANCH
    gpad = jnp.zeros((D, GATE_ROWS - per_g), w_in.dtype)
    wt = jnp.concatenate([col(7), col(9), gl[:, :per_g], gpad, gl[:, per_g:], gpad], axis=1).T.astype(BF16)
    cos, sa, sb = tables
    nq = wq.shape[1]
    nr = wt.shape[0]
    tok = lambda w: pl.BlockSpec((None, tm, w), lambda b, t: (b, t, 0))
    const = lambda shape: pl.BlockSpec(shape, lambda b, t: (0,) * len(shape))
    tab = pl.BlockSpec((tm, LANES), lambda b, t: (t, 0))
    out_shapes = (
        jax.ShapeDtypeStruct((B, T, A_WIDTH), BF16),
        jax.ShapeDtypeStruct((B, T, A_WIDTH), BF16),
        jax.ShapeDtypeStruct((B, T, A_WIDTH), BF16),
        jax.ShapeDtypeStruct((B, B_KV_HEADS, B_GROUP, T, LANES), BF16),
        jax.ShapeDtypeStruct((B, B_KV_HEADS, T, LANES), BF16),
        jax.ShapeDtypeStruct((B, B_KV_HEADS, T, LANES), BF16),
        jax.ShapeDtypeStruct((B, T // NSA_CK, B_KV_HEADS * NSA_VROWS, NSA_CK), BF16),
        jax.ShapeDtypeStruct((B, T // NSA_CK, B_KV_HEADS * NSA_VROWS, NSA_CK), BF16),
        jax.ShapeDtypeStruct((B, T, 2 * KV_WIDTH), F32),
        jax.ShapeDtypeStruct((B, 2 * GATE_ROWS, T), F32),
    )
    frame = pl.BlockSpec((1, B_KV_HEADS, tm, LANES), lambda b, t: (b, 0, t, 0))
    vt = pl.BlockSpec((1, tm // NSA_CK, B_KV_HEADS * NSA_VROWS, NSA_CK), lambda b, t: (b, t, 0, 0))
    qframe = pl.BlockSpec((1, B_KV_HEADS, B_GROUP, tm, LANES), lambda b, t: (b, 0, 0, t, 0))
    out_specs = (tok(A_WIDTH), tok(A_WIDTH), tok(A_WIDTH), qframe, frame, frame, vt, vt,
                 tok(2 * KV_WIDTH), pl.BlockSpec((1, 2 * GATE_ROWS, tm), lambda b, t: (b, 0, t)))
    return pl.pallas_call(
        functools.partial(_in_proj_kernel, tm=tm),
        grid=(B, nt),
        in_specs=[tok(D), const((1, D)), const((D, nq)), const((D, 2 * KV_WIDTH)), const((nr, D)), tab, tab, tab],
        out_specs=out_specs,
        out_shape=out_shapes,
        compiler_params=pltpu.CompilerParams(dimension_semantics=("parallel", "parallel"),
                                             vmem_limit_bytes=VMEM_LIMIT),
        name="in_proj",
    )(x, norm_g.reshape(1, D), wq, wkk, wt, cos, sa, sb)


def _compress_kernel(ak_ref, av_ref, w1k_ref, w1v_ref, pek_ref, pev_ref, w2k_ref, w2vt_ref, cos_ref, sa_ref, sb_ref,
                     kc_ref, vct_ref, *, nc_pad):
    half = CMP_LEN // 2
    hid_w = B_KV_HEADS * CMP_HIDDEN

    zeros = jnp.zeros((HEAD_DIM, CMP_HIDDEN), BF16)

    def both_groups(w):
        return jnp.concatenate([jnp.concatenate([w, zeros], axis=1), jnp.concatenate([zeros, w], axis=1)], axis=0)

    def hidden(a_ref, w1_ref, pe_ref):
        acc_u = jnp.zeros((nc_pad, hid_w), F32)
        acc_v = jnp.zeros((nc_pad, hid_w), F32)
        for p in range(half):
            ap = a_ref[0, pl.ds(p, nc_pad, stride=CMP_STRIDE), :]
            acc_u = acc_u + _dot((ap + pe_ref[p:p + 1, :]).astype(BF16), both_groups(w1_ref[p]))
            acc_v = acc_v + _dot((ap + pe_ref[half + p:half + p + 1, :]).astype(BF16), both_groups(w1_ref[half + p]))
        return jax.nn.gelu(acc_u + pltpu.roll(acc_v, nc_pad - 1, 0))

    hk = hidden(ak_ref, w1k_ref, pek_ref).astype(BF16)
    hv = hidden(av_ref, w1v_ref, pev_ref).astype(BF16)
    for g in range(B_KV_HEADS):
        hg = hk[:, g * CMP_HIDDEN:(g + 1) * CMP_HIDDEN]
        kc = _dot(hg, w2k_ref[...])
        kc_ref[0, g] = _rope_rows(kc, cos_ref[...], sa_ref[...], sb_ref[...]).astype(BF16)
        vg = hv[:, g * CMP_HIDDEN:(g + 1) * CMP_HIDDEN]
        vct_ref[0, g] = _dot_nt(w2vt_ref[...], vg).astype(BF16)


def _compress(kcvc, pe_k, w1_k, w2_k, pe_v, w1_v, w2_v, cmp_tables):
    B, T, _ = kcvc.shape
    nc_pad = T // CMP_STRIDE
    w2k = jnp.concatenate([w2_k, jnp.zeros_like(w2_k)], axis=1).astype(BF16)
    w2vt = w2_v.T.astype(BF16)
    pek = jnp.tile(pe_k, (1, B_KV_HEADS))
    pev = jnp.tile(pe_v, (1, B_KV_HEADS))
    per_pos = lambda w1: w1.reshape(CMP_LEN, HEAD_DIM, CMP_HIDDEN).astype(BF16)
    const = lambda shape: pl.BlockSpec(shape, lambda b: (0,) * len(shape))
    cos, sa, sb = cmp_tables
    return pl.pallas_call(
        functools.partial(_compress_kernel, nc_pad=nc_pad),
        grid=(B,),
        in_specs=[pl.BlockSpec((1, T, KV_WIDTH), lambda b: (b, 0, 0)), pl.BlockSpec((1, T, KV_WIDTH), lambda b: (b, 0, 1)),
                  const((CMP_LEN, HEAD_DIM, CMP_HIDDEN)), const((CMP_LEN, HEAD_DIM, CMP_HIDDEN)),
                  const((CMP_LEN, LANES)), const((CMP_LEN, LANES)),
                  const((CMP_HIDDEN, LANES)), const((HEAD_DIM, CMP_HIDDEN)),
                  const((nc_pad, LANES)), const((nc_pad, LANES)), const((nc_pad, LANES))],
        out_specs=(pl.BlockSpec((1, B_KV_HEADS, nc_pad, LANES), lambda b: (b, 0, 0, 0)),
                   pl.BlockSpec((1, B_KV_HEADS, HEAD_DIM, nc_pad), lambda b: (b, 0, 0, 0))),
        out_shape=(jax.ShapeDtypeStruct((B, B_KV_HEADS, nc_pad, LANES), BF16),
                   jax.ShapeDtypeStruct((B, B_KV_HEADS, HEAD_DIM, nc_pad), BF16)),
        compiler_params=pltpu.CompilerParams(dimension_semantics=("parallel",), vmem_limit_bytes=VMEM_LIMIT),
        name="compress",
    )(kcvc, kcvc, per_pos(w1_k), per_pos(w1_v), pek, pev, w2k, w2vt, cos, sa, sb)


def _mixer_a_kernel(q_ref, k_ref, v_ref, bias_ref, o_ref, nat_q, nat_k, nat_v, qd0, qd1, kd, vd0, vd1, u_s, m_s, l_s,
                    *, seq):
    blk = A_BLOCK
    nres = seq // blk
    npat = len(A_PATTERNS)
    order = sorted(range(npat), key=lambda p: -A_PATTERNS[p][1])
    slot = {p: n for n, p in enumerate(order[:-1])}
    lane = lax.broadcasted_iota(jnp.int32, (blk, LANES), 1)
    lo = lane < HEAD_DIM
    nat_q[...] = q_ref[0].astype(F32)
    nat_k[...] = k_ref[0].astype(F32)
    nat_v[...] = v_ref[0].astype(F32)

    def deinterleave(r):
        rows = pl.ds(r * blk, blk)
        q, v = nat_q[pl.ds(r, blk, stride=nres), :], nat_v[pl.ds(r, blk, stride=nres), :]
        qd0[rows, :] = jnp.where(lo, q, 0.0)
        qd1[rows, :] = jnp.where(lo, 0.0, q)
        kd[rows, :] = nat_k[pl.ds(r, blk, stride=nres), :]
        vd0[rows, :] = jnp.where(lo, v, 1.0)
        vd1[rows, :] = jnp.where(lo, 1.0, v)

    def pieces(dil, rd, row_off, rows):
        return [pl.ds((rd + dil * jj) * blk + row_off, rows) for jj in range(nres // dil)]

    def gather(ref, idx, lead=()):
        parts = [ref[lead + (i, slice(None))] for i in idx]
        return parts[0] if len(parts) == 1 else jnp.concatenate(parts, axis=0)

    def attend(pi, blocks):
        dil = A_PATTERNS[pi][1]
        pr = blk // (nres // dil)
        q_idxs, vbs, scores = [], [], []
        for rd, n, first in blocks:
            q_idx = pieces(dil, rd, n * pr, pr)
            k_idx = q_idx if first else pieces(dil, rd, (n - 1) * pr, 2 * pr)
            bias = bias_ref[pi, :, 0:blk] if first else bias_ref[pi, :, blk:3 * blk]
            kb = gather(kd, k_idx).astype(BF16)
            q_idxs.append(q_idx)
            for qd, vd in ((qd0, vd0), (qd1, vd1)):
                vbs.append(gather(vd, k_idx).astype(BF16))
                scores.append(_dot_nt(gather(qd, q_idx).astype(BF16), kb) + bias)
        es, ms = [], []
        for s in scores:
            m = jnp.max(s, axis=-1, keepdims=True)
            ms.append(m)
            es.append(jnp.exp2((s - m).astype(BF16)))
        pvs = [_dot(e, vb) for e, vb in zip(es, vbs)]
        for b, q_idx in enumerate(q_idxs):
            u = jnp.where(lo, pvs[2 * b], pvs[2 * b + 1])
            l_swapped = jnp.where(lo, pvs[2 * b + 1], pvs[2 * b])
            m = jnp.where(lo, ms[2 * b], ms[2 * b + 1])
            if pi != order[-1]:
                for jj, idx in enumerate(q_idx):
                    u_s[slot[pi], idx, :] = u[jj * pr:(jj + 1) * pr]
                    m_s[slot[pi], idx, :] = m[jj * pr:(jj + 1) * pr]
                    l_s[slot[pi], idx, :] = l_swapped[jj * pr:(jj + 1) * pr]
                continue
            parts = [(u, m, l_swapped)] + [(gather(u_s, q_idx, (sl,)), gather(m_s, q_idx, (sl,)),
                                            gather(l_s, q_idx, (sl,))) for sl in slot.values()]
            m_all = functools.reduce(jnp.maximum, [mm for _, mm, _ in parts])
            num = jnp.zeros((blk, LANES), F32)
            den = jnp.zeros((blk, LANES), F32)
            for uu, mm, ll in parts:
                a = jnp.exp2(mm - m_all)
                num = num + a * uu
                den = den + a * pltpu.roll(ll, HEAD_DIM, 1)
            out = num / den
            n = blocks[b][1]
            for jj in range(nres):
                o_ref[0, pl.ds(n * blk + jj, pr, stride=nres), :] = out[jj * pr:(jj + 1) * pr]

    for pi in order:
        dil = A_PATTERNS[pi][1]
        nb = seq // dil // blk
        blocks = [(rd, n, n == 0) for rd in range(dil) for n in range(nb)]
        for g0 in range(0, len(blocks), A_GROUP):
            if pi == order[0]:
                for rd, _, _ in blocks[g0:g0 + A_GROUP]:
                    deinterleave(rd)
            attend(pi, blocks[g0:g0 + A_GROUP])


def _mixer_a_bias(nres):
    blk = A_BLOCK

    def sub_pos(i, fold, rows):
        return fold * (i % rows) + i // rows

    out = np.zeros((len(A_PATTERNS), blk, 3 * blk), np.float32)
    for pi, (window, dil) in enumerate(A_PATTERNS):
        n_back = window // dil
        fold = nres // dil
        pr = blk // fold
        sq = sub_pos(np.arange(blk), fold, pr)[:, None]
        d_first = sq - sub_pos(np.arange(blk), fold, pr)[None, :]
        d_band = sq + blk - sub_pos(np.arange(2 * blk), fold, 2 * pr)[None, :]
        dist = np.concatenate([d_first, d_band], axis=1)
        out[pi] = np.where((dist >= 0) & (dist <= n_back), 0.0, NEG)
    return jnp.asarray(out)


def _mixer_a(qa, ka, va):
    B, T, W = qa.shape
    npair = W // LANES
    spec = pl.BlockSpec((1, T, LANES), lambda b, p: (b, 0, p))
    npat = len(A_PATTERNS)
    nres = T // A_BLOCK
    dils = sorted(d for _, d in A_PATTERNS)
    assert all(nres % d == 0 and T % (d * A_BLOCK) == 0 for d in dils) and dils[0] == 1 and dils[-1] == nres
    return pl.pallas_call(
        functools.partial(_mixer_a_kernel, seq=T),
        grid=(B, npair),
        in_specs=[spec, spec, spec, pl.BlockSpec((npat, A_BLOCK, 3 * A_BLOCK), lambda b, p: (0, 0, 0))],
        out_specs=spec,
        out_shape=jax.ShapeDtypeStruct((B, T, W), F32),
        scratch_shapes=[pltpu.VMEM((T, LANES), F32)] * 8 + [pltpu.VMEM((npat - 1, T, LANES), F32)] * 3,
        compiler_params=pltpu.CompilerParams(dimension_semantics=("parallel", "parallel"),
                                             vmem_limit_bytes=VMEM_LIMIT),
        name="mixer_a",
    )(qa, ka, va, _mixer_a_bias(nres))


def _nsa_kernel(q_ref, ks_ref, kw_ref, vst_ref, vwt_ref, kc_ref, vct_ref, gt_ref, ovt_ref, bias_ref, cbias_ref, o_ref,
                *, n_sel_blocks, nq):
    tq, ck = NSA_TQ, NSA_CK
    nrow = B_GROUP * tq
    gw = B_GROUP * HEAD_DIM
    step = pl.program_id(1)
    per_part = nq // NSA_SPLIT
    lanes = [(h, g) for h in range(NSA_SPLIT) for g in range(B_KV_HEADS)]
    qi = [step + h * per_part for h in range(NSA_SPLIT)]
    col_i = lax.broadcasted_iota(jnp.int32, (1, nrow), 1) & (tq - 1)

    def chunk(ref, g, c):
        return ref[0, g, pl.ds(pl.multiple_of(c * ck, ck), ck), :]

    def vt_chunk(ref, g, c):
        return ref[0, c, g * NSA_VROWS:(g + 1) * NSA_VROWS, :]

    def softmax_pv(s, vt):
        m = jnp.max(s, axis=0, keepdims=True)
        return m, _dot(vt, jnp.exp2(s - m).astype(BF16))

    def merged(parts):
        m_new = functools.reduce(jnp.maximum, [m for m, _ in parts])
        acc = sum(jnp.exp2(m - m_new) * a for m, a in parts)
        return m_new, acc

    def normalised(state):
        return state[1][0:HEAD_DIM] * (1.0 / state[1][HEAD_DIM:HEAD_DIM + 1])

    q_b = {(h, g): jnp.concatenate([q_ref[0, g, r, h] for r in range(B_GROUP)], axis=0) for h, g in lanes}

    nwin = WIN // ck
    win_chunks = [[jnp.maximum(qi[h] - back, 0) for back in range(nwin, -1, -1)] for h in range(NSA_SPLIT)]
    s_win = {}
    for h, g in lanes:
        s = _dot_nt(jnp.concatenate([chunk(kw_ref, g, c) for c in win_chunks[h]], axis=0), q_b[h, g])
        pieces = [jnp.where(qi[h] >= nwin, s[0:ck] + bias_ref[1], NEG)]
        for n in range(1, nwin):
            pieces.append(jnp.where(qi[h] >= nwin - n, s[n * ck:(n + 1) * ck], NEG))
        pieces.append(s[nwin * ck:] + bias_ref[0])
        s_win[h, g] = pieces

    s_cmp = {(h, g): _dot_nt(kc_ref[0, g], q_b[h, g]) + cbias_ref[h, 0] for h, g in lanes}
    p_cmp, o_cmp = {}, {}
    for h, g in lanes:
        m = jnp.max(s_cmp[h, g], axis=0, keepdims=True)
        e = jnp.exp2(s_cmp[h, g] - m)
        den = jnp.sum(e, axis=0, keepdims=True)
        sees_block = qi[h] * tq + col_i >= CMP_LEN - 1
        p_cmp[h, g] = e * jnp.where(sees_block, 1.0 / jnp.maximum(den, 1e-30), 0.0)
        o_cmp[h, g] = _dot(vct_ref[0, g], p_cmp[h, g].astype(BF16))

    imp = {}
    for h, g in lanes:
        psum = p_cmp[h, g][:, 0:tq]
        for r in range(1, B_GROUP):
            psum = psum + p_cmp[h, g][:, r * tq:(r + 1) * tq]
        p_hi = psum.astype(BF16)
        p_lo = (psum - p_hi.astype(F32)).astype(BF16)
        imp[h, g] = _dot(ovt_ref[...], p_hi) + _dot(ovt_ref[...], p_lo)
    j = lax.broadcasted_iota(jnp.int32, (n_sel_blocks, tq), 0)
    j_f = j.astype(F32)
    low = -3e38
    q_aug = {}
    for h, g in lanes:
        cur = (qi[h] * tq + lax.broadcasted_iota(jnp.int32, (n_sel_blocks, tq), 1)) >> SEL_SHIFT
        forced = (j == 0) | (j == cur) | (j == cur - 1)
        score = jnp.where(forced, imp[h, g] + 2.0, jnp.where(j > cur, -1.0, imp[h, g]))
        sel = jnp.zeros((n_sel_blocks, tq), jnp.bool_)
        for _ in range(min(SEL_TOPK, n_sel_blocks)):
            mx = jnp.max(score, axis=0, keepdims=True)
            first = jnp.min(jnp.where(score == mx, j_f, 4.0 * LANES), axis=0, keepdims=True)
            hit = j_f == first
            sel = sel | hit
            score = jnp.where(hit, low, score)
        selneg = jnp.concatenate([jnp.zeros((SEL_LANE0, tq), F32), jnp.where(sel, 0.0, NEG),
                                  jnp.zeros((LANES - SEL_LANE0 - n_sel_blocks, tq), F32)], axis=0).T
        q_aug[h, g] = q_b[h, g] + jnp.concatenate([selneg.astype(BF16)] * B_GROUP, axis=0)

    s_diag = {(h, g): _dot_nt(chunk(ks_ref, g, qi[h]), q_aug[h, g]) + bias_ref[0] for h, g in lanes}
    o_win = {(h, g): normalised(merged([softmax_pv(s_win[h, g][n], vt_chunk(vwt_ref, g, c))
                                        for n, c in enumerate(win_chunks[h])])) for h, g in lanes}
    states = tuple(softmax_pv(s_diag[h, g], vt_chunk(vst_ref, g, qi[h])) for h, g in lanes)

    def earlier(work, st):
        s = [_dot_nt(chunk(ks_ref, lanes[n][1], c), q_aug[lanes[n]]) for n, c in work]
        parts = [[st[n]] for n in range(len(lanes))]
        for (n, c), sc in zip(work, s):
            parts[n].append(softmax_pv(sc, vt_chunk(vst_ref, lanes[n][1], c)))
        return tuple(merged(p) if len(p) > 1 else p[0] for p in parts)

    states = lax.fori_loop(0, step, lambda c, st: earlier([(n, c) for n in range(len(lanes))], st), states)
    for h in range(1, NSA_SPLIT):
        mine = [n for n, (hh, _) in enumerate(lanes) if hh >= h]
        pairs = per_part // 2
        states = lax.fori_loop(
            0, pairs,
            lambda i, st, mine=mine, h=h: earlier(
                [(n, step + (h - 1) * per_part + 2 * i + k) for n in mine for k in range(2)], st), states)

    for n, (h, g) in enumerate(lanes):
        o_sel = normalised(states[n])
        gt = gt_ref[0, g * GATE_ROWS:(g + 1) * GATE_ROWS, h, :]
        outs = []
        for r in range(B_GROUP):
            cs = slice(r * tq, (r + 1) * tq)
            outs.append(gt[r * N_BRANCH:r * N_BRANCH + 1, :] * o_cmp[h, g][:, cs]
                        + gt[r * N_BRANCH + 1:r * N_BRANCH + 2, :] * o_sel[:, cs]
                        + gt[r * N_BRANCH + 2:r * N_BRANCH + 3, :] * o_win[h, g][:, cs])
        for c in range(B_GROUP // 2):
            pair = jnp.concatenate([outs[2 * c], outs[2 * c + 1]], axis=0)
            o_ref[0, h, :, g * gw + c * LANES:g * gw + (c + 1) * LANES] = pair.T


def _overlap_t(nc_pad, ns):
    nc = nc_pad - 1
    c0 = np.arange(nc_pad) * CMP_STRIDE
    s0 = np.arange(ns) * SEL_BLOCK
    ov = np.minimum(c0[None, :] + CMP_LEN, s0[:, None] + SEL_BLOCK) - np.maximum(c0[None, :], s0[:, None])
    ov = np.clip(ov, 0, None).astype(np.float32) / CMP_LEN
    ov = np.where(np.arange(nc_pad)[None, :] < nc, ov, 0.0)
    return jnp.asarray(ov, BF16)


def _nsa(qb, ks, kw, vst, vwt, kc, vct, gt):
    B, _, _, T, _ = qb.shape
    W = B_WIDTH
    nq = T // NSA_TQ
    ns = T // SEL_BLOCK
    nc_pad = kc.shape[2]
    nrow = B_GROUP * NSA_TQ
    part = T // NSA_SPLIT
    assert NSA_TQ == NSA_CK and WIN % NSA_CK == 0 and SEL_LANE0 + ns <= LANES and (nq // NSA_SPLIT) % 2 == 0
    kspec = pl.BlockSpec((1, B_KV_HEADS, T, LANES), lambda b, i: (b, 0, 0, 0))
    vspec = pl.BlockSpec((1, T // NSA_CK, B_KV_HEADS * NSA_VROWS, NSA_CK), lambda b, i: (b, 0, 0, 0))
    key = np.arange(NSA_CK)[:, None]
    qry = (np.arange(nrow) % NSA_TQ)[None, :]
    bias = jnp.asarray(np.stack([np.where(key <= qry, 0.0, NEG), np.where(key > qry, 0.0, NEG)]).astype(np.float32))
    cmp_end = (np.arange(nc_pad) * CMP_STRIDE + CMP_LEN - 1)[None, :, None]
    t_query = (np.arange(nq) * NSA_TQ)[:, None, None] + qry[None]
    cbias = jnp.asarray(np.where((cmp_end <= t_query) & (np.arange(nc_pad) < nc_pad - 1)[None, :, None], 0.0, NEG)
                        .astype(np.float32)).reshape(NSA_SPLIT, nq // NSA_SPLIT, nc_pad, nrow)
    out = pl.pallas_call(
        functools.partial(_nsa_kernel, n_sel_blocks=ns, nq=nq),
        grid=(B, nq // NSA_SPLIT),
        in_specs=[pl.BlockSpec((1, B_KV_HEADS, B_GROUP, NSA_SPLIT, NSA_TQ, LANES), lambda b, i: (b, 0, 0, 0, i, 0)),
                  kspec, kspec, vspec, vspec,
                  pl.BlockSpec((1, B_KV_HEADS, nc_pad, LANES), lambda b, i: (b, 0, 0, 0)),
                  pl.BlockSpec((1, B_KV_HEADS, HEAD_DIM, nc_pad), lambda b, i: (b, 0, 0, 0)),
                  pl.BlockSpec((1, B_KV_HEADS * GATE_ROWS, NSA_SPLIT, NSA_TQ), lambda b, i: (b, 0, 0, i)),
                  pl.BlockSpec((ns, nc_pad), lambda b, i: (0, 0)),
                  pl.BlockSpec((2, NSA_CK, nrow), lambda b, i: (0, 0, 0)),
                  pl.BlockSpec((NSA_SPLIT, 1, nc_pad, nrow), lambda b, i: (0, i, 0, 0))],
        out_specs=pl.BlockSpec((1, NSA_SPLIT, NSA_TQ, W), lambda b, i: (b, 0, i, 0)),
        out_shape=jax.ShapeDtypeStruct((B, NSA_SPLIT, part, W), F32),
        compiler_params=pltpu.CompilerParams(dimension_semantics=("parallel", "arbitrary"),
                                             vmem_limit_bytes=VMEM_LIMIT),
        name="nsa",
    )(qb.reshape(B, B_KV_HEADS, B_GROUP, NSA_SPLIT, part, LANES), ks, kw, vst, vwt, kc, vct,
      gt.reshape(B, B_KV_HEADS * GATE_ROWS, NSA_SPLIT, part), _overlap_t(nc_pad, ns), bias, cbias)
    return out.reshape(B, T, W)


def _post_kernel(x_ref, oa_ref, ob_ref, ga_ref, gb_ref, wo_ref, gm_ref, wu_ref, wd_ref, gf_ref, o_ref, *, final):
    def norm(v, g):
        return v * lax.rsqrt(jnp.mean(v * v, axis=-1, keepdims=True) + EPS) * g

    na = norm(oa_ref[...], ga_ref[...]).astype(BF16)
    nb = norm(ob_ref[...], gb_ref[...]).astype(BF16)
    aw = na.shape[1]
    h_res = x_ref[...] + _dot(na, wo_ref[0:aw, :]) + _dot(nb, wo_ref[aw:, :])
    h = norm(h_res, gm_ref[...]).astype(BF16)
    u = jnp.square(jnp.maximum(_dot(h, wu_ref[...]), 0.0)).astype(BF16)
    acc = h_res + _dot(u, wd_ref[...])
    o_ref[...] = norm(acc, gf_ref[...]) if final else acc


def _post(x, oa, ob, g_a, g_b, w_out, g_mlp, w_up, w_down, g_final, *, final, tm=512):
    B, T, D = x.shape
    n = B * T
    dff = w_up.shape[1]
    tok = lambda w: pl.BlockSpec((tm, w), lambda i: (i, 0))
    const = lambda shape: pl.BlockSpec(shape, lambda i: (0, 0), pipeline_mode=pl.Buffered(1))
    out = pl.pallas_call(
        functools.partial(_post_kernel, final=final),
        grid=(n // tm,),
        in_specs=[tok(D), tok(A_WIDTH), tok(B_WIDTH), const((1, A_WIDTH)), const((1, B_WIDTH)),
                  const((A_WIDTH + B_WIDTH, D)), const((1, D)), const((D, dff)), const((dff, D)), const((1, D))],
        out_specs=tok(D),
        out_shape=jax.ShapeDtypeStruct((n, D), F32),
        compiler_params=pltpu.CompilerParams(dimension_semantics=("parallel",), vmem_limit_bytes=VMEM_LIMIT),
        name="post",
    )(x.reshape(n, D), oa.reshape(n, A_WIDTH), ob.reshape(n, B_WIDTH), g_a.reshape(1, -1), g_b.reshape(1, -1),
      w_out.astype(BF16), g_mlp.reshape(1, D), w_up.astype(BF16), w_down.astype(BF16), g_final.reshape(1, D))
    return out.reshape(B, T, D)


def kernel(x, norm_mix, w_in, cmp_pe_k, cmp_w1_k, cmp_w2_k, cmp_pe_v, cmp_w1_v, cmp_w2_v, g_out_a, g_out_b,
           w_out, norm_mlp, w_up, w_down, norm_final):
    B, T, D = x.shape
    depth = w_in.shape[0]
    tables = _rope_tables(np.arange(T))
    cmp_tables = _rope_tables(np.arange(T // CMP_STRIDE) * CMP_STRIDE + CMP_LEN - 1)
    h_res = x
    for l in range(depth):
        qa, ka, va, qb, ks, kw, vst, vwt, kcvc, gt = _in_proj(h_res, norm_mix[l], w_in[l], tables)
        kc, vct = _compress(kcvc, cmp_pe_k[l], cmp_w1_k[l], cmp_w2_k[l], cmp_pe_v[l], cmp_w1_v[l], cmp_w2_v[l],
                            cmp_tables)
        oa = _mixer_a(qa, ka, va)
        ob = _nsa(qb, ks, kw, vst, vwt, kc, vct, gt)
        h_res = _post(h_res, oa, ob, g_out_a[l], g_out_b[l], w_out[l], norm_mlp[l], w_up[l], w_down[l], norm_final,
                      final=(l == depth - 1))
    return h_res
```

```python
import functools

import jax
import jax.numpy as jnp
import numpy as np
from jax import lax
from jax.experimental import pallas as pl
from jax.experimental.pallas import tpu as pltpu

F32 = jnp.float32
BF16 = jnp.bfloat16

HEAD_DIM = 64
ROT_DIM = HEAD_DIM // 4
ROPE_THETA = 500000.0
EPS = 1e-6
NEG = -1e30
Q_SCALE = HEAD_DIM ** -0.5 * 1.4426950408889634
LANES = 128

A_HEADS = 8
A_PATTERNS = ((128, 1), (512, 4), (2048, 16))
A_BLOCK = 128
A_GROUP = 16

B_HEADS = 8
B_KV_HEADS = 2
B_GROUP = B_HEADS // B_KV_HEADS
CMP_LEN = 32
CMP_STRIDE = 16
CMP_HIDDEN = 256
SEL_BLOCK = 64
SEL_SHIFT = 6
SEL_TOPK = 8
WIN = 512
N_BRANCH = 3

A_WIDTH = A_HEADS * HEAD_DIM
B_WIDTH = B_HEADS * HEAD_DIM
KV_WIDTH = B_KV_HEADS * HEAD_DIM

NSA_TQ = 256
NSA_CK = 256
NSA_SPLIT = 2
NSA_VROWS = 80
GATE_ROWS = 32
SEL_LANE0 = HEAD_DIM

VMEM_LIMIT = 56 * 1024 * 1024


def _dot(a, b):
    return jnp.dot(a, b, preferred_element_type=F32)


def _dot_nt(a, b):
    return lax.dot_general(a, b, (((1,), (1,)), ((), ())), preferred_element_type=F32)


def _rope_rows(y, cos, sin_a, sin_b):
    outs = []
    for c in range(y.shape[1] // LANES):
        yc = y[:, c * LANES:(c + 1) * LANES]
        outs.append(yc * cos + pltpu.roll(yc, LANES - ROT_DIM // 2, 1) * sin_a
                    + pltpu.roll(yc, ROT_DIM // 2, 1) * sin_b)
    return outs[0] if len(outs) == 1 else jnp.concatenate(outs, axis=1)


def _in_proj_kernel(x_ref, g_ref, wq_ref, wkk_ref, wt_ref, cos_ref, sa_ref, sb_ref,
                    qa_ref, ka_ref, va_ref, qb_ref, ks_ref, kw_ref, vst_ref, vwt_ref, kcvc_ref, gt_ref,
                    *, tm):
    tt = pl.program_id(1)
    x = x_ref[...]
    ms = jnp.mean(x * x, axis=-1, keepdims=True)
    h = (x * lax.rsqrt(ms + EPS) * g_ref[...]).astype(BF16)
    cos, sa, sb = cos_ref[...], sa_ref[...], sb_ref[...]
    scale = Q_SCALE

    def proj(c0, c1):
        return _dot(h, wq_ref[:, c0:c1])

    o = 0
    qa_ref[...] = (_rope_rows(proj(o, o + A_WIDTH), cos, sa, sb) * scale).astype(BF16)
    o += A_WIDTH
    ka_ref[...] = _rope_rows(proj(o, o + A_WIDTH), cos, sa, sb).astype(BF16)
    o += A_WIDTH
    va_ref[...] = proj(o, o + A_WIDTH).astype(BF16)
    o += A_WIDTH
    lane = lax.broadcasted_iota(jnp.int32, (tm, LANES), 1)
    row = lax.broadcasted_iota(jnp.int32, (tm, LANES), 0)
    lo = lane < HEAD_DIM
    qb = _rope_rows(proj(o, o + B_WIDTH), cos, sa, sb) * scale
    for hh in range(B_HEADS):
        ch = qb[:, (hh // 2) * LANES:(hh // 2 + 1) * LANES]
        if hh % 2:
            ch = pltpu.roll(ch, HEAD_DIM, 1)
        qb_ref[0, hh // B_GROUP, hh % B_GROUP] = jnp.where(lo, ch, 0.0).astype(BF16)
    o += B_WIDTH
    kcvc_ref[...] = proj(o, o + 2 * KV_WIDTH)
    o += 2 * KV_WIDTH
    ksw = _rope_rows(_dot(h, wkk_ref[...]), cos, sa, sb)

    blk = (tt * tm + row) >> SEL_SHIFT
    onehot = jnp.where(lane - SEL_LANE0 == blk, 1.0, 0.0)
    for kind, ref in ((0, ks_ref), (1, kw_ref)):
        kk = ksw[:, kind * LANES:(kind + 1) * LANES]
        tail = onehot if kind == 0 else 0.0
        ref[0, 0] = jnp.where(lo, kk, tail).astype(BF16)
        ref[0, 1] = jnp.where(lo, pltpu.roll(kk, HEAD_DIM, 1), tail).astype(BF16)

    tr = _dot_nt(wt_ref[...], h)
    ones_row = jnp.where(lax.broadcasted_iota(jnp.int32, (NSA_VROWS - HEAD_DIM, tm), 0) == 0, 1.0, 0.0)
    for kind, ref in ((0, vst_ref), (1, vwt_ref)):
        rows = [tr[kind * LANES + gg * HEAD_DIM:kind * LANES + (gg + 1) * HEAD_DIM] for gg in range(B_KV_HEADS)]
        slab = jnp.concatenate([rows[0], ones_row, rows[1], ones_row], axis=0).astype(BF16)
        for c in range(tm // NSA_CK):
            ref[0, c] = slab[:, c * NSA_CK:(c + 1) * NSA_CK]
    gt_ref[0, 0] = jax.nn.sigmoid(tr[2 * LANES:2 * LANES + GATE_ROWS, :])


def _rope_tables(pos):
    half = ROT_DIM // 2
    inv = ROPE_THETA ** (-np.arange(0, ROT_DIM, 2, dtype=np.float64) / ROT_DIM)
    ang = np.asarray(pos, np.float64)[:, None] * inv[None, :]
    cos, sin = np.cos(ang), np.sin(ang)
    n = len(pos)
    ones = np.ones((n, HEAD_DIM - ROT_DIM))
    zeros = np.zeros((n, HEAD_DIM - ROT_DIM))
    zh = np.zeros((n, half))
    c_head = np.concatenate([cos, cos, ones], axis=1)
    a_head = np.concatenate([-sin, zh, zeros], axis=1)
    b_head = np.concatenate([zh, sin, zeros], axis=1)
    rep = LANES // HEAD_DIM
    return tuple(jnp.asarray(np.tile(t, (1, rep)).astype(np.float32)) for t in (c_head, a_head, b_head))


def _in_proj(x, norm_g, w_in, tables, *, tm=1024):
    B, T, D = x.shape
    nt = T // tm
    offs = [0]
    for n in (A_WIDTH, A_WIDTH, A_WIDTH, B_WIDTH, KV_WIDTH, KV_WIDTH, KV_WIDTH, KV_WIDTH, KV_WIDTH, KV_WIDTH,
              B_HEADS * N_BRANCH):
        offs.append(offs[-1] + n)
    col = lambda i: w_in[:, offs[i]:offs[i + 1]]
    wq = w_in[:, :offs[6]].astype(BF16)
    wkk = jnp.concatenate([col(6), col(8)], axis=1).astype(BF16)
    gpad = jnp.zeros((D, GATE_ROWS - B_HEADS * N_BRANCH), w_in.dtype)
    wt = jnp.concatenate([col(7), col(9), col(10), gpad], axis=1).T.astype(BF16)
    cos, sa, sb = tables
    nq = wq.shape[1]
    nr = wt.shape[0]
    tok = lambda w: pl.BlockSpec((None, tm, w), lambda b, t: (b, t, 0))
    const = lambda shape: pl.BlockSpec(shape, lambda b, t: (0,) * len(shape))
    tab = pl.BlockSpec((tm, LANES), lambda b, t: (t, 0))
    out_shapes = (
        jax.ShapeDtypeStruct((B, T, A_WIDTH), BF16),
        jax.ShapeDtypeStruct((B, T, A_WIDTH), BF16),
        jax.ShapeDtypeStruct((B, T, A_WIDTH), BF16),
        jax.ShapeDtypeStruct((B, B_KV_HEADS, B_GROUP, T, LANES), BF16),
        jax.ShapeDtypeStruct((B, B_KV_HEADS, T, LANES), BF16),
        jax.ShapeDtypeStruct((B, B_KV_HEADS, T, LANES), BF16),
        jax.ShapeDtypeStruct((B, T // NSA_CK, B_KV_HEADS * NSA_VROWS, NSA_CK), BF16),
        jax.ShapeDtypeStruct((B, T // NSA_CK, B_KV_HEADS * NSA_VROWS, NSA_CK), BF16),
        jax.ShapeDtypeStruct((B, T, 2 * KV_WIDTH), F32),
        jax.ShapeDtypeStruct((B, nt, GATE_ROWS, tm), F32),
    )
    frame = pl.BlockSpec((1, B_KV_HEADS, tm, LANES), lambda b, t: (b, 0, t, 0))
    vt = pl.BlockSpec((1, tm // NSA_CK, B_KV_HEADS * NSA_VROWS, NSA_CK), lambda b, t: (b, t, 0, 0))
    qframe = pl.BlockSpec((1, B_KV_HEADS, B_GROUP, tm, LANES), lambda b, t: (b, 0, 0, t, 0))
    out_specs = (tok(A_WIDTH), tok(A_WIDTH), tok(A_WIDTH), qframe, frame, frame, vt, vt,
                 tok(2 * KV_WIDTH), pl.BlockSpec((1, 1, GATE_ROWS, tm), lambda b, t: (b, t, 0, 0)))
    return pl.pallas_call(
        functools.partial(_in_proj_kernel, tm=tm),
        grid=(B, nt),
        in_specs=[tok(D), const((1, D)), const((D, nq)), const((D, 2 * KV_WIDTH)), const((nr, D)), tab, tab, tab],
        out_specs=out_specs,
        out_shape=out_shapes,
        compiler_params=pltpu.CompilerParams(dimension_semantics=("parallel", "parallel"),
                                             vmem_limit_bytes=VMEM_LIMIT),
        name="in_proj",
    )(x, norm_g.reshape(1, D), wq, wkk, wt, cos, sa, sb)


def _compress_kernel(ak_ref, av_ref, w1k_ref, w1v_ref, pek_ref, pev_ref, w2k_ref, w2vt_ref, cos_ref, sa_ref, sb_ref,
                     kc_ref, vct_ref, *, nc_pad):
    half = CMP_LEN // 2
    hid_w = B_KV_HEADS * CMP_HIDDEN

    zeros = jnp.zeros((HEAD_DIM, CMP_HIDDEN), BF16)

    def both_groups(w):
        return jnp.concatenate([jnp.concatenate([w, zeros], axis=1), jnp.concatenate([zeros, w], axis=1)], axis=0)

    def hidden(a_ref, w1_ref, pe_ref):
        acc_u = jnp.zeros((nc_pad, hid_w), F32)
        acc_v = jnp.zeros((nc_pad, hid_w), F32)
        for p in range(half):
            ap = a_ref[0, pl.ds(p, nc_pad, stride=CMP_STRIDE), :]
            acc_u = acc_u + _dot((ap + pe_ref[p:p + 1, :]).astype(BF16), both_groups(w1_ref[p]))
            acc_v = acc_v + _dot((ap + pe_ref[half + p:half + p + 1, :]).astype(BF16), both_groups(w1_ref[half + p]))
        return jax.nn.gelu(acc_u + pltpu.roll(acc_v, nc_pad - 1, 0))

    hk = hidden(ak_ref, w1k_ref, pek_ref).astype(BF16)
    hv = hidden(av_ref, w1v_ref, pev_ref).astype(BF16)
    for g in range(B_KV_HEADS):
        hg = hk[:, g * CMP_HIDDEN:(g + 1) * CMP_HIDDEN]
        kc = _dot(hg, w2k_ref[...])
        kc_ref[0, g] = _rope_rows(kc, cos_ref[...], sa_ref[...], sb_ref[...]).astype(BF16)
        vg = hv[:, g * CMP_HIDDEN:(g + 1) * CMP_HIDDEN]
        vct_ref[0, g] = _dot_nt(w2vt_ref[...], vg).astype(BF16)


def _compress(kcvc, pe_k, w1_k, w2_k, pe_v, w1_v, w2_v, cmp_tables):
    B, T, _ = kcvc.shape
    nc_pad = T // CMP_STRIDE
    w2k = jnp.concatenate([w2_k, jnp.zeros_like(w2_k)], axis=1).astype(BF16)
    w2vt = w2_v.T.astype(BF16)
    pek = jnp.tile(pe_k, (1, B_KV_HEADS))
    pev = jnp.tile(pe_v, (1, B_KV_HEADS))
    per_pos = lambda w1: w1.reshape(CMP_LEN, HEAD_DIM, CMP_HIDDEN).astype(BF16)
    const = lambda shape: pl.BlockSpec(shape, lambda b: (0,) * len(shape))
    cos, sa, sb = cmp_tables
    return pl.pallas_call(
        functools.partial(_compress_kernel, nc_pad=nc_pad),
        grid=(B,),
        in_specs=[pl.BlockSpec((1, T, KV_WIDTH), lambda b: (b, 0, 0)), pl.BlockSpec((1, T, KV_WIDTH), lambda b: (b, 0, 1)),
                  const((CMP_LEN, HEAD_DIM, CMP_HIDDEN)), const((CMP_LEN, HEAD_DIM, CMP_HIDDEN)),
                  const((CMP_LEN, LANES)), const((CMP_LEN, LANES)),
                  const((CMP_HIDDEN, LANES)), const((HEAD_DIM, CMP_HIDDEN)),
                  const((nc_pad, LANES)), const((nc_pad, LANES)), const((nc_pad, LANES))],
        out_specs=(pl.BlockSpec((1, B_KV_HEADS, nc_pad, LANES), lambda b: (b, 0, 0, 0)),
                   pl.BlockSpec((1, B_KV_HEADS, HEAD_DIM, nc_pad), lambda b: (b, 0, 0, 0))),
        out_shape=(jax.ShapeDtypeStruct((B, B_KV_HEADS, nc_pad, LANES), BF16),
                   jax.ShapeDtypeStruct((B, B_KV_HEADS, HEAD_DIM, nc_pad), BF16)),
        compiler_params=pltpu.CompilerParams(dimension_semantics=("parallel",), vmem_limit_bytes=VMEM_LIMIT),
        name="compress",
    )(kcvc, kcvc, per_pos(w1_k), per_pos(w1_v), pek, pev, w2k, w2vt, cos, sa, sb)


def _mixer_a_kernel(q_ref, k_ref, v_ref, bias_ref, o_ref, nat_q, nat_k, nat_v, qd0, qd1, kd, vd0, vd1, u_s, m_s, l_s,
                    *, seq):
    blk = A_BLOCK
    nres = seq // blk
    npat = len(A_PATTERNS)
    order = sorted(range(npat), key=lambda p: -A_PATTERNS[p][1])
    slot = {p: n for n, p in enumerate(order[:-1])}
    lane = lax.broadcasted_iota(jnp.int32, (blk, LANES), 1)
    lo = lane < HEAD_DIM
    nat_q[...] = q_ref[0].astype(F32)
    nat_k[...] = k_ref[0].astype(F32)
    nat_v[...] = v_ref[0].astype(F32)

    def deinterleave(r):
        rows = pl.ds(r * blk, blk)
        q, v = nat_q[pl.ds(r, blk, stride=nres), :], nat_v[pl.ds(r, blk, stride=nres), :]
        qd0[rows, :] = jnp.where(lo, q, 0.0)
        qd1[rows, :] = jnp.where(lo, 0.0, q)
        kd[rows, :] = nat_k[pl.ds(r, blk, stride=nres), :]
        vd0[rows, :] = jnp.where(lo, v, 1.0)
        vd1[rows, :] = jnp.where(lo, 1.0, v)

    def pieces(dil, rd, row_off, rows):
        return [pl.ds((rd + dil * jj) * blk + row_off, rows) for jj in range(nres // dil)]

    def gather(ref, idx, lead=()):
        parts = [ref[lead + (i, slice(None))] for i in idx]
        return parts[0] if len(parts) == 1 else jnp.concatenate(parts, axis=0)

    def attend(pi, blocks):
        dil = A_PATTERNS[pi][1]
        pr = blk // (nres // dil)
        q_idxs, vbs, scores = [], [], []
        for rd, n, first in blocks:
            q_idx = pieces(dil, rd, n * pr, pr)
            k_idx = q_idx if first else pieces(dil, rd, (n - 1) * pr, 2 * pr)
            bias = bias_ref[pi, :, 0:blk] if first else bias_ref[pi, :, blk:3 * blk]
            kb = gather(kd, k_idx).astype(BF16)
            q_idxs.append(q_idx)
            for qd, vd in ((qd0, vd0), (qd1, vd1)):
                vbs.append(gather(vd, k_idx).astype(BF16))
                scores.append(_dot_nt(gather(qd, q_idx).astype(BF16), kb) + bias)
        es, ms = [], []
        for s in scores:
            m = jnp.max(s, axis=-1, keepdims=True)
            ms.append(m)
            es.append(jnp.exp2((s - m).astype(BF16)))
        pvs = [_dot(e, vb) for e, vb in zip(es, vbs)]
        for b, q_idx in enumerate(q_idxs):
            u = jnp.where(lo, pvs[2 * b], pvs[2 * b + 1])
            l_swapped = jnp.where(lo, pvs[2 * b + 1], pvs[2 * b])
            m = jnp.where(lo, ms[2 * b], ms[2 * b + 1])
            if pi != order[-1]:
                for jj, idx in enumerate(q_idx):
                    u_s[slot[pi], idx, :] = u[jj * pr:(jj + 1) * pr]
                    m_s[slot[pi], idx, :] = m[jj * pr:(jj + 1) * pr]
                    l_s[slot[pi], idx, :] = l_swapped[jj * pr:(jj + 1) * pr]
                continue
            parts = [(u, m, l_swapped)] + [(gather(u_s, q_idx, (sl,)), gather(m_s, q_idx, (sl,)),
                                            gather(l_s, q_idx, (sl,))) for sl in slot.values()]
            m_all = functools.reduce(jnp.maximum, [mm for _, mm, _ in parts])
            num = jnp.zeros((blk, LANES), F32)
            den = jnp.zeros((blk, LANES), F32)
            for uu, mm, ll in parts:
                a = jnp.exp2(mm - m_all)
                num = num + a * uu
                den = den + a * pltpu.roll(ll, HEAD_DIM, 1)
            out = num / den
            n = blocks[b][1]
            for jj in range(nres):
                o_ref[0, pl.ds(n * blk + jj, pr, stride=nres), :] = out[jj * pr:(jj + 1) * pr]

    for pi in order:
        dil = A_PATTERNS[pi][1]
        nb = seq // dil // blk
        blocks = [(rd, n, n == 0) for rd in range(dil) for n in range(nb)]
        for g0 in range(0, len(blocks), A_GROUP):
            if pi == order[0]:
                for rd, _, _ in blocks[g0:g0 + A_GROUP]:
                    deinterleave(rd)
            attend(pi, blocks[g0:g0 + A_GROUP])


def _mixer_a_bias(nres):
    blk = A_BLOCK

    def sub_pos(i, fold, rows):
        return fold * (i % rows) + i // rows

    out = np.zeros((len(A_PATTERNS), blk, 3 * blk), np.float32)
    for pi, (window, dil) in enumerate(A_PATTERNS):
        n_back = window // dil
        fold = nres // dil
        pr = blk // fold
        sq = sub_pos(np.arange(blk), fold, pr)[:, None]
        d_first = sq - sub_pos(np.arange(blk), fold, pr)[None, :]
        d_band = sq + blk - sub_pos(np.arange(2 * blk), fold, 2 * pr)[None, :]
        dist = np.concatenate([d_first, d_band], axis=1)
        out[pi] = np.where((dist >= 0) & (dist <= n_back), 0.0, NEG)
    return jnp.asarray(out)


def _mixer_a(qa, ka, va):
    B, T, W = qa.shape
    npair = W // LANES
    spec = pl.BlockSpec((1, T, LANES), lambda b, p: (b, 0, p))
    npat = len(A_PATTERNS)
    nres = T // A_BLOCK
    dils = sorted(d for _, d in A_PATTERNS)
    assert all(nres % d == 0 and T % (d * A_BLOCK) == 0 for d in dils) and dils[0] == 1 and dils[-1] == nres
    return pl.pallas_call(
        functools.partial(_mixer_a_kernel, seq=T),
        grid=(B, npair),
        in_specs=[spec, spec, spec, pl.BlockSpec((npat, A_BLOCK, 3 * A_BLOCK), lambda b, p: (0, 0, 0))],
        out_specs=spec,
        out_shape=jax.ShapeDtypeStruct((B, T, W), F32),
        scratch_shapes=[pltpu.VMEM((T, LANES), F32)] * 8 + [pltpu.VMEM((npat - 1, T, LANES), F32)] * 3,
        compiler_params=pltpu.CompilerParams(dimension_semantics=("parallel", "parallel"),
                                             vmem_limit_bytes=VMEM_LIMIT),
        name="mixer_a",
    )(qa, ka, va, _mixer_a_bias(nres))


def _nsa_kernel(q_ref, ks_ref, kw_ref, vst_ref, vwt_ref, kc_ref, vct_ref, gt_ref, ovt_ref, bias_ref, cbias_ref, o_ref,
                *, n_sel_blocks, nq):
    tq, ck = NSA_TQ, NSA_CK
    nrow = B_GROUP * tq
    gw = B_GROUP * HEAD_DIM
    step = pl.program_id(1)
    per_part = nq // NSA_SPLIT
    lanes = [(h, g) for h in range(NSA_SPLIT) for g in range(B_KV_HEADS)]
    qi = [step + h * per_part for h in range(NSA_SPLIT)]
    col_i = lax.broadcasted_iota(jnp.int32, (1, nrow), 1) & (tq - 1)

    def chunk(ref, g, c):
        return ref[0, g, pl.ds(pl.multiple_of(c * ck, ck), ck), :]

    def vt_chunk(ref, g, c):
        return ref[0, c, g * NSA_VROWS:(g + 1) * NSA_VROWS, :]

    def softmax_pv(s, vt):
        m = jnp.max(s, axis=0, keepdims=True)
        return m, _dot(vt, jnp.exp2(s - m).astype(BF16))

    def merged(parts):
        m_new = functools.reduce(jnp.maximum, [m for m, _ in parts])
        acc = sum(jnp.exp2(m - m_new) * a for m, a in parts)
        return m_new, acc

    def normalised(state):
        return state[1][0:HEAD_DIM] * (1.0 / state[1][HEAD_DIM:HEAD_DIM + 1])

    q_b = {(h, g): jnp.concatenate([q_ref[0, g, r, h] for r in range(B_GROUP)], axis=0) for h, g in lanes}

    nwin = WIN // ck
    win_chunks = [[jnp.maximum(qi[h] - back, 0) for back in range(nwin, -1, -1)] for h in range(NSA_SPLIT)]
    s_win = {}
    for h, g in lanes:
        s = _dot_nt(jnp.concatenate([chunk(kw_ref, g, c) for c in win_chunks[h]], axis=0), q_b[h, g])
        pieces = [jnp.where(qi[h] >= nwin, s[0:ck] + bias_ref[1], NEG)]
        for n in range(1, nwin):
            pieces.append(jnp.where(qi[h] >= nwin - n, s[n * ck:(n + 1) * ck], NEG))
        pieces.append(s[nwin * ck:] + bias_ref[0])
        s_win[h, g] = pieces

    s_cmp = {(h, g): _dot_nt(kc_ref[0, g], q_b[h, g]) + cbias_ref[h, 0] for h, g in lanes}
    p_cmp, o_cmp = {}, {}
    for h, g in lanes:
        m = jnp.max(s_cmp[h, g], axis=0, keepdims=True)
        e = jnp.exp2(s_cmp[h, g] - m)
        den = jnp.sum(e, axis=0, keepdims=True)
        sees_block = qi[h] * tq + col_i >= CMP_LEN - 1
        p_cmp[h, g] = e * jnp.where(sees_block, 1.0 / jnp.maximum(den, 1e-30), 0.0)
        o_cmp[h, g] = _dot(vct_ref[0, g], p_cmp[h, g].astype(BF16))

    imp = {}
    for h, g in lanes:
        psum = p_cmp[h, g][:, 0:tq]
        for r in range(1, B_GROUP):
            psum = psum + p_cmp[h, g][:, r * tq:(r + 1) * tq]
        p_hi = psum.astype(BF16)
        p_lo = (psum - p_hi.astype(F32)).astype(BF16)
        imp[h, g] = _dot(ovt_ref[...], p_hi) + _dot(ovt_ref[...], p_lo)
    j = lax.broadcasted_iota(jnp.int32, (n_sel_blocks, tq), 0)
    j_f = j.astype(F32)
    low = -3e38
    q_aug = {}
    for h, g in lanes:
        cur = (qi[h] * tq + lax.broadcasted_iota(jnp.int32, (n_sel_blocks, tq), 1)) >> SEL_SHIFT
        forced = (j == 0) | (j == cur) | (j == cur - 1)
        score = jnp.where(forced, imp[h, g] + 2.0, jnp.where(j > cur, -1.0, imp[h, g]))
        sel = jnp.zeros((n_sel_blocks, tq), jnp.bool_)
        for _ in range(min(SEL_TOPK, n_sel_blocks)):
            mx = jnp.max(score, axis=0, keepdims=True)
            first = jnp.min(jnp.where(score == mx, j_f, 4.0 * LANES), axis=0, keepdims=True)
            hit = j_f == first
            sel = sel | hit
            score = jnp.where(hit, low, score)
        selneg = jnp.concatenate([jnp.zeros((SEL_LANE0, tq), F32), jnp.where(sel, 0.0, NEG),
                                  jnp.zeros((LANES - SEL_LANE0 - n_sel_blocks, tq), F32)], axis=0).T
        q_aug[h, g] = q_b[h, g] + jnp.concatenate([selneg.astype(BF16)] * B_GROUP, axis=0)

    s_diag = {(h, g): _dot_nt(chunk(ks_ref, g, qi[h]), q_aug[h, g]) + bias_ref[0] for h, g in lanes}
    o_win = {(h, g): normalised(merged([softmax_pv(s_win[h, g][n], vt_chunk(vwt_ref, g, c))
                                        for n, c in enumerate(win_chunks[h])])) for h, g in lanes}
    states = tuple(softmax_pv(s_diag[h, g], vt_chunk(vst_ref, g, qi[h])) for h, g in lanes)

    def earlier(work, st):
        s = [_dot_nt(chunk(ks_ref, lanes[n][1], c), q_aug[lanes[n]]) for n, c in work]
        parts = [[st[n]] for n in range(len(lanes))]
        for (n, c), sc in zip(work, s):
            parts[n].append(softmax_pv(sc, vt_chunk(vst_ref, lanes[n][1], c)))
        return tuple(merged(p) if len(p) > 1 else p[0] for p in parts)

    states = lax.fori_loop(0, step, lambda c, st: earlier([(n, c) for n in range(len(lanes))], st), states)
    for h in range(1, NSA_SPLIT):
        mine = [n for n, (hh, _) in enumerate(lanes) if hh >= h]
        pairs = per_part // 2
        states = lax.fori_loop(
            0, pairs,
            lambda i, st, mine=mine, h=h: earlier(
                [(n, step + (h - 1) * per_part + 2 * i + k) for n in mine for k in range(2)], st), states)

    for n, (h, g) in enumerate(lanes):
        o_sel = normalised(states[n])
        gt = gt_ref[0, h, g * B_GROUP * N_BRANCH:(g + 1) * B_GROUP * N_BRANCH, :]
        outs = []
        for r in range(B_GROUP):
            cs = slice(r * tq, (r + 1) * tq)
            outs.append(gt[r * N_BRANCH:r * N_BRANCH + 1, :] * o_cmp[h, g][:, cs]
                        + gt[r * N_BRANCH + 1:r * N_BRANCH + 2, :] * o_sel[:, cs]
                        + gt[r * N_BRANCH + 2:r * N_BRANCH + 3, :] * o_win[h, g][:, cs])
        for c in range(B_GROUP // 2):
            pair = jnp.concatenate([outs[2 * c], outs[2 * c + 1]], axis=0)
            o_ref[0, h, :, g * gw + c * LANES:g * gw + (c + 1) * LANES] = pair.T


def _overlap_t(nc_pad, ns):
    nc = nc_pad - 1
    c0 = np.arange(nc_pad) * CMP_STRIDE
    s0 = np.arange(ns) * SEL_BLOCK
    ov = np.minimum(c0[None, :] + CMP_LEN, s0[:, None] + SEL_BLOCK) - np.maximum(c0[None, :], s0[:, None])
    ov = np.clip(ov, 0, None).astype(np.float32) / CMP_LEN
    ov = np.where(np.arange(nc_pad)[None, :] < nc, ov, 0.0)
    return jnp.asarray(ov, BF16)


def _nsa(qb, ks, kw, vst, vwt, kc, vct, gt):
    B, _, _, T, _ = qb.shape
    W = B_WIDTH
    nq = T // NSA_TQ
    ns = T // SEL_BLOCK
    nc_pad = kc.shape[2]
    nrow = B_GROUP * NSA_TQ
    part = T // NSA_SPLIT
    assert NSA_TQ == NSA_CK and WIN % NSA_CK == 0 and SEL_LANE0 + ns <= LANES and (nq // NSA_SPLIT) % 2 == 0
    assert gt.shape == (B, NSA_SPLIT, GATE_ROWS, part)
    kspec = pl.BlockSpec((1, B_KV_HEADS, T, LANES), lambda b, i: (b, 0, 0, 0))
    vspec = pl.BlockSpec((1, T // NSA_CK, B_KV_HEADS * NSA_VROWS, NSA_CK), lambda b, i: (b, 0, 0, 0))
    key = np.arange(NSA_CK)[:, None]
    qry = (np.arange(nrow) % NSA_TQ)[None, :]
    bias = jnp.asarray(np.stack([np.where(key <= qry, 0.0, NEG), np.where(key > qry, 0.0, NEG)]).astype(np.float32))
    cmp_end = (np.arange(nc_pad) * CMP_STRIDE + CMP_LEN - 1)[None, :, None]
    t_query = (np.arange(nq) * NSA_TQ)[:, None, None] + qry[None]
    cbias = jnp.asarray(np.where((cmp_end <= t_query) & (np.arange(nc_pad) < nc_pad - 1)[None, :, None], 0.0, NEG)
                        .astype(np.float32)).reshape(NSA_SPLIT, nq // NSA_SPLIT, nc_pad, nrow)
    out = pl.pallas_call(
        functools.partial(_nsa_kernel, n_sel_blocks=ns, nq=nq),
        grid=(B, nq // NSA_SPLIT),
        in_specs=[pl.BlockSpec((1, B_KV_HEADS, B_GROUP, NSA_SPLIT, NSA_TQ, LANES), lambda b, i: (b, 0, 0, 0, i, 0)),
                  kspec, kspec, vspec, vspec,
                  pl.BlockSpec((1, B_KV_HEADS, nc_pad, LANES), lambda b, i: (b, 0, 0, 0)),
                  pl.BlockSpec((1, B_KV_HEADS, HEAD_DIM, nc_pad), lambda b, i: (b, 0, 0, 0)),
                  pl.BlockSpec((1, NSA_SPLIT, GATE_ROWS, NSA_TQ), lambda b, i: (b, 0, 0, i)),
                  pl.BlockSpec((ns, nc_pad), lambda b, i: (0, 0)),
                  pl.BlockSpec((2, NSA_CK, nrow), lambda b, i: (0, 0, 0)),
                  pl.BlockSpec((NSA_SPLIT, 1, nc_pad, nrow), lambda b, i: (0, i, 0, 0))],
        out_specs=pl.BlockSpec((1, NSA_SPLIT, NSA_TQ, W), lambda b, i: (b, 0, i, 0)),
        out_shape=jax.ShapeDtypeStruct((B, NSA_SPLIT, part, W), F32),
        compiler_params=pltpu.CompilerParams(dimension_semantics=("parallel", "arbitrary"),
                                             vmem_limit_bytes=VMEM_LIMIT),
        name="nsa",
    )(qb.reshape(B, B_KV_HEADS, B_GROUP, NSA_SPLIT, part, LANES), ks, kw, vst, vwt, kc, vct,
      gt, _overlap_t(nc_pad, ns), bias, cbias)
    return out.reshape(B, T, W)


def _post_kernel(x_ref, oa_ref, ob_ref, ga_ref, gb_ref, wo_ref, gm_ref, wu_ref, wd_ref, gf_ref, o_ref, *, final):
    def norm(v, g):
        return v * lax.rsqrt(jnp.mean(v * v, axis=-1, keepdims=True) + EPS) * g

    na = norm(oa_ref[...], ga_ref[...]).astype(BF16)
    nb = norm(ob_ref[...], gb_ref[...]).astype(BF16)
    aw = na.shape[1]
    h_res = x_ref[...] + _dot(na, wo_ref[0:aw, :]) + _dot(nb, wo_ref[aw:, :])
    h = norm(h_res, gm_ref[...]).astype(BF16)
    u = jnp.square(jnp.maximum(_dot(h, wu_ref[...]), 0.0)).astype(BF16)
    acc = h_res + _dot(u, wd_ref[...])
    o_ref[...] = norm(acc, gf_ref[...]) if final else acc


def _post(x, oa, ob, g_a, g_b, w_out, g_mlp, w_up, w_down, g_final, *, final, tm=512):
    B, T, D = x.shape
    n = B * T
    dff = w_up.shape[1]
    tok = lambda w: pl.BlockSpec((tm, w), lambda i: (i, 0))
    const = lambda shape: pl.BlockSpec(shape, lambda i: (0, 0), pipeline_mode=pl.Buffered(1))
    out = pl.pallas_call(
        functools.partial(_post_kernel, final=final),
        grid=(n // tm,),
        in_specs=[tok(D), tok(A_WIDTH), tok(B_WIDTH), const((1, A_WIDTH)), const((1, B_WIDTH)),
                  const((A_WIDTH + B_WIDTH, D)), const((1, D)), const((D, dff)), const((dff, D)), const((1, D))],
        out_specs=tok(D),
        out_shape=jax.ShapeDtypeStruct((n, D), F32),
        compiler_params=pltpu.CompilerParams(dimension_semantics=("parallel",), vmem_limit_bytes=VMEM_LIMIT),
        name="post",
    )(x.reshape(n, D), oa.reshape(n, A_WIDTH), ob.reshape(n, B_WIDTH), g_a.reshape(1, -1), g_b.reshape(1, -1),
      w_out.astype(BF16), g_mlp.reshape(1, D), w_up.astype(BF16), w_down.astype(BF16), g_final.reshape(1, D))
    return out.reshape(B, T, D)


def kernel(x, norm_mix, w_in, cmp_pe_k, cmp_w1_k, cmp_w2_k, cmp_pe_v, cmp_w1_v, cmp_w2_v, g_out_a, g_out_b,
           w_out, norm_mlp, w_up, w_down, norm_final):
    B, T, D = x.shape
    depth = w_in.shape[0]
    tables = _rope_tables(np.arange(T))
    cmp_tables = _rope_tables(np.arange(T // CMP_STRIDE) * CMP_STRIDE + CMP_LEN - 1)
    h_res = x
    for l in range(depth):
        qa, ka, va, qb, ks, kw, vst, vwt, kcvc, gt = _in_proj(h_res, norm_mix[l], w_in[l], tables)
        kc, vct = _compress(kcvc, cmp_pe_k[l], cmp_w1_k[l], cmp_w2_k[l], cmp_pe_v[l], cmp_w1_v[l], cmp_w2_v[l],
                            cmp_tables)
        oa = _mixer_a(qa, ka, va)
        ob = _nsa(qb, ks, kw, vst, vwt, kc, vct, gt)
        h_res = _post(h_res, oa, ob, g_out_a[l], g_out_b[l], w_out[l], norm_mlp[l], w_up[l], w_down[l], norm_final,
                      final=(l == depth - 1))
    return h_res
```

```python
import functools

import jax
import jax.numpy as jnp
import numpy as np
from jax import lax
from jax.experimental import pallas as pl
from jax.experimental.pallas import tpu as pltpu

F32 = jnp.float32
BF16 = jnp.bfloat16

HEAD_DIM = 64
ROT_DIM = HEAD_DIM // 4
ROPE_THETA = 500000.0
EPS = 1e-6
NEG = -1e30
Q_SCALE = HEAD_DIM ** -0.5 * 1.4426950408889634
LANES = 128

A_HEADS = 8
A_PATTERNS = ((128, 1), (512, 4), (2048, 16))
A_BLOCK = 128
A_GROUP = 16

B_HEADS = 8
B_KV_HEADS = 2
B_GROUP = B_HEADS // B_KV_HEADS
CMP_LEN = 32
CMP_STRIDE = 16
CMP_HIDDEN = 256
SEL_BLOCK = 64
SEL_SHIFT = 6
SEL_TOPK = 8
WIN = 512
N_BRANCH = 3

A_WIDTH = A_HEADS * HEAD_DIM
B_WIDTH = B_HEADS * HEAD_DIM
KV_WIDTH = B_KV_HEADS * HEAD_DIM

NSA_TQ = 256
NSA_CK = 256
NSA_SPLIT = 2
NSA_VROWS = 80
GATE_ROWS = 32
SEL_LANE0 = HEAD_DIM

VMEM_LIMIT = 56 * 1024 * 1024


def _dot(a, b):
    return jnp.dot(a, b, preferred_element_type=F32)


def _dot_nt(a, b):
    return lax.dot_general(a, b, (((1,), (1,)), ((), ())), preferred_element_type=F32)


def _rope_rows(y, cos, sin_a, sin_b):
    outs = []
    for c in range(y.shape[1] // LANES):
        yc = y[:, c * LANES:(c + 1) * LANES]
        outs.append(yc * cos + pltpu.roll(yc, LANES - ROT_DIM // 2, 1) * sin_a
                    + pltpu.roll(yc, ROT_DIM // 2, 1) * sin_b)
    return outs[0] if len(outs) == 1 else jnp.concatenate(outs, axis=1)


def _in_proj_kernel(x_ref, g_ref, wq_ref, wkk_ref, wt_ref, cos_ref, sa_ref, sb_ref,
                    qa_ref, ka_ref, va_ref, qb_ref, ks_ref, kw_ref, vst_ref, vwt_ref, kcvc_ref, gt_ref,
                    *, tm):
    tt = pl.program_id(1)
    x = x_ref[...]
    ms = jnp.mean(x * x, axis=-1, keepdims=True)
    h = (x * lax.rsqrt(ms + EPS) * g_ref[...]).astype(BF16)
    cos, sa, sb = cos_ref[...], sa_ref[...], sb_ref[...]
    scale = Q_SCALE

    def proj(c0, c1):
        return _dot(h, wq_ref[:, c0:c1])

    o = 0
    qa_ref[...] = _rope_rows(proj(o, o + A_WIDTH), cos, sa, sb) * scale
    o += A_WIDTH
    ka_ref[...] = _rope_rows(proj(o, o + A_WIDTH), cos, sa, sb)
    o += A_WIDTH
    va_ref[...] = proj(o, o + A_WIDTH)
    o += A_WIDTH
    lane = lax.broadcasted_iota(jnp.int32, (tm, LANES), 1)
    row = lax.broadcasted_iota(jnp.int32, (tm, LANES), 0)
    lo = lane < HEAD_DIM
    qb = _rope_rows(proj(o, o + B_WIDTH), cos, sa, sb) * scale
    for hh in range(B_HEADS):
        ch = qb[:, (hh // 2) * LANES:(hh // 2 + 1) * LANES]
        if hh % 2:
            ch = pltpu.roll(ch, HEAD_DIM, 1)
        qb_ref[0, hh // B_GROUP, hh % B_GROUP] = jnp.where(lo, ch, 0.0).astype(BF16)
    o += B_WIDTH
    kcvc_ref[...] = proj(o, o + 2 * KV_WIDTH)
    o += 2 * KV_WIDTH
    ksw = _rope_rows(_dot(h, wkk_ref[...]), cos, sa, sb)

    blk = (tt * tm + row) >> SEL_SHIFT
    onehot = jnp.where(lane - SEL_LANE0 == blk, 1.0, 0.0)
    for kind, ref in ((0, ks_ref), (1, kw_ref)):
        kk = ksw[:, kind * LANES:(kind + 1) * LANES]
        tail = onehot if kind == 0 else 0.0
        ref[0, 0] = jnp.where(lo, kk, tail).astype(BF16)
        ref[0, 1] = jnp.where(lo, pltpu.roll(kk, HEAD_DIM, 1), tail).astype(BF16)

    tr = _dot_nt(wt_ref[...], h)
    ones_row = jnp.where(lax.broadcasted_iota(jnp.int32, (NSA_VROWS - HEAD_DIM, tm), 0) == 0, 1.0, 0.0)
    for kind, ref in ((0, vst_ref), (1, vwt_ref)):
        rows = [tr[kind * LANES + gg * HEAD_DIM:kind * LANES + (gg + 1) * HEAD_DIM] for gg in range(B_KV_HEADS)]
        slab = jnp.concatenate([rows[0], ones_row, rows[1], ones_row], axis=0).astype(BF16)
        for c in range(tm // NSA_CK):
            ref[0, c] = slab[:, c * NSA_CK:(c + 1) * NSA_CK]
    gt_ref[0, 0] = jax.nn.sigmoid(tr[2 * LANES:2 * LANES + GATE_ROWS, :])


def _rope_tables(pos):
    half = ROT_DIM // 2
    inv = ROPE_THETA ** (-np.arange(0, ROT_DIM, 2, dtype=np.float64) / ROT_DIM)
    ang = np.asarray(pos, np.float64)[:, None] * inv[None, :]
    cos, sin = np.cos(ang), np.sin(ang)
    n = len(pos)
    ones = np.ones((n, HEAD_DIM - ROT_DIM))
    zeros = np.zeros((n, HEAD_DIM - ROT_DIM))
    zh = np.zeros((n, half))
    c_head = np.concatenate([cos, cos, ones], axis=1)
    a_head = np.concatenate([-sin, zh, zeros], axis=1)
    b_head = np.concatenate([zh, sin, zeros], axis=1)
    rep = LANES // HEAD_DIM
    return tuple(jnp.asarray(np.tile(t, (1, rep)).astype(np.float32)) for t in (c_head, a_head, b_head))


def _in_proj(x, norm_g, w_in, tables, *, tm=1024):
    B, T, D = x.shape
    nt = T // tm
    offs = [0]
    for n in (A_WIDTH, A_WIDTH, A_WIDTH, B_WIDTH, KV_WIDTH, KV_WIDTH, KV_WIDTH, KV_WIDTH, KV_WIDTH, KV_WIDTH,
              B_HEADS * N_BRANCH):
        offs.append(offs[-1] + n)
    col = lambda i: w_in[:, offs[i]:offs[i + 1]]
    wq = w_in[:, :offs[6]].astype(BF16)
    wkk = jnp.concatenate([col(6), col(8)], axis=1).astype(BF16)
    gpad = jnp.zeros((D, GATE_ROWS - B_HEADS * N_BRANCH), w_in.dtype)
    wt = jnp.concatenate([col(7), col(9), col(10), gpad], axis=1).T.astype(BF16)
    cos, sa, sb = tables
    nq = wq.shape[1]
    nr = wt.shape[0]
    tok = lambda w: pl.BlockSpec((None, tm, w), lambda b, t: (b, t, 0))
    const = lambda shape: pl.BlockSpec(shape, lambda b, t: (0,) * len(shape))
    tab = pl.BlockSpec((tm, LANES), lambda b, t: (t, 0))
    out_shapes = (
        jax.ShapeDtypeStruct((B, T, A_WIDTH), F32),
        jax.ShapeDtypeStruct((B, T, A_WIDTH), F32),
        jax.ShapeDtypeStruct((B, T, A_WIDTH), F32),
        jax.ShapeDtypeStruct((B, B_KV_HEADS, B_GROUP, T, LANES), BF16),
        jax.ShapeDtypeStruct((B, B_KV_HEADS, T, LANES), BF16),
        jax.ShapeDtypeStruct((B, B_KV_HEADS, T, LANES), BF16),
        jax.ShapeDtypeStruct((B, T // NSA_CK, B_KV_HEADS * NSA_VROWS, NSA_CK), BF16),
        jax.ShapeDtypeStruct((B, T // NSA_CK, B_KV_HEADS * NSA_VROWS, NSA_CK), BF16),
        jax.ShapeDtypeStruct((B, T, 2 * KV_WIDTH), F32),
        jax.ShapeDtypeStruct((B, nt, GATE_ROWS, tm), F32),
    )
    frame = pl.BlockSpec((1, B_KV_HEADS, tm, LANES), lambda b, t: (b, 0, t, 0))
    vt = pl.BlockSpec((1, tm // NSA_CK, B_KV_HEADS * NSA_VROWS, NSA_CK), lambda b, t: (b, t, 0, 0))
    qframe = pl.BlockSpec((1, B_KV_HEADS, B_GROUP, tm, LANES), lambda b, t: (b, 0, 0, t, 0))
    out_specs = (tok(A_WIDTH), tok(A_WIDTH), tok(A_WIDTH), qframe, frame, frame, vt, vt,
                 tok(2 * KV_WIDTH), pl.BlockSpec((1, 1, GATE_ROWS, tm), lambda b, t: (b, t, 0, 0)))
    return pl.pallas_call(
        functools.partial(_in_proj_kernel, tm=tm),
        grid=(B, nt),
        in_specs=[tok(D), const((1, D)), const((D, nq)), const((D, 2 * KV_WIDTH)), const((nr, D)), tab, tab, tab],
        out_specs=out_specs,
        out_shape=out_shapes,
        compiler_params=pltpu.CompilerParams(dimension_semantics=("parallel", "parallel"),
                                             vmem_limit_bytes=VMEM_LIMIT),
        name="in_proj",
    )(x, norm_g.reshape(1, D), wq, wkk, wt, cos, sa, sb)


def _compress_kernel(ak_ref, av_ref, w1k_ref, w1v_ref, pek_ref, pev_ref, w2k_ref, w2vt_ref, cos_ref, sa_ref, sb_ref,
                     kc_ref, vct_ref, *, nc_pad):
    half = CMP_LEN // 2
    hid_w = B_KV_HEADS * CMP_HIDDEN

    zeros = jnp.zeros((HEAD_DIM, CMP_HIDDEN), BF16)

    def both_groups(w):
        return jnp.concatenate([jnp.concatenate([w, zeros], axis=1), jnp.concatenate([zeros, w], axis=1)], axis=0)

    def hidden(a_ref, w1_ref, pe_ref):
        acc_u = jnp.zeros((nc_pad, hid_w), F32)
        acc_v = jnp.zeros((nc_pad, hid_w), F32)
        for p in range(half):
            ap = a_ref[0, pl.ds(p, nc_pad, stride=CMP_STRIDE), :]
            acc_u = acc_u + _dot((ap + pe_ref[p:p + 1, :]).astype(BF16), both_groups(w1_ref[p]))
            acc_v = acc_v + _dot((ap + pe_ref[half + p:half + p + 1, :]).astype(BF16), both_groups(w1_ref[half + p]))
        return jax.nn.gelu(acc_u + pltpu.roll(acc_v, nc_pad - 1, 0))

    hk = hidden(ak_ref, w1k_ref, pek_ref).astype(BF16)
    hv = hidden(av_ref, w1v_ref, pev_ref).astype(BF16)
    for g in range(B_KV_HEADS):
        hg = hk[:, g * CMP_HIDDEN:(g + 1) * CMP_HIDDEN]
        kc = _dot(hg, w2k_ref[...])
        kc_ref[0, g] = _rope_rows(kc, cos_ref[...], sa_ref[...], sb_ref[...]).astype(BF16)
        vg = hv[:, g * CMP_HIDDEN:(g + 1) * CMP_HIDDEN]
        vct_ref[0, g] = _dot_nt(w2vt_ref[...], vg).astype(BF16)


def _compress(kcvc, pe_k, w1_k, w2_k, pe_v, w1_v, w2_v, cmp_tables):
    B, T, _ = kcvc.shape
    nc_pad = T // CMP_STRIDE
    w2k = jnp.concatenate([w2_k, jnp.zeros_like(w2_k)], axis=1).astype(BF16)
    w2vt = w2_v.T.astype(BF16)
    pek = jnp.tile(pe_k, (1, B_KV_HEADS))
    pev = jnp.tile(pe_v, (1, B_KV_HEADS))
    per_pos = lambda w1: w1.reshape(CMP_LEN, HEAD_DIM, CMP_HIDDEN).astype(BF16)
    const = lambda shape: pl.BlockSpec(shape, lambda b: (0,) * len(shape))
    cos, sa, sb = cmp_tables
    return pl.pallas_call(
        functools.partial(_compress_kernel, nc_pad=nc_pad),
        grid=(B,),
        in_specs=[pl.BlockSpec((1, T, KV_WIDTH), lambda b: (b, 0, 0)), pl.BlockSpec((1, T, KV_WIDTH), lambda b: (b, 0, 1)),
                  const((CMP_LEN, HEAD_DIM, CMP_HIDDEN)), const((CMP_LEN, HEAD_DIM, CMP_HIDDEN)),
                  const((CMP_LEN, LANES)), const((CMP_LEN, LANES)),
                  const((CMP_HIDDEN, LANES)), const((HEAD_DIM, CMP_HIDDEN)),
                  const((nc_pad, LANES)), const((nc_pad, LANES)), const((nc_pad, LANES))],
        out_specs=(pl.BlockSpec((1, B_KV_HEADS, nc_pad, LANES), lambda b: (b, 0, 0, 0)),
                   pl.BlockSpec((1, B_KV_HEADS, HEAD_DIM, nc_pad), lambda b: (b, 0, 0, 0))),
        out_shape=(jax.ShapeDtypeStruct((B, B_KV_HEADS, nc_pad, LANES), BF16),
                   jax.ShapeDtypeStruct((B, B_KV_HEADS, HEAD_DIM, nc_pad), BF16)),
        compiler_params=pltpu.CompilerParams(dimension_semantics=("parallel",), vmem_limit_bytes=VMEM_LIMIT),
        name="compress",
    )(kcvc, kcvc, per_pos(w1_k), per_pos(w1_v), pek, pev, w2k, w2vt, cos, sa, sb)


def _mixer_a_kernel(q_ref, k_ref, v_ref, bias_ref, o_ref, qd0, qd1, kd, vd0, vd1, u_s, m_s, l_s, *, seq):
    blk = A_BLOCK
    nres = seq // blk
    npat = len(A_PATTERNS)
    order = sorted(range(npat), key=lambda p: -A_PATTERNS[p][1])
    slot = {p: n for n, p in enumerate(order[:-1])}
    lane = lax.broadcasted_iota(jnp.int32, (blk, LANES), 1)
    lo = lane < HEAD_DIM

    def deinterleave(r):
        rows = pl.ds(r * blk, blk)
        q, v = q_ref[0, pl.ds(r, blk, stride=nres), :], v_ref[0, pl.ds(r, blk, stride=nres), :]
        qd0[rows, :] = jnp.where(lo, q, 0.0)
        qd1[rows, :] = jnp.where(lo, 0.0, q)
        kd[rows, :] = k_ref[0, pl.ds(r, blk, stride=nres), :]
        vd0[rows, :] = jnp.where(lo, v, 1.0)
        vd1[rows, :] = jnp.where(lo, 1.0, v)

    def pieces(dil, rd, row_off, rows):
        return [pl.ds((rd + dil * jj) * blk + row_off, rows) for jj in range(nres // dil)]

    def gather(ref, idx, lead=()):
        parts = [ref[lead + (i, slice(None))] for i in idx]
        return parts[0] if len(parts) == 1 else jnp.concatenate(parts, axis=0)

    def attend(pi, blocks):
        dil = A_PATTERNS[pi][1]
        pr = blk // (nres // dil)
        q_idxs, vbs, scores = [], [], []
        for rd, n, first in blocks:
            q_idx = pieces(dil, rd, n * pr, pr)
            k_idx = q_idx if first else pieces(dil, rd, (n - 1) * pr, 2 * pr)
            bias = bias_ref[pi, :, 0:blk] if first else bias_ref[pi, :, blk:3 * blk]
            kb = gather(kd, k_idx).astype(BF16)
            q_idxs.append(q_idx)
            for qd, vd in ((qd0, vd0), (qd1, vd1)):
                vbs.append(gather(vd, k_idx).astype(BF16))
                scores.append(_dot_nt(gather(qd, q_idx).astype(BF16), kb) + bias)
        es, ms = [], []
        for s in scores:
            m = jnp.max(s, axis=-1, keepdims=True)
            ms.append(m)
            es.append(jnp.exp2((s - m).astype(BF16)))
        pvs = [_dot(e, vb) for e, vb in zip(es, vbs)]
        for b, q_idx in enumerate(q_idxs):
            u = jnp.where(lo, pvs[2 * b], pvs[2 * b + 1])
            l_swapped = jnp.where(lo, pvs[2 * b + 1], pvs[2 * b])
            m = jnp.where(lo, ms[2 * b], ms[2 * b + 1])
            if pi != order[-1]:
                for jj, idx in enumerate(q_idx):
                    u_s[slot[pi], idx, :] = u[jj * pr:(jj + 1) * pr]
                    m_s[slot[pi], idx, :] = m[jj * pr:(jj + 1) * pr]
                    l_s[slot[pi], idx, :] = l_swapped[jj * pr:(jj + 1) * pr]
                continue
            parts = [(u, m, l_swapped)] + [(gather(u_s, q_idx, (sl,)), gather(m_s, q_idx, (sl,)),
                                            gather(l_s, q_idx, (sl,))) for sl in slot.values()]
            m_all = functools.reduce(jnp.maximum, [mm for _, mm, _ in parts])
            num = jnp.zeros((blk, LANES), F32)
            den = jnp.zeros((blk, LANES), F32)
            for uu, mm, ll in parts:
                a = jnp.exp2(mm - m_all)
                num = num + a * uu
                den = den + a * pltpu.roll(ll, HEAD_DIM, 1)
            out = num / den
            n = blocks[b][1]
            for jj in range(nres):
                o_ref[0, pl.ds(n * blk + jj, pr, stride=nres), :] = out[jj * pr:(jj + 1) * pr]

    for pi in order:
        dil = A_PATTERNS[pi][1]
        nb = seq // dil // blk
        blocks = [(rd, n, n == 0) for rd in range(dil) for n in range(nb)]
        for g0 in range(0, len(blocks), A_GROUP):
            if pi == order[0]:
                for rd, _, _ in blocks[g0:g0 + A_GROUP]:
                    deinterleave(rd)
            attend(pi, blocks[g0:g0 + A_GROUP])


def _mixer_a_bias(nres):
    blk = A_BLOCK

    def sub_pos(i, fold, rows):
        return fold * (i % rows) + i // rows

    out = np.zeros((len(A_PATTERNS), blk, 3 * blk), np.float32)
    for pi, (window, dil) in enumerate(A_PATTERNS):
        n_back = window // dil
        fold = nres // dil
        pr = blk // fold
        sq = sub_pos(np.arange(blk), fold, pr)[:, None]
        d_first = sq - sub_pos(np.arange(blk), fold, pr)[None, :]
        d_band = sq + blk - sub_pos(np.arange(2 * blk), fold, 2 * pr)[None, :]
        dist = np.concatenate([d_first, d_band], axis=1)
        out[pi] = np.where((dist >= 0) & (dist <= n_back), 0.0, NEG)
    return jnp.asarray(out)


def _mixer_a(qa, ka, va):
    B, T, W = qa.shape
    npair = W // LANES
    spec = pl.BlockSpec((1, T, LANES), lambda b, p: (b, 0, p))
    npat = len(A_PATTERNS)
    nres = T // A_BLOCK
    dils = sorted(d for _, d in A_PATTERNS)
    assert all(nres % d == 0 and T % (d * A_BLOCK) == 0 for d in dils) and dils[0] == 1 and dils[-1] == nres
    return pl.pallas_call(
        functools.partial(_mixer_a_kernel, seq=T),
        grid=(B, npair),
        in_specs=[spec, spec, spec, pl.BlockSpec((npat, A_BLOCK, 3 * A_BLOCK), lambda b, p: (0, 0, 0))],
        out_specs=spec,
        out_shape=jax.ShapeDtypeStruct((B, T, W), F32),
        scratch_shapes=[pltpu.VMEM((T, LANES), F32)] * 5 + [pltpu.VMEM((npat - 1, T, LANES), F32)] * 3,
        compiler_params=pltpu.CompilerParams(dimension_semantics=("parallel", "parallel"),
                                             vmem_limit_bytes=VMEM_LIMIT),
        name="mixer_a",
    )(qa, ka, va, _mixer_a_bias(nres))


def _nsa_kernel(q_ref, ks_ref, kw_ref, vst_ref, vwt_ref, kc_ref, vct_ref, gt_ref, ovt_ref, bias_ref, cbias_ref, o_ref,
                *, n_sel_blocks, nq):
    tq, ck = NSA_TQ, NSA_CK
    nrow = B_GROUP * tq
    gw = B_GROUP * HEAD_DIM
    step = pl.program_id(1)
    per_part = nq // NSA_SPLIT
    lanes = [(h, g) for h in range(NSA_SPLIT) for g in range(B_KV_HEADS)]
    qi = [step + h * per_part for h in range(NSA_SPLIT)]
    col_i = lax.broadcasted_iota(jnp.int32, (1, nrow), 1) & (tq - 1)

    def chunk(ref, g, c):
        return ref[0, g, pl.ds(pl.multiple_of(c * ck, ck), ck), :]

    def vt_chunk(ref, g, c):
        return ref[0, c, g * NSA_VROWS:(g + 1) * NSA_VROWS, :]

    def softmax_pv(s, vt):
        m = jnp.max(s, axis=0, keepdims=True)
        return m, _dot(vt, jnp.exp2(s - m).astype(BF16))

    def merged(parts):
        m_new = functools.reduce(jnp.maximum, [m for m, _ in parts])
        acc = sum(jnp.exp2(m - m_new) * a for m, a in parts)
        return m_new, acc

    def normalised(state):
        return state[1][0:HEAD_DIM] * (1.0 / state[1][HEAD_DIM:HEAD_DIM + 1])

    q_b = {(h, g): jnp.concatenate([q_ref[0, g, r, h] for r in range(B_GROUP)], axis=0) for h, g in lanes}

    nwin = WIN // ck
    win_chunks = [[jnp.maximum(qi[h] - back, 0) for back in range(nwin, -1, -1)] for h in range(NSA_SPLIT)]
    s_win = {}
    for h, g in lanes:
        s = _dot_nt(jnp.concatenate([chunk(kw_ref, g, c) for c in win_chunks[h]], axis=0), q_b[h, g])
        pieces = [jnp.where(qi[h] >= nwin, s[0:ck] + bias_ref[1], NEG)]
        for n in range(1, nwin):
            pieces.append(jnp.where(qi[h] >= nwin - n, s[n * ck:(n + 1) * ck], NEG))
        pieces.append(s[nwin * ck:] + bias_ref[0])
        s_win[h, g] = pieces

    s_cmp = {(h, g): _dot_nt(kc_ref[0, g], q_b[h, g]) + cbias_ref[h, 0] for h, g in lanes}
    p_cmp, o_cmp = {}, {}
    for h, g in lanes:
        m = jnp.max(s_cmp[h, g], axis=0, keepdims=True)
        e = jnp.exp2(s_cmp[h, g] - m)
        den = jnp.sum(e, axis=0, keepdims=True)
        sees_block = qi[h] * tq + col_i >= CMP_LEN - 1
        p_cmp[h, g] = e * jnp.where(sees_block, 1.0 / jnp.maximum(den, 1e-30), 0.0)
        o_cmp[h, g] = _dot(vct_ref[0, g], p_cmp[h, g].astype(BF16))

    imp = {}
    for h, g in lanes:
        psum = p_cmp[h, g][:, 0:tq]
        for r in range(1, B_GROUP):
            psum = psum + p_cmp[h, g][:, r * tq:(r + 1) * tq]
        p_hi = psum.astype(BF16)
        p_lo = (psum - p_hi.astype(F32)).astype(BF16)
        imp[h, g] = _dot(ovt_ref[...], p_hi) + _dot(ovt_ref[...], p_lo)
    j = lax.broadcasted_iota(jnp.int32, (n_sel_blocks, tq), 0)
    j_f = j.astype(F32)
    low = -3e38
    q_aug = {}
    for h, g in lanes:
        cur = (qi[h] * tq + lax.broadcasted_iota(jnp.int32, (n_sel_blocks, tq), 1)) >> SEL_SHIFT
        forced = (j == 0) | (j == cur) | (j == cur - 1)
        score = jnp.where(forced, imp[h, g] + 2.0, jnp.where(j > cur, -1.0, imp[h, g]))
        sel = jnp.zeros((n_sel_blocks, tq), jnp.bool_)
        for _ in range(min(SEL_TOPK, n_sel_blocks)):
            mx = jnp.max(score, axis=0, keepdims=True)
            first = jnp.min(jnp.where(score == mx, j_f, 4.0 * LANES), axis=0, keepdims=True)
            hit = j_f == first
            sel = sel | hit
            score = jnp.where(hit, low, score)
        selneg = jnp.concatenate([jnp.zeros((SEL_LANE0, tq), F32), jnp.where(sel, 0.0, NEG),
                                  jnp.zeros((LANES - SEL_LANE0 - n_sel_blocks, tq), F32)], axis=0).T
        q_aug[h, g] = q_b[h, g] + jnp.concatenate([selneg.astype(BF16)] * B_GROUP, axis=0)

    s_diag = {(h, g): _dot_nt(chunk(ks_ref, g, qi[h]), q_aug[h, g]) + bias_ref[0] for h, g in lanes}
    o_win = {(h, g): normalised(merged([softmax_pv(s_win[h, g][n], vt_chunk(vwt_ref, g, c))
                                        for n, c in enumerate(win_chunks[h])])) for h, g in lanes}
    states = tuple(softmax_pv(s_diag[h, g], vt_chunk(vst_ref, g, qi[h])) for h, g in lanes)

    def earlier(work, st):
        s = [_dot_nt(chunk(ks_ref, lanes[n][1], c), q_aug[lanes[n]]) for n, c in work]
        parts = [[st[n]] for n in range(len(lanes))]
        for (n, c), sc in zip(work, s):
            parts[n].append(softmax_pv(sc, vt_chunk(vst_ref, lanes[n][1], c)))
        return tuple(merged(p) if len(p) > 1 else p[0] for p in parts)

    states = lax.fori_loop(0, step, lambda c, st: earlier([(n, c) for n in range(len(lanes))], st), states)
    for h in range(1, NSA_SPLIT):
        mine = [n for n, (hh, _) in enumerate(lanes) if hh >= h]
        pairs = per_part // 2
        states = lax.fori_loop(
            0, pairs,
            lambda i, st, mine=mine, h=h: earlier(
                [(n, step + (h - 1) * per_part + 2 * i + k) for n in mine for k in range(2)], st), states)

    for n, (h, g) in enumerate(lanes):
        o_sel = normalised(states[n])
        gt = gt_ref[0, h, g * B_GROUP * N_BRANCH:(g + 1) * B_GROUP * N_BRANCH, :]
        outs = []
        for r in range(B_GROUP):
            cs = slice(r * tq, (r + 1) * tq)
            outs.append(gt[r * N_BRANCH:r * N_BRANCH + 1, :] * o_cmp[h, g][:, cs]
                        + gt[r * N_BRANCH + 1:r * N_BRANCH + 2, :] * o_sel[:, cs]
                        + gt[r * N_BRANCH + 2:r * N_BRANCH + 3, :] * o_win[h, g][:, cs])
        for c in range(B_GROUP // 2):
            pair = jnp.concatenate([outs[2 * c], outs[2 * c + 1]], axis=0)
            o_ref[0, h, :, g * gw + c * LANES:g * gw + (c + 1) * LANES] = pair.T


def _overlap_t(nc_pad, ns):
    nc = nc_pad - 1
    c0 = np.arange(nc_pad) * CMP_STRIDE
    s0 = np.arange(ns) * SEL_BLOCK
    ov = np.minimum(c0[None, :] + CMP_LEN, s0[:, None] + SEL_BLOCK) - np.maximum(c0[None, :], s0[:, None])
    ov = np.clip(ov, 0, None).astype(np.float32) / CMP_LEN
    ov = np.where(np.arange(nc_pad)[None, :] < nc, ov, 0.0)
    return jnp.asarray(ov, BF16)


def _nsa(qb, ks, kw, vst, vwt, kc, vct, gt):
    B, _, _, T, _ = qb.shape
    W = B_WIDTH
    nq = T // NSA_TQ
    ns = T // SEL_BLOCK
    nc_pad = kc.shape[2]
    nrow = B_GROUP * NSA_TQ
    part = T // NSA_SPLIT
    assert NSA_TQ == NSA_CK and WIN % NSA_CK == 0 and SEL_LANE0 + ns <= LANES and (nq // NSA_SPLIT) % 2 == 0
    assert gt.shape == (B, NSA_SPLIT, GATE_ROWS, part)
    kspec = pl.BlockSpec((1, B_KV_HEADS, T, LANES), lambda b, i: (b, 0, 0, 0))
    vspec = pl.BlockSpec((1, T // NSA_CK, B_KV_HEADS * NSA_VROWS, NSA_CK), lambda b, i: (b, 0, 0, 0))
    key = np.arange(NSA_CK)[:, None]
    qry = (np.arange(nrow) % NSA_TQ)[None, :]
    bias = jnp.asarray(np.stack([np.where(key <= qry, 0.0, NEG), np.where(key > qry, 0.0, NEG)]).astype(np.float32))
    cmp_end = (np.arange(nc_pad) * CMP_STRIDE + CMP_LEN - 1)[None, :, None]
    t_query = (np.arange(nq) * NSA_TQ)[:, None, None] + qry[None]
    cbias = jnp.asarray(np.where((cmp_end <= t_query) & (np.arange(nc_pad) < nc_pad - 1)[None, :, None], 0.0, NEG)
                        .astype(np.float32)).reshape(NSA_SPLIT, nq // NSA_SPLIT, nc_pad, nrow)
    out = pl.pallas_call(
        functools.partial(_nsa_kernel, n_sel_blocks=ns, nq=nq),
        grid=(B, nq // NSA_SPLIT),
        in_specs=[pl.BlockSpec((1, B_KV_HEADS, B_GROUP, NSA_SPLIT, NSA_TQ, LANES), lambda b, i: (b, 0, 0, 0, i, 0)),
                  kspec, kspec, vspec, vspec,
                  pl.BlockSpec((1, B_KV_HEADS, nc_pad, LANES), lambda b, i: (b, 0, 0, 0)),
                  pl.BlockSpec((1, B_KV_HEADS, HEAD_DIM, nc_pad), lambda b, i: (b, 0, 0, 0)),
                  pl.BlockSpec((1, NSA_SPLIT, GATE_ROWS, NSA_TQ), lambda b, i: (b, 0, 0, i)),
                  pl.BlockSpec((ns, nc_pad), lambda b, i: (0, 0)),
                  pl.BlockSpec((2, NSA_CK, nrow), lambda b, i: (0, 0, 0)),
                  pl.BlockSpec((NSA_SPLIT, 1, nc_pad, nrow), lambda b, i: (0, i, 0, 0))],
        out_specs=pl.BlockSpec((1, NSA_SPLIT, NSA_TQ, W), lambda b, i: (b, 0, i, 0)),
        out_shape=jax.ShapeDtypeStruct((B, NSA_SPLIT, part, W), F32),
        compiler_params=pltpu.CompilerParams(dimension_semantics=("parallel", "arbitrary"),
                                             vmem_limit_bytes=VMEM_LIMIT),
        name="nsa",
    )(qb.reshape(B, B_KV_HEADS, B_GROUP, NSA_SPLIT, part, LANES), ks, kw, vst, vwt, kc, vct,
      gt, _overlap_t(nc_pad, ns), bias, cbias)
    return out.reshape(B, T, W)


def _post_kernel(x_ref, oa_ref, ob_ref, ga_ref, gb_ref, wo_ref, gm_ref, wu_ref, wd_ref, gf_ref, o_ref, *, final):
    def norm(v, g):
        return v * lax.rsqrt(jnp.mean(v * v, axis=-1, keepdims=True) + EPS) * g

    na = norm(oa_ref[...], ga_ref[...]).astype(BF16)
    nb = norm(ob_ref[...], gb_ref[...]).astype(BF16)
    aw = na.shape[1]
    h_res = x_ref[...] + _dot(na, wo_ref[0:aw, :]) + _dot(nb, wo_ref[aw:, :])
    h = norm(h_res, gm_ref[...]).astype(BF16)
    u = jnp.square(jnp.maximum(_dot(h, wu_ref[...]), 0.0)).astype(BF16)
    acc = h_res + _dot(u, wd_ref[...])
    o_ref[...] = norm(acc, gf_ref[...]) if final else acc


def _post(x, oa, ob, g_a, g_b, w_out, g_mlp, w_up, w_down, g_final, *, final, tm=512):
    B, T, D = x.shape
    n = B * T
    dff = w_up.shape[1]
    tok = lambda w: pl.BlockSpec((tm, w), lambda i: (i, 0))
    const = lambda shape: pl.BlockSpec(shape, lambda i: (0, 0), pipeline_mode=pl.Buffered(1))
    out = pl.pallas_call(
        functools.partial(_post_kernel, final=final),
        grid=(n // tm,),
        in_specs=[tok(D), tok(A_WIDTH), tok(B_WIDTH), const((1, A_WIDTH)), const((1, B_WIDTH)),
                  const((A_WIDTH + B_WIDTH, D)), const((1, D)), const((D, dff)), const((dff, D)), const((1, D))],
        out_specs=tok(D),
        out_shape=jax.ShapeDtypeStruct((n, D), F32),
        compiler_params=pltpu.CompilerParams(dimension_semantics=("parallel",), vmem_limit_bytes=VMEM_LIMIT),
        name="post",
    )(x.reshape(n, D), oa.reshape(n, A_WIDTH), ob.reshape(n, B_WIDTH), g_a.reshape(1, -1), g_b.reshape(1, -1),
      w_out.astype(BF16), g_mlp.reshape(1, D), w_up.astype(BF16), w_down.astype(BF16), g_final.reshape(1, D))
    return out.reshape(B, T, D)


def kernel(x, norm_mix, w_in, cmp_pe_k, cmp_w1_k, cmp_w2_k, cmp_pe_v, cmp_w1_v, cmp_w2_v, g_out_a, g_out_b,
           w_out, norm_mlp, w_up, w_down, norm_final):
    B, T, D = x.shape
    depth = w_in.shape[0]
    tables = _rope_tables(np.arange(T))
    cmp_tables = _rope_tables(np.arange(T // CMP_STRIDE) * CMP_STRIDE + CMP_LEN - 1)
    h_res = x
    for l in range(depth):
        qa, ka, va, qb, ks, kw, vst, vwt, kcvc, gt = _in_proj(h_res, norm_mix[l], w_in[l], tables)
        kc, vct = _compress(kcvc, cmp_pe_k[l], cmp_w1_k[l], cmp_w2_k[l], cmp_pe_v[l], cmp_w1_v[l], cmp_w2_v[l],
                            cmp_tables)
        oa = _mixer_a(qa, ka, va)
        ob = _nsa(qb, ks, kw, vst, vwt, kc, vct, gt)
        h_res = _post(h_res, oa, ob, g_out_a[l], g_out_b[l], w_out[l], norm_mlp[l], w_up[l], w_down[l], norm_final,
                      final=(l == depth - 1))
    return h_res
```

```python
import functools

import jax
import jax.numpy as jnp
import numpy as np
from jax import lax
from jax.experimental import pallas as pl
from jax.experimental.pallas import tpu as pltpu

F32 = jnp.float32
BF16 = jnp.bfloat16

HEAD_DIM = 64
ROT_DIM = HEAD_DIM // 4
ROPE_THETA = 500000.0
EPS = 1e-6
NEG = -1e30
Q_SCALE = HEAD_DIM ** -0.5 * 1.4426950408889634
LANES = 128

A_HEADS = 8
A_PATTERNS = ((128, 1), (512, 4), (2048, 16))
A_BLOCK = 128
A_GROUP = 16

B_HEADS = 8
B_KV_HEADS = 2
B_GROUP = B_HEADS // B_KV_HEADS
CMP_LEN = 32
CMP_STRIDE = 16
CMP_HIDDEN = 256
SEL_BLOCK = 64
SEL_SHIFT = 6
SEL_TOPK = 8
WIN = 512
N_BRANCH = 3

A_WIDTH = A_HEADS * HEAD_DIM
B_WIDTH = B_HEADS * HEAD_DIM
KV_WIDTH = B_KV_HEADS * HEAD_DIM

NSA_TQ = 256
NSA_CK = 256
NSA_SPLIT = 2
NSA_VROWS = 80
GATE_ROWS = 32
SEL_LANE0 = HEAD_DIM

VMEM_LIMIT = 56 * 1024 * 1024


def _dot(a, b):
    return jnp.dot(a, b, preferred_element_type=F32)


def _dot_nt(a, b):
    return lax.dot_general(a, b, (((1,), (1,)), ((), ())), preferred_element_type=F32)


def _rope_rows(y, cos, sin_a, sin_b):
    outs = []
    for c in range(y.shape[1] // LANES):
        yc = y[:, c * LANES:(c + 1) * LANES]
        outs.append(yc * cos + pltpu.roll(yc, LANES - ROT_DIM // 2, 1) * sin_a
                    + pltpu.roll(yc, ROT_DIM // 2, 1) * sin_b)
    return outs[0] if len(outs) == 1 else jnp.concatenate(outs, axis=1)


def _in_proj_kernel(x_ref, g_ref, wq_ref, wkk_ref, wt_ref, cos_ref, sa_ref, sb_ref,
                    qa_ref, ka_ref, va_ref, qb_ref, ks_ref, kw_ref, vst_ref, vwt_ref, kcvc_ref, gt_ref,
                    *, tm):
    tt = pl.program_id(1)
    x = x_ref[...]
    ms = jnp.mean(x * x, axis=-1, keepdims=True)
    h = (x * lax.rsqrt(ms + EPS) * g_ref[...]).astype(BF16)
    cos, sa, sb = cos_ref[...], sa_ref[...], sb_ref[...]
    scale = Q_SCALE

    def proj(c0, c1):
        return _dot(h, wq_ref[:, c0:c1])

    o = 0
    qa_ref[...] = _rope_rows(proj(o, o + A_WIDTH), cos, sa, sb) * scale
    o += A_WIDTH
    ka_ref[...] = _rope_rows(proj(o, o + A_WIDTH), cos, sa, sb)
    o += A_WIDTH
    va_ref[...] = proj(o, o + A_WIDTH)
    o += A_WIDTH
    lane = lax.broadcasted_iota(jnp.int32, (tm, LANES), 1)
    row = lax.broadcasted_iota(jnp.int32, (tm, LANES), 0)
    lo = lane < HEAD_DIM
    qb = _rope_rows(proj(o, o + B_WIDTH), cos, sa, sb) * scale
    for hh in range(B_HEADS):
        ch = qb[:, (hh // 2) * LANES:(hh // 2 + 1) * LANES]
        if hh % 2:
            ch = pltpu.roll(ch, HEAD_DIM, 1)
        qb_ref[0, hh // B_GROUP, hh % B_GROUP] = jnp.where(lo, ch, 0.0).astype(BF16)
    o += B_WIDTH
    kcvc_ref[...] = proj(o, o + 2 * KV_WIDTH)
    o += 2 * KV_WIDTH
    ksw = _rope_rows(_dot(h, wkk_ref[...]), cos, sa, sb)

    blk = (tt * tm + row) >> SEL_SHIFT
    onehot = jnp.where(lane - SEL_LANE0 == blk, 1.0, 0.0)
    for kind, ref in ((0, ks_ref), (1, kw_ref)):
        kk = ksw[:, kind * LANES:(kind + 1) * LANES]
        tail = onehot if kind == 0 else 0.0
        ref[0, 0] = jnp.where(lo, kk, tail).astype(BF16)
        ref[0, 1] = jnp.where(lo, pltpu.roll(kk, HEAD_DIM, 1), tail).astype(BF16)

    tr = _dot_nt(wt_ref[...], h)
    ones_row = jnp.where(lax.broadcasted_iota(jnp.int32, (NSA_VROWS - HEAD_DIM, tm), 0) == 0, 1.0, 0.0)
    for kind, ref in ((0, vst_ref), (1, vwt_ref)):
        rows = [tr[kind * LANES + gg * HEAD_DIM:kind * LANES + (gg + 1) * HEAD_DIM] for gg in range(B_KV_HEADS)]
        slab = jnp.concatenate([rows[0], ones_row, rows[1], ones_row], axis=0).astype(BF16)
        for c in range(tm // NSA_CK):
            ref[0, c] = slab[:, c * NSA_CK:(c + 1) * NSA_CK]
    gt_ref[0, 0] = jax.nn.sigmoid(tr[2 * LANES:2 * LANES + GATE_ROWS, :])


def _rope_tables(pos):
    half = ROT_DIM // 2
    inv = ROPE_THETA ** (-np.arange(0, ROT_DIM, 2, dtype=np.float64) / ROT_DIM)
    ang = np.asarray(pos, np.float64)[:, None] * inv[None, :]
    cos, sin = np.cos(ang), np.sin(ang)
    n = len(pos)
    ones = np.ones((n, HEAD_DIM - ROT_DIM))
    zeros = np.zeros((n, HEAD_DIM - ROT_DIM))
    zh = np.zeros((n, half))
    c_head = np.concatenate([cos, cos, ones], axis=1)
    a_head = np.concatenate([-sin, zh, zeros], axis=1)
    b_head = np.concatenate([zh, sin, zeros], axis=1)
    rep = LANES // HEAD_DIM
    return tuple(jnp.asarray(np.tile(t, (1, rep)).astype(np.float32)) for t in (c_head, a_head, b_head))


def _in_proj(x, norm_g, w_in, tables, *, tm=1024):
    B, T, D = x.shape
    nt = T // tm
    offs = [0]
    for n in (A_WIDTH, A_WIDTH, A_WIDTH, B_WIDTH, KV_WIDTH, KV_WIDTH, KV_WIDTH, KV_WIDTH, KV_WIDTH, KV_WIDTH,
              B_HEADS * N_BRANCH):
        offs.append(offs[-1] + n)
    col = lambda i: w_in[:, offs[i]:offs[i + 1]]
    wq = w_in[:, :offs[6]].astype(BF16)
    wkk = jnp.concatenate([col(6), col(8)], axis=1).astype(BF16)
    gpad = jnp.zeros((D, GATE_ROWS - B_HEADS * N_BRANCH), w_in.dtype)
    wt = jnp.concatenate([col(7), col(9), col(10), gpad], axis=1).T.astype(BF16)
    cos, sa, sb = tables
    nq = wq.shape[1]
    nr = wt.shape[0]
    tok = lambda w: pl.BlockSpec((None, tm, w), lambda b, t: (b, t, 0))
    const = lambda shape: pl.BlockSpec(shape, lambda b, t: (0,) * len(shape))
    tab = pl.BlockSpec((tm, LANES), lambda b, t: (t, 0))
    out_shapes = (
        jax.ShapeDtypeStruct((B, T, A_WIDTH), F32),
        jax.ShapeDtypeStruct((B, T, A_WIDTH), F32),
        jax.ShapeDtypeStruct((B, T, A_WIDTH), F32),
        jax.ShapeDtypeStruct((B, B_KV_HEADS, B_GROUP, T, LANES), BF16),
        jax.ShapeDtypeStruct((B, B_KV_HEADS, T, LANES), BF16),
        jax.ShapeDtypeStruct((B, B_KV_HEADS, T, LANES), BF16),
        jax.ShapeDtypeStruct((B, T // NSA_CK, B_KV_HEADS * NSA_VROWS, NSA_CK), BF16),
        jax.ShapeDtypeStruct((B, T // NSA_CK, B_KV_HEADS * NSA_VROWS, NSA_CK), BF16),
        jax.ShapeDtypeStruct((B, T, 2 * KV_WIDTH), F32),
        jax.ShapeDtypeStruct((B, nt, GATE_ROWS, tm), F32),
    )
    frame = pl.BlockSpec((1, B_KV_HEADS, tm, LANES), lambda b, t: (b, 0, t, 0))
    vt = pl.BlockSpec((1, tm // NSA_CK, B_KV_HEADS * NSA_VROWS, NSA_CK), lambda b, t: (b, t, 0, 0))
    qframe = pl.BlockSpec((1, B_KV_HEADS, B_GROUP, tm, LANES), lambda b, t: (b, 0, 0, t, 0))
    out_specs = (tok(A_WIDTH), tok(A_WIDTH), tok(A_WIDTH), qframe, frame, frame, vt, vt,
                 tok(2 * KV_WIDTH), pl.BlockSpec((1, 1, GATE_ROWS, tm), lambda b, t: (b, t, 0, 0)))
    return pl.pallas_call(
        functools.partial(_in_proj_kernel, tm=tm),
        grid=(B, nt),
        in_specs=[tok(D), const((1, D)), const((D, nq)), const((D, 2 * KV_WIDTH)), const((nr, D)), tab, tab, tab],
        out_specs=out_specs,
        out_shape=out_shapes,
        compiler_params=pltpu.CompilerParams(dimension_semantics=("parallel", "parallel"),
                                             vmem_limit_bytes=VMEM_LIMIT),
        name="in_proj",
    )(x, norm_g.reshape(1, D), wq, wkk, wt, cos, sa, sb)


def _compress_kernel(ak_ref, av_ref, w1k_ref, w1v_ref, pek_ref, pev_ref, w2k_ref, w2vt_ref, cos_ref, sa_ref, sb_ref,
                     kc_ref, vct_ref, *, nc_pad):
    half = CMP_LEN // 2
    hid_w = B_KV_HEADS * CMP_HIDDEN

    zeros = jnp.zeros((HEAD_DIM, CMP_HIDDEN), BF16)

    def both_groups(w):
        return jnp.concatenate([jnp.concatenate([w, zeros], axis=1), jnp.concatenate([zeros, w], axis=1)], axis=0)

    def hidden(a_ref, w1_ref, pe_ref):
        acc_u = jnp.zeros((nc_pad, hid_w), F32)
        acc_v = jnp.zeros((nc_pad, hid_w), F32)
        for p in range(half):
            ap = a_ref[0, pl.ds(p, nc_pad, stride=CMP_STRIDE), :]
            acc_u = acc_u + _dot((ap + pe_ref[p:p + 1, :]).astype(BF16), both_groups(w1_ref[p]))
            acc_v = acc_v + _dot((ap + pe_ref[half + p:half + p + 1, :]).astype(BF16), both_groups(w1_ref[half + p]))
        return jax.nn.gelu(acc_u + pltpu.roll(acc_v, nc_pad - 1, 0))

    hk = hidden(ak_ref, w1k_ref, pek_ref).astype(BF16)
    hv = hidden(av_ref, w1v_ref, pev_ref).astype(BF16)
    for g in range(B_KV_HEADS):
        hg = hk[:, g * CMP_HIDDEN:(g + 1) * CMP_HIDDEN]
        kc = _dot(hg, w2k_ref[...])
        kc_ref[0, g] = _rope_rows(kc, cos_ref[...], sa_ref[...], sb_ref[...]).astype(BF16)
        vg = hv[:, g * CMP_HIDDEN:(g + 1) * CMP_HIDDEN]
        vct_ref[0, g] = _dot_nt(w2vt_ref[...], vg).astype(BF16)


def _compress(kcvc, pe_k, w1_k, w2_k, pe_v, w1_v, w2_v, cmp_tables):
    B, T, _ = kcvc.shape
    nc_pad = T // CMP_STRIDE
    w2k = jnp.concatenate([w2_k, jnp.zeros_like(w2_k)], axis=1).astype(BF16)
    w2vt = w2_v.T.astype(BF16)
    pek = jnp.tile(pe_k, (1, B_KV_HEADS))
    pev = jnp.tile(pe_v, (1, B_KV_HEADS))
    per_pos = lambda w1: w1.reshape(CMP_LEN, HEAD_DIM, CMP_HIDDEN).astype(BF16)
    const = lambda shape: pl.BlockSpec(shape, lambda b: (0,) * len(shape))
    cos, sa, sb = cmp_tables
    return pl.pallas_call(
        functools.partial(_compress_kernel, nc_pad=nc_pad),
        grid=(B,),
        in_specs=[pl.BlockSpec((1, T, KV_WIDTH), lambda b: (b, 0, 0)), pl.BlockSpec((1, T, KV_WIDTH), lambda b: (b, 0, 1)),
                  const((CMP_LEN, HEAD_DIM, CMP_HIDDEN)), const((CMP_LEN, HEAD_DIM, CMP_HIDDEN)),
                  const((CMP_LEN, LANES)), const((CMP_LEN, LANES)),
                  const((CMP_HIDDEN, LANES)), const((HEAD_DIM, CMP_HIDDEN)),
                  const((nc_pad, LANES)), const((nc_pad, LANES)), const((nc_pad, LANES))],
        out_specs=(pl.BlockSpec((1, B_KV_HEADS, nc_pad, LANES), lambda b: (b, 0, 0, 0)),
                   pl.BlockSpec((1, B_KV_HEADS, HEAD_DIM, nc_pad), lambda b: (b, 0, 0, 0))),
        out_shape=(jax.ShapeDtypeStruct((B, B_KV_HEADS, nc_pad, LANES), BF16),
                   jax.ShapeDtypeStruct((B, B_KV_HEADS, HEAD_DIM, nc_pad), BF16)),
        compiler_params=pltpu.CompilerParams(dimension_semantics=("parallel",), vmem_limit_bytes=VMEM_LIMIT),
        name="compress",
    )(kcvc, kcvc, per_pos(w1_k), per_pos(w1_v), pek, pev, w2k, w2vt, cos, sa, sb)


def _mixer_a_kernel(q_ref, k_ref, v_ref, bias_ref, o_ref, qd0, qd1, kd, vd0, vd1, u_s, m_s, l_s, *, seq):
    blk = A_BLOCK
    nres = seq // blk
    npat = len(A_PATTERNS)
    order = sorted(range(npat), key=lambda p: -A_PATTERNS[p][1])
    slot = {p: n for n, p in enumerate(order[:-1])}
    lane = lax.broadcasted_iota(jnp.int32, (blk, LANES), 1)
    lo = lane < HEAD_DIM

    def deinterleave(r):
        rows = pl.ds(r * blk, blk)
        q, v = q_ref[0, pl.ds(r, blk, stride=nres), :], v_ref[0, pl.ds(r, blk, stride=nres), :]
        qd0[rows, :] = jnp.where(lo, q, 0.0)
        qd1[rows, :] = jnp.where(lo, 0.0, q)
        kd[rows, :] = k_ref[0, pl.ds(r, blk, stride=nres), :]
        vd0[rows, :] = jnp.where(lo, v, 1.0)
        vd1[rows, :] = jnp.where(lo, 1.0, v)

    def pieces(dil, rd, row_off, rows):
        return [pl.ds((rd + dil * jj) * blk + row_off, rows) for jj in range(nres // dil)]

    def gather(ref, idx, lead=()):
        parts = [ref[lead + (i, slice(None))] for i in idx]
        return parts[0] if len(parts) == 1 else jnp.concatenate(parts, axis=0)

    def attend(pi, blocks):
        dil = A_PATTERNS[pi][1]
        pr = blk // (nres // dil)
        q_idxs, vbs, scores = [], [], []
        for rd, n, first in blocks:
            q_idx = pieces(dil, rd, n * pr, pr)
            k_idx = q_idx if first else pieces(dil, rd, (n - 1) * pr, 2 * pr)
            bias = bias_ref[pi, :, 0:blk] if first else bias_ref[pi, :, blk:3 * blk]
            kb = gather(kd, k_idx).astype(BF16)
            q_idxs.append(q_idx)
            for qd, vd in ((qd0, vd0), (qd1, vd1)):
                vbs.append(gather(vd, k_idx).astype(BF16))
                scores.append(_dot_nt(gather(qd, q_idx).astype(BF16), kb) + bias)
        es, ms = [], []
        for s in scores:
            m = jnp.max(s, axis=-1, keepdims=True)
            ms.append(m)
            es.append(jnp.exp2((s - m).astype(BF16)))
        pvs = [_dot(e, vb) for e, vb in zip(es, vbs)]
        for b, q_idx in enumerate(q_idxs):
            u = jnp.where(lo, pvs[2 * b], pvs[2 * b + 1])
            l_swapped = jnp.where(lo, pvs[2 * b + 1], pvs[2 * b])
            m = jnp.where(lo, ms[2 * b], ms[2 * b + 1])
            if pi != order[-1]:
                for jj, idx in enumerate(q_idx):
                    u_s[slot[pi], idx, :] = u[jj * pr:(jj + 1) * pr]
                    m_s[slot[pi], idx, :] = m[jj * pr:(jj + 1) * pr]
                    l_s[slot[pi], idx, :] = l_swapped[jj * pr:(jj + 1) * pr]
                continue
            parts = [(u, m, l_swapped)] + [(gather(u_s, q_idx, (sl,)), gather(m_s, q_idx, (sl,)),
                                            gather(l_s, q_idx, (sl,))) for sl in slot.values()]
            m_all = functools.reduce(jnp.maximum, [mm for _, mm, _ in parts])
            num = jnp.zeros((blk, LANES), F32)
            den = jnp.zeros((blk, LANES), F32)
            for uu, mm, ll in parts:
                a = jnp.exp2(mm - m_all)
                num = num + a * uu
                den = den + a * pltpu.roll(ll, HEAD_DIM, 1)
            out = num / den
            n = blocks[b][1]
            for jj in range(nres):
                o_ref[0, pl.ds(n * blk + jj, pr, stride=nres), :] = out[jj * pr:(jj + 1) * pr]

    for pi in order:
        dil = A_PATTERNS[pi][1]
        nb = seq // dil // blk
        blocks = [(rd, n, n == 0) for rd in range(dil) for n in range(nb)]
        for g0 in range(0, len(blocks), A_GROUP):
            if pi == order[0]:
                for rd, _, _ in blocks[g0:g0 + A_GROUP]:
                    deinterleave(rd)
            attend(pi, blocks[g0:g0 + A_GROUP])


def _mixer_a_bias(nres):
    blk = A_BLOCK

    def sub_pos(i, fold, rows):
        return fold * (i % rows) + i // rows

    out = np.zeros((len(A_PATTERNS), blk, 3 * blk), np.float32)
    for pi, (window, dil) in enumerate(A_PATTERNS):
        n_back = window // dil
        fold = nres // dil
        pr = blk // fold
        sq = sub_pos(np.arange(blk), fold, pr)[:, None]
        d_first = sq - sub_pos(np.arange(blk), fold, pr)[None, :]
        d_band = sq + blk - sub_pos(np.arange(2 * blk), fold, 2 * pr)[None, :]
        dist = np.concatenate([d_first, d_band], axis=1)
        out[pi] = np.where((dist >= 0) & (dist <= n_back), 0.0, NEG)
    return jnp.asarray(out)


def _mixer_a(qa, ka, va):
    B, T, W = qa.shape
    npair = W // LANES
    spec = pl.BlockSpec((1, T, LANES), lambda b, p: (b, 0, p))
    npat = len(A_PATTERNS)
    nres = T // A_BLOCK
    dils = sorted(d for _, d in A_PATTERNS)
    assert all(nres % d == 0 and T % (d * A_BLOCK) == 0 for d in dils) and dils[0] == 1 and dils[-1] == nres
    return pl.pallas_call(
        functools.partial(_mixer_a_kernel, seq=T),
        grid=(B, npair),
        in_specs=[spec, spec, spec, pl.BlockSpec((npat, A_BLOCK, 3 * A_BLOCK), lambda b, p: (0, 0, 0))],
        out_specs=spec,
        out_shape=jax.ShapeDtypeStruct((B, T, W), F32),
        scratch_shapes=[pltpu.VMEM((T, LANES), F32)] * 5 + [pltpu.VMEM((npat - 1, T, LANES), F32)] * 3,
        compiler_params=pltpu.CompilerParams(dimension_semantics=("parallel", "parallel"),
                                             vmem_limit_bytes=VMEM_LIMIT),
        name="mixer_a",
    )(qa, ka, va, _mixer_a_bias(nres))


def _nsa_kernel(q_ref, ks_ref, kw_ref, vst_ref, vwt_ref, kc_ref, vct_ref, gt_ref, ovt_ref, bias_ref, cbias_ref, o_ref,
                *, n_sel_blocks, nq):
    tq, ck = NSA_TQ, NSA_CK
    nrow = B_GROUP * tq
    gw = B_GROUP * HEAD_DIM
    step = pl.program_id(1)
    per_part = nq // NSA_SPLIT
    lanes = [(h, g) for h in range(NSA_SPLIT) for g in range(B_KV_HEADS)]
    qi = [step + h * per_part for h in range(NSA_SPLIT)]
    col_i = lax.broadcasted_iota(jnp.int32, (1, nrow), 1) & (tq - 1)

    def chunk(ref, g, c):
        return ref[0, g, pl.ds(pl.multiple_of(c * ck, ck), ck), :]

    def vt_chunk(ref, g, c):
        return ref[0, c, g * NSA_VROWS:(g + 1) * NSA_VROWS, :]

    def softmax_pv(s, vt):
        m = jnp.max(s, axis=0, keepdims=True)
        return m, _dot(vt, jnp.exp2(s - m).astype(BF16))

    def merged(parts):
        m_new = functools.reduce(jnp.maximum, [m for m, _ in parts])
        acc = sum(jnp.exp2(m - m_new) * a for m, a in parts)
        return m_new, acc

    def normalised(state):
        return state[1][0:HEAD_DIM] * (1.0 / state[1][HEAD_DIM:HEAD_DIM + 1])

    q_b = {(h, g): jnp.concatenate([q_ref[0, g, r, h] for r in range(B_GROUP)], axis=0) for h, g in lanes}

    nwin = WIN // ck
    win_chunks = [[jnp.maximum(qi[h] - back, 0) for back in range(nwin, -1, -1)] for h in range(NSA_SPLIT)]
    s_win = {}
    for h, g in lanes:
        s = _dot_nt(jnp.concatenate([chunk(kw_ref, g, c) for c in win_chunks[h]], axis=0), q_b[h, g])
        pieces = [jnp.where(qi[h] >= nwin, s[0:ck] + bias_ref[1], NEG)]
        for n in range(1, nwin):
            pieces.append(jnp.where(qi[h] >= nwin - n, s[n * ck:(n + 1) * ck], NEG))
        pieces.append(s[nwin * ck:] + bias_ref[0])
        s_win[h, g] = pieces

    s_cmp = {(h, g): _dot_nt(kc_ref[0, g], q_b[h, g]) + cbias_ref[h, 0] for h, g in lanes}
    p_cmp, o_cmp = {}, {}
    for h, g in lanes:
        m = jnp.max(s_cmp[h, g], axis=0, keepdims=True)
        e = jnp.exp2(s_cmp[h, g] - m)
        den = jnp.sum(e, axis=0, keepdims=True)
        sees_block = qi[h] * tq + col_i >= CMP_LEN - 1
        p_cmp[h, g] = e * jnp.where(sees_block, 1.0 / jnp.maximum(den, 1e-30), 0.0)
        o_cmp[h, g] = _dot(vct_ref[0, g], p_cmp[h, g].astype(BF16))

    imp = {}
    for h, g in lanes:
        psum = p_cmp[h, g][:, 0:tq]
        for r in range(1, B_GROUP):
            psum = psum + p_cmp[h, g][:, r * tq:(r + 1) * tq]
        p_hi = psum.astype(BF16)
        p_lo = (psum - p_hi.astype(F32)).astype(BF16)
        imp[h, g] = _dot(ovt_ref[...], p_hi) + _dot(ovt_ref[...], p_lo)
    j = lax.broadcasted_iota(jnp.int32, (n_sel_blocks, tq), 0)
    j_f = j.astype(F32)
    low = -3e38
    q_aug = {}
    for h, g in lanes:
        cur = (qi[h] * tq + lax.broadcasted_iota(jnp.int32, (n_sel_blocks, tq), 1)) >> SEL_SHIFT
        forced = (j == 0) | (j == cur) | (j == cur - 1)
        score = jnp.where(forced, imp[h, g] + 2.0, jnp.where(j > cur, -1.0, imp[h, g]))
        sel = jnp.zeros((n_sel_blocks, tq), jnp.bool_)
        for _ in range(min(SEL_TOPK, n_sel_blocks)):
            mx = jnp.max(score, axis=0, keepdims=True)
            first = jnp.min(jnp.where(score == mx, j_f, 4.0 * LANES), axis=0, keepdims=True)
            hit = j_f == first
            sel = sel | hit
            score = jnp.where(hit, low, score)
        selneg = jnp.concatenate([jnp.zeros((SEL_LANE0, tq), F32), jnp.where(sel, 0.0, NEG),
                                  jnp.zeros((LANES - SEL_LANE0 - n_sel_blocks, tq), F32)], axis=0).T
        q_aug[h, g] = q_b[h, g] + jnp.concatenate([selneg.astype(BF16)] * B_GROUP, axis=0)

    s_diag = {(h, g): _dot_nt(chunk(ks_ref, g, qi[h]), q_aug[h, g]) + bias_ref[0] for h, g in lanes}
    o_win = {(h, g): normalised(merged([softmax_pv(s_win[h, g][n], vt_chunk(vwt_ref, g, c))
                                        for n, c in enumerate(win_chunks[h])])) for h, g in lanes}
    states = tuple(softmax_pv(s_diag[h, g], vt_chunk(vst_ref, g, qi[h])) for h, g in lanes)

    def earlier(work, st):
        s = [_dot_nt(chunk(ks_ref, lanes[n][1], c), q_aug[lanes[n]]) for n, c in work]
        parts = [[st[n]] for n in range(len(lanes))]
        for (n, c), sc in zip(work, s):
            parts[n].append(softmax_pv(sc, vt_chunk(vst_ref, lanes[n][1], c)))
        return tuple(merged(p) if len(p) > 1 else p[0] for p in parts)

    every = range(len(lanes))
    states = lax.fori_loop(0, step // 2, lambda i, st: earlier([(n, 2 * i + k) for n in every for k in range(2)], st),
                           states)
    states = lax.cond((step & 1) == 1, lambda st: earlier([(n, step - 1) for n in every], st), lambda st: st, states)
    for h in range(1, NSA_SPLIT):
        mine = [n for n, (hh, _) in enumerate(lanes) if hh >= h]
        pairs = per_part // 2
        states = lax.fori_loop(
            0, pairs,
            lambda i, st, mine=mine, h=h: earlier(
                [(n, step + (h - 1) * per_part + 2 * i + k) for n in mine for k in range(2)], st), states)

    for n, (h, g) in enumerate(lanes):
        o_sel = normalised(states[n])
        gt = gt_ref[0, h, g * B_GROUP * N_BRANCH:(g + 1) * B_GROUP * N_BRANCH, :]
        outs = []
        for r in range(B_GROUP):
            cs = slice(r * tq, (r + 1) * tq)
            outs.append(gt[r * N_BRANCH:r * N_BRANCH + 1, :] * o_cmp[h, g][:, cs]
                        + gt[r * N_BRANCH + 1:r * N_BRANCH + 2, :] * o_sel[:, cs]
                        + gt[r * N_BRANCH + 2:r * N_BRANCH + 3, :] * o_win[h, g][:, cs])
        for c in range(B_GROUP // 2):
            pair = jnp.concatenate([outs[2 * c], outs[2 * c + 1]], axis=0)
            o_ref[0, h, :, g * gw + c * LANES:g * gw + (c + 1) * LANES] = pair.T


def _overlap_t(nc_pad, ns):
    nc = nc_pad - 1
    c0 = np.arange(nc_pad) * CMP_STRIDE
    s0 = np.arange(ns) * SEL_BLOCK
    ov = np.minimum(c0[None, :] + CMP_LEN, s0[:, None] + SEL_BLOCK) - np.maximum(c0[None, :], s0[:, None])
    ov = np.clip(ov, 0, None).astype(np.float32) / CMP_LEN
    ov = np.where(np.arange(nc_pad)[None, :] < nc, ov, 0.0)
    return jnp.asarray(ov, BF16)


def _nsa(qb, ks, kw, vst, vwt, kc, vct, gt):
    B, _, _, T, _ = qb.shape
    W = B_WIDTH
    nq = T // NSA_TQ
    ns = T // SEL_BLOCK
    nc_pad = kc.shape[2]
    nrow = B_GROUP * NSA_TQ
    part = T // NSA_SPLIT
    assert NSA_TQ == NSA_CK and WIN % NSA_CK == 0 and SEL_LANE0 + ns <= LANES and (nq // NSA_SPLIT) % 2 == 0
    assert gt.shape == (B, NSA_SPLIT, GATE_ROWS, part)
    kspec = pl.BlockSpec((1, B_KV_HEADS, T, LANES), lambda b, i: (b, 0, 0, 0))
    vspec = pl.BlockSpec((1, T // NSA_CK, B_KV_HEADS * NSA_VROWS, NSA_CK), lambda b, i: (b, 0, 0, 0))
    key = np.arange(NSA_CK)[:, None]
    qry = (np.arange(nrow) % NSA_TQ)[None, :]
    bias = jnp.asarray(np.stack([np.where(key <= qry, 0.0, NEG), np.where(key > qry, 0.0, NEG)]).astype(np.float32))
    cmp_end = (np.arange(nc_pad) * CMP_STRIDE + CMP_LEN - 1)[None, :, None]
    t_query = (np.arange(nq) * NSA_TQ)[:, None, None] + qry[None]
    cbias = jnp.asarray(np.where((cmp_end <= t_query) & (np.arange(nc_pad) < nc_pad - 1)[None, :, None], 0.0, NEG)
                        .astype(np.float32)).reshape(NSA_SPLIT, nq // NSA_SPLIT, nc_pad, nrow)
    out = pl.pallas_call(
        functools.partial(_nsa_kernel, n_sel_blocks=ns, nq=nq),
        grid=(B, nq // NSA_SPLIT),
        in_specs=[pl.BlockSpec((1, B_KV_HEADS, B_GROUP, NSA_SPLIT, NSA_TQ, LANES), lambda b, i: (b, 0, 0, 0, i, 0)),
                  kspec, kspec, vspec, vspec,
                  pl.BlockSpec((1, B_KV_HEADS, nc_pad, LANES), lambda b, i: (b, 0, 0, 0)),
                  pl.BlockSpec((1, B_KV_HEADS, HEAD_DIM, nc_pad), lambda b, i: (b, 0, 0, 0)),
                  pl.BlockSpec((1, NSA_SPLIT, GATE_ROWS, NSA_TQ), lambda b, i: (b, 0, 0, i)),
                  pl.BlockSpec((ns, nc_pad), lambda b, i: (0, 0)),
                  pl.BlockSpec((2, NSA_CK, nrow), lambda b, i: (0, 0, 0)),
                  pl.BlockSpec((NSA_SPLIT, 1, nc_pad, nrow), lambda b, i: (0, i, 0, 0))],
        out_specs=pl.BlockSpec((1, NSA_SPLIT, NSA_TQ, W), lambda b, i: (b, 0, i, 0)),
        out_shape=jax.ShapeDtypeStruct((B, NSA_SPLIT, part, W), F32),
        compiler_params=pltpu.CompilerParams(dimension_semantics=("parallel", "arbitrary"),
                                             vmem_limit_bytes=VMEM_LIMIT),
        name="nsa",
    )(qb.reshape(B, B_KV_HEADS, B_GROUP, NSA_SPLIT, part, LANES), ks, kw, vst, vwt, kc, vct,
      gt, _overlap_t(nc_pad, ns), bias, cbias)
    return out.reshape(B, T, W)


def _post_kernel(x_ref, oa_ref, ob_ref, ga_ref, gb_ref, wo_ref, gm_ref, wu_ref, wd_ref, gf_ref, o_ref, *, final):
    def norm(v, g):
        return v * lax.rsqrt(jnp.mean(v * v, axis=-1, keepdims=True) + EPS) * g

    na = norm(oa_ref[...], ga_ref[...]).astype(BF16)
    nb = norm(ob_ref[...], gb_ref[...]).astype(BF16)
    aw = na.shape[1]
    h_res = x_ref[...] + _dot(na, wo_ref[0:aw, :]) + _dot(nb, wo_ref[aw:, :])
    h = norm(h_res, gm_ref[...]).astype(BF16)
    u = jnp.square(jnp.maximum(_dot(h, wu_ref[...]), 0.0)).astype(BF16)
    acc = h_res + _dot(u, wd_ref[...])
    o_ref[...] = norm(acc, gf_ref[...]) if final else acc


def _post(x, oa, ob, g_a, g_b, w_out, g_mlp, w_up, w_down, g_final, *, final, tm=512):
    B, T, D = x.shape
    n = B * T
    dff = w_up.shape[1]
    tok = lambda w: pl.BlockSpec((tm, w), lambda i: (i, 0))
    const = lambda shape: pl.BlockSpec(shape, lambda i: (0, 0), pipeline_mode=pl.Buffered(1))
    out = pl.pallas_call(
        functools.partial(_post_kernel, final=final),
        grid=(n // tm,),
        in_specs=[tok(D), tok(A_WIDTH), tok(B_WIDTH), const((1, A_WIDTH)), const((1, B_WIDTH)),
                  const((A_WIDTH + B_WIDTH, D)), const((1, D)), const((D, dff)), const((dff, D)), const((1, D))],
        out_specs=tok(D),
        out_shape=jax.ShapeDtypeStruct((n, D), F32),
        compiler_params=pltpu.CompilerParams(dimension_semantics=("parallel",), vmem_limit_bytes=VMEM_LIMIT),
        name="post",
    )(x.reshape(n, D), oa.reshape(n, A_WIDTH), ob.reshape(n, B_WIDTH), g_a.reshape(1, -1), g_b.reshape(1, -1),
      w_out.astype(BF16), g_mlp.reshape(1, D), w_up.astype(BF16), w_down.astype(BF16), g_final.reshape(1, D))
    return out.reshape(B, T, D)


def kernel(x, norm_mix, w_in, cmp_pe_k, cmp_w1_k, cmp_w2_k, cmp_pe_v, cmp_w1_v, cmp_w2_v, g_out_a, g_out_b,
           w_out, norm_mlp, w_up, w_down, norm_final):
    B, T, D = x.shape
    depth = w_in.shape[0]
    tables = _rope_tables(np.arange(T))
    cmp_tables = _rope_tables(np.arange(T // CMP_STRIDE) * CMP_STRIDE + CMP_LEN - 1)
    h_res = x
    for l in range(depth):
        qa, ka, va, qb, ks, kw, vst, vwt, kcvc, gt = _in_proj(h_res, norm_mix[l], w_in[l], tables)
        kc, vct = _compress(kcvc, cmp_pe_k[l], cmp_w1_k[l], cmp_w2_k[l], cmp_pe_v[l], cmp_w1_v[l], cmp_w2_v[l],
                            cmp_tables)
        oa = _mixer_a(qa, ka, va)
        ob = _nsa(qb, ks, kw, vst, vwt, kc, vct, gt)
        h_res = _post(h_res, oa, ob, g_out_a[l], g_out_b[l], w_out[l], norm_mlp[l], w_up[l], w_down[l], norm_final,
                      final=(l == depth - 1))
    return h_res
```

```python
import functools

import jax
import jax.numpy as jnp
import numpy as np
from jax import lax
from jax.experimental import pallas as pl
from jax.experimental.pallas import tpu as pltpu

F32 = jnp.float32
BF16 = jnp.bfloat16

HEAD_DIM = 64
ROT_DIM = HEAD_DIM // 4
ROPE_THETA = 500000.0
EPS = 1e-6
NEG = -1e30
Q_SCALE = HEAD_DIM ** -0.5 * 1.4426950408889634
LANES = 128

A_HEADS = 8
A_PATTERNS = ((128, 1), (512, 4), (2048, 16))
A_BLOCK = 128
A_GROUP = 16

B_HEADS = 8
B_KV_HEADS = 2
B_GROUP = B_HEADS // B_KV_HEADS
CMP_LEN = 32
CMP_STRIDE = 16
CMP_HIDDEN = 256
SEL_BLOCK = 64
SEL_SHIFT = 6
SEL_TOPK = 8
WIN = 512
N_BRANCH = 3

A_WIDTH = A_HEADS * HEAD_DIM
B_WIDTH = B_HEADS * HEAD_DIM
KV_WIDTH = B_KV_HEADS * HEAD_DIM

NSA_TQ = 256
NSA_CK = 256
NSA_SPLIT = 2
NSA_VROWS = 80
GATE_ROWS = 32
SEL_LANE0 = HEAD_DIM

N_LATER = 3

VMEM_LIMIT = 56 * 1024 * 1024


def _dot(a, b):
    return jnp.dot(a, b, preferred_element_type=F32)


def _dot_nt(a, b):
    return lax.dot_general(a, b, (((1,), (1,)), ((), ())), preferred_element_type=F32)


def _rope_rows(y, cos, sin_a, sin_b):
    outs = []
    for c in range(y.shape[1] // LANES):
        yc = y[:, c * LANES:(c + 1) * LANES]
        outs.append(yc * cos + pltpu.roll(yc, LANES - ROT_DIM // 2, 1) * sin_a
                    + pltpu.roll(yc, ROT_DIM // 2, 1) * sin_b)
    return outs[0] if len(outs) == 1 else jnp.concatenate(outs, axis=1)


def _in_proj_kernel(x_ref, g_ref, wq_ref, wkk_ref, wt_ref, cos_ref, sa_ref, sb_ref, *refs, tm):
    (later_f32, (qa_ref, ka_ref, va_ref, qb_ref, ks_ref, kw_ref, vst_ref, vwt_ref, kcvc_ref, gt_ref), later_bf16) = (
        refs[:N_LATER], refs[N_LATER:-N_LATER], refs[-N_LATER:])
    for src, dst in zip(later_f32, later_bf16):
        dst[...] = src[...].astype(BF16)
    tt = pl.program_id(1)
    x = x_ref[...]
    ms = jnp.mean(x * x, axis=-1, keepdims=True)
    h = (x * lax.rsqrt(ms + EPS) * g_ref[...]).astype(BF16)
    cos, sa, sb = cos_ref[...], sa_ref[...], sb_ref[...]
    scale = Q_SCALE

    def proj(c0, c1):
        return _dot(h, wq_ref[:, c0:c1])

    o = 0
    qa_ref[...] = _rope_rows(proj(o, o + A_WIDTH), cos, sa, sb) * scale
    o += A_WIDTH
    ka_ref[...] = _rope_rows(proj(o, o + A_WIDTH), cos, sa, sb)
    o += A_WIDTH
    va_ref[...] = proj(o, o + A_WIDTH)
    o += A_WIDTH
    lane = lax.broadcasted_iota(jnp.int32, (tm, LANES), 1)
    row = lax.broadcasted_iota(jnp.int32, (tm, LANES), 0)
    lo = lane < HEAD_DIM
    qb = _rope_rows(proj(o, o + B_WIDTH), cos, sa, sb) * scale
    for hh in range(B_HEADS):
        ch = qb[:, (hh // 2) * LANES:(hh // 2 + 1) * LANES]
        if hh % 2:
            ch = pltpu.roll(ch, HEAD_DIM, 1)
        qb_ref[0, hh // B_GROUP, hh % B_GROUP] = jnp.where(lo, ch, 0.0).astype(BF16)
    o += B_WIDTH
    kcvc_ref[...] = proj(o, o + 2 * KV_WIDTH)
    o += 2 * KV_WIDTH
    ksw = _rope_rows(_dot(h, wkk_ref[...]), cos, sa, sb)

    blk = (tt * tm + row) >> SEL_SHIFT
    onehot = jnp.where(lane - SEL_LANE0 == blk, 1.0, 0.0)
    for kind, ref in ((0, ks_ref), (1, kw_ref)):
        kk = ksw[:, kind * LANES:(kind + 1) * LANES]
        tail = onehot if kind == 0 else 0.0
        ref[0, 0] = jnp.where(lo, kk, tail).astype(BF16)
        ref[0, 1] = jnp.where(lo, pltpu.roll(kk, HEAD_DIM, 1), tail).astype(BF16)

    tr = _dot_nt(wt_ref[...], h)
    ones_row = jnp.where(lax.broadcasted_iota(jnp.int32, (NSA_VROWS - HEAD_DIM, tm), 0) == 0, 1.0, 0.0)
    for kind, ref in ((0, vst_ref), (1, vwt_ref)):
        rows = [tr[kind * LANES + gg * HEAD_DIM:kind * LANES + (gg + 1) * HEAD_DIM] for gg in range(B_KV_HEADS)]
        slab = jnp.concatenate([rows[0], ones_row, rows[1], ones_row], axis=0).astype(BF16)
        for c in range(tm // NSA_CK):
            ref[0, c] = slab[:, c * NSA_CK:(c + 1) * NSA_CK]
    gt_ref[0, 0] = jax.nn.sigmoid(tr[2 * LANES:2 * LANES + GATE_ROWS, :])


def _rope_tables(pos):
    half = ROT_DIM // 2
    inv = ROPE_THETA ** (-np.arange(0, ROT_DIM, 2, dtype=np.float64) / ROT_DIM)
    ang = np.asarray(pos, np.float64)[:, None] * inv[None, :]
    cos, sin = np.cos(ang), np.sin(ang)
    n = len(pos)
    ones = np.ones((n, HEAD_DIM - ROT_DIM))
    zeros = np.zeros((n, HEAD_DIM - ROT_DIM))
    zh = np.zeros((n, half))
    c_head = np.concatenate([cos, cos, ones], axis=1)
    a_head = np.concatenate([-sin, zh, zeros], axis=1)
    b_head = np.concatenate([zh, sin, zeros], axis=1)
    rep = LANES // HEAD_DIM
    return tuple(jnp.asarray(np.tile(t, (1, rep)).astype(np.float32)) for t in (c_head, a_head, b_head))


def _in_proj(x, norm_g, w_in, tables, later, *, tm=1024):
    B, T, D = x.shape
    nt = T // tm
    assert len(later) == N_LATER and all(w.shape[0] % (16 * B * nt) == 0 for w in later)
    slab = lambda w: pl.BlockSpec((w.shape[0] // (B * nt), w.shape[1]), lambda b, t: (b * nt + t, 0))
    offs = [0]
    for n in (A_WIDTH, A_WIDTH, A_WIDTH, B_WIDTH, KV_WIDTH, KV_WIDTH, KV_WIDTH, KV_WIDTH, KV_WIDTH, KV_WIDTH,
              B_HEADS * N_BRANCH):
        offs.append(offs[-1] + n)
    col = lambda i: w_in[:, offs[i]:offs[i + 1]]
    wq = w_in[:, :offs[6]].astype(BF16)
    wkk = jnp.concatenate([col(6), col(8)], axis=1).astype(BF16)
    gpad = jnp.zeros((D, GATE_ROWS - B_HEADS * N_BRANCH), w_in.dtype)
    wt = jnp.concatenate([col(7), col(9), col(10), gpad], axis=1).T.astype(BF16)
    cos, sa, sb = tables
    nq = wq.shape[1]
    nr = wt.shape[0]
    tok = lambda w: pl.BlockSpec((None, tm, w), lambda b, t: (b, t, 0))
    const = lambda shape: pl.BlockSpec(shape, lambda b, t: (0,) * len(shape))
    tab = pl.BlockSpec((tm, LANES), lambda b, t: (t, 0))
    out_shapes = (
        jax.ShapeDtypeStruct((B, T, A_WIDTH), F32),
        jax.ShapeDtypeStruct((B, T, A_WIDTH), F32),
        jax.ShapeDtypeStruct((B, T, A_WIDTH), F32),
        jax.ShapeDtypeStruct((B, B_KV_HEADS, B_GROUP, T, LANES), BF16),
        jax.ShapeDtypeStruct((B, B_KV_HEADS, T, LANES), BF16),
        jax.ShapeDtypeStruct((B, B_KV_HEADS, T, LANES), BF16),
        jax.ShapeDtypeStruct((B, T // NSA_CK, B_KV_HEADS * NSA_VROWS, NSA_CK), BF16),
        jax.ShapeDtypeStruct((B, T // NSA_CK, B_KV_HEADS * NSA_VROWS, NSA_CK), BF16),
        jax.ShapeDtypeStruct((B, T, 2 * KV_WIDTH), F32),
        jax.ShapeDtypeStruct((B, nt, GATE_ROWS, tm), F32),
    )
    frame = pl.BlockSpec((1, B_KV_HEADS, tm, LANES), lambda b, t: (b, 0, t, 0))
    vt = pl.BlockSpec((1, tm // NSA_CK, B_KV_HEADS * NSA_VROWS, NSA_CK), lambda b, t: (b, t, 0, 0))
    qframe = pl.BlockSpec((1, B_KV_HEADS, B_GROUP, tm, LANES), lambda b, t: (b, 0, 0, t, 0))
    out_specs = (tok(A_WIDTH), tok(A_WIDTH), tok(A_WIDTH), qframe, frame, frame, vt, vt,
                 tok(2 * KV_WIDTH), pl.BlockSpec((1, 1, GATE_ROWS, tm), lambda b, t: (b, t, 0, 0)))
    return pl.pallas_call(
        functools.partial(_in_proj_kernel, tm=tm),
        grid=(B, nt),
        in_specs=[tok(D), const((1, D)), const((D, nq)), const((D, 2 * KV_WIDTH)), const((nr, D)), tab, tab, tab]
        + [slab(w) for w in later],
        out_specs=out_specs + tuple(slab(w) for w in later),
        out_shape=out_shapes + tuple(jax.ShapeDtypeStruct(w.shape, BF16) for w in later),
        compiler_params=pltpu.CompilerParams(dimension_semantics=("parallel", "parallel"),
                                             vmem_limit_bytes=VMEM_LIMIT),
        name="in_proj",
    )(x, norm_g.reshape(1, D), wq, wkk, wt, cos, sa, sb, *later)


def _compress_kernel(ak_ref, av_ref, w1k_ref, w1v_ref, pek_ref, pev_ref, w2k_ref, w2vt_ref, cos_ref, sa_ref, sb_ref,
                     kc_ref, vct_ref, *, nc_pad):
    half = CMP_LEN // 2
    hid_w = B_KV_HEADS * CMP_HIDDEN

    zeros = jnp.zeros((HEAD_DIM, CMP_HIDDEN), BF16)

    def both_groups(w):
        return jnp.concatenate([jnp.concatenate([w, zeros], axis=1), jnp.concatenate([zeros, w], axis=1)], axis=0)

    def hidden(a_ref, w1_ref, pe_ref):
        acc_u = jnp.zeros((nc_pad, hid_w), F32)
        acc_v = jnp.zeros((nc_pad, hid_w), F32)
        for p in range(half):
            ap = a_ref[0, pl.ds(p, nc_pad, stride=CMP_STRIDE), :]
            acc_u = acc_u + _dot((ap + pe_ref[p:p + 1, :]).astype(BF16), both_groups(w1_ref[p]))
            acc_v = acc_v + _dot((ap + pe_ref[half + p:half + p + 1, :]).astype(BF16), both_groups(w1_ref[half + p]))
        return jax.nn.gelu(acc_u + pltpu.roll(acc_v, nc_pad - 1, 0))

    hk = hidden(ak_ref, w1k_ref, pek_ref).astype(BF16)
    hv = hidden(av_ref, w1v_ref, pev_ref).astype(BF16)
    for g in range(B_KV_HEADS):
        hg = hk[:, g * CMP_HIDDEN:(g + 1) * CMP_HIDDEN]
        kc = _dot(hg, w2k_ref[...])
        kc_ref[0, g] = _rope_rows(kc, cos_ref[...], sa_ref[...], sb_ref[...]).astype(BF16)
        vg = hv[:, g * CMP_HIDDEN:(g + 1) * CMP_HIDDEN]
        vct_ref[0, g] = _dot_nt(w2vt_ref[...], vg).astype(BF16)


def _compress(kcvc, pe_k, w1_k, w2_k, pe_v, w1_v, w2_v, cmp_tables):
    B, T, _ = kcvc.shape
    nc_pad = T // CMP_STRIDE
    w2k = jnp.concatenate([w2_k, jnp.zeros_like(w2_k)], axis=1).astype(BF16)
    w2vt = w2_v.T.astype(BF16)
    pek = jnp.tile(pe_k, (1, B_KV_HEADS))
    pev = jnp.tile(pe_v, (1, B_KV_HEADS))
    per_pos = lambda w1: w1.reshape(CMP_LEN, HEAD_DIM, CMP_HIDDEN).astype(BF16)
    const = lambda shape: pl.BlockSpec(shape, lambda b: (0,) * len(shape))
    cos, sa, sb = cmp_tables
    return pl.pallas_call(
        functools.partial(_compress_kernel, nc_pad=nc_pad),
        grid=(B,),
        in_specs=[pl.BlockSpec((1, T, KV_WIDTH), lambda b: (b, 0, 0)), pl.BlockSpec((1, T, KV_WIDTH), lambda b: (b, 0, 1)),
                  const((CMP_LEN, HEAD_DIM, CMP_HIDDEN)), const((CMP_LEN, HEAD_DIM, CMP_HIDDEN)),
                  const((CMP_LEN, LANES)), const((CMP_LEN, LANES)),
                  const((CMP_HIDDEN, LANES)), const((HEAD_DIM, CMP_HIDDEN)),
                  const((nc_pad, LANES)), const((nc_pad, LANES)), const((nc_pad, LANES))],
        out_specs=(pl.BlockSpec((1, B_KV_HEADS, nc_pad, LANES), lambda b: (b, 0, 0, 0)),
                   pl.BlockSpec((1, B_KV_HEADS, HEAD_DIM, nc_pad), lambda b: (b, 0, 0, 0))),
        out_shape=(jax.ShapeDtypeStruct((B, B_KV_HEADS, nc_pad, LANES), BF16),
                   jax.ShapeDtypeStruct((B, B_KV_HEADS, HEAD_DIM, nc_pad), BF16)),
        compiler_params=pltpu.CompilerParams(dimension_semantics=("parallel",), vmem_limit_bytes=VMEM_LIMIT),
        name="compress",
    )(kcvc, kcvc, per_pos(w1_k), per_pos(w1_v), pek, pev, w2k, w2vt, cos, sa, sb)


def _mixer_a_kernel(q_ref, k_ref, v_ref, bias_ref, o_ref, qd0, qd1, kd, vd0, vd1, u_s, m_s, l_s, *, seq):
    blk = A_BLOCK
    nres = seq // blk
    npat = len(A_PATTERNS)
    order = sorted(range(npat), key=lambda p: -A_PATTERNS[p][1])
    slot = {p: n for n, p in enumerate(order[:-1])}
    lane = lax.broadcasted_iota(jnp.int32, (blk, LANES), 1)
    lo = lane < HEAD_DIM

    def deinterleave(r):
        rows = pl.ds(r * blk, blk)
        q, v = q_ref[0, pl.ds(r, blk, stride=nres), :], v_ref[0, pl.ds(r, blk, stride=nres), :]
        qd0[rows, :] = jnp.where(lo, q, 0.0)
        qd1[rows, :] = jnp.where(lo, 0.0, q)
        kd[rows, :] = k_ref[0, pl.ds(r, blk, stride=nres), :]
        vd0[rows, :] = jnp.where(lo, v, 1.0)
        vd1[rows, :] = jnp.where(lo, 1.0, v)

    def pieces(dil, rd, row_off, rows):
        return [pl.ds((rd + dil * jj) * blk + row_off, rows) for jj in range(nres // dil)]

    def gather(ref, idx, lead=()):
        parts = [ref[lead + (i, slice(None))] for i in idx]
        return parts[0] if len(parts) == 1 else jnp.concatenate(parts, axis=0)

    def attend(pi, blocks):
        dil = A_PATTERNS[pi][1]
        pr = blk // (nres // dil)
        q_idxs, vbs, scores = [], [], []
        for rd, n, first in blocks:
            q_idx = pieces(dil, rd, n * pr, pr)
            k_idx = q_idx if first else pieces(dil, rd, (n - 1) * pr, 2 * pr)
            bias = bias_ref[pi, :, 0:blk] if first else bias_ref[pi, :, blk:3 * blk]
            kb = gather(kd, k_idx).astype(BF16)
            q_idxs.append(q_idx)
            for qd, vd in ((qd0, vd0), (qd1, vd1)):
                vbs.append(gather(vd, k_idx).astype(BF16))
                scores.append(_dot_nt(gather(qd, q_idx).astype(BF16), kb) + bias)
        es, ms = [], []
        for s in scores:
            m = jnp.max(s, axis=-1, keepdims=True)
            ms.append(m)
            es.append(jnp.exp2((s - m).astype(BF16)))
        pvs = [_dot(e, vb) for e, vb in zip(es, vbs)]
        for b, q_idx in enumerate(q_idxs):
            u = jnp.where(lo, pvs[2 * b], pvs[2 * b + 1])
            l_swapped = jnp.where(lo, pvs[2 * b + 1], pvs[2 * b])
            m = jnp.where(lo, ms[2 * b], ms[2 * b + 1])
            if pi != order[-1]:
                for jj, idx in enumerate(q_idx):
                    u_s[slot[pi], idx, :] = u[jj * pr:(jj + 1) * pr]
                    m_s[slot[pi], idx, :] = m[jj * pr:(jj + 1) * pr]
                    l_s[slot[pi], idx, :] = l_swapped[jj * pr:(jj + 1) * pr]
                continue
            parts = [(u, m, l_swapped)] + [(gather(u_s, q_idx, (sl,)), gather(m_s, q_idx, (sl,)),
                                            gather(l_s, q_idx, (sl,))) for sl in slot.values()]
            m_all = functools.reduce(jnp.maximum, [mm for _, mm, _ in parts])
            num = jnp.zeros((blk, LANES), F32)
            den = jnp.zeros((blk, LANES), F32)
            for uu, mm, ll in parts:
                a = jnp.exp2(mm - m_all)
                num = num + a * uu
                den = den + a * pltpu.roll(ll, HEAD_DIM, 1)
            out = num / den
            n = blocks[b][1]
            for jj in range(nres):
                o_ref[0, pl.ds(n * blk + jj, pr, stride=nres), :] = out[jj * pr:(jj + 1) * pr]

    for pi in order:
        dil = A_PATTERNS[pi][1]
        nb = seq // dil // blk
        blocks = [(rd, n, n == 0) for rd in range(dil) for n in range(nb)]
        for g0 in range(0, len(blocks), A_GROUP):
            if pi == order[0]:
                for rd, _, _ in blocks[g0:g0 + A_GROUP]:
                    deinterleave(rd)
            attend(pi, blocks[g0:g0 + A_GROUP])


def _mixer_a_bias(nres):
    blk = A_BLOCK

    def sub_pos(i, fold, rows):
        return fold * (i % rows) + i // rows

    out = np.zeros((len(A_PATTERNS), blk, 3 * blk), np.float32)
    for pi, (window, dil) in enumerate(A_PATTERNS):
        n_back = window // dil
        fold = nres // dil
        pr = blk // fold
        sq = sub_pos(np.arange(blk), fold, pr)[:, None]
        d_first = sq - sub_pos(np.arange(blk), fold, pr)[None, :]
        d_band = sq + blk - sub_pos(np.arange(2 * blk), fold, 2 * pr)[None, :]
        dist = np.concatenate([d_first, d_band], axis=1)
        out[pi] = np.where((dist >= 0) & (dist <= n_back), 0.0, NEG)
    return jnp.asarray(out)


def _mixer_a(qa, ka, va):
    B, T, W = qa.shape
    npair = W // LANES
    spec = pl.BlockSpec((1, T, LANES), lambda b, p: (b, 0, p))
    npat = len(A_PATTERNS)
    nres = T // A_BLOCK
    dils = sorted(d for _, d in A_PATTERNS)
    assert all(nres % d == 0 and T % (d * A_BLOCK) == 0 for d in dils) and dils[0] == 1 and dils[-1] == nres
    return pl.pallas_call(
        functools.partial(_mixer_a_kernel, seq=T),
        grid=(B, npair),
        in_specs=[spec, spec, spec, pl.BlockSpec((npat, A_BLOCK, 3 * A_BLOCK), lambda b, p: (0, 0, 0))],
        out_specs=spec,
        out_shape=jax.ShapeDtypeStruct((B, T, W), F32),
        scratch_shapes=[pltpu.VMEM((T, LANES), F32)] * 5 + [pltpu.VMEM((npat - 1, T, LANES), F32)] * 3,
        compiler_params=pltpu.CompilerParams(dimension_semantics=("parallel", "parallel"),
                                             vmem_limit_bytes=VMEM_LIMIT),
        name="mixer_a",
    )(qa, ka, va, _mixer_a_bias(nres))


def _nsa_kernel(q_ref, ks_ref, kw_ref, vst_ref, vwt_ref, kc_ref, vct_ref, gt_ref, ovt_ref, bias_ref, cbias_ref, o_ref,
                *, n_sel_blocks, nq):
    tq, ck = NSA_TQ, NSA_CK
    nrow = B_GROUP * tq
    gw = B_GROUP * HEAD_DIM
    step = pl.program_id(1)
    per_part = nq // NSA_SPLIT
    lanes = [(h, g) for h in range(NSA_SPLIT) for g in range(B_KV_HEADS)]
    qi = [step + h * per_part for h in range(NSA_SPLIT)]
    col_i = lax.broadcasted_iota(jnp.int32, (1, nrow), 1) & (tq - 1)

    def chunk(ref, g, c):
        return ref[0, g, pl.ds(pl.multiple_of(c * ck, ck), ck), :]

    def vt_chunk(ref, g, c):
        return ref[0, c, g * NSA_VROWS:(g + 1) * NSA_VROWS, :]

    def softmax_pv(s, vt):
        m = jnp.max(s, axis=0, keepdims=True)
        return m, _dot(vt, jnp.exp2(s - m).astype(BF16))

    def merged(parts):
        m_new = functools.reduce(jnp.maximum, [m for m, _ in parts])
        acc = sum(jnp.exp2(m - m_new) * a for m, a in parts)
        return m_new, acc

    def normalised(state):
        return state[1][0:HEAD_DIM] * (1.0 / state[1][HEAD_DIM:HEAD_DIM + 1])

    q_b = {(h, g): jnp.concatenate([q_ref[0, g, r, h] for r in range(B_GROUP)], axis=0) for h, g in lanes}

    nwin = WIN // ck
    win_chunks = [[jnp.maximum(qi[h] - back, 0) for back in range(nwin, -1, -1)] for h in range(NSA_SPLIT)]
    s_win = {}
    for h, g in lanes:
        s = _dot_nt(jnp.concatenate([chunk(kw_ref, g, c) for c in win_chunks[h]], axis=0), q_b[h, g])
        pieces = [jnp.where(qi[h] >= nwin, s[0:ck] + bias_ref[1], NEG)]
        for n in range(1, nwin):
            pieces.append(jnp.where(qi[h] >= nwin - n, s[n * ck:(n + 1) * ck], NEG))
        pieces.append(s[nwin * ck:] + bias_ref[0])
        s_win[h, g] = pieces

    s_cmp = {(h, g): _dot_nt(kc_ref[0, g], q_b[h, g]) + cbias_ref[h, 0] for h, g in lanes}
    p_cmp, o_cmp = {}, {}
    for h, g in lanes:
        m = jnp.max(s_cmp[h, g], axis=0, keepdims=True)
        e = jnp.exp2(s_cmp[h, g] - m)
        den = jnp.sum(e, axis=0, keepdims=True)
        sees_block = qi[h] * tq + col_i >= CMP_LEN - 1
        p_cmp[h, g] = e * jnp.where(sees_block, 1.0 / jnp.maximum(den, 1e-30), 0.0)
        o_cmp[h, g] = _dot(vct_ref[0, g], p_cmp[h, g].astype(BF16))

    imp = {}
    for h, g in lanes:
        psum = p_cmp[h, g][:, 0:tq]
        for r in range(1, B_GROUP):
            psum = psum + p_cmp[h, g][:, r * tq:(r + 1) * tq]
        p_hi = psum.astype(BF16)
        p_lo = (psum - p_hi.astype(F32)).astype(BF16)
        imp[h, g] = _dot(ovt_ref[...], p_hi) + _dot(ovt_ref[...], p_lo)
    j = lax.broadcasted_iota(jnp.int32, (n_sel_blocks, tq), 0)
    j_f = j.astype(F32)
    low = -3e38
    q_aug = {}
    for h, g in lanes:
        cur = (qi[h] * tq + lax.broadcasted_iota(jnp.int32, (n_sel_blocks, tq), 1)) >> SEL_SHIFT
        forced = (j == 0) | (j == cur) | (j == cur - 1)
        score = jnp.where(forced, imp[h, g] + 2.0, jnp.where(j > cur, -1.0, imp[h, g]))
        sel = jnp.zeros((n_sel_blocks, tq), jnp.bool_)
        for _ in range(min(SEL_TOPK, n_sel_blocks)):
            mx = jnp.max(score, axis=0, keepdims=True)
            first = jnp.min(jnp.where(score == mx, j_f, 4.0 * LANES), axis=0, keepdims=True)
            hit = j_f == first
            sel = sel | hit
            score = jnp.where(hit, low, score)
        selneg = jnp.concatenate([jnp.zeros((SEL_LANE0, tq), F32), jnp.where(sel, 0.0, NEG),
                                  jnp.zeros((LANES - SEL_LANE0 - n_sel_blocks, tq), F32)], axis=0).T
        q_aug[h, g] = q_b[h, g] + jnp.concatenate([selneg.astype(BF16)] * B_GROUP, axis=0)

    s_diag = {(h, g): _dot_nt(chunk(ks_ref, g, qi[h]), q_aug[h, g]) + bias_ref[0] for h, g in lanes}
    o_win = {(h, g): normalised(merged([softmax_pv(s_win[h, g][n], vt_chunk(vwt_ref, g, c))
                                        for n, c in enumerate(win_chunks[h])])) for h, g in lanes}
    states = tuple(softmax_pv(s_diag[h, g], vt_chunk(vst_ref, g, qi[h])) for h, g in lanes)

    def earlier(work, st):
        s = [_dot_nt(chunk(ks_ref, lanes[n][1], c), q_aug[lanes[n]]) for n, c in work]
        parts = [[st[n]] for n in range(len(lanes))]
        for (n, c), sc in zip(work, s):
            parts[n].append(softmax_pv(sc, vt_chunk(vst_ref, lanes[n][1], c)))
        return tuple(merged(p) if len(p) > 1 else p[0] for p in parts)

    states = lax.fori_loop(0, step, lambda c, st: earlier([(n, c) for n in range(len(lanes))], st), states)
    for h in range(1, NSA_SPLIT):
        mine = [n for n, (hh, _) in enumerate(lanes) if hh >= h]
        pairs = per_part // 2
        states = lax.fori_loop(
            0, pairs,
            lambda i, st, mine=mine, h=h: earlier(
                [(n, step + (h - 1) * per_part + 2 * i + k) for n in mine for k in range(2)], st), states)

    for n, (h, g) in enumerate(lanes):
        o_sel = normalised(states[n])
        gt = gt_ref[0, h, g * B_GROUP * N_BRANCH:(g + 1) * B_GROUP * N_BRANCH, :]
        outs = []
        for r in range(B_GROUP):
            cs = slice(r * tq, (r + 1) * tq)
            outs.append(gt[r * N_BRANCH:r * N_BRANCH + 1, :] * o_cmp[h, g][:, cs]
                        + gt[r * N_BRANCH + 1:r * N_BRANCH + 2, :] * o_sel[:, cs]
                        + gt[r * N_BRANCH + 2:r * N_BRANCH + 3, :] * o_win[h, g][:, cs])
        for c in range(B_GROUP // 2):
            pair = jnp.concatenate([outs[2 * c], outs[2 * c + 1]], axis=0)
            o_ref[0, h, :, g * gw + c * LANES:g * gw + (c + 1) * LANES] = pair.T


def _overlap_t(nc_pad, ns):
    nc = nc_pad - 1
    c0 = np.arange(nc_pad) * CMP_STRIDE
    s0 = np.arange(ns) * SEL_BLOCK
    ov = np.minimum(c0[None, :] + CMP_LEN, s0[:, None] + SEL_BLOCK) - np.maximum(c0[None, :], s0[:, None])
    ov = np.clip(ov, 0, None).astype(np.float32) / CMP_LEN
    ov = np.where(np.arange(nc_pad)[None, :] < nc, ov, 0.0)
    return jnp.asarray(ov, BF16)


def _nsa(qb, ks, kw, vst, vwt, kc, vct, gt):
    B, _, _, T, _ = qb.shape
    W = B_WIDTH
    nq = T // NSA_TQ
    ns = T // SEL_BLOCK
    nc_pad = kc.shape[2]
    nrow = B_GROUP * NSA_TQ
    part = T // NSA_SPLIT
    assert NSA_TQ == NSA_CK and WIN % NSA_CK == 0 and SEL_LANE0 + ns <= LANES and (nq // NSA_SPLIT) % 2 == 0
    assert gt.shape == (B, NSA_SPLIT, GATE_ROWS, part)
    kspec = pl.BlockSpec((1, B_KV_HEADS, T, LANES), lambda b, i: (b, 0, 0, 0))
    vspec = pl.BlockSpec((1, T // NSA_CK, B_KV_HEADS * NSA_VROWS, NSA_CK), lambda b, i: (b, 0, 0, 0))
    key = np.arange(NSA_CK)[:, None]
    qry = (np.arange(nrow) % NSA_TQ)[None, :]
    bias = jnp.asarray(np.stack([np.where(key <= qry, 0.0, NEG), np.where(key > qry, 0.0, NEG)]).astype(np.float32))
    cmp_end = (np.arange(nc_pad) * CMP_STRIDE + CMP_LEN - 1)[None, :, None]
    t_query = (np.arange(nq) * NSA_TQ)[:, None, None] + qry[None]
    cbias = jnp.asarray(np.where((cmp_end <= t_query) & (np.arange(nc_pad) < nc_pad - 1)[None, :, None], 0.0, NEG)
                        .astype(np.float32)).reshape(NSA_SPLIT, nq // NSA_SPLIT, nc_pad, nrow)
    out = pl.pallas_call(
        functools.partial(_nsa_kernel, n_sel_blocks=ns, nq=nq),
        grid=(B, nq // NSA_SPLIT),
        in_specs=[pl.BlockSpec((1, B_KV_HEADS, B_GROUP, NSA_SPLIT, NSA_TQ, LANES), lambda b, i: (b, 0, 0, 0, i, 0)),
                  kspec, kspec, vspec, vspec,
                  pl.BlockSpec((1, B_KV_HEADS, nc_pad, LANES), lambda b, i: (b, 0, 0, 0)),
                  pl.BlockSpec((1, B_KV_HEADS, HEAD_DIM, nc_pad), lambda b, i: (b, 0, 0, 0)),
                  pl.BlockSpec((1, NSA_SPLIT, GATE_ROWS, NSA_TQ), lambda b, i: (b, 0, 0, i)),
                  pl.BlockSpec((ns, nc_pad), lambda b, i: (0, 0)),
                  pl.BlockSpec((2, NSA_CK, nrow), lambda b, i: (0, 0, 0)),
                  pl.BlockSpec((NSA_SPLIT, 1, nc_pad, nrow), lambda b, i: (0, i, 0, 0))],
        out_specs=pl.BlockSpec((1, NSA_SPLIT, NSA_TQ, W), lambda b, i: (b, 0, i, 0)),
        out_shape=jax.ShapeDtypeStruct((B, NSA_SPLIT, part, W), F32),
        compiler_params=pltpu.CompilerParams(dimension_semantics=("parallel", "arbitrary"),
                                             vmem_limit_bytes=VMEM_LIMIT),
        name="nsa",
    )(qb.reshape(B, B_KV_HEADS, B_GROUP, NSA_SPLIT, part, LANES), ks, kw, vst, vwt, kc, vct,
      gt, _overlap_t(nc_pad, ns), bias, cbias)
    return out.reshape(B, T, W)


def _post_kernel(x_ref, oa_ref, ob_ref, ga_ref, gb_ref, wo_ref, gm_ref, wu_ref, wd_ref, gf_ref, o_ref, *, final):
    def norm(v, g):
        return v * lax.rsqrt(jnp.mean(v * v, axis=-1, keepdims=True) + EPS) * g

    na = norm(oa_ref[...], ga_ref[...]).astype(BF16)
    nb = norm(ob_ref[...], gb_ref[...]).astype(BF16)
    aw = na.shape[1]
    h_res = x_ref[...] + _dot(na, wo_ref[0:aw, :]) + _dot(nb, wo_ref[aw:, :])
    h = norm(h_res, gm_ref[...]).astype(BF16)
    u = jnp.square(jnp.maximum(_dot(h, wu_ref[...]), 0.0)).astype(BF16)
    acc = h_res + _dot(u, wd_ref[...])
    o_ref[...] = norm(acc, gf_ref[...]) if final else acc


def _post(x, oa, ob, g_a, g_b, w_out, g_mlp, w_up, w_down, g_final, *, final, tm=512):
    B, T, D = x.shape
    n = B * T
    dff = w_up.shape[1]
    tok = lambda w: pl.BlockSpec((tm, w), lambda i: (i, 0))
    const = lambda shape: pl.BlockSpec(shape, lambda i: (0, 0), pipeline_mode=pl.Buffered(1))
    out = pl.pallas_call(
        functools.partial(_post_kernel, final=final),
        grid=(n // tm,),
        in_specs=[tok(D), tok(A_WIDTH), tok(B_WIDTH), const((1, A_WIDTH)), const((1, B_WIDTH)),
                  const((A_WIDTH + B_WIDTH, D)), const((1, D)), const((D, dff)), const((dff, D)), const((1, D))],
        out_specs=tok(D),
        out_shape=jax.ShapeDtypeStruct((n, D), F32),
        compiler_params=pltpu.CompilerParams(dimension_semantics=("parallel",), vmem_limit_bytes=VMEM_LIMIT),
        name="post",
    )(x.reshape(n, D), oa.reshape(n, A_WIDTH), ob.reshape(n, B_WIDTH), g_a.reshape(1, -1), g_b.reshape(1, -1),
      w_out.astype(BF16), g_mlp.reshape(1, D), w_up.astype(BF16), w_down.astype(BF16), g_final.reshape(1, D))
    return out.reshape(B, T, D)


def kernel(x, norm_mix, w_in, cmp_pe_k, cmp_w1_k, cmp_w2_k, cmp_pe_v, cmp_w1_v, cmp_w2_v, g_out_a, g_out_b,
           w_out, norm_mlp, w_up, w_down, norm_final):
    B, T, D = x.shape
    depth = w_in.shape[0]
    tables = _rope_tables(np.arange(T))
    cmp_tables = _rope_tables(np.arange(T // CMP_STRIDE) * CMP_STRIDE + CMP_LEN - 1)
    h_res = x
    for l in range(depth):
        qa, ka, va, qb, ks, kw, vst, vwt, kcvc, gt, wo_b, wu_b, wd_b = _in_proj(
            h_res, norm_mix[l], w_in[l], tables, (w_out[l], w_up[l], w_down[l]))
        kc, vct = _compress(kcvc, cmp_pe_k[l], cmp_w1_k[l], cmp_w2_k[l], cmp_pe_v[l], cmp_w1_v[l], cmp_w2_v[l],
                            cmp_tables)
        oa = _mixer_a(qa, ka, va)
        ob = _nsa(qb, ks, kw, vst, vwt, kc, vct, gt)
        h_res = _post(h_res, oa, ob, g_out_a[l], g_out_b[l], wo_b, norm_mlp[l], wu_b, wd_b, norm_final,
                      final=(l == depth - 1))
    return h_res
```

```python
import functools

import jax
import jax.numpy as jnp
import numpy as np
from jax import lax
from jax.experimental import pallas as pl
from jax.experimental.pallas import tpu as pltpu

F32 = jnp.float32
BF16 = jnp.bfloat16

HEAD_DIM = 64
ROT_DIM = HEAD_DIM // 4
ROPE_THETA = 500000.0
EPS = 1e-6
NEG = -1e30
Q_SCALE = HEAD_DIM ** -0.5 * 1.4426950408889634
LANES = 128

A_HEADS = 8
A_PATTERNS = ((128, 1), (512, 4), (2048, 16))
A_BLOCK = 128
A_GROUP = 16

B_HEADS = 8
B_KV_HEADS = 2
B_GROUP = B_HEADS // B_KV_HEADS
CMP_LEN = 32
CMP_STRIDE = 16
CMP_HIDDEN = 256
SEL_BLOCK = 64
SEL_SHIFT = 6
SEL_TOPK = 8
WIN = 512
N_BRANCH = 3

A_WIDTH = A_HEADS * HEAD_DIM
B_WIDTH = B_HEADS * HEAD_DIM
KV_WIDTH = B_KV_HEADS * HEAD_DIM

NSA_TQ = 256
NSA_CK = 256
NSA_SPLIT = 2
NSA_VROWS = 80
GATE_ROWS = 32
SEL_LANE0 = HEAD_DIM

N_LATER = 5

VMEM_LIMIT = 56 * 1024 * 1024


def _dot(a, b):
    return jnp.dot(a, b, preferred_element_type=F32)


def _dot_nt(a, b):
    return lax.dot_general(a, b, (((1,), (1,)), ((), ())), preferred_element_type=F32)


def _rope_rows(y, cos, sin_a, sin_b):
    outs = []
    for c in range(y.shape[1] // LANES):
        yc = y[:, c * LANES:(c + 1) * LANES]
        outs.append(yc * cos + pltpu.roll(yc, LANES - ROT_DIM // 2, 1) * sin_a
                    + pltpu.roll(yc, ROT_DIM // 2, 1) * sin_b)
    return outs[0] if len(outs) == 1 else jnp.concatenate(outs, axis=1)


def _in_proj_kernel(x_ref, g_ref, wq_ref, wkk_ref, wt_ref, cos_ref, sa_ref, sb_ref, *refs, tm):
    (later_f32, (qa_ref, ka_ref, va_ref, qb_ref, ks_ref, kw_ref, vst_ref, vwt_ref, kcvc_ref, gt_ref), later_bf16) = (
        refs[:N_LATER], refs[N_LATER:-N_LATER], refs[-N_LATER:])
    for src, dst in zip(later_f32, later_bf16):
        dst[...] = src[...].astype(BF16)
    tt = pl.program_id(1)
    x = x_ref[...]
    ms = jnp.mean(x * x, axis=-1, keepdims=True)
    h = (x * lax.rsqrt(ms + EPS) * g_ref[...]).astype(BF16)
    cos, sa, sb = cos_ref[...], sa_ref[...], sb_ref[...]
    scale = Q_SCALE

    def proj(c0, c1):
        return _dot(h, wq_ref[:, c0:c1])

    o = 0
    qa_ref[...] = _rope_rows(proj(o, o + A_WIDTH), cos, sa, sb) * scale
    o += A_WIDTH
    ka_ref[...] = _rope_rows(proj(o, o + A_WIDTH), cos, sa, sb)
    o += A_WIDTH
    va_ref[...] = proj(o, o + A_WIDTH)
    o += A_WIDTH
    lane = lax.broadcasted_iota(jnp.int32, (tm, LANES), 1)
    row = lax.broadcasted_iota(jnp.int32, (tm, LANES), 0)
    lo = lane < HEAD_DIM
    qb = _rope_rows(proj(o, o + B_WIDTH), cos, sa, sb) * scale
    for hh in range(B_HEADS):
        ch = qb[:, (hh // 2) * LANES:(hh // 2 + 1) * LANES]
        if hh % 2:
            ch = pltpu.roll(ch, HEAD_DIM, 1)
        qb_ref[0, hh // B_GROUP, hh % B_GROUP] = jnp.where(lo, ch, 0.0).astype(BF16)
    o += B_WIDTH
    kcvc_ref[...] = proj(o, o + 2 * KV_WIDTH)
    o += 2 * KV_WIDTH
    ksw = _rope_rows(_dot(h, wkk_ref[...]), cos, sa, sb)

    blk = (tt * tm + row) >> SEL_SHIFT
    onehot = jnp.where(lane - SEL_LANE0 == blk, 1.0, 0.0)
    for kind, ref in ((0, ks_ref), (1, kw_ref)):
        kk = ksw[:, kind * LANES:(kind + 1) * LANES]
        tail = onehot if kind == 0 else 0.0
        ref[0, 0] = jnp.where(lo, kk, tail).astype(BF16)
        ref[0, 1] = jnp.where(lo, pltpu.roll(kk, HEAD_DIM, 1), tail).astype(BF16)

    tr = _dot_nt(wt_ref[...], h)
    ones_row = jnp.where(lax.broadcasted_iota(jnp.int32, (NSA_VROWS - HEAD_DIM, tm), 0) == 0, 1.0, 0.0)
    for kind, ref in ((0, vst_ref), (1, vwt_ref)):
        rows = [tr[kind * LANES + gg * HEAD_DIM:kind * LANES + (gg + 1) * HEAD_DIM] for gg in range(B_KV_HEADS)]
        slab = jnp.concatenate([rows[0], ones_row, rows[1], ones_row], axis=0).astype(BF16)
        for c in range(tm // NSA_CK):
            ref[0, c] = slab[:, c * NSA_CK:(c + 1) * NSA_CK]
    gt_ref[0, 0] = jax.nn.sigmoid(tr[2 * LANES:2 * LANES + GATE_ROWS, :])


def _rope_tables(pos):
    half = ROT_DIM // 2
    inv = ROPE_THETA ** (-np.arange(0, ROT_DIM, 2, dtype=np.float64) / ROT_DIM)
    ang = np.asarray(pos, np.float64)[:, None] * inv[None, :]
    cos, sin = np.cos(ang), np.sin(ang)
    n = len(pos)
    ones = np.ones((n, HEAD_DIM - ROT_DIM))
    zeros = np.zeros((n, HEAD_DIM - ROT_DIM))
    zh = np.zeros((n, half))
    c_head = np.concatenate([cos, cos, ones], axis=1)
    a_head = np.concatenate([-sin, zh, zeros], axis=1)
    b_head = np.concatenate([zh, sin, zeros], axis=1)
    rep = LANES // HEAD_DIM
    return tuple(jnp.asarray(np.tile(t, (1, rep)).astype(np.float32)) for t in (c_head, a_head, b_head))


def _in_proj(x, norm_g, w_in, tables, later, *, tm=1024):
    B, T, D = x.shape
    nt = T // tm
    assert len(later) == N_LATER and all(w.shape[0] % (16 * B * nt) == 0 for w in later)
    slab = lambda w: pl.BlockSpec((w.shape[0] // (B * nt), w.shape[1]), lambda b, t: (b * nt + t, 0))
    offs = [0]
    for n in (A_WIDTH, A_WIDTH, A_WIDTH, B_WIDTH, KV_WIDTH, KV_WIDTH, KV_WIDTH, KV_WIDTH, KV_WIDTH, KV_WIDTH,
              B_HEADS * N_BRANCH):
        offs.append(offs[-1] + n)
    col = lambda i: w_in[:, offs[i]:offs[i + 1]]
    wq = w_in[:, :offs[6]].astype(BF16)
    wkk = jnp.concatenate([col(6), col(8)], axis=1).astype(BF16)
    gpad = jnp.zeros((D, GATE_ROWS - B_HEADS * N_BRANCH), w_in.dtype)
    wt = jnp.concatenate([col(7), col(9), col(10), gpad], axis=1).T.astype(BF16)
    cos, sa, sb = tables
    nq = wq.shape[1]
    nr = wt.shape[0]
    tok = lambda w: pl.BlockSpec((None, tm, w), lambda b, t: (b, t, 0))
    const = lambda shape: pl.BlockSpec(shape, lambda b, t: (0,) * len(shape))
    tab = pl.BlockSpec((tm, LANES), lambda b, t: (t, 0))
    out_shapes = (
        jax.ShapeDtypeStruct((B, T, A_WIDTH), F32),
        jax.ShapeDtypeStruct((B, T, A_WIDTH), F32),
        jax.ShapeDtypeStruct((B, T, A_WIDTH), F32),
        jax.ShapeDtypeStruct((B, B_KV_HEADS, B_GROUP, T, LANES), BF16),
        jax.ShapeDtypeStruct((B, B_KV_HEADS, T, LANES), BF16),
        jax.ShapeDtypeStruct((B, B_KV_HEADS, T, LANES), BF16),
        jax.ShapeDtypeStruct((B, T // NSA_CK, B_KV_HEADS * NSA_VROWS, NSA_CK), BF16),
        jax.ShapeDtypeStruct((B, T // NSA_CK, B_KV_HEADS * NSA_VROWS, NSA_CK), BF16),
        jax.ShapeDtypeStruct((B, T, 2 * KV_WIDTH), F32),
        jax.ShapeDtypeStruct((B, nt, GATE_ROWS, tm), F32),
    )
    frame = pl.BlockSpec((1, B_KV_HEADS, tm, LANES), lambda b, t: (b, 0, t, 0))
    vt = pl.BlockSpec((1, tm // NSA_CK, B_KV_HEADS * NSA_VROWS, NSA_CK), lambda b, t: (b, t, 0, 0))
    qframe = pl.BlockSpec((1, B_KV_HEADS, B_GROUP, tm, LANES), lambda b, t: (b, 0, 0, t, 0))
    out_specs = (tok(A_WIDTH), tok(A_WIDTH), tok(A_WIDTH), qframe, frame, frame, vt, vt,
                 tok(2 * KV_WIDTH), pl.BlockSpec((1, 1, GATE_ROWS, tm), lambda b, t: (b, t, 0, 0)))
    return pl.pallas_call(
        functools.partial(_in_proj_kernel, tm=tm),
        grid=(B, nt),
        in_specs=[tok(D), const((1, D)), const((D, nq)), const((D, 2 * KV_WIDTH)), const((nr, D)), tab, tab, tab]
        + [slab(w) for w in later],
        out_specs=out_specs + tuple(slab(w) for w in later),
        out_shape=out_shapes + tuple(jax.ShapeDtypeStruct(w.shape, BF16) for w in later),
        compiler_params=pltpu.CompilerParams(dimension_semantics=("parallel", "parallel"),
                                             vmem_limit_bytes=VMEM_LIMIT),
        name="in_proj",
    )(x, norm_g.reshape(1, D), wq, wkk, wt, cos, sa, sb, *later)


def _compress_kernel(ak_ref, av_ref, w1k_ref, w1v_ref, pek_ref, pev_ref, w2k_ref, w2vt_ref, cos_ref, sa_ref, sb_ref,
                     kc_ref, vct_ref, *, nc_pad):
    half = CMP_LEN // 2
    hid_w = B_KV_HEADS * CMP_HIDDEN

    zeros = jnp.zeros((HEAD_DIM, CMP_HIDDEN), BF16)

    def both_groups(w):
        return jnp.concatenate([jnp.concatenate([w, zeros], axis=1), jnp.concatenate([zeros, w], axis=1)], axis=0)

    def hidden(a_ref, w1_ref, pe_ref):
        acc_u = jnp.zeros((nc_pad, hid_w), F32)
        acc_v = jnp.zeros((nc_pad, hid_w), F32)
        for p in range(half):
            ap = a_ref[0, pl.ds(p, nc_pad, stride=CMP_STRIDE), :]
            acc_u = acc_u + _dot((ap + pe_ref[p:p + 1, :]).astype(BF16), both_groups(w1_ref[p]))
            acc_v = acc_v + _dot((ap + pe_ref[half + p:half + p + 1, :]).astype(BF16), both_groups(w1_ref[half + p]))
        return jax.nn.gelu(acc_u + pltpu.roll(acc_v, nc_pad - 1, 0))

    hk = hidden(ak_ref, w1k_ref, pek_ref).astype(BF16)
    hv = hidden(av_ref, w1v_ref, pev_ref).astype(BF16)
    for g in range(B_KV_HEADS):
        hg = hk[:, g * CMP_HIDDEN:(g + 1) * CMP_HIDDEN]
        kc = _dot(hg, w2k_ref[...])
        kc_ref[0, g] = _rope_rows(kc, cos_ref[...], sa_ref[...], sb_ref[...]).astype(BF16)
        vg = hv[:, g * CMP_HIDDEN:(g + 1) * CMP_HIDDEN]
        vct_ref[0, g] = _dot_nt(w2vt_ref[...], vg).astype(BF16)


def _compress(kcvc, pe_k, w1_k, w2_k, pe_v, w1_v, w2_v, cmp_tables):
    B, T, _ = kcvc.shape
    nc_pad = T // CMP_STRIDE
    w2k = jnp.concatenate([w2_k, jnp.zeros_like(w2_k)], axis=1).astype(BF16)
    w2vt = w2_v.T.astype(BF16)
    pek = jnp.tile(pe_k, (1, B_KV_HEADS))
    pev = jnp.tile(pe_v, (1, B_KV_HEADS))
    per_pos = lambda w1: w1.reshape(CMP_LEN, HEAD_DIM, CMP_HIDDEN).astype(BF16)
    const = lambda shape: pl.BlockSpec(shape, lambda b: (0,) * len(shape))
    cos, sa, sb = cmp_tables
    return pl.pallas_call(
        functools.partial(_compress_kernel, nc_pad=nc_pad),
        grid=(B,),
        in_specs=[pl.BlockSpec((1, T, KV_WIDTH), lambda b: (b, 0, 0)), pl.BlockSpec((1, T, KV_WIDTH), lambda b: (b, 0, 1)),
                  const((CMP_LEN, HEAD_DIM, CMP_HIDDEN)), const((CMP_LEN, HEAD_DIM, CMP_HIDDEN)),
                  const((CMP_LEN, LANES)), const((CMP_LEN, LANES)),
                  const((CMP_HIDDEN, LANES)), const((HEAD_DIM, CMP_HIDDEN)),
                  const((nc_pad, LANES)), const((nc_pad, LANES)), const((nc_pad, LANES))],
        out_specs=(pl.BlockSpec((1, B_KV_HEADS, nc_pad, LANES), lambda b: (b, 0, 0, 0)),
                   pl.BlockSpec((1, B_KV_HEADS, HEAD_DIM, nc_pad), lambda b: (b, 0, 0, 0))),
        out_shape=(jax.ShapeDtypeStruct((B, B_KV_HEADS, nc_pad, LANES), BF16),
                   jax.ShapeDtypeStruct((B, B_KV_HEADS, HEAD_DIM, nc_pad), BF16)),
        compiler_params=pltpu.CompilerParams(dimension_semantics=("parallel",), vmem_limit_bytes=VMEM_LIMIT),
        name="compress",
    )(kcvc, kcvc, per_pos(w1_k), per_pos(w1_v), pek, pev, w2k, w2vt, cos, sa, sb)


def _mixer_a_kernel(q_ref, k_ref, v_ref, bias_ref, o_ref, qd0, qd1, kd, vd0, vd1, u_s, m_s, l_s, *, seq):
    blk = A_BLOCK
    nres = seq // blk
    npat = len(A_PATTERNS)
    order = sorted(range(npat), key=lambda p: -A_PATTERNS[p][1])
    slot = {p: n for n, p in enumerate(order[:-1])}
    lane = lax.broadcasted_iota(jnp.int32, (blk, LANES), 1)
    lo = lane < HEAD_DIM

    def deinterleave(r):
        rows = pl.ds(r * blk, blk)
        q, v = q_ref[0, pl.ds(r, blk, stride=nres), :], v_ref[0, pl.ds(r, blk, stride=nres), :]
        qd0[rows, :] = jnp.where(lo, q, 0.0)
        qd1[rows, :] = jnp.where(lo, 0.0, q)
        kd[rows, :] = k_ref[0, pl.ds(r, blk, stride=nres), :]
        vd0[rows, :] = jnp.where(lo, v, 1.0)
        vd1[rows, :] = jnp.where(lo, 1.0, v)

    def pieces(dil, rd, row_off, rows):
        return [pl.ds((rd + dil * jj) * blk + row_off, rows) for jj in range(nres // dil)]

    def gather(ref, idx, lead=()):
        parts = [ref[lead + (i, slice(None))] for i in idx]
        return parts[0] if len(parts) == 1 else jnp.concatenate(parts, axis=0)

    def attend(pi, blocks):
        dil = A_PATTERNS[pi][1]
        pr = blk // (nres // dil)
        q_idxs, vbs, scores = [], [], []
        for rd, n, first in blocks:
            q_idx = pieces(dil, rd, n * pr, pr)
            k_idx = q_idx if first else pieces(dil, rd, (n - 1) * pr, 2 * pr)
            bias = bias_ref[pi, :, 0:blk] if first else bias_ref[pi, :, blk:3 * blk]
            kb = gather(kd, k_idx).astype(BF16)
            q_idxs.append(q_idx)
            for qd, vd in ((qd0, vd0), (qd1, vd1)):
                vbs.append(gather(vd, k_idx).astype(BF16))
                scores.append(_dot_nt(gather(qd, q_idx).astype(BF16), kb) + bias)
        es, ms = [], []
        for s in scores:
            m = jnp.max(s, axis=-1, keepdims=True)
            ms.append(m)
            es.append(jnp.exp2((s - m).astype(BF16)))
        pvs = [_dot(e, vb) for e, vb in zip(es, vbs)]
        for b, q_idx in enumerate(q_idxs):
            u = jnp.where(lo, pvs[2 * b], pvs[2 * b + 1])
            l_swapped = jnp.where(lo, pvs[2 * b + 1], pvs[2 * b])
            m = jnp.where(lo, ms[2 * b], ms[2 * b + 1])
            if pi != order[-1]:
                for jj, idx in enumerate(q_idx):
                    u_s[slot[pi], idx, :] = u[jj * pr:(jj + 1) * pr]
                    m_s[slot[pi], idx, :] = m[jj * pr:(jj + 1) * pr]
                    l_s[slot[pi], idx, :] = l_swapped[jj * pr:(jj + 1) * pr]
                continue
            parts = [(u, m, l_swapped)] + [(gather(u_s, q_idx, (sl,)), gather(m_s, q_idx, (sl,)),
                                            gather(l_s, q_idx, (sl,))) for sl in slot.values()]
            m_all = functools.reduce(jnp.maximum, [mm for _, mm, _ in parts])
            num = jnp.zeros((blk, LANES), F32)
            den = jnp.zeros((blk, LANES), F32)
            for uu, mm, ll in parts:
                a = jnp.exp2(mm - m_all)
                num = num + a * uu
                den = den + a * pltpu.roll(ll, HEAD_DIM, 1)
            out = num / den
            n = blocks[b][1]
            for jj in range(nres):
                o_ref[0, pl.ds(n * blk + jj, pr, stride=nres), :] = out[jj * pr:(jj + 1) * pr]

    for pi in order:
        dil = A_PATTERNS[pi][1]
        nb = seq // dil // blk
        blocks = [(rd, n, n == 0) for rd in range(dil) for n in range(nb)]
        for g0 in range(0, len(blocks), A_GROUP):
            if pi == order[0]:
                for rd, _, _ in blocks[g0:g0 + A_GROUP]:
                    deinterleave(rd)
            attend(pi, blocks[g0:g0 + A_GROUP])


def _mixer_a_bias(nres):
    blk = A_BLOCK

    def sub_pos(i, fold, rows):
        return fold * (i % rows) + i // rows

    out = np.zeros((len(A_PATTERNS), blk, 3 * blk), np.float32)
    for pi, (window, dil) in enumerate(A_PATTERNS):
        n_back = window // dil
        fold = nres // dil
        pr = blk // fold
        sq = sub_pos(np.arange(blk), fold, pr)[:, None]
        d_first = sq - sub_pos(np.arange(blk), fold, pr)[None, :]
        d_band = sq + blk - sub_pos(np.arange(2 * blk), fold, 2 * pr)[None, :]
        dist = np.concatenate([d_first, d_band], axis=1)
        out[pi] = np.where((dist >= 0) & (dist <= n_back), 0.0, NEG)
    return jnp.asarray(out)


def _mixer_a(qa, ka, va):
    B, T, W = qa.shape
    npair = W // LANES
    spec = pl.BlockSpec((1, T, LANES), lambda b, p: (b, 0, p))
    npat = len(A_PATTERNS)
    nres = T // A_BLOCK
    dils = sorted(d for _, d in A_PATTERNS)
    assert all(nres % d == 0 and T % (d * A_BLOCK) == 0 for d in dils) and dils[0] == 1 and dils[-1] == nres
    return pl.pallas_call(
        functools.partial(_mixer_a_kernel, seq=T),
        grid=(B, npair),
        in_specs=[spec, spec, spec, pl.BlockSpec((npat, A_BLOCK, 3 * A_BLOCK), lambda b, p: (0, 0, 0))],
        out_specs=spec,
        out_shape=jax.ShapeDtypeStruct((B, T, W), F32),
        scratch_shapes=[pltpu.VMEM((T, LANES), F32)] * 5 + [pltpu.VMEM((npat - 1, T, LANES), F32)] * 3,
        compiler_params=pltpu.CompilerParams(dimension_semantics=("parallel", "parallel"),
                                             vmem_limit_bytes=VMEM_LIMIT),
        name="mixer_a",
    )(qa, ka, va, _mixer_a_bias(nres))


def _nsa_kernel(q_ref, ks_ref, kw_ref, vst_ref, vwt_ref, kc_ref, vct_ref, gt_ref, ovt_ref, bias_ref, cbias_ref, o_ref,
                *, n_sel_blocks, nq):
    tq, ck = NSA_TQ, NSA_CK
    nrow = B_GROUP * tq
    gw = B_GROUP * HEAD_DIM
    step = pl.program_id(1)
    per_part = nq // NSA_SPLIT
    lanes = [(h, g) for h in range(NSA_SPLIT) for g in range(B_KV_HEADS)]
    qi = [step + h * per_part for h in range(NSA_SPLIT)]
    col_i = lax.broadcasted_iota(jnp.int32, (1, nrow), 1) & (tq - 1)

    def chunk(ref, g, c):
        return ref[0, g, pl.ds(pl.multiple_of(c * ck, ck), ck), :]

    def vt_chunk(ref, g, c):
        return ref[0, c, g * NSA_VROWS:(g + 1) * NSA_VROWS, :]

    def softmax_pv(s, vt):
        m = jnp.max(s, axis=0, keepdims=True)
        return m, _dot(vt, jnp.exp2(s - m).astype(BF16))

    def merged(parts):
        m_new = functools.reduce(jnp.maximum, [m for m, _ in parts])
        acc = sum(jnp.exp2(m - m_new) * a for m, a in parts)
        return m_new, acc

    def normalised(state):
        return state[1][0:HEAD_DIM] * (1.0 / state[1][HEAD_DIM:HEAD_DIM + 1])

    q_b = {(h, g): jnp.concatenate([q_ref[0, g, r, h] for r in range(B_GROUP)], axis=0) for h, g in lanes}

    nwin = WIN // ck
    win_chunks = [[jnp.maximum(qi[h] - back, 0) for back in range(nwin, -1, -1)] for h in range(NSA_SPLIT)]
    s_win = {}
    for h, g in lanes:
        s = _dot_nt(jnp.concatenate([chunk(kw_ref, g, c) for c in win_chunks[h]], axis=0), q_b[h, g])
        exists = lambda back: True if h * per_part >= back else qi[h] >= back
        masked = lambda piece, back: piece if exists(back) is True else jnp.where(exists(back), piece, NEG)
        pieces = [masked(s[0:ck] + bias_ref[1], nwin)]
        for n in range(1, nwin):
            pieces.append(masked(s[n * ck:(n + 1) * ck], nwin - n))
        pieces.append(s[nwin * ck:] + bias_ref[0])
        s_win[h, g] = pieces

    s_cmp = {(h, g): _dot_nt(kc_ref[0, g], q_b[h, g]) + cbias_ref[h, 0] for h, g in lanes}
    p_cmp, o_cmp = {}, {}
    for h, g in lanes:
        m = jnp.max(s_cmp[h, g], axis=0, keepdims=True)
        e = jnp.exp2(s_cmp[h, g] - m)
        den = jnp.sum(e, axis=0, keepdims=True)
        sees_block = qi[h] * tq + col_i >= CMP_LEN - 1
        p_cmp[h, g] = e * jnp.where(sees_block, 1.0 / jnp.maximum(den, 1e-30), 0.0)
        o_cmp[h, g] = _dot(vct_ref[0, g], p_cmp[h, g].astype(BF16))

    imp = {}
    for h, g in lanes:
        psum = p_cmp[h, g][:, 0:tq]
        for r in range(1, B_GROUP):
            psum = psum + p_cmp[h, g][:, r * tq:(r + 1) * tq]
        p_hi = psum.astype(BF16)
        p_lo = (psum - p_hi.astype(F32)).astype(BF16)
        imp[h, g] = _dot(ovt_ref[...], p_hi) + _dot(ovt_ref[...], p_lo)
    j = lax.broadcasted_iota(jnp.int32, (n_sel_blocks, tq), 0)
    j_f = j.astype(F32)
    low = -3e38
    q_aug = {}
    for h, g in lanes:
        cur = (qi[h] * tq + lax.broadcasted_iota(jnp.int32, (n_sel_blocks, tq), 1)) >> SEL_SHIFT
        forced = (j == 0) | (j == cur) | (j == cur - 1)
        score = jnp.where(forced, imp[h, g] + 2.0, jnp.where(j > cur, -1.0, imp[h, g]))
        sel = jnp.zeros((n_sel_blocks, tq), jnp.bool_)
        for _ in range(min(SEL_TOPK, n_sel_blocks)):
            mx = jnp.max(score, axis=0, keepdims=True)
            first = jnp.min(jnp.where(score == mx, j_f, 4.0 * LANES), axis=0, keepdims=True)
            hit = j_f == first
            sel = sel | hit
            score = jnp.where(hit, low, score)
        selneg = jnp.concatenate([jnp.zeros((SEL_LANE0, tq), F32), jnp.where(sel, 0.0, NEG),
                                  jnp.zeros((LANES - SEL_LANE0 - n_sel_blocks, tq), F32)], axis=0).T
        q_aug[h, g] = q_b[h, g] + jnp.concatenate([selneg.astype(BF16)] * B_GROUP, axis=0)

    s_diag = {(h, g): _dot_nt(chunk(ks_ref, g, qi[h]), q_aug[h, g]) + bias_ref[0] for h, g in lanes}
    o_win = {(h, g): normalised(merged([softmax_pv(s_win[h, g][n], vt_chunk(vwt_ref, g, c))
                                        for n, c in enumerate(win_chunks[h])])) for h, g in lanes}
    states = tuple(softmax_pv(s_diag[h, g], vt_chunk(vst_ref, g, qi[h])) for h, g in lanes)

    def earlier(work, st):
        s = [_dot_nt(chunk(ks_ref, lanes[n][1], c), q_aug[lanes[n]]) for n, c in work]
        parts = [[st[n]] for n in range(len(lanes))]
        for (n, c), sc in zip(work, s):
            parts[n].append(softmax_pv(sc, vt_chunk(vst_ref, lanes[n][1], c)))
        return tuple(merged(p) if len(p) > 1 else p[0] for p in parts)

    states = lax.fori_loop(0, step, lambda c, st: earlier([(n, c) for n in range(len(lanes))], st), states)
    for h in range(1, NSA_SPLIT):
        mine = [n for n, (hh, _) in enumerate(lanes) if hh >= h]
        pairs = per_part // 2
        states = lax.fori_loop(
            0, pairs,
            lambda i, st, mine=mine, h=h: earlier(
                [(n, step + (h - 1) * per_part + 2 * i + k) for n in mine for k in range(2)], st), states)

    for n, (h, g) in enumerate(lanes):
        o_sel = normalised(states[n])
        gt = gt_ref[0, h, g * B_GROUP * N_BRANCH:(g + 1) * B_GROUP * N_BRANCH, :]
        outs = []
        for r in range(B_GROUP):
            cs = slice(r * tq, (r + 1) * tq)
            outs.append(gt[r * N_BRANCH:r * N_BRANCH + 1, :] * o_cmp[h, g][:, cs]
                        + gt[r * N_BRANCH + 1:r * N_BRANCH + 2, :] * o_sel[:, cs]
                        + gt[r * N_BRANCH + 2:r * N_BRANCH + 3, :] * o_win[h, g][:, cs])
        for c in range(B_GROUP // 2):
            pair = jnp.concatenate([outs[2 * c], outs[2 * c + 1]], axis=0)
            o_ref[0, h, :, g * gw + c * LANES:g * gw + (c + 1) * LANES] = pair.T


def _overlap_t(nc_pad, ns):
    nc = nc_pad - 1
    c0 = np.arange(nc_pad) * CMP_STRIDE
    s0 = np.arange(ns) * SEL_BLOCK
    ov = np.minimum(c0[None, :] + CMP_LEN, s0[:, None] + SEL_BLOCK) - np.maximum(c0[None, :], s0[:, None])
    ov = np.clip(ov, 0, None).astype(np.float32) / CMP_LEN
    ov = np.where(np.arange(nc_pad)[None, :] < nc, ov, 0.0)
    return jnp.asarray(ov, BF16)


def _nsa(qb, ks, kw, vst, vwt, kc, vct, gt):
    B, _, _, T, _ = qb.shape
    W = B_WIDTH
    nq = T // NSA_TQ
    ns = T // SEL_BLOCK
    nc_pad = kc.shape[2]
    nrow = B_GROUP * NSA_TQ
    part = T // NSA_SPLIT
    assert NSA_TQ == NSA_CK and WIN % NSA_CK == 0 and SEL_LANE0 + ns <= LANES and (nq // NSA_SPLIT) % 2 == 0
    assert gt.shape == (B, NSA_SPLIT, GATE_ROWS, part)
    kspec = pl.BlockSpec((1, B_KV_HEADS, T, LANES), lambda b, i: (b, 0, 0, 0))
    vspec = pl.BlockSpec((1, T // NSA_CK, B_KV_HEADS * NSA_VROWS, NSA_CK), lambda b, i: (b, 0, 0, 0))
    key = np.arange(NSA_CK)[:, None]
    qry = (np.arange(nrow) % NSA_TQ)[None, :]
    bias = jnp.asarray(np.stack([np.where(key <= qry, 0.0, NEG), np.where(key > qry, 0.0, NEG)]).astype(np.float32))
    cmp_end = (np.arange(nc_pad) * CMP_STRIDE + CMP_LEN - 1)[None, :, None]
    t_query = (np.arange(nq) * NSA_TQ)[:, None, None] + qry[None]
    cbias = jnp.asarray(np.where((cmp_end <= t_query) & (np.arange(nc_pad) < nc_pad - 1)[None, :, None], 0.0, NEG)
                        .astype(np.float32)).reshape(NSA_SPLIT, nq // NSA_SPLIT, nc_pad, nrow)
    out = pl.pallas_call(
        functools.partial(_nsa_kernel, n_sel_blocks=ns, nq=nq),
        grid=(B, nq // NSA_SPLIT),
        in_specs=[pl.BlockSpec((1, B_KV_HEADS, B_GROUP, NSA_SPLIT, NSA_TQ, LANES), lambda b, i: (b, 0, 0, 0, i, 0)),
                  kspec, kspec, vspec, vspec,
                  pl.BlockSpec((1, B_KV_HEADS, nc_pad, LANES), lambda b, i: (b, 0, 0, 0)),
                  pl.BlockSpec((1, B_KV_HEADS, HEAD_DIM, nc_pad), lambda b, i: (b, 0, 0, 0)),
                  pl.BlockSpec((1, NSA_SPLIT, GATE_ROWS, NSA_TQ), lambda b, i: (b, 0, 0, i)),
                  pl.BlockSpec((ns, nc_pad), lambda b, i: (0, 0)),
                  pl.BlockSpec((2, NSA_CK, nrow), lambda b, i: (0, 0, 0)),
                  pl.BlockSpec((NSA_SPLIT, 1, nc_pad, nrow), lambda b, i: (0, i, 0, 0))],
        out_specs=pl.BlockSpec((1, NSA_SPLIT, NSA_TQ, W), lambda b, i: (b, 0, i, 0)),
        out_shape=jax.ShapeDtypeStruct((B, NSA_SPLIT, part, W), F32),
        compiler_params=pltpu.CompilerParams(dimension_semantics=("parallel", "arbitrary"),
                                             vmem_limit_bytes=VMEM_LIMIT),
        name="nsa",
    )(qb.reshape(B, B_KV_HEADS, B_GROUP, NSA_SPLIT, part, LANES), ks, kw, vst, vwt, kc, vct,
      gt, _overlap_t(nc_pad, ns), bias, cbias)
    return out.reshape(B, T, W)


def _post_kernel(x_ref, oa_ref, ob_ref, ga_ref, gb_ref, wo_ref, gm_ref, wu_ref, wd_ref, gf_ref, o_ref, *, final):
    def norm(v, g):
        return v * lax.rsqrt(jnp.mean(v * v, axis=-1, keepdims=True) + EPS) * g

    na = norm(oa_ref[...], ga_ref[...]).astype(BF16)
    nb = norm(ob_ref[...], gb_ref[...]).astype(BF16)
    aw = na.shape[1]
    h_res = x_ref[...] + _dot(na, wo_ref[0:aw, :]) + _dot(nb, wo_ref[aw:, :])
    h = norm(h_res, gm_ref[...]).astype(BF16)
    u = jnp.square(jnp.maximum(_dot(h, wu_ref[...]), 0.0)).astype(BF16)
    acc = h_res + _dot(u, wd_ref[...])
    o_ref[...] = norm(acc, gf_ref[...]) if final else acc


def _post(x, oa, ob, g_a, g_b, w_out, g_mlp, w_up, w_down, g_final, *, final, tm=512):
    B, T, D = x.shape
    n = B * T
    dff = w_up.shape[1]
    tok = lambda w: pl.BlockSpec((tm, w), lambda i: (i, 0))
    const = lambda shape: pl.BlockSpec(shape, lambda i: (0, 0), pipeline_mode=pl.Buffered(1))
    out = pl.pallas_call(
        functools.partial(_post_kernel, final=final),
        grid=(n // tm,),
        in_specs=[tok(D), tok(A_WIDTH), tok(B_WIDTH), const((1, A_WIDTH)), const((1, B_WIDTH)),
                  const((A_WIDTH + B_WIDTH, D)), const((1, D)), const((D, dff)), const((dff, D)), const((1, D))],
        out_specs=tok(D),
        out_shape=jax.ShapeDtypeStruct((n, D), F32),
        compiler_params=pltpu.CompilerParams(dimension_semantics=("parallel",), vmem_limit_bytes=VMEM_LIMIT),
        name="post",
    )(x.reshape(n, D), oa.reshape(n, A_WIDTH), ob.reshape(n, B_WIDTH), g_a.reshape(1, -1), g_b.reshape(1, -1),
      w_out.astype(BF16), g_mlp.reshape(1, D), w_up.astype(BF16), w_down.astype(BF16), g_final.reshape(1, D))
    return out.reshape(B, T, D)


def kernel(x, norm_mix, w_in, cmp_pe_k, cmp_w1_k, cmp_w2_k, cmp_pe_v, cmp_w1_v, cmp_w2_v, g_out_a, g_out_b,
           w_out, norm_mlp, w_up, w_down, norm_final):
    B, T, D = x.shape
    depth = w_in.shape[0]
    tables = _rope_tables(np.arange(T))
    cmp_tables = _rope_tables(np.arange(T // CMP_STRIDE) * CMP_STRIDE + CMP_LEN - 1)
    h_res = x
    for l in range(depth):
        qa, ka, va, qb, ks, kw, vst, vwt, kcvc, gt, wo_b, wu_b, wd_b, w1k_b, w1v_b = _in_proj(
            h_res, norm_mix[l], w_in[l], tables, (w_out[l], w_up[l], w_down[l], cmp_w1_k[l], cmp_w1_v[l]))
        kc, vct = _compress(kcvc, cmp_pe_k[l], w1k_b, cmp_w2_k[l], cmp_pe_v[l], w1v_b, cmp_w2_v[l], cmp_tables)
        oa = _mixer_a(qa, ka, va)
        ob = _nsa(qb, ks, kw, vst, vwt, kc, vct, gt)
        h_res = _post(h_res, oa, ob, g_out_a[l], g_out_b[l], wo_b, norm_mlp[l], wu_b, wd_b, norm_final,
                      final=(l == depth - 1))
    return h_res
```

```python
import functools

import jax
import jax.numpy as jnp
import numpy as np
from jax import lax
from jax.experimental import pallas as pl
from jax.experimental.pallas import tpu as pltpu

F32 = jnp.float32
BF16 = jnp.bfloat16

HEAD_DIM = 64
ROT_DIM = HEAD_DIM // 4
ROPE_THETA = 500000.0
EPS = 1e-6
NEG = -1e30
Q_SCALE = HEAD_DIM ** -0.5 * 1.4426950408889634
LANES = 128

A_HEADS = 8
A_PATTERNS = ((128, 1), (512, 4), (2048, 16))
A_BLOCK = 128
A_GROUP = 16

B_HEADS = 8
B_KV_HEADS = 2
B_GROUP = B_HEADS // B_KV_HEADS
CMP_LEN = 32
CMP_STRIDE = 16
CMP_HIDDEN = 256
SEL_BLOCK = 64
SEL_SHIFT = 6
SEL_TOPK = 8
WIN = 512
N_BRANCH = 3

A_WIDTH = A_HEADS * HEAD_DIM
B_WIDTH = B_HEADS * HEAD_DIM
KV_WIDTH = B_KV_HEADS * HEAD_DIM

NSA_TQ = 256
NSA_CK = 256
NSA_SPLIT = 2
NSA_VROWS = 80
GATE_ROWS = 32
SEL_LANE0 = HEAD_DIM

N_LATER = 5

VMEM_LIMIT = 56 * 1024 * 1024


def _dot(a, b):
    return jnp.dot(a, b, preferred_element_type=F32)


def _dot_nt(a, b):
    return lax.dot_general(a, b, (((1,), (1,)), ((), ())), preferred_element_type=F32)


def _rope_rows(y, cos, sin_a, sin_b):
    outs = []
    for c in range(y.shape[1] // LANES):
        yc = y[:, c * LANES:(c + 1) * LANES]
        outs.append(yc * cos + pltpu.roll(yc, LANES - ROT_DIM // 2, 1) * sin_a
                    + pltpu.roll(yc, ROT_DIM // 2, 1) * sin_b)
    return outs[0] if len(outs) == 1 else jnp.concatenate(outs, axis=1)


def _in_proj_kernel(x_ref, g_ref, wq_ref, wkk_ref, wt_ref, cos_ref, sa_ref, sb_ref, *refs, tm):
    (later_f32, (qa_ref, ka_ref, va_ref, qb_ref, ks_ref, kw_ref, vst_ref, vwt_ref, kcvc_ref, gt_ref), later_bf16) = (
        refs[:N_LATER], refs[N_LATER:-N_LATER], refs[-N_LATER:])
    for src, dst in zip(later_f32, later_bf16):
        dst[...] = src[...].astype(BF16)
    tt = pl.program_id(1)
    x = x_ref[...]
    ms = jnp.mean(x * x, axis=-1, keepdims=True)
    h = (x * lax.rsqrt(ms + EPS) * g_ref[...]).astype(BF16)
    cos, sa, sb = cos_ref[...], sa_ref[...], sb_ref[...]
    scale = Q_SCALE

    def proj(c0, c1):
        return _dot(h, wq_ref[:, c0:c1])

    o = 0
    qa_ref[...] = _rope_rows(proj(o, o + A_WIDTH), cos, sa, sb) * scale
    o += A_WIDTH
    ka_ref[...] = _rope_rows(proj(o, o + A_WIDTH), cos, sa, sb)
    o += A_WIDTH
    va_ref[...] = proj(o, o + A_WIDTH)
    o += A_WIDTH
    lane = lax.broadcasted_iota(jnp.int32, (tm, LANES), 1)
    row = lax.broadcasted_iota(jnp.int32, (tm, LANES), 0)
    lo = lane < HEAD_DIM
    qb = _rope_rows(proj(o, o + B_WIDTH), cos, sa, sb) * scale
    for hh in range(B_HEADS):
        ch = qb[:, (hh // 2) * LANES:(hh // 2 + 1) * LANES]
        if hh % 2:
            ch = pltpu.roll(ch, HEAD_DIM, 1)
        qb_ref[0, hh // B_GROUP, hh % B_GROUP] = jnp.where(lo, ch, 0.0).astype(BF16)
    o += B_WIDTH
    kcvc_ref[...] = proj(o, o + 2 * KV_WIDTH)
    o += 2 * KV_WIDTH
    ksw = _rope_rows(_dot(h, wkk_ref[...]), cos, sa, sb)

    blk = (tt * tm + row) >> SEL_SHIFT
    onehot = jnp.where(lane - SEL_LANE0 == blk, 1.0, 0.0)
    for kind, ref in ((0, ks_ref), (1, kw_ref)):
        kk = ksw[:, kind * LANES:(kind + 1) * LANES]
        tail = onehot if kind == 0 else 0.0
        ref[0, 0] = jnp.where(lo, kk, tail).astype(BF16)
        ref[0, 1] = jnp.where(lo, pltpu.roll(kk, HEAD_DIM, 1), tail).astype(BF16)

    tr = _dot_nt(wt_ref[...], h)
    ones_row = jnp.where(lax.broadcasted_iota(jnp.int32, (NSA_VROWS - HEAD_DIM, tm), 0) == 0, 1.0, 0.0)
    for kind, ref in ((0, vst_ref), (1, vwt_ref)):
        rows = [tr[kind * LANES + gg * HEAD_DIM:kind * LANES + (gg + 1) * HEAD_DIM] for gg in range(B_KV_HEADS)]
        slab = jnp.concatenate([rows[0], ones_row, rows[1], ones_row], axis=0).astype(BF16)
        for c in range(tm // NSA_CK):
            ref[0, c] = slab[:, c * NSA_CK:(c + 1) * NSA_CK]
    gt_ref[0, 0] = jax.nn.sigmoid(tr[2 * LANES:2 * LANES + GATE_ROWS, :])


def _rope_tables(pos):
    half = ROT_DIM // 2
    inv = ROPE_THETA ** (-np.arange(0, ROT_DIM, 2, dtype=np.float64) / ROT_DIM)
    ang = np.asarray(pos, np.float64)[:, None] * inv[None, :]
    cos, sin = np.cos(ang), np.sin(ang)
    n = len(pos)
    ones = np.ones((n, HEAD_DIM - ROT_DIM))
    zeros = np.zeros((n, HEAD_DIM - ROT_DIM))
    zh = np.zeros((n, half))
    c_head = np.concatenate([cos, cos, ones], axis=1)
    a_head = np.concatenate([-sin, zh, zeros], axis=1)
    b_head = np.concatenate([zh, sin, zeros], axis=1)
    rep = LANES // HEAD_DIM
    return tuple(jnp.asarray(np.tile(t, (1, rep)).astype(np.float32)) for t in (c_head, a_head, b_head))


def _in_proj(x, norm_g, w_in, tables, later, *, tm=1024):
    B, T, D = x.shape
    nt = T // tm
    assert len(later) == N_LATER and all(w.shape[0] % (16 * B * nt) == 0 for w in later)
    slab = lambda w: pl.BlockSpec((w.shape[0] // (B * nt), w.shape[1]), lambda b, t: (b * nt + t, 0))
    offs = [0]
    for n in (A_WIDTH, A_WIDTH, A_WIDTH, B_WIDTH, KV_WIDTH, KV_WIDTH, KV_WIDTH, KV_WIDTH, KV_WIDTH, KV_WIDTH,
              B_HEADS * N_BRANCH):
        offs.append(offs[-1] + n)
    col = lambda i: w_in[:, offs[i]:offs[i + 1]]
    wq = w_in[:, :offs[6]].astype(BF16)
    wkk = jnp.concatenate([col(6), col(8)], axis=1).astype(BF16)
    gpad = jnp.zeros((D, GATE_ROWS - B_HEADS * N_BRANCH), w_in.dtype)
    wt = jnp.concatenate([col(7), col(9), col(10), gpad], axis=1).T.astype(BF16)
    cos, sa, sb = tables
    nq = wq.shape[1]
    nr = wt.shape[0]
    tok = lambda w: pl.BlockSpec((None, tm, w), lambda b, t: (b, t, 0))
    const = lambda shape: pl.BlockSpec(shape, lambda b, t: (0,) * len(shape))
    tab = pl.BlockSpec((tm, LANES), lambda b, t: (t, 0))
    out_shapes = (
        jax.ShapeDtypeStruct((B, T, A_WIDTH), F32),
        jax.ShapeDtypeStruct((B, T, A_WIDTH), F32),
        jax.ShapeDtypeStruct((B, T, A_WIDTH), F32),
        jax.ShapeDtypeStruct((B, B_KV_HEADS, B_GROUP, T, LANES), BF16),
        jax.ShapeDtypeStruct((B, B_KV_HEADS, T, LANES), BF16),
        jax.ShapeDtypeStruct((B, B_KV_HEADS, T, LANES), BF16),
        jax.ShapeDtypeStruct((B, T // NSA_CK, B_KV_HEADS * NSA_VROWS, NSA_CK), BF16),
        jax.ShapeDtypeStruct((B, T // NSA_CK, B_KV_HEADS * NSA_VROWS, NSA_CK), BF16),
        jax.ShapeDtypeStruct((B, T, 2 * KV_WIDTH), F32),
        jax.ShapeDtypeStruct((B, nt, GATE_ROWS, tm), F32),
    )
    frame = pl.BlockSpec((1, B_KV_HEADS, tm, LANES), lambda b, t: (b, 0, t, 0))
    vt = pl.BlockSpec((1, tm // NSA_CK, B_KV_HEADS * NSA_VROWS, NSA_CK), lambda b, t: (b, t, 0, 0))
    qframe = pl.BlockSpec((1, B_KV_HEADS, B_GROUP, tm, LANES), lambda b, t: (b, 0, 0, t, 0))
    out_specs = (tok(A_WIDTH), tok(A_WIDTH), tok(A_WIDTH), qframe, frame, frame, vt, vt,
                 tok(2 * KV_WIDTH), pl.BlockSpec((1, 1, GATE_ROWS, tm), lambda b, t: (b, t, 0, 0)))
    return pl.pallas_call(
        functools.partial(_in_proj_kernel, tm=tm),
        grid=(B, nt),
        in_specs=[tok(D), const((1, D)), const((D, nq)), const((D, 2 * KV_WIDTH)), const((nr, D)), tab, tab, tab]
        + [slab(w) for w in later],
        out_specs=out_specs + tuple(slab(w) for w in later),
        out_shape=out_shapes + tuple(jax.ShapeDtypeStruct(w.shape, BF16) for w in later),
        compiler_params=pltpu.CompilerParams(dimension_semantics=("parallel", "parallel"),
                                             vmem_limit_bytes=VMEM_LIMIT),
        name="in_proj",
    )(x, norm_g.reshape(1, D), wq, wkk, wt, cos, sa, sb, *later)


def _compress_kernel(ak_ref, av_ref, w1k_ref, w1v_ref, pek_ref, pev_ref, w2k_ref, w2vt_ref, cos_ref, sa_ref, sb_ref,
                     kc_ref, vct_ref, *, nc_pad):
    half = CMP_LEN // 2
    hid_w = B_KV_HEADS * CMP_HIDDEN

    zeros = jnp.zeros((HEAD_DIM, CMP_HIDDEN), BF16)

    def both_groups(w):
        return jnp.concatenate([jnp.concatenate([w, zeros], axis=1), jnp.concatenate([zeros, w], axis=1)], axis=0)

    def hidden(a_ref, w1_ref, pe_ref):
        acc_u = jnp.zeros((nc_pad, hid_w), F32)
        acc_v = jnp.zeros((nc_pad, hid_w), F32)
        for p in range(half):
            ap = a_ref[0, pl.ds(p, nc_pad, stride=CMP_STRIDE), :]
            acc_u = acc_u + _dot((ap + pe_ref[p:p + 1, :]).astype(BF16), both_groups(w1_ref[p]))
            acc_v = acc_v + _dot((ap + pe_ref[half + p:half + p + 1, :]).astype(BF16), both_groups(w1_ref[half + p]))
        return jax.nn.gelu(acc_u + pltpu.roll(acc_v, nc_pad - 1, 0))

    hk = hidden(ak_ref, w1k_ref, pek_ref).astype(BF16)
    hv = hidden(av_ref, w1v_ref, pev_ref).astype(BF16)
    for g in range(B_KV_HEADS):
        hg = hk[:, g * CMP_HIDDEN:(g + 1) * CMP_HIDDEN]
        kc = _dot(hg, w2k_ref[...])
        kc_ref[0, g] = _rope_rows(kc, cos_ref[...], sa_ref[...], sb_ref[...]).astype(BF16)
        vg = hv[:, g * CMP_HIDDEN:(g + 1) * CMP_HIDDEN]
        vct_ref[0, g] = _dot_nt(w2vt_ref[...], vg).astype(BF16)


def _compress(kcvc, pe_k, w1_k, w2_k, pe_v, w1_v, w2_v, cmp_tables):
    B, T, _ = kcvc.shape
    nc_pad = T // CMP_STRIDE
    w2k = jnp.concatenate([w2_k, jnp.zeros_like(w2_k)], axis=1).astype(BF16)
    w2vt = w2_v.T.astype(BF16)
    pek = jnp.tile(pe_k, (1, B_KV_HEADS))
    pev = jnp.tile(pe_v, (1, B_KV_HEADS))
    per_pos = lambda w1: w1.reshape(CMP_LEN, HEAD_DIM, CMP_HIDDEN).astype(BF16)
    const = lambda shape: pl.BlockSpec(shape, lambda b: (0,) * len(shape))
    cos, sa, sb = cmp_tables
    return pl.pallas_call(
        functools.partial(_compress_kernel, nc_pad=nc_pad),
        grid=(B,),
        in_specs=[pl.BlockSpec((1, T, KV_WIDTH), lambda b: (b, 0, 0)), pl.BlockSpec((1, T, KV_WIDTH), lambda b: (b, 0, 1)),
                  const((CMP_LEN, HEAD_DIM, CMP_HIDDEN)), const((CMP_LEN, HEAD_DIM, CMP_HIDDEN)),
                  const((CMP_LEN, LANES)), const((CMP_LEN, LANES)),
                  const((CMP_HIDDEN, LANES)), const((HEAD_DIM, CMP_HIDDEN)),
                  const((nc_pad, LANES)), const((nc_pad, LANES)), const((nc_pad, LANES))],
        out_specs=(pl.BlockSpec((1, B_KV_HEADS, nc_pad, LANES), lambda b: (b, 0, 0, 0)),
                   pl.BlockSpec((1, B_KV_HEADS, HEAD_DIM, nc_pad), lambda b: (b, 0, 0, 0))),
        out_shape=(jax.ShapeDtypeStruct((B, B_KV_HEADS, nc_pad, LANES), BF16),
                   jax.ShapeDtypeStruct((B, B_KV_HEADS, HEAD_DIM, nc_pad), BF16)),
        compiler_params=pltpu.CompilerParams(dimension_semantics=("parallel",), vmem_limit_bytes=VMEM_LIMIT),
        name="compress",
    )(kcvc, kcvc, per_pos(w1_k), per_pos(w1_v), pek, pev, w2k, w2vt, cos, sa, sb)


def _mixer_a_kernel(q_ref, k_ref, v_ref, bias_ref, o_ref, qd0, qd1, kd, vd0, vd1, u_s, m_s, l_s, *, seq):
    blk = A_BLOCK
    nres = seq // blk
    npat = len(A_PATTERNS)
    order = sorted(range(npat), key=lambda p: -A_PATTERNS[p][1])
    slot = {p: n for n, p in enumerate(order[:-1])}
    lane = lax.broadcasted_iota(jnp.int32, (blk, LANES), 1)
    lo = lane < HEAD_DIM

    def deinterleave(r):
        rows = pl.ds(r * blk, blk)
        q, v = q_ref[0, pl.ds(r, blk, stride=nres), :], v_ref[0, pl.ds(r, blk, stride=nres), :]
        qd0[rows, :] = jnp.where(lo, q, 0.0)
        qd1[rows, :] = jnp.where(lo, 0.0, q)
        kd[rows, :] = k_ref[0, pl.ds(r, blk, stride=nres), :]
        vd0[rows, :] = jnp.where(lo, v, 1.0)
        vd1[rows, :] = jnp.where(lo, 1.0, v)

    def pieces(dil, rd, row_off, rows):
        return [pl.ds((rd + dil * jj) * blk + row_off, rows) for jj in range(nres // dil)]

    def gather(ref, idx, lead=()):
        parts = [ref[lead + (i, slice(None))] for i in idx]
        return parts[0] if len(parts) == 1 else jnp.concatenate(parts, axis=0)

    def attend(pi, blocks):
        dil = A_PATTERNS[pi][1]
        pr = blk // (nres // dil)
        q_idxs, vbs, scores = [], [], []
        for rd, n, first in blocks:
            q_idx = pieces(dil, rd, n * pr, pr)
            k_idx = q_idx if first else pieces(dil, rd, (n - 1) * pr, 2 * pr)
            bias = bias_ref[pi, :, 0:blk] if first else bias_ref[pi, :, blk:3 * blk]
            kb = gather(kd, k_idx).astype(BF16)
            q_idxs.append(q_idx)
            for qd, vd in ((qd0, vd0), (qd1, vd1)):
                vbs.append(gather(vd, k_idx).astype(BF16))
                scores.append(_dot_nt(gather(qd, q_idx).astype(BF16), kb) + bias)
        es, ms = [], []
        for s in scores:
            m = jnp.max(s, axis=-1, keepdims=True)
            ms.append(m)
            es.append(jnp.exp2((s - m).astype(BF16)))
        pvs = [_dot(e, vb) for e, vb in zip(es, vbs)]
        for b, q_idx in enumerate(q_idxs):
            u = jnp.where(lo, pvs[2 * b], pvs[2 * b + 1])
            l_swapped = jnp.where(lo, pvs[2 * b + 1], pvs[2 * b])
            m = jnp.where(lo, ms[2 * b], ms[2 * b + 1])
            if pi != order[-1]:
                for jj, idx in enumerate(q_idx):
                    u_s[slot[pi], idx, :] = u[jj * pr:(jj + 1) * pr]
                    m_s[slot[pi], idx, :] = m[jj * pr:(jj + 1) * pr]
                    l_s[slot[pi], idx, :] = l_swapped[jj * pr:(jj + 1) * pr]
                continue
            parts = [(u, m, l_swapped)] + [(gather(u_s, q_idx, (sl,)), gather(m_s, q_idx, (sl,)),
                                            gather(l_s, q_idx, (sl,))) for sl in slot.values()]
            m_all = functools.reduce(jnp.maximum, [mm for _, mm, _ in parts])
            num = jnp.zeros((blk, LANES), F32)
            den = jnp.zeros((blk, LANES), F32)
            for uu, mm, ll in parts:
                a = jnp.exp2(mm - m_all)
                num = num + a * uu
                den = den + a * pltpu.roll(ll, HEAD_DIM, 1)
            out = num / den
            n = blocks[b][1]
            for jj in range(nres):
                o_ref[0, pl.ds(n * blk + jj, pr, stride=nres), :] = out[jj * pr:(jj + 1) * pr]

    for pi in order:
        dil = A_PATTERNS[pi][1]
        nb = seq // dil // blk
        blocks = [(rd, n, n == 0) for rd in range(dil) for n in range(nb)]
        for g0 in range(0, len(blocks), A_GROUP):
            if pi == order[0]:
                for rd, _, _ in blocks[g0:g0 + A_GROUP]:
                    deinterleave(rd)
            attend(pi, blocks[g0:g0 + A_GROUP])


def _mixer_a_bias(nres):
    blk = A_BLOCK

    def sub_pos(i, fold, rows):
        return fold * (i % rows) + i // rows

    out = np.zeros((len(A_PATTERNS), blk, 3 * blk), np.float32)
    for pi, (window, dil) in enumerate(A_PATTERNS):
        n_back = window // dil
        fold = nres // dil
        pr = blk // fold
        sq = sub_pos(np.arange(blk), fold, pr)[:, None]
        d_first = sq - sub_pos(np.arange(blk), fold, pr)[None, :]
        d_band = sq + blk - sub_pos(np.arange(2 * blk), fold, 2 * pr)[None, :]
        dist = np.concatenate([d_first, d_band], axis=1)
        out[pi] = np.where((dist >= 0) & (dist <= n_back), 0.0, NEG)
    return jnp.asarray(out)


def _mixer_a(qa, ka, va):
    B, T, W = qa.shape
    npair = W // LANES
    spec = pl.BlockSpec((1, T, LANES), lambda b, p: (b, 0, p))
    npat = len(A_PATTERNS)
    nres = T // A_BLOCK
    dils = sorted(d for _, d in A_PATTERNS)
    assert all(nres % d == 0 and T % (d * A_BLOCK) == 0 for d in dils) and dils[0] == 1 and dils[-1] == nres
    return pl.pallas_call(
        functools.partial(_mixer_a_kernel, seq=T),
        grid=(B, npair),
        in_specs=[spec, spec, spec, pl.BlockSpec((npat, A_BLOCK, 3 * A_BLOCK), lambda b, p: (0, 0, 0))],
        out_specs=spec,
        out_shape=jax.ShapeDtypeStruct((B, T, W), F32),
        scratch_shapes=[pltpu.VMEM((T, LANES), F32)] * 5 + [pltpu.VMEM((npat - 1, T, LANES), F32)] * 3,
        compiler_params=pltpu.CompilerParams(dimension_semantics=("parallel", "parallel"),
                                             vmem_limit_bytes=VMEM_LIMIT),
        name="mixer_a",
    )(qa, ka, va, _mixer_a_bias(nres))


def _nsa_kernel(q_ref, ks_ref, kw_ref, vst_ref, vwt_ref, kc_ref, vct_ref, gt_ref, ovt_ref, bias_ref, cbias_ref, o_ref,
                *, n_sel_blocks, nq):
    tq, ck = NSA_TQ, NSA_CK
    nrow = B_GROUP * tq
    gw = B_GROUP * HEAD_DIM
    step = pl.program_id(1)
    per_part = nq // NSA_SPLIT
    lanes = [(h, g) for h in range(NSA_SPLIT) for g in range(B_KV_HEADS)]
    qi = [step + h * per_part for h in range(NSA_SPLIT)]
    col_i = lax.broadcasted_iota(jnp.int32, (1, nrow), 1) & (tq - 1)

    def chunk(ref, g, c):
        return ref[0, g, pl.ds(pl.multiple_of(c * ck, ck), ck), :]

    def vt_chunk(ref, g, c):
        return ref[0, c, g * NSA_VROWS:(g + 1) * NSA_VROWS, :]

    def softmax_pv(s, vt):
        m = jnp.max(s, axis=0, keepdims=True)
        return m, _dot(vt, jnp.exp2(s - m).astype(BF16))

    def merged(parts):
        m_new = functools.reduce(jnp.maximum, [m for m, _ in parts])
        acc = sum(jnp.exp2(m - m_new) * a for m, a in parts)
        return m_new, acc

    def normalised(state):
        return state[1][0:HEAD_DIM] * (1.0 / state[1][HEAD_DIM:HEAD_DIM + 1])

    q_b = {(h, g): jnp.concatenate([q_ref[0, g, r, h] for r in range(B_GROUP)], axis=0) for h, g in lanes}

    nwin = WIN // ck
    win_chunks = [[jnp.maximum(qi[h] - back, 0) for back in range(nwin, -1, -1)] for h in range(NSA_SPLIT)]
    s_win = {}
    for h, g in lanes:
        s = _dot_nt(jnp.concatenate([chunk(kw_ref, g, c) for c in win_chunks[h]], axis=0), q_b[h, g])
        exists = lambda back: True if h * per_part >= back else qi[h] >= back
        masked = lambda piece, back: piece if exists(back) is True else jnp.where(exists(back), piece, NEG)
        pieces = [masked(s[0:ck] + bias_ref[1], nwin)]
        for n in range(1, nwin):
            pieces.append(masked(s[n * ck:(n + 1) * ck], nwin - n))
        pieces.append(s[nwin * ck:] + bias_ref[0])
        s_win[h, g] = pieces

    s_cmp = {(h, g): _dot_nt(kc_ref[0, g], q_b[h, g]) + cbias_ref[h, 0] for h, g in lanes}
    p_cmp, o_cmp = {}, {}
    for h, g in lanes:
        m = jnp.max(s_cmp[h, g], axis=0, keepdims=True)
        e = jnp.exp2(s_cmp[h, g] - m)
        den = jnp.sum(e, axis=0, keepdims=True)
        sees_block = qi[h] * tq + col_i >= CMP_LEN - 1
        p_cmp[h, g] = e * jnp.where(sees_block, 1.0 / jnp.maximum(den, 1e-30), 0.0)
        o_cmp[h, g] = _dot(vct_ref[0, g], p_cmp[h, g].astype(BF16))

    imp = {}
    for h, g in lanes:
        psum = p_cmp[h, g][:, 0:tq]
        for r in range(1, B_GROUP):
            psum = psum + p_cmp[h, g][:, r * tq:(r + 1) * tq]
        p_hi = psum.astype(BF16)
        p_lo = (psum - p_hi.astype(F32)).astype(BF16)
        imp[h, g] = _dot(ovt_ref[...], p_hi) + _dot(ovt_ref[...], p_lo)
    j = lax.broadcasted_iota(jnp.int32, (n_sel_blocks, tq), 0)
    j_f = j.astype(F32)
    low = -3e38
    q_aug = {}
    for h, g in lanes:
        cur = (qi[h] * tq + lax.broadcasted_iota(jnp.int32, (n_sel_blocks, tq), 1)) >> SEL_SHIFT
        forced = (j == 0) | (j == cur) | (j == cur - 1)
        score = jnp.where(forced, imp[h, g] + 2.0, jnp.where(j > cur, -1.0, imp[h, g]))
        sel = jnp.zeros((n_sel_blocks, tq), jnp.bool_)
        for _ in range(min(SEL_TOPK, n_sel_blocks)):
            mx = jnp.max(score, axis=0, keepdims=True)
            first = jnp.min(jnp.where(score == mx, j_f, 4.0 * LANES), axis=0, keepdims=True)
            hit = j_f == first
            sel = sel | hit
            score = jnp.where(hit, low, score)
        selneg = jnp.concatenate([jnp.zeros((SEL_LANE0, tq), F32), jnp.where(sel, 0.0, NEG),
                                  jnp.zeros((LANES - SEL_LANE0 - n_sel_blocks, tq), F32)], axis=0).T
        q_aug[h, g] = q_b[h, g] + jnp.concatenate([selneg.astype(BF16)] * B_GROUP, axis=0)

    s_diag = {(h, g): _dot_nt(chunk(ks_ref, g, qi[h]), q_aug[h, g]) + bias_ref[0] for h, g in lanes}
    o_win = {(h, g): normalised(merged([softmax_pv(s_win[h, g][n], vt_chunk(vwt_ref, g, c))
                                        for n, c in enumerate(win_chunks[h])])) for h, g in lanes}
    states = tuple(softmax_pv(s_diag[h, g], vt_chunk(vst_ref, g, qi[h])) for h, g in lanes)

    def earlier(work, st):
        s = [_dot_nt(chunk(ks_ref, lanes[n][1], c), q_aug[lanes[n]]) for n, c in work]
        parts = [[st[n]] for n in range(len(lanes))]
        for (n, c), sc in zip(work, s):
            parts[n].append(softmax_pv(sc, vt_chunk(vst_ref, lanes[n][1], c)))
        return tuple(merged(p) if len(p) > 1 else p[0] for p in parts)

    states = lax.fori_loop(0, step, lambda c, st: earlier([(n, c) for n in range(len(lanes))], st), states)
    for h in range(1, NSA_SPLIT):
        mine = [n for n, (hh, _) in enumerate(lanes) if hh >= h]
        states = earlier([(n, step + (h - 1) * per_part + k) for n in mine for k in range(per_part)], states)

    for n, (h, g) in enumerate(lanes):
        o_sel = normalised(states[n])
        gt = gt_ref[0, h, g * B_GROUP * N_BRANCH:(g + 1) * B_GROUP * N_BRANCH, :]
        outs = []
        for r in range(B_GROUP):
            cs = slice(r * tq, (r + 1) * tq)
            outs.append(gt[r * N_BRANCH:r * N_BRANCH + 1, :] * o_cmp[h, g][:, cs]
                        + gt[r * N_BRANCH + 1:r * N_BRANCH + 2, :] * o_sel[:, cs]
                        + gt[r * N_BRANCH + 2:r * N_BRANCH + 3, :] * o_win[h, g][:, cs])
        for c in range(B_GROUP // 2):
            pair = jnp.concatenate([outs[2 * c], outs[2 * c + 1]], axis=0)
            o_ref[0, h, :, g * gw + c * LANES:g * gw + (c + 1) * LANES] = pair.T


def _overlap_t(nc_pad, ns):
    nc = nc_pad - 1
    c0 = np.arange(nc_pad) * CMP_STRIDE
    s0 = np.arange(ns) * SEL_BLOCK
    ov = np.minimum(c0[None, :] + CMP_LEN, s0[:, None] + SEL_BLOCK) - np.maximum(c0[None, :], s0[:, None])
    ov = np.clip(ov, 0, None).astype(np.float32) / CMP_LEN
    ov = np.where(np.arange(nc_pad)[None, :] < nc, ov, 0.0)
    return jnp.asarray(ov, BF16)


def _nsa(qb, ks, kw, vst, vwt, kc, vct, gt):
    B, _, _, T, _ = qb.shape
    W = B_WIDTH
    nq = T // NSA_TQ
    ns = T // SEL_BLOCK
    nc_pad = kc.shape[2]
    nrow = B_GROUP * NSA_TQ
    part = T // NSA_SPLIT
    assert NSA_TQ == NSA_CK and WIN % NSA_CK == 0 and SEL_LANE0 + ns <= LANES and (nq // NSA_SPLIT) % 2 == 0
    assert gt.shape == (B, NSA_SPLIT, GATE_ROWS, part)
    kspec = pl.BlockSpec((1, B_KV_HEADS, T, LANES), lambda b, i: (b, 0, 0, 0))
    vspec = pl.BlockSpec((1, T // NSA_CK, B_KV_HEADS * NSA_VROWS, NSA_CK), lambda b, i: (b, 0, 0, 0))
    key = np.arange(NSA_CK)[:, None]
    qry = (np.arange(nrow) % NSA_TQ)[None, :]
    bias = jnp.asarray(np.stack([np.where(key <= qry, 0.0, NEG), np.where(key > qry, 0.0, NEG)]).astype(np.float32))
    cmp_end = (np.arange(nc_pad) * CMP_STRIDE + CMP_LEN - 1)[None, :, None]
    t_query = (np.arange(nq) * NSA_TQ)[:, None, None] + qry[None]
    cbias = jnp.asarray(np.where((cmp_end <= t_query) & (np.arange(nc_pad) < nc_pad - 1)[None, :, None], 0.0, NEG)
                        .astype(np.float32)).reshape(NSA_SPLIT, nq // NSA_SPLIT, nc_pad, nrow)
    out = pl.pallas_call(
        functools.partial(_nsa_kernel, n_sel_blocks=ns, nq=nq),
        grid=(B, nq // NSA_SPLIT),
        in_specs=[pl.BlockSpec((1, B_KV_HEADS, B_GROUP, NSA_SPLIT, NSA_TQ, LANES), lambda b, i: (b, 0, 0, 0, i, 0)),
                  kspec, kspec, vspec, vspec,
                  pl.BlockSpec((1, B_KV_HEADS, nc_pad, LANES), lambda b, i: (b, 0, 0, 0)),
                  pl.BlockSpec((1, B_KV_HEADS, HEAD_DIM, nc_pad), lambda b, i: (b, 0, 0, 0)),
                  pl.BlockSpec((1, NSA_SPLIT, GATE_ROWS, NSA_TQ), lambda b, i: (b, 0, 0, i)),
                  pl.BlockSpec((ns, nc_pad), lambda b, i: (0, 0)),
                  pl.BlockSpec((2, NSA_CK, nrow), lambda b, i: (0, 0, 0)),
                  pl.BlockSpec((NSA_SPLIT, 1, nc_pad, nrow), lambda b, i: (0, i, 0, 0))],
        out_specs=pl.BlockSpec((1, NSA_SPLIT, NSA_TQ, W), lambda b, i: (b, 0, i, 0)),
        out_shape=jax.ShapeDtypeStruct((B, NSA_SPLIT, part, W), F32),
        compiler_params=pltpu.CompilerParams(dimension_semantics=("parallel", "arbitrary"),
                                             vmem_limit_bytes=VMEM_LIMIT),
        name="nsa",
    )(qb.reshape(B, B_KV_HEADS, B_GROUP, NSA_SPLIT, part, LANES), ks, kw, vst, vwt, kc, vct,
      gt, _overlap_t(nc_pad, ns), bias, cbias)
    return out.reshape(B, T, W)


def _post_kernel(x_ref, oa_ref, ob_ref, ga_ref, gb_ref, wo_ref, gm_ref, wu_ref, wd_ref, gf_ref, o_ref, *, final):
    def norm(v, g):
        return v * lax.rsqrt(jnp.mean(v * v, axis=-1, keepdims=True) + EPS) * g

    na = norm(oa_ref[...], ga_ref[...]).astype(BF16)
    nb = norm(ob_ref[...], gb_ref[...]).astype(BF16)
    aw = na.shape[1]
    h_res = x_ref[...] + _dot(na, wo_ref[0:aw, :]) + _dot(nb, wo_ref[aw:, :])
    h = norm(h_res, gm_ref[...]).astype(BF16)
    u = jnp.square(jnp.maximum(_dot(h, wu_ref[...]), 0.0)).astype(BF16)
    acc = h_res + _dot(u, wd_ref[...])
    o_ref[...] = norm(acc, gf_ref[...]) if final else acc


def _post(x, oa, ob, g_a, g_b, w_out, g_mlp, w_up, w_down, g_final, *, final, tm=512):
    B, T, D = x.shape
    n = B * T
    dff = w_up.shape[1]
    tok = lambda w: pl.BlockSpec((tm, w), lambda i: (i, 0))
    const = lambda shape: pl.BlockSpec(shape, lambda i: (0, 0), pipeline_mode=pl.Buffered(1))
    out = pl.pallas_call(
        functools.partial(_post_kernel, final=final),
        grid=(n // tm,),
        in_specs=[tok(D), tok(A_WIDTH), tok(B_WIDTH), const((1, A_WIDTH)), const((1, B_WIDTH)),
                  const((A_WIDTH + B_WIDTH, D)), const((1, D)), const((D, dff)), const((dff, D)), const((1, D))],
        out_specs=tok(D),
        out_shape=jax.ShapeDtypeStruct((n, D), F32),
        compiler_params=pltpu.CompilerParams(dimension_semantics=("parallel",), vmem_limit_bytes=VMEM_LIMIT),
        name="post",
    )(x.reshape(n, D), oa.reshape(n, A_WIDTH), ob.reshape(n, B_WIDTH), g_a.reshape(1, -1), g_b.reshape(1, -1),
      w_out.astype(BF16), g_mlp.reshape(1, D), w_up.astype(BF16), w_down.astype(BF16), g_final.reshape(1, D))
    return out.reshape(B, T, D)


def kernel(x, norm_mix, w_in, cmp_pe_k, cmp_w1_k, cmp_w2_k, cmp_pe_v, cmp_w1_v, cmp_w2_v, g_out_a, g_out_b,
           w_out, norm_mlp, w_up, w_down, norm_final):
    B, T, D = x.shape
    depth = w_in.shape[0]
    tables = _rope_tables(np.arange(T))
    cmp_tables = _rope_tables(np.arange(T // CMP_STRIDE) * CMP_STRIDE + CMP_LEN - 1)
    h_res = x
    for l in range(depth):
        qa, ka, va, qb, ks, kw, vst, vwt, kcvc, gt, wo_b, wu_b, wd_b, w1k_b, w1v_b = _in_proj(
            h_res, norm_mix[l], w_in[l], tables, (w_out[l], w_up[l], w_down[l], cmp_w1_k[l], cmp_w1_v[l]))
        kc, vct = _compress(kcvc, cmp_pe_k[l], w1k_b, cmp_w2_k[l], cmp_pe_v[l], w1v_b, cmp_w2_v[l], cmp_tables)
        oa = _mixer_a(qa, ka, va)
        ob = _nsa(qb, ks, kw, vst, vwt, kc, vct, gt)
        h_res = _post(h_res, oa, ob, g_out_a[l], g_out_b[l], wo_b, norm_mlp[l], wu_b, wd_b, norm_final,
                      final=(l == depth - 1))
    return h_res
```

```python
import functools

import jax
import jax.numpy as jnp
import numpy as np
from jax import lax
from jax.experimental import pallas as pl
from jax.experimental.pallas import tpu as pltpu

F32 = jnp.float32
BF16 = jnp.bfloat16

HEAD_DIM = 64
ROT_DIM = HEAD_DIM // 4
ROPE_THETA = 500000.0
EPS = 1e-6
NEG = -1e30
Q_SCALE = HEAD_DIM ** -0.5 * 1.4426950408889634
LANES = 128

A_HEADS = 8
A_PATTERNS = ((128, 1), (512, 4), (2048, 16))
A_BLOCK = 128
A_GROUP = 16

B_HEADS = 8
B_KV_HEADS = 2
B_GROUP = B_HEADS // B_KV_HEADS
CMP_LEN = 32
CMP_STRIDE = 16
CMP_HIDDEN = 256
SEL_BLOCK = 64
SEL_SHIFT = 6
SEL_TOPK = 8
WIN = 512
N_BRANCH = 3

A_WIDTH = A_HEADS * HEAD_DIM
B_WIDTH = B_HEADS * HEAD_DIM
KV_WIDTH = B_KV_HEADS * HEAD_DIM

NSA_TQ = 256
NSA_CK = 256
NSA_SPLIT = 2
NSA_VROWS = 80
GATE_ROWS = 32
SEL_LANE0 = HEAD_DIM

N_LATER = 5

VMEM_LIMIT = 56 * 1024 * 1024


def _dot(a, b):
    return jnp.dot(a, b, preferred_element_type=F32)


def _dot_nt(a, b):
    return lax.dot_general(a, b, (((1,), (1,)), ((), ())), preferred_element_type=F32)


def _rope_rows(y, cos, sin_a, sin_b):
    outs = []
    for c in range(y.shape[1] // LANES):
        yc = y[:, c * LANES:(c + 1) * LANES]
        outs.append(yc * cos + pltpu.roll(yc, LANES - ROT_DIM // 2, 1) * sin_a
                    + pltpu.roll(yc, ROT_DIM // 2, 1) * sin_b)
    return outs[0] if len(outs) == 1 else jnp.concatenate(outs, axis=1)


def _in_proj_kernel(x_ref, g_ref, wq_ref, wkk_ref, wt_ref, cos_ref, sa_ref, sb_ref, *refs, tm):
    (later_f32, (qa_ref, ka_ref, va_ref, qb_ref, ks_ref, kw_ref, vst_ref, vwt_ref, kcvc_ref, gt_ref), later_bf16) = (
        refs[:N_LATER], refs[N_LATER:-N_LATER], refs[-N_LATER:])
    for src, dst in zip(later_f32, later_bf16):
        dst[...] = src[...].astype(BF16)
    tt = pl.program_id(1)
    x = x_ref[...]
    ms = jnp.mean(x * x, axis=-1, keepdims=True)
    h = (x * lax.rsqrt(ms + EPS) * g_ref[...]).astype(BF16)
    cos, sa, sb = cos_ref[...], sa_ref[...], sb_ref[...]
    scale = Q_SCALE

    def proj(c0, c1):
        return _dot(h, wq_ref[:, c0:c1])

    o = 0
    qa_ref[...] = _rope_rows(proj(o, o + A_WIDTH), cos, sa, sb) * scale
    o += A_WIDTH
    ka_ref[...] = _rope_rows(proj(o, o + A_WIDTH), cos, sa, sb)
    o += A_WIDTH
    va_ref[...] = proj(o, o + A_WIDTH)
    o += A_WIDTH
    lane = lax.broadcasted_iota(jnp.int32, (tm, LANES), 1)
    row = lax.broadcasted_iota(jnp.int32, (tm, LANES), 0)
    lo = lane < HEAD_DIM
    qb = _rope_rows(proj(o, o + B_WIDTH), cos, sa, sb) * scale
    for hh in range(B_HEADS):
        ch = qb[:, (hh // 2) * LANES:(hh // 2 + 1) * LANES]
        if hh % 2:
            ch = pltpu.roll(ch, HEAD_DIM, 1)
        qb_ref[0, hh // B_GROUP, hh % B_GROUP] = jnp.where(lo, ch, 0.0).astype(BF16)
    o += B_WIDTH
    kcvc_ref[...] = proj(o, o + 2 * KV_WIDTH)
    o += 2 * KV_WIDTH
    ksw = _rope_rows(_dot(h, wkk_ref[...]), cos, sa, sb)

    blk = (tt * tm + row) >> SEL_SHIFT
    onehot = jnp.where(lane - SEL_LANE0 == blk, 1.0, 0.0)
    for kind, ref in ((0, ks_ref), (1, kw_ref)):
        kk = ksw[:, kind * LANES:(kind + 1) * LANES]
        tail = onehot if kind == 0 else 0.0
        ref[0, 0] = jnp.where(lo, kk, tail).astype(BF16)
        ref[0, 1] = jnp.where(lo, pltpu.roll(kk, HEAD_DIM, 1), tail).astype(BF16)

    tr = _dot_nt(wt_ref[...], h)
    ones_row = jnp.where(lax.broadcasted_iota(jnp.int32, (NSA_VROWS - HEAD_DIM, tm), 0) == 0, 1.0, 0.0)
    for kind, ref in ((0, vst_ref), (1, vwt_ref)):
        rows = [tr[kind * LANES + gg * HEAD_DIM:kind * LANES + (gg + 1) * HEAD_DIM] for gg in range(B_KV_HEADS)]
        slab = jnp.concatenate([rows[0], ones_row, rows[1], ones_row], axis=0).astype(BF16)
        for c in range(tm // NSA_CK):
            ref[0, c] = slab[:, c * NSA_CK:(c + 1) * NSA_CK]
    gt_ref[0, 0] = jax.nn.sigmoid(tr[2 * LANES:2 * LANES + GATE_ROWS, :])


def _rope_tables(pos):
    half = ROT_DIM // 2
    inv = ROPE_THETA ** (-np.arange(0, ROT_DIM, 2, dtype=np.float64) / ROT_DIM)
    ang = np.asarray(pos, np.float64)[:, None] * inv[None, :]
    cos, sin = np.cos(ang), np.sin(ang)
    n = len(pos)
    ones = np.ones((n, HEAD_DIM - ROT_DIM))
    zeros = np.zeros((n, HEAD_DIM - ROT_DIM))
    zh = np.zeros((n, half))
    c_head = np.concatenate([cos, cos, ones], axis=1)
    a_head = np.concatenate([-sin, zh, zeros], axis=1)
    b_head = np.concatenate([zh, sin, zeros], axis=1)
    rep = LANES // HEAD_DIM
    return tuple(jnp.asarray(np.tile(t, (1, rep)).astype(np.float32)) for t in (c_head, a_head, b_head))


def _in_proj(x, norm_g, w_in, tables, later, *, tm=1024):
    B, T, D = x.shape
    nt = T // tm
    assert len(later) == N_LATER and all(w.shape[0] % (16 * B * nt) == 0 for w in later)
    slab = lambda w: pl.BlockSpec((w.shape[0] // (B * nt), w.shape[1]), lambda b, t: (b * nt + t, 0))
    offs = [0]
    for n in (A_WIDTH, A_WIDTH, A_WIDTH, B_WIDTH, KV_WIDTH, KV_WIDTH, KV_WIDTH, KV_WIDTH, KV_WIDTH, KV_WIDTH,
              B_HEADS * N_BRANCH):
        offs.append(offs[-1] + n)
    col = lambda i: w_in[:, offs[i]:offs[i + 1]]
    wq = w_in[:, :offs[6]].astype(BF16)
    wkk = jnp.concatenate([col(6), col(8)], axis=1).astype(BF16)
    gpad = jnp.zeros((D, GATE_ROWS - B_HEADS * N_BRANCH), w_in.dtype)
    wt = jnp.concatenate([col(7), col(9), col(10), gpad], axis=1).T.astype(BF16)
    cos, sa, sb = tables
    nq = wq.shape[1]
    nr = wt.shape[0]
    tok = lambda w: pl.BlockSpec((None, tm, w), lambda b, t: (b, t, 0))
    const = lambda shape: pl.BlockSpec(shape, lambda b, t: (0,) * len(shape))
    tab = pl.BlockSpec((tm, LANES), lambda b, t: (t, 0))
    out_shapes = (
        jax.ShapeDtypeStruct((B, T, A_WIDTH), F32),
        jax.ShapeDtypeStruct((B, T, A_WIDTH), F32),
        jax.ShapeDtypeStruct((B, T, A_WIDTH), F32),
        jax.ShapeDtypeStruct((B, B_KV_HEADS, B_GROUP, T, LANES), BF16),
        jax.ShapeDtypeStruct((B, B_KV_HEADS, T, LANES), BF16),
        jax.ShapeDtypeStruct((B, B_KV_HEADS, T, LANES), BF16),
        jax.ShapeDtypeStruct((B, T // NSA_CK, B_KV_HEADS * NSA_VROWS, NSA_CK), BF16),
        jax.ShapeDtypeStruct((B, T // NSA_CK, B_KV_HEADS * NSA_VROWS, NSA_CK), BF16),
        jax.ShapeDtypeStruct((B, T, 2 * KV_WIDTH), F32),
        jax.ShapeDtypeStruct((B, nt, GATE_ROWS, tm), F32),
    )
    frame = pl.BlockSpec((1, B_KV_HEADS, tm, LANES), lambda b, t: (b, 0, t, 0))
    vt = pl.BlockSpec((1, tm // NSA_CK, B_KV_HEADS * NSA_VROWS, NSA_CK), lambda b, t: (b, t, 0, 0))
    qframe = pl.BlockSpec((1, B_KV_HEADS, B_GROUP, tm, LANES), lambda b, t: (b, 0, 0, t, 0))
    out_specs = (tok(A_WIDTH), tok(A_WIDTH), tok(A_WIDTH), qframe, frame, frame, vt, vt,
                 tok(2 * KV_WIDTH), pl.BlockSpec((1, 1, GATE_ROWS, tm), lambda b, t: (b, t, 0, 0)))
    return pl.pallas_call(
        functools.partial(_in_proj_kernel, tm=tm),
        grid=(B, nt),
        in_specs=[tok(D), const((1, D)), const((D, nq)), const((D, 2 * KV_WIDTH)), const((nr, D)), tab, tab, tab]
        + [slab(w) for w in later],
        out_specs=out_specs + tuple(slab(w) for w in later),
        out_shape=out_shapes + tuple(jax.ShapeDtypeStruct(w.shape, BF16) for w in later),
        compiler_params=pltpu.CompilerParams(dimension_semantics=("parallel", "parallel"),
                                             vmem_limit_bytes=VMEM_LIMIT),
        name="in_proj",
    )(x, norm_g.reshape(1, D), wq, wkk, wt, cos, sa, sb, *later)


def _compress_kernel(ak_ref, av_ref, w1k_ref, w1v_ref, pek_ref, pev_ref, w2k_ref, w2vt_ref, cos_ref, sa_ref, sb_ref,
                     kc_ref, vct_ref, *, nc_pad):
    half = CMP_LEN // 2
    hid_w = B_KV_HEADS * CMP_HIDDEN

    zeros = jnp.zeros((HEAD_DIM, CMP_HIDDEN), BF16)

    def both_groups(w):
        return jnp.concatenate([jnp.concatenate([w, zeros], axis=1), jnp.concatenate([zeros, w], axis=1)], axis=0)

    def hidden(a_ref, w1_ref, pe_ref):
        acc_u = jnp.zeros((nc_pad, hid_w), F32)
        acc_v = jnp.zeros((nc_pad, hid_w), F32)
        for p in range(half):
            ap = a_ref[0, pl.ds(p, nc_pad, stride=CMP_STRIDE), :]
            acc_u = acc_u + _dot((ap + pe_ref[p:p + 1, :]).astype(BF16), both_groups(w1_ref[p]))
            acc_v = acc_v + _dot((ap + pe_ref[half + p:half + p + 1, :]).astype(BF16), both_groups(w1_ref[half + p]))
        return jax.nn.gelu(acc_u + pltpu.roll(acc_v, nc_pad - 1, 0))

    hk = hidden(ak_ref, w1k_ref, pek_ref).astype(BF16)
    hv = hidden(av_ref, w1v_ref, pev_ref).astype(BF16)
    for g in range(B_KV_HEADS):
        hg = hk[:, g * CMP_HIDDEN:(g + 1) * CMP_HIDDEN]
        kc = _dot(hg, w2k_ref[...])
        kc_ref[0, g] = _rope_rows(kc, cos_ref[...], sa_ref[...], sb_ref[...]).astype(BF16)
        vg = hv[:, g * CMP_HIDDEN:(g + 1) * CMP_HIDDEN]
        vct_ref[0, g] = _dot_nt(w2vt_ref[...], vg).astype(BF16)


def _compress(kcvc, pe_k, w1_k, w2_k, pe_v, w1_v, w2_v, cmp_tables):
    B, T, _ = kcvc.shape
    nc_pad = T // CMP_STRIDE
    w2k = jnp.concatenate([w2_k, jnp.zeros_like(w2_k)], axis=1).astype(BF16)
    w2vt = w2_v.T.astype(BF16)
    pek = jnp.tile(pe_k, (1, B_KV_HEADS))
    pev = jnp.tile(pe_v, (1, B_KV_HEADS))
    per_pos = lambda w1: w1.reshape(CMP_LEN, HEAD_DIM, CMP_HIDDEN).astype(BF16)
    const = lambda shape: pl.BlockSpec(shape, lambda b: (0,) * len(shape))
    cos, sa, sb = cmp_tables
    return pl.pallas_call(
        functools.partial(_compress_kernel, nc_pad=nc_pad),
        grid=(B,),
        in_specs=[pl.BlockSpec((1, T, KV_WIDTH), lambda b: (b, 0, 0)), pl.BlockSpec((1, T, KV_WIDTH), lambda b: (b, 0, 1)),
                  const((CMP_LEN, HEAD_DIM, CMP_HIDDEN)), const((CMP_LEN, HEAD_DIM, CMP_HIDDEN)),
                  const((CMP_LEN, LANES)), const((CMP_LEN, LANES)),
                  const((CMP_HIDDEN, LANES)), const((HEAD_DIM, CMP_HIDDEN)),
                  const((nc_pad, LANES)), const((nc_pad, LANES)), const((nc_pad, LANES))],
        out_specs=(pl.BlockSpec((1, B_KV_HEADS, nc_pad, LANES), lambda b: (b, 0, 0, 0)),
                   pl.BlockSpec((1, B_KV_HEADS, HEAD_DIM, nc_pad), lambda b: (b, 0, 0, 0))),
        out_shape=(jax.ShapeDtypeStruct((B, B_KV_HEADS, nc_pad, LANES), BF16),
                   jax.ShapeDtypeStruct((B, B_KV_HEADS, HEAD_DIM, nc_pad), BF16)),
        compiler_params=pltpu.CompilerParams(dimension_semantics=("parallel",), vmem_limit_bytes=VMEM_LIMIT),
        name="compress",
    )(kcvc, kcvc, per_pos(w1_k), per_pos(w1_v), pek, pev, w2k, w2vt, cos, sa, sb)


def _mixer_a_kernel(q_ref, k_ref, v_ref, bias_ref, o_ref, qd0, qd1, kd, vd0, vd1, u_s, m_s, l_s, *, seq):
    blk = A_BLOCK
    nres = seq // blk
    npat = len(A_PATTERNS)
    order = sorted(range(npat), key=lambda p: -A_PATTERNS[p][1])
    slot = {p: n for n, p in enumerate(order[:-1])}
    lane = lax.broadcasted_iota(jnp.int32, (blk, LANES), 1)
    lo = lane < HEAD_DIM

    def deinterleave(r):
        rows = pl.ds(r * blk, blk)
        q, v = q_ref[0, pl.ds(r, blk, stride=nres), :], v_ref[0, pl.ds(r, blk, stride=nres), :]
        qd0[rows, :] = jnp.where(lo, q, 0.0)
        qd1[rows, :] = jnp.where(lo, 0.0, q)
        kd[rows, :] = k_ref[0, pl.ds(r, blk, stride=nres), :]
        vd0[rows, :] = jnp.where(lo, v, 1.0)
        vd1[rows, :] = jnp.where(lo, 1.0, v)

    def pieces(dil, rd, row_off, rows):
        return [pl.ds((rd + dil * jj) * blk + row_off, rows) for jj in range(nres // dil)]

    def gather(ref, idx, lead=()):
        parts = [ref[lead + (i, slice(None))] for i in idx]
        return parts[0] if len(parts) == 1 else jnp.concatenate(parts, axis=0)

    def attend(pi, blocks):
        dil = A_PATTERNS[pi][1]
        pr = blk // (nres // dil)
        q_idxs, vbs, scores = [], [], []
        for rd, n, first in blocks:
            q_idx = pieces(dil, rd, n * pr, pr)
            k_idx = q_idx if first else pieces(dil, rd, (n - 1) * pr, 2 * pr)
            bias = bias_ref[pi, :, 0:blk] if first else bias_ref[pi, :, blk:3 * blk]
            kb = gather(kd, k_idx).astype(BF16)
            q_idxs.append(q_idx)
            for qd, vd in ((qd0, vd0), (qd1, vd1)):
                vbs.append(gather(vd, k_idx).astype(BF16))
                scores.append(_dot_nt(gather(qd, q_idx).astype(BF16), kb) + bias)
        es, ms = [], []
        for s in scores:
            m = jnp.max(s, axis=-1, keepdims=True)
            ms.append(m)
            es.append(jnp.exp2((s - m).astype(BF16)))
        pvs = [_dot(e, vb) for e, vb in zip(es, vbs)]
        for b, q_idx in enumerate(q_idxs):
            u = jnp.where(lo, pvs[2 * b], pvs[2 * b + 1])
            l_swapped = jnp.where(lo, pvs[2 * b + 1], pvs[2 * b])
            m = jnp.where(lo, ms[2 * b], ms[2 * b + 1])
            if pi != order[-1]:
                for jj, idx in enumerate(q_idx):
                    u_s[slot[pi], idx, :] = u[jj * pr:(jj + 1) * pr]
                    m_s[slot[pi], idx, :] = m[jj * pr:(jj + 1) * pr]
                    l_s[slot[pi], idx, :] = l_swapped[jj * pr:(jj + 1) * pr]
                continue
            parts = [(u, m, l_swapped)] + [(gather(u_s, q_idx, (sl,)), gather(m_s, q_idx, (sl,)),
                                            gather(l_s, q_idx, (sl,))) for sl in slot.values()]
            m_all = functools.reduce(jnp.maximum, [mm for _, mm, _ in parts])
            num = jnp.zeros((blk, LANES), F32)
            den = jnp.zeros((blk, LANES), F32)
            for uu, mm, ll in parts:
                a = jnp.exp2(mm - m_all)
                num = num + a * uu
                den = den + a * pltpu.roll(ll, HEAD_DIM, 1)
            out = num / den
            n = blocks[b][1]
            for jj in range(nres):
                o_ref[0, pl.ds(n * blk + jj, pr, stride=nres), :] = out[jj * pr:(jj + 1) * pr]

    for pi in order:
        dil = A_PATTERNS[pi][1]
        nb = seq // dil // blk
        blocks = [(rd, n, n == 0) for rd in range(dil) for n in range(nb)]
        for g0 in range(0, len(blocks), A_GROUP):
            if pi == order[0]:
                for rd, _, _ in blocks[g0:g0 + A_GROUP]:
                    deinterleave(rd)
            attend(pi, blocks[g0:g0 + A_GROUP])


def _mixer_a_bias(nres):
    blk = A_BLOCK

    def sub_pos(i, fold, rows):
        return fold * (i % rows) + i // rows

    out = np.zeros((len(A_PATTERNS), blk, 3 * blk), np.float32)
    for pi, (window, dil) in enumerate(A_PATTERNS):
        n_back = window // dil
        fold = nres // dil
        pr = blk // fold
        sq = sub_pos(np.arange(blk), fold, pr)[:, None]
        d_first = sq - sub_pos(np.arange(blk), fold, pr)[None, :]
        d_band = sq + blk - sub_pos(np.arange(2 * blk), fold, 2 * pr)[None, :]
        dist = np.concatenate([d_first, d_band], axis=1)
        out[pi] = np.where((dist >= 0) & (dist <= n_back), 0.0, NEG)
    return jnp.asarray(out)


def _mixer_a(qa, ka, va):
    B, T, W = qa.shape
    npair = W // LANES
    spec = pl.BlockSpec((1, T, LANES), lambda b, p: (b, 0, p))
    npat = len(A_PATTERNS)
    nres = T // A_BLOCK
    dils = sorted(d for _, d in A_PATTERNS)
    assert all(nres % d == 0 and T % (d * A_BLOCK) == 0 for d in dils) and dils[0] == 1 and dils[-1] == nres
    return pl.pallas_call(
        functools.partial(_mixer_a_kernel, seq=T),
        grid=(B, npair),
        in_specs=[spec, spec, spec, pl.BlockSpec((npat, A_BLOCK, 3 * A_BLOCK), lambda b, p: (0, 0, 0))],
        out_specs=spec,
        out_shape=jax.ShapeDtypeStruct((B, T, W), F32),
        scratch_shapes=[pltpu.VMEM((T, LANES), F32)] * 5 + [pltpu.VMEM((npat - 1, T, LANES), F32)] * 3,
        compiler_params=pltpu.CompilerParams(dimension_semantics=("parallel", "parallel"),
                                             vmem_limit_bytes=VMEM_LIMIT),
        name="mixer_a",
    )(qa, ka, va, _mixer_a_bias(nres))


def _nsa_kernel(q_ref, ks_ref, kw_ref, vst_ref, vwt_ref, kc_ref, vct_ref, gt_ref, ovt_ref, bias_ref, cbias_ref, o_ref,
                *, n_sel_blocks, nq):
    tq, ck = NSA_TQ, NSA_CK
    nrow = B_GROUP * tq
    gw = B_GROUP * HEAD_DIM
    step = pl.program_id(1)
    per_part = nq // NSA_SPLIT
    lanes = [(h, g) for h in range(NSA_SPLIT) for g in range(B_KV_HEADS)]
    qi = [step + h * per_part for h in range(NSA_SPLIT)]
    col_i = lax.broadcasted_iota(jnp.int32, (1, nrow), 1) & (tq - 1)

    def chunk(ref, g, c):
        return ref[0, g, pl.ds(pl.multiple_of(c * ck, ck), ck), :]

    def vt_chunk(ref, g, c):
        return ref[0, c, g * NSA_VROWS:(g + 1) * NSA_VROWS, :]

    def softmax_pv(s, vt):
        m = jnp.max(s, axis=0, keepdims=True)
        return m, _dot(vt, jnp.exp2(s - m).astype(BF16))

    def merged(parts):
        m_new = functools.reduce(jnp.maximum, [m for m, _ in parts])
        acc = sum(jnp.exp2(m - m_new) * a for m, a in parts)
        return m_new, acc

    def normalised(state):
        return state[1][0:HEAD_DIM] * (1.0 / state[1][HEAD_DIM:HEAD_DIM + 1])

    q_b = {(h, g): jnp.concatenate([q_ref[0, g, r, h] for r in range(B_GROUP)], axis=0) for h, g in lanes}

    nwin = WIN // ck
    win_chunks = [[jnp.maximum(qi[h] - back, 0) for back in range(nwin, -1, -1)] for h in range(NSA_SPLIT)]
    s_win = {}
    for h, g in lanes:
        s = _dot_nt(jnp.concatenate([chunk(kw_ref, g, c) for c in win_chunks[h]], axis=0), q_b[h, g])
        exists = lambda back: True if h * per_part >= back else qi[h] >= back
        masked = lambda piece, back: piece if exists(back) is True else jnp.where(exists(back), piece, NEG)
        pieces = [masked(s[0:ck] + bias_ref[1], nwin)]
        for n in range(1, nwin):
            pieces.append(masked(s[n * ck:(n + 1) * ck], nwin - n))
        pieces.append(s[nwin * ck:] + bias_ref[0])
        s_win[h, g] = pieces

    s_cmp = {(h, g): _dot_nt(kc_ref[0, g], q_b[h, g]) + cbias_ref[h, 0] for h, g in lanes}
    p_cmp, o_cmp = {}, {}
    for h, g in lanes:
        m = jnp.max(s_cmp[h, g], axis=0, keepdims=True)
        e = jnp.exp2(s_cmp[h, g] - m)
        den = jnp.sum(e, axis=0, keepdims=True)
        sees_block = qi[h] * tq + col_i >= CMP_LEN - 1
        p_cmp[h, g] = e * jnp.where(sees_block, 1.0 / jnp.maximum(den, 1e-30), 0.0)
        o_cmp[h, g] = _dot(vct_ref[0, g], p_cmp[h, g].astype(BF16))

    imp = {}
    for h, g in lanes:
        psum = p_cmp[h, g][:, 0:tq]
        for r in range(1, B_GROUP):
            psum = psum + p_cmp[h, g][:, r * tq:(r + 1) * tq]
        p_hi = psum.astype(BF16)
        p_lo = (psum - p_hi.astype(F32)).astype(BF16)
        imp[h, g] = _dot(ovt_ref[...], p_hi) + _dot(ovt_ref[...], p_lo)
    j = lax.broadcasted_iota(jnp.int32, (n_sel_blocks, tq), 0)
    j_f = j.astype(F32)
    low = -3e38
    q_aug = {}
    for h, g in lanes:
        cur = (qi[h] * tq + lax.broadcasted_iota(jnp.int32, (n_sel_blocks, tq), 1)) >> SEL_SHIFT
        forced = (j == 0) | (j == cur) | (j == cur - 1)
        score = jnp.where(forced, imp[h, g] + 2.0, jnp.where(j > cur, -1.0, imp[h, g]))
        sel = jnp.zeros((n_sel_blocks, tq), jnp.bool_)
        for _ in range(min(SEL_TOPK, n_sel_blocks)):
            mx = jnp.max(score, axis=0, keepdims=True)
            first = jnp.min(jnp.where(score == mx, j_f, 4.0 * LANES), axis=0, keepdims=True)
            hit = j_f == first
            sel = sel | hit
            score = jnp.where(hit, low, score)
        selneg = jnp.concatenate([jnp.zeros((SEL_LANE0, tq), F32), jnp.where(sel, 0.0, NEG),
                                  jnp.zeros((LANES - SEL_LANE0 - n_sel_blocks, tq), F32)], axis=0).T
        q_aug[h, g] = q_b[h, g] + jnp.concatenate([selneg.astype(BF16)] * B_GROUP, axis=0)

    s_diag = {(h, g): _dot_nt(chunk(ks_ref, g, qi[h]), q_aug[h, g]) + bias_ref[0] for h, g in lanes}
    o_win = {(h, g): normalised(merged([softmax_pv(s_win[h, g][n], vt_chunk(vwt_ref, g, c))
                                        for n, c in enumerate(win_chunks[h])])) for h, g in lanes}
    states = tuple(softmax_pv(s_diag[h, g], vt_chunk(vst_ref, g, qi[h])) for h, g in lanes)

    def earlier(work, st):
        s = [_dot_nt(chunk(ks_ref, lanes[n][1], c), q_aug[lanes[n]]) for n, c in work]
        parts = [[st[n]] for n in range(len(lanes))]
        for (n, c), sc in zip(work, s):
            parts[n].append(softmax_pv(sc, vt_chunk(vst_ref, lanes[n][1], c)))
        return tuple(merged(p) if len(p) > 1 else p[0] for p in parts)

    states = lax.switch(step, [functools.partial(lambda st, cnt: st if cnt == 0 else earlier(
        [(n, c) for n in range(len(lanes)) for c in range(cnt)], st), cnt=cnt) for cnt in range(per_part)], states)
    for h in range(1, NSA_SPLIT):
        mine = [n for n, (hh, _) in enumerate(lanes) if hh >= h]
        states = earlier([(n, step + (h - 1) * per_part + k) for n in mine for k in range(per_part)], states)

    for n, (h, g) in enumerate(lanes):
        o_sel = normalised(states[n])
        gt = gt_ref[0, h, g * B_GROUP * N_BRANCH:(g + 1) * B_GROUP * N_BRANCH, :]
        outs = []
        for r in range(B_GROUP):
            cs = slice(r * tq, (r + 1) * tq)
            outs.append(gt[r * N_BRANCH:r * N_BRANCH + 1, :] * o_cmp[h, g][:, cs]
                        + gt[r * N_BRANCH + 1:r * N_BRANCH + 2, :] * o_sel[:, cs]
                        + gt[r * N_BRANCH + 2:r * N_BRANCH + 3, :] * o_win[h, g][:, cs])
        for c in range(B_GROUP // 2):
            pair = jnp.concatenate([outs[2 * c], outs[2 * c + 1]], axis=0)
            o_ref[0, h, :, g * gw + c * LANES:g * gw + (c + 1) * LANES] = pair.T


def _overlap_t(nc_pad, ns):
    nc = nc_pad - 1
    c0 = np.arange(nc_pad) * CMP_STRIDE
    s0 = np.arange(ns) * SEL_BLOCK
    ov = np.minimum(c0[None, :] + CMP_LEN, s0[:, None] + SEL_BLOCK) - np.maximum(c0[None, :], s0[:, None])
    ov = np.clip(ov, 0, None).astype(np.float32) / CMP_LEN
    ov = np.where(np.arange(nc_pad)[None, :] < nc, ov, 0.0)
    return jnp.asarray(ov, BF16)


def _nsa(qb, ks, kw, vst, vwt, kc, vct, gt):
    B, _, _, T, _ = qb.shape
    W = B_WIDTH
    nq = T // NSA_TQ
    ns = T // SEL_BLOCK
    nc_pad = kc.shape[2]
    nrow = B_GROUP * NSA_TQ
    part = T // NSA_SPLIT
    assert NSA_TQ == NSA_CK and WIN % NSA_CK == 0 and SEL_LANE0 + ns <= LANES and (nq // NSA_SPLIT) % 2 == 0
    assert gt.shape == (B, NSA_SPLIT, GATE_ROWS, part)
    kspec = pl.BlockSpec((1, B_KV_HEADS, T, LANES), lambda b, i: (b, 0, 0, 0))
    vspec = pl.BlockSpec((1, T // NSA_CK, B_KV_HEADS * NSA_VROWS, NSA_CK), lambda b, i: (b, 0, 0, 0))
    key = np.arange(NSA_CK)[:, None]
    qry = (np.arange(nrow) % NSA_TQ)[None, :]
    bias = jnp.asarray(np.stack([np.where(key <= qry, 0.0, NEG), np.where(key > qry, 0.0, NEG)]).astype(np.float32))
    cmp_end = (np.arange(nc_pad) * CMP_STRIDE + CMP_LEN - 1)[None, :, None]
    t_query = (np.arange(nq) * NSA_TQ)[:, None, None] + qry[None]
    cbias = jnp.asarray(np.where((cmp_end <= t_query) & (np.arange(nc_pad) < nc_pad - 1)[None, :, None], 0.0, NEG)
                        .astype(np.float32)).reshape(NSA_SPLIT, nq // NSA_SPLIT, nc_pad, nrow)
    out = pl.pallas_call(
        functools.partial(_nsa_kernel, n_sel_blocks=ns, nq=nq),
        grid=(B, nq // NSA_SPLIT),
        in_specs=[pl.BlockSpec((1, B_KV_HEADS, B_GROUP, NSA_SPLIT, NSA_TQ, LANES), lambda b, i: (b, 0, 0, 0, i, 0)),
                  kspec, kspec, vspec, vspec,
                  pl.BlockSpec((1, B_KV_HEADS, nc_pad, LANES), lambda b, i: (b, 0, 0, 0)),
                  pl.BlockSpec((1, B_KV_HEADS, HEAD_DIM, nc_pad), lambda b, i: (b, 0, 0, 0)),
                  pl.BlockSpec((1, NSA_SPLIT, GATE_ROWS, NSA_TQ), lambda b, i: (b, 0, 0, i)),
                  pl.BlockSpec((ns, nc_pad), lambda b, i: (0, 0)),
                  pl.BlockSpec((2, NSA_CK, nrow), lambda b, i: (0, 0, 0)),
                  pl.BlockSpec((NSA_SPLIT, 1, nc_pad, nrow), lambda b, i: (0, i, 0, 0))],
        out_specs=pl.BlockSpec((1, NSA_SPLIT, NSA_TQ, W), lambda b, i: (b, 0, i, 0)),
        out_shape=jax.ShapeDtypeStruct((B, NSA_SPLIT, part, W), F32),
        compiler_params=pltpu.CompilerParams(dimension_semantics=("parallel", "arbitrary"),
                                             vmem_limit_bytes=VMEM_LIMIT),
        name="nsa",
    )(qb.reshape(B, B_KV_HEADS, B_GROUP, NSA_SPLIT, part, LANES), ks, kw, vst, vwt, kc, vct,
      gt, _overlap_t(nc_pad, ns), bias, cbias)
    return out.reshape(B, T, W)


def _post_kernel(x_ref, oa_ref, ob_ref, ga_ref, gb_ref, wo_ref, gm_ref, wu_ref, wd_ref, gf_ref, o_ref, *, final):
    def norm(v, g):
        return v * lax.rsqrt(jnp.mean(v * v, axis=-1, keepdims=True) + EPS) * g

    na = norm(oa_ref[...], ga_ref[...]).astype(BF16)
    nb = norm(ob_ref[...], gb_ref[...]).astype(BF16)
    aw = na.shape[1]
    h_res = x_ref[...] + _dot(na, wo_ref[0:aw, :]) + _dot(nb, wo_ref[aw:, :])
    h = norm(h_res, gm_ref[...]).astype(BF16)
    u = jnp.square(jnp.maximum(_dot(h, wu_ref[...]), 0.0)).astype(BF16)
    acc = h_res + _dot(u, wd_ref[...])
    o_ref[...] = norm(acc, gf_ref[...]) if final else acc


def _post(x, oa, ob, g_a, g_b, w_out, g_mlp, w_up, w_down, g_final, *, final, tm=512):
    B, T, D = x.shape
    n = B * T
    dff = w_up.shape[1]
    tok = lambda w: pl.BlockSpec((tm, w), lambda i: (i, 0))
    const = lambda shape: pl.BlockSpec(shape, lambda i: (0, 0), pipeline_mode=pl.Buffered(1))
    out = pl.pallas_call(
        functools.partial(_post_kernel, final=final),
        grid=(n // tm,),
        in_specs=[tok(D), tok(A_WIDTH), tok(B_WIDTH), const((1, A_WIDTH)), const((1, B_WIDTH)),
                  const((A_WIDTH + B_WIDTH, D)), const((1, D)), const((D, dff)), const((dff, D)), const((1, D))],
        out_specs=tok(D),
        out_shape=jax.ShapeDtypeStruct((n, D), F32),
        compiler_params=pltpu.CompilerParams(dimension_semantics=("parallel",), vmem_limit_bytes=VMEM_LIMIT),
        name="post",
    )(x.reshape(n, D), oa.reshape(n, A_WIDTH), ob.reshape(n, B_WIDTH), g_a.reshape(1, -1), g_b.reshape(1, -1),
      w_out.astype(BF16), g_mlp.reshape(1, D), w_up.astype(BF16), w_down.astype(BF16), g_final.reshape(1, D))
    return out.reshape(B, T, D)


def kernel(x, norm_mix, w_in, cmp_pe_k, cmp_w1_k, cmp_w2_k, cmp_pe_v, cmp_w1_v, cmp_w2_v, g_out_a, g_out_b,
           w_out, norm_mlp, w_up, w_down, norm_final):
    B, T, D = x.shape
    depth = w_in.shape[0]
    tables = _rope_tables(np.arange(T))
    cmp_tables = _rope_tables(np.arange(T // CMP_STRIDE) * CMP_STRIDE + CMP_LEN - 1)
    h_res = x
    for l in range(depth):
        qa, ka, va, qb, ks, kw, vst, vwt, kcvc, gt, wo_b, wu_b, wd_b, w1k_b, w1v_b = _in_proj(
            h_res, norm_mix[l], w_in[l], tables, (w_out[l], w_up[l], w_down[l], cmp_w1_k[l], cmp_w1_v[l]))
        kc, vct = _compress(kcvc, cmp_pe_k[l], w1k_b, cmp_w2_k[l], cmp_pe_v[l], w1v_b, cmp_w2_v[l], cmp_tables)
        oa = _mixer_a(qa, ka, va)
        ob = _nsa(qb, ks, kw, vst, vwt, kc, vct, gt)
        h_res = _post(h_res, oa, ob, g_out_a[l], g_out_b[l], wo_b, norm_mlp[l], wu_b, wd_b, norm_final,
                      final=(l == depth - 1))
    return h_res
```

```python
import functools

import jax
import jax.numpy as jnp
import numpy as np
from jax import lax
from jax.experimental import pallas as pl
from jax.experimental.pallas import tpu as pltpu

F32 = jnp.float32
BF16 = jnp.bfloat16

HEAD_DIM = 64
ROT_DIM = HEAD_DIM // 4
ROPE_THETA = 500000.0
EPS = 1e-6
NEG = -1e30
Q_SCALE = HEAD_DIM ** -0.5 * 1.4426950408889634
LANES = 128

A_HEADS = 8
A_PATTERNS = ((128, 1), (512, 4), (2048, 16))
A_BLOCK = 128
A_GROUP = 16

B_HEADS = 8
B_KV_HEADS = 2
B_GROUP = B_HEADS // B_KV_HEADS
CMP_LEN = 32
CMP_STRIDE = 16
CMP_HIDDEN = 256
CMP_POS_PER_DOT = 2
SEL_BLOCK = 64
SEL_SHIFT = 6
SEL_TOPK = 8
WIN = 512
N_BRANCH = 3

A_WIDTH = A_HEADS * HEAD_DIM
B_WIDTH = B_HEADS * HEAD_DIM
KV_WIDTH = B_KV_HEADS * HEAD_DIM

NSA_TQ = 256
NSA_CK = 256
NSA_SPLIT = 2
NSA_VROWS = 80
GATE_ROWS = 32
SEL_LANE0 = HEAD_DIM

N_LATER = 5

VMEM_LIMIT = 56 * 1024 * 1024


def _dot(a, b):
    return jnp.dot(a, b, preferred_element_type=F32)


def _dot_nt(a, b):
    return lax.dot_general(a, b, (((1,), (1,)), ((), ())), preferred_element_type=F32)


def _rope_rows(y, cos, sin_a, sin_b):
    outs = []
    for c in range(y.shape[1] // LANES):
        yc = y[:, c * LANES:(c + 1) * LANES]
        outs.append(yc * cos + pltpu.roll(yc, LANES - ROT_DIM // 2, 1) * sin_a
                    + pltpu.roll(yc, ROT_DIM // 2, 1) * sin_b)
    return outs[0] if len(outs) == 1 else jnp.concatenate(outs, axis=1)


def _in_proj_kernel(x_ref, g_ref, wq_ref, wkk_ref, wt_ref, cos_ref, sa_ref, sb_ref, *refs, tm):
    (later_f32, (qa_ref, ka_ref, va_ref, qb_ref, ks_ref, kw_ref, vst_ref, vwt_ref, kcvc_ref, gt_ref), later_bf16) = (
        refs[:N_LATER], refs[N_LATER:-N_LATER], refs[-N_LATER:])
    for src, dst in zip(later_f32, later_bf16):
        dst[...] = src[...].astype(BF16)
    tt = pl.program_id(1)
    x = x_ref[...]
    ms = jnp.mean(x * x, axis=-1, keepdims=True)
    h = (x * lax.rsqrt(ms + EPS) * g_ref[...]).astype(BF16)
    cos, sa, sb = cos_ref[...], sa_ref[...], sb_ref[...]
    scale = Q_SCALE

    def proj(c0, c1):
        return _dot(h, wq_ref[:, c0:c1])

    def put_pairs(ref, y):
        for p in range(A_WIDTH // LANES):
            ref[0, p] = y[:, p * LANES:(p + 1) * LANES]

    o = 0
    put_pairs(qa_ref, _rope_rows(proj(o, o + A_WIDTH), cos, sa, sb) * scale)
    o += A_WIDTH
    put_pairs(ka_ref, _rope_rows(proj(o, o + A_WIDTH), cos, sa, sb))
    o += A_WIDTH
    put_pairs(va_ref, proj(o, o + A_WIDTH))
    o += A_WIDTH
    lane = lax.broadcasted_iota(jnp.int32, (tm, LANES), 1)
    row = lax.broadcasted_iota(jnp.int32, (tm, LANES), 0)
    lo = lane < HEAD_DIM
    qb = _rope_rows(proj(o, o + B_WIDTH), cos, sa, sb) * scale
    for hh in range(B_HEADS):
        ch = qb[:, (hh // 2) * LANES:(hh // 2 + 1) * LANES]
        if hh % 2:
            ch = pltpu.roll(ch, HEAD_DIM, 1)
        qb_ref[0, hh // B_GROUP, hh % B_GROUP] = jnp.where(lo, ch, 0.0).astype(BF16)
    o += B_WIDTH
    kcvc_ref[...] = proj(o, o + 2 * KV_WIDTH)
    o += 2 * KV_WIDTH
    ksw = _rope_rows(_dot(h, wkk_ref[...]), cos, sa, sb)

    blk = (tt * tm + row) >> SEL_SHIFT
    onehot = jnp.where(lane - SEL_LANE0 == blk, 1.0, 0.0)
    for kind, ref in ((0, ks_ref), (1, kw_ref)):
        kk = ksw[:, kind * LANES:(kind + 1) * LANES]
        tail = onehot if kind == 0 else 0.0
        ref[0, 0] = jnp.where(lo, kk, tail).astype(BF16)
        ref[0, 1] = jnp.where(lo, pltpu.roll(kk, HEAD_DIM, 1), tail).astype(BF16)

    tr = _dot_nt(wt_ref[...], h)
    ones_row = jnp.where(lax.broadcasted_iota(jnp.int32, (NSA_VROWS - HEAD_DIM, tm), 0) == 0, 1.0, 0.0)
    for kind, ref in ((0, vst_ref), (1, vwt_ref)):
        rows = [tr[kind * LANES + gg * HEAD_DIM:kind * LANES + (gg + 1) * HEAD_DIM] for gg in range(B_KV_HEADS)]
        slab = jnp.concatenate([rows[0], ones_row, rows[1], ones_row], axis=0).astype(BF16)
        for c in range(tm // NSA_CK):
            ref[0, c] = slab[:, c * NSA_CK:(c + 1) * NSA_CK]
    gt_ref[0, 0] = jax.nn.sigmoid(tr[2 * LANES:2 * LANES + GATE_ROWS, :])


def _rope_tables(pos):
    half = ROT_DIM // 2
    inv = ROPE_THETA ** (-np.arange(0, ROT_DIM, 2, dtype=np.float64) / ROT_DIM)
    ang = np.asarray(pos, np.float64)[:, None] * inv[None, :]
    cos, sin = np.cos(ang), np.sin(ang)
    n = len(pos)
    ones = np.ones((n, HEAD_DIM - ROT_DIM))
    zeros = np.zeros((n, HEAD_DIM - ROT_DIM))
    zh = np.zeros((n, half))
    c_head = np.concatenate([cos, cos, ones], axis=1)
    a_head = np.concatenate([-sin, zh, zeros], axis=1)
    b_head = np.concatenate([zh, sin, zeros], axis=1)
    rep = LANES // HEAD_DIM
    return tuple(jnp.asarray(np.tile(t, (1, rep)).astype(np.float32)) for t in (c_head, a_head, b_head))


def _in_proj(x, norm_g, w_in, tables, later, *, tm=1024):
    B, T, D = x.shape
    nt = T // tm
    assert len(later) == N_LATER and all(w.shape[0] % (16 * B * nt) == 0 for w in later)
    slab = lambda w: pl.BlockSpec((w.shape[0] // (B * nt), w.shape[1]), lambda b, t: (b * nt + t, 0))
    offs = [0]
    for n in (A_WIDTH, A_WIDTH, A_WIDTH, B_WIDTH, KV_WIDTH, KV_WIDTH, KV_WIDTH, KV_WIDTH, KV_WIDTH, KV_WIDTH,
              B_HEADS * N_BRANCH):
        offs.append(offs[-1] + n)
    col = lambda i: w_in[:, offs[i]:offs[i + 1]]
    wq = w_in[:, :offs[6]].astype(BF16)
    wkk = jnp.concatenate([col(6), col(8)], axis=1).astype(BF16)
    gpad = jnp.zeros((D, GATE_ROWS - B_HEADS * N_BRANCH), w_in.dtype)
    wt = jnp.concatenate([col(7), col(9), col(10), gpad], axis=1).T.astype(BF16)
    cos, sa, sb = tables
    nq = wq.shape[1]
    nr = wt.shape[0]
    tok = lambda w: pl.BlockSpec((None, tm, w), lambda b, t: (b, t, 0))
    const = lambda shape: pl.BlockSpec(shape, lambda b, t: (0,) * len(shape))
    tab = pl.BlockSpec((tm, LANES), lambda b, t: (t, 0))
    npair = A_WIDTH // LANES
    out_shapes = (
        jax.ShapeDtypeStruct((B, npair, T, LANES), F32),
        jax.ShapeDtypeStruct((B, npair, T, LANES), F32),
        jax.ShapeDtypeStruct((B, npair, T, LANES), F32),
        jax.ShapeDtypeStruct((B, B_KV_HEADS, B_GROUP, T, LANES), BF16),
        jax.ShapeDtypeStruct((B, B_KV_HEADS, T, LANES), BF16),
        jax.ShapeDtypeStruct((B, B_KV_HEADS, T, LANES), BF16),
        jax.ShapeDtypeStruct((B, T // NSA_CK, B_KV_HEADS * NSA_VROWS, NSA_CK), BF16),
        jax.ShapeDtypeStruct((B, T // NSA_CK, B_KV_HEADS * NSA_VROWS, NSA_CK), BF16),
        jax.ShapeDtypeStruct((B, T, 2 * KV_WIDTH), F32),
        jax.ShapeDtypeStruct((B, nt, GATE_ROWS, tm), F32),
    )
    frame = pl.BlockSpec((1, B_KV_HEADS, tm, LANES), lambda b, t: (b, 0, t, 0))
    vt = pl.BlockSpec((1, tm // NSA_CK, B_KV_HEADS * NSA_VROWS, NSA_CK), lambda b, t: (b, t, 0, 0))
    qframe = pl.BlockSpec((1, B_KV_HEADS, B_GROUP, tm, LANES), lambda b, t: (b, 0, 0, t, 0))
    pairs = pl.BlockSpec((1, npair, tm, LANES), lambda b, t: (b, 0, t, 0))
    out_specs = (pairs, pairs, pairs, qframe, frame, frame, vt, vt,
                 tok(2 * KV_WIDTH), pl.BlockSpec((1, 1, GATE_ROWS, tm), lambda b, t: (b, t, 0, 0)))
    return pl.pallas_call(
        functools.partial(_in_proj_kernel, tm=tm),
        grid=(B, nt),
        in_specs=[tok(D), const((1, D)), const((D, nq)), const((D, 2 * KV_WIDTH)), const((nr, D)), tab, tab, tab]
        + [slab(w) for w in later],
        out_specs=out_specs + tuple(slab(w) for w in later),
        out_shape=out_shapes + tuple(jax.ShapeDtypeStruct(w.shape, BF16) for w in later),
        compiler_params=pltpu.CompilerParams(dimension_semantics=("parallel", "parallel"),
                                             vmem_limit_bytes=VMEM_LIMIT),
        name="in_proj",
    )(x, norm_g.reshape(1, D), wq, wkk, wt, cos, sa, sb, *later)


def _compress_kernel(ak_ref, av_ref, w1k_ref, w1v_ref, pek_ref, pev_ref, w2k_ref, w2vt_ref, cos_ref, sa_ref, sb_ref,
                     kc_ref, vct_ref, *, nc_pad):
    half = CMP_LEN // 2
    hid_w = B_KV_HEADS * CMP_HIDDEN

    zeros = jnp.zeros((HEAD_DIM, CMP_HIDDEN), BF16)

    def both_groups(w):
        return jnp.concatenate([jnp.concatenate([w, zeros], axis=1), jnp.concatenate([zeros, w], axis=1)], axis=0)

    def hidden(a_ref, w1_ref, pe_ref):
        acc_u = jnp.zeros((nc_pad, hid_w), F32)
        acc_v = jnp.zeros((nc_pad, hid_w), F32)
        for p in range(0, half, CMP_POS_PER_DOT):
            ks = range(CMP_POS_PER_DOT)
            aps = [a_ref[0, pl.ds(p + k, nc_pad, stride=CMP_STRIDE), :] for k in ks]

            def partial_hidden(base):
                lhs = jnp.concatenate([(aps[k] + pe_ref[base + p + k:base + p + k + 1, :]).astype(BF16) for k in ks],
                                      axis=1)
                return _dot(lhs, jnp.concatenate([both_groups(w1_ref[base + p + k]) for k in ks], axis=0))

            acc_u = acc_u + partial_hidden(0)
            acc_v = acc_v + partial_hidden(half)
        return jax.nn.gelu(acc_u + pltpu.roll(acc_v, nc_pad - 1, 0))

    hk = hidden(ak_ref, w1k_ref, pek_ref).astype(BF16)
    hv = hidden(av_ref, w1v_ref, pev_ref).astype(BF16)
    for g in range(B_KV_HEADS):
        hg = hk[:, g * CMP_HIDDEN:(g + 1) * CMP_HIDDEN]
        kc = _dot(hg, w2k_ref[...])
        kc_ref[0, g] = _rope_rows(kc, cos_ref[...], sa_ref[...], sb_ref[...]).astype(BF16)
        vg = hv[:, g * CMP_HIDDEN:(g + 1) * CMP_HIDDEN]
        vct_ref[0, g] = _dot_nt(w2vt_ref[...], vg).astype(BF16)


def _compress(kcvc, pe_k, w1_k, w2_k, pe_v, w1_v, w2_v, cmp_tables):
    B, T, _ = kcvc.shape
    nc_pad = T // CMP_STRIDE
    w2k = jnp.concatenate([w2_k, jnp.zeros_like(w2_k)], axis=1).astype(BF16)
    w2vt = w2_v.T.astype(BF16)
    pek = jnp.tile(pe_k, (1, B_KV_HEADS))
    pev = jnp.tile(pe_v, (1, B_KV_HEADS))
    per_pos = lambda w1: w1.reshape(CMP_LEN, HEAD_DIM, CMP_HIDDEN).astype(BF16)
    const = lambda shape: pl.BlockSpec(shape, lambda b: (0,) * len(shape))
    cos, sa, sb = cmp_tables
    return pl.pallas_call(
        functools.partial(_compress_kernel, nc_pad=nc_pad),
        grid=(B,),
        in_specs=[pl.BlockSpec((1, T, KV_WIDTH), lambda b: (b, 0, 0)), pl.BlockSpec((1, T, KV_WIDTH), lambda b: (b, 0, 1)),
                  const((CMP_LEN, HEAD_DIM, CMP_HIDDEN)), const((CMP_LEN, HEAD_DIM, CMP_HIDDEN)),
                  const((CMP_LEN, LANES)), const((CMP_LEN, LANES)),
                  const((CMP_HIDDEN, LANES)), const((HEAD_DIM, CMP_HIDDEN)),
                  const((nc_pad, LANES)), const((nc_pad, LANES)), const((nc_pad, LANES))],
        out_specs=(pl.BlockSpec((1, B_KV_HEADS, nc_pad, LANES), lambda b: (b, 0, 0, 0)),
                   pl.BlockSpec((1, B_KV_HEADS, HEAD_DIM, nc_pad), lambda b: (b, 0, 0, 0))),
        out_shape=(jax.ShapeDtypeStruct((B, B_KV_HEADS, nc_pad, LANES), BF16),
                   jax.ShapeDtypeStruct((B, B_KV_HEADS, HEAD_DIM, nc_pad), BF16)),
        compiler_params=pltpu.CompilerParams(dimension_semantics=("parallel",), vmem_limit_bytes=VMEM_LIMIT),
        name="compress",
    )(kcvc, kcvc, per_pos(w1_k), per_pos(w1_v), pek, pev, w2k, w2vt, cos, sa, sb)


def _mixer_a_kernel(q_ref, k_ref, v_ref, bias_ref, o_ref, qd0, qd1, kd, vd0, vd1, u_s, m_s, l_s, *, seq):
    blk = A_BLOCK
    nres = seq // blk
    npat = len(A_PATTERNS)
    order = sorted(range(npat), key=lambda p: -A_PATTERNS[p][1])
    slot = {p: n for n, p in enumerate(order[:-1])}
    lane = lax.broadcasted_iota(jnp.int32, (blk, LANES), 1)
    lo = lane < HEAD_DIM

    def deinterleave(r):
        rows = pl.ds(r * blk, blk)
        q, v = q_ref[0, pl.ds(r, blk, stride=nres), :], v_ref[0, pl.ds(r, blk, stride=nres), :]
        qd0[rows, :] = jnp.where(lo, q, 0.0)
        qd1[rows, :] = jnp.where(lo, 0.0, q)
        kd[rows, :] = k_ref[0, pl.ds(r, blk, stride=nres), :]
        vd0[rows, :] = jnp.where(lo, v, 1.0)
        vd1[rows, :] = jnp.where(lo, 1.0, v)

    def pieces(dil, rd, row_off, rows):
        return [pl.ds((rd + dil * jj) * blk + row_off, rows) for jj in range(nres // dil)]

    def gather(ref, idx, lead=()):
        parts = [ref[lead + (i, slice(None))] for i in idx]
        return parts[0] if len(parts) == 1 else jnp.concatenate(parts, axis=0)

    def attend(pi, blocks):
        dil = A_PATTERNS[pi][1]
        pr = blk // (nres // dil)
        q_idxs, vbs, scores = [], [], []
        for rd, n, first in blocks:
            q_idx = pieces(dil, rd, n * pr, pr)
            k_idx = q_idx if first else pieces(dil, rd, (n - 1) * pr, 2 * pr)
            bias = bias_ref[pi, :, 0:blk] if first else bias_ref[pi, :, blk:3 * blk]
            kb = gather(kd, k_idx).astype(BF16)
            q_idxs.append(q_idx)
            for qd, vd in ((qd0, vd0), (qd1, vd1)):
                vbs.append(gather(vd, k_idx).astype(BF16))
                scores.append(_dot_nt(gather(qd, q_idx).astype(BF16), kb) + bias)
        es, ms = [], []
        for s in scores:
            m = jnp.max(s, axis=-1, keepdims=True)
            ms.append(m)
            es.append(jnp.exp2((s - m).astype(BF16)))
        pvs = [_dot(e, vb) for e, vb in zip(es, vbs)]
        for b, q_idx in enumerate(q_idxs):
            u = jnp.where(lo, pvs[2 * b], pvs[2 * b + 1])
            l_swapped = jnp.where(lo, pvs[2 * b + 1], pvs[2 * b])
            m = jnp.where(lo, ms[2 * b], ms[2 * b + 1])
            if pi != order[-1]:
                for jj, idx in enumerate(q_idx):
                    u_s[slot[pi], idx, :] = u[jj * pr:(jj + 1) * pr]
                    m_s[slot[pi], idx, :] = m[jj * pr:(jj + 1) * pr]
                    l_s[slot[pi], idx, :] = l_swapped[jj * pr:(jj + 1) * pr]
                continue
            parts = [(u, m, l_swapped)] + [(gather(u_s, q_idx, (sl,)), gather(m_s, q_idx, (sl,)),
                                            gather(l_s, q_idx, (sl,))) for sl in slot.values()]
            m_all = functools.reduce(jnp.maximum, [mm for _, mm, _ in parts])
            num = jnp.zeros((blk, LANES), F32)
            den = jnp.zeros((blk, LANES), F32)
            for uu, mm, ll in parts:
                a = jnp.exp2(mm - m_all)
                num = num + a * uu
                den = den + a * pltpu.roll(ll, HEAD_DIM, 1)
            out = num / den
            n = blocks[b][1]
            for jj in range(nres):
                o_ref[0, pl.ds(n * blk + jj, pr, stride=nres), :] = out[jj * pr:(jj + 1) * pr]

    for pi in order:
        dil = A_PATTERNS[pi][1]
        nb = seq // dil // blk
        blocks = [(rd, n, n == 0) for rd in range(dil) for n in range(nb)]
        for g0 in range(0, len(blocks), A_GROUP):
            if pi == order[0]:
                for rd, _, _ in blocks[g0:g0 + A_GROUP]:
                    deinterleave(rd)
            attend(pi, blocks[g0:g0 + A_GROUP])


def _mixer_a_bias(nres):
    blk = A_BLOCK

    def sub_pos(i, fold, rows):
        return fold * (i % rows) + i // rows

    out = np.zeros((len(A_PATTERNS), blk, 3 * blk), np.float32)
    for pi, (window, dil) in enumerate(A_PATTERNS):
        n_back = window // dil
        fold = nres // dil
        pr = blk // fold
        sq = sub_pos(np.arange(blk), fold, pr)[:, None]
        d_first = sq - sub_pos(np.arange(blk), fold, pr)[None, :]
        d_band = sq + blk - sub_pos(np.arange(2 * blk), fold, 2 * pr)[None, :]
        dist = np.concatenate([d_first, d_band], axis=1)
        out[pi] = np.where((dist >= 0) & (dist <= n_back), 0.0, NEG)
    return jnp.asarray(out)


def _mixer_a(qa, ka, va):
    B, npair, T, _ = qa.shape
    W = npair * LANES
    spec = pl.BlockSpec((1, T, LANES), lambda b, p: (b, 0, p))
    frame = pl.BlockSpec((None, 1, T, LANES), lambda b, p: (b, p, 0, 0))
    npat = len(A_PATTERNS)
    nres = T // A_BLOCK
    dils = sorted(d for _, d in A_PATTERNS)
    assert all(nres % d == 0 and T % (d * A_BLOCK) == 0 for d in dils) and dils[0] == 1 and dils[-1] == nres
    return pl.pallas_call(
        functools.partial(_mixer_a_kernel, seq=T),
        grid=(B, npair),
        in_specs=[frame, frame, frame, pl.BlockSpec((npat, A_BLOCK, 3 * A_BLOCK), lambda b, p: (0, 0, 0))],
        out_specs=spec,
        out_shape=jax.ShapeDtypeStruct((B, T, W), F32),
        scratch_shapes=[pltpu.VMEM((T, LANES), F32)] * 5 + [pltpu.VMEM((npat - 1, T, LANES), F32)] * 3,
        compiler_params=pltpu.CompilerParams(dimension_semantics=("parallel", "parallel"),
                                             vmem_limit_bytes=VMEM_LIMIT),
        name="mixer_a",
    )(qa, ka, va, _mixer_a_bias(nres))


def _nsa_kernel(q_ref, ks_ref, kw_ref, vst_ref, vwt_ref, kc_ref, vct_ref, gt_ref, ovt_ref, bias_ref, cbias_ref, o_ref,
                *, n_sel_blocks, nq):
    tq, ck = NSA_TQ, NSA_CK
    nrow = B_GROUP * tq
    gw = B_GROUP * HEAD_DIM
    step = pl.program_id(1)
    per_part = nq // NSA_SPLIT
    lanes = [(h, g) for h in range(NSA_SPLIT) for g in range(B_KV_HEADS)]
    qi = [step + h * per_part for h in range(NSA_SPLIT)]
    col_i = lax.broadcasted_iota(jnp.int32, (1, nrow), 1) & (tq - 1)

    def chunk(ref, g, c):
        return ref[0, g, pl.ds(pl.multiple_of(c * ck, ck), ck), :]

    def vt_chunk(ref, g, c):
        return ref[0, c, g * NSA_VROWS:(g + 1) * NSA_VROWS, :]

    def softmax_pv(s, vt):
        m = jnp.max(s, axis=0, keepdims=True)
        return m, _dot(vt, jnp.exp2(s - m).astype(BF16))

    def merged(parts):
        m_new = functools.reduce(jnp.maximum, [m for m, _ in parts])
        acc = sum(jnp.exp2(m - m_new) * a for m, a in parts)
        return m_new, acc

    def normalised(state):
        return state[1][0:HEAD_DIM] * (1.0 / state[1][HEAD_DIM:HEAD_DIM + 1])

    q_b = {(h, g): jnp.concatenate([q_ref[0, g, r, h] for r in range(B_GROUP)], axis=0) for h, g in lanes}

    nwin = WIN // ck
    win_chunks = [[jnp.maximum(qi[h] - back, 0) for back in range(nwin, -1, -1)] for h in range(NSA_SPLIT)]
    s_win = {}
    for h, g in lanes:
        s = _dot_nt(jnp.concatenate([chunk(kw_ref, g, c) for c in win_chunks[h]], axis=0), q_b[h, g])
        exists = lambda back: True if h * per_part >= back else qi[h] >= back
        masked = lambda piece, back: piece if exists(back) is True else jnp.where(exists(back), piece, NEG)
        pieces = [masked(s[0:ck] + bias_ref[1], nwin)]
        for n in range(1, nwin):
            pieces.append(masked(s[n * ck:(n + 1) * ck], nwin - n))
        pieces.append(s[nwin * ck:] + bias_ref[0])
        s_win[h, g] = pieces

    s_cmp = {(h, g): _dot_nt(kc_ref[0, g], q_b[h, g]) + cbias_ref[h, 0] for h, g in lanes}
    p_cmp, o_cmp = {}, {}
    for h, g in lanes:
        m = jnp.max(s_cmp[h, g], axis=0, keepdims=True)
        e = jnp.exp2(s_cmp[h, g] - m)
        den = jnp.sum(e, axis=0, keepdims=True)
        sees_block = qi[h] * tq + col_i >= CMP_LEN - 1
        p_cmp[h, g] = e * jnp.where(sees_block, 1.0 / jnp.maximum(den, 1e-30), 0.0)
        o_cmp[h, g] = _dot(vct_ref[0, g], p_cmp[h, g].astype(BF16))

    imp = {}
    for h, g in lanes:
        psum = p_cmp[h, g][:, 0:tq]
        for r in range(1, B_GROUP):
            psum = psum + p_cmp[h, g][:, r * tq:(r + 1) * tq]
        p_hi = psum.astype(BF16)
        p_lo = (psum - p_hi.astype(F32)).astype(BF16)
        imp[h, g] = _dot(ovt_ref[...], p_hi) + _dot(ovt_ref[...], p_lo)
    j = lax.broadcasted_iota(jnp.int32, (n_sel_blocks, tq), 0)
    j_f = j.astype(F32)
    low = -3e38
    q_aug = {}
    for h, g in lanes:
        cur = (qi[h] * tq + lax.broadcasted_iota(jnp.int32, (n_sel_blocks, tq), 1)) >> SEL_SHIFT
        forced = (j == 0) | (j == cur) | (j == cur - 1)
        score = jnp.where(forced, imp[h, g] + 2.0, jnp.where(j > cur, -1.0, imp[h, g]))
        sel = jnp.zeros((n_sel_blocks, tq), jnp.bool_)
        for _ in range(min(SEL_TOPK, n_sel_blocks)):
            mx = jnp.max(score, axis=0, keepdims=True)
            first = jnp.min(jnp.where(score == mx, j_f, 4.0 * LANES), axis=0, keepdims=True)
            hit = j_f == first
            sel = sel | hit
            score = jnp.where(hit, low, score)
        selneg = jnp.concatenate([jnp.zeros((SEL_LANE0, tq), F32), jnp.where(sel, 0.0, NEG),
                                  jnp.zeros((LANES - SEL_LANE0 - n_sel_blocks, tq), F32)], axis=0).T
        q_aug[h, g] = q_b[h, g] + jnp.concatenate([selneg.astype(BF16)] * B_GROUP, axis=0)

    s_diag = {(h, g): _dot_nt(chunk(ks_ref, g, qi[h]), q_aug[h, g]) + bias_ref[0] for h, g in lanes}
    o_win = {(h, g): normalised(merged([softmax_pv(s_win[h, g][n], vt_chunk(vwt_ref, g, c))
                                        for n, c in enumerate(win_chunks[h])])) for h, g in lanes}
    states = tuple(softmax_pv(s_diag[h, g], vt_chunk(vst_ref, g, qi[h])) for h, g in lanes)

    def earlier(work, st):
        s = [_dot_nt(chunk(ks_ref, lanes[n][1], c), q_aug[lanes[n]]) for n, c in work]
        parts = [[st[n]] for n in range(len(lanes))]
        for (n, c), sc in zip(work, s):
            parts[n].append(softmax_pv(sc, vt_chunk(vst_ref, lanes[n][1], c)))
        return tuple(merged(p) if len(p) > 1 else p[0] for p in parts)

    states = lax.fori_loop(0, step, lambda c, st: earlier([(n, c) for n in range(len(lanes))], st), states)
    for h in range(1, NSA_SPLIT):
        mine = [n for n, (hh, _) in enumerate(lanes) if hh >= h]
        states = earlier([(n, step + (h - 1) * per_part + k) for n in mine for k in range(per_part)], states)

    for n, (h, g) in enumerate(lanes):
        o_sel = normalised(states[n])
        gt = gt_ref[0, h, g * B_GROUP * N_BRANCH:(g + 1) * B_GROUP * N_BRANCH, :]
        outs = []
        for r in range(B_GROUP):
            cs = slice(r * tq, (r + 1) * tq)
            outs.append(gt[r * N_BRANCH:r * N_BRANCH + 1, :] * o_cmp[h, g][:, cs]
                        + gt[r * N_BRANCH + 1:r * N_BRANCH + 2, :] * o_sel[:, cs]
                        + gt[r * N_BRANCH + 2:r * N_BRANCH + 3, :] * o_win[h, g][:, cs])
        for c in range(B_GROUP // 2):
            pair = jnp.concatenate([outs[2 * c], outs[2 * c + 1]], axis=0)
            o_ref[0, h, :, g * gw + c * LANES:g * gw + (c + 1) * LANES] = pair.T


def _overlap_t(nc_pad, ns):
    nc = nc_pad - 1
    c0 = np.arange(nc_pad) * CMP_STRIDE
    s0 = np.arange(ns) * SEL_BLOCK
    ov = np.minimum(c0[None, :] + CMP_LEN, s0[:, None] + SEL_BLOCK) - np.maximum(c0[None, :], s0[:, None])
    ov = np.clip(ov, 0, None).astype(np.float32) / CMP_LEN
    ov = np.where(np.arange(nc_pad)[None, :] < nc, ov, 0.0)
    return jnp.asarray(ov, BF16)


def _nsa(qb, ks, kw, vst, vwt, kc, vct, gt):
    B, _, _, T, _ = qb.shape
    W = B_WIDTH
    nq = T // NSA_TQ
    ns = T // SEL_BLOCK
    nc_pad = kc.shape[2]
    nrow = B_GROUP * NSA_TQ
    part = T // NSA_SPLIT
    assert NSA_TQ == NSA_CK and WIN % NSA_CK == 0 and SEL_LANE0 + ns <= LANES and (nq // NSA_SPLIT) % 2 == 0
    assert gt.shape == (B, NSA_SPLIT, GATE_ROWS, part)
    kspec = pl.BlockSpec((1, B_KV_HEADS, T, LANES), lambda b, i: (b, 0, 0, 0))
    vspec = pl.BlockSpec((1, T // NSA_CK, B_KV_HEADS * NSA_VROWS, NSA_CK), lambda b, i: (b, 0, 0, 0))
    key = np.arange(NSA_CK)[:, None]
    qry = (np.arange(nrow) % NSA_TQ)[None, :]
    bias = jnp.asarray(np.stack([np.where(key <= qry, 0.0, NEG), np.where(key > qry, 0.0, NEG)]).astype(np.float32))
    cmp_end = (np.arange(nc_pad) * CMP_STRIDE + CMP_LEN - 1)[None, :, None]
    t_query = (np.arange(nq) * NSA_TQ)[:, None, None] + qry[None]
    cbias = jnp.asarray(np.where((cmp_end <= t_query) & (np.arange(nc_pad) < nc_pad - 1)[None, :, None], 0.0, NEG)
                        .astype(np.float32)).reshape(NSA_SPLIT, nq // NSA_SPLIT, nc_pad, nrow)
    out = pl.pallas_call(
        functools.partial(_nsa_kernel, n_sel_blocks=ns, nq=nq),
        grid=(B, nq // NSA_SPLIT),
        in_specs=[pl.BlockSpec((1, B_KV_HEADS, B_GROUP, NSA_SPLIT, NSA_TQ, LANES), lambda b, i: (b, 0, 0, 0, i, 0)),
                  kspec, kspec, vspec, vspec,
                  pl.BlockSpec((1, B_KV_HEADS, nc_pad, LANES), lambda b, i: (b, 0, 0, 0)),
                  pl.BlockSpec((1, B_KV_HEADS, HEAD_DIM, nc_pad), lambda b, i: (b, 0, 0, 0)),
                  pl.BlockSpec((1, NSA_SPLIT, GATE_ROWS, NSA_TQ), lambda b, i: (b, 0, 0, i)),
                  pl.BlockSpec((ns, nc_pad), lambda b, i: (0, 0)),
                  pl.BlockSpec((2, NSA_CK, nrow), lambda b, i: (0, 0, 0)),
                  pl.BlockSpec((NSA_SPLIT, 1, nc_pad, nrow), lambda b, i: (0, i, 0, 0))],
        out_specs=pl.BlockSpec((1, NSA_SPLIT, NSA_TQ, W), lambda b, i: (b, 0, i, 0)),
        out_shape=jax.ShapeDtypeStruct((B, NSA_SPLIT, part, W), F32),
        compiler_params=pltpu.CompilerParams(dimension_semantics=("parallel", "arbitrary"),
                                             vmem_limit_bytes=VMEM_LIMIT),
        name="nsa",
    )(qb.reshape(B, B_KV_HEADS, B_GROUP, NSA_SPLIT, part, LANES), ks, kw, vst, vwt, kc, vct,
      gt, _overlap_t(nc_pad, ns), bias, cbias)
    return out.reshape(B, T, W)


def _post_kernel(x_ref, oa_ref, ob_ref, ga_ref, gb_ref, wo_ref, gm_ref, wu_ref, wd_ref, gf_ref, o_ref, *, final):
    def norm(v, g):
        return v * lax.rsqrt(jnp.mean(v * v, axis=-1, keepdims=True) + EPS) * g

    na = norm(oa_ref[...], ga_ref[...]).astype(BF16)
    nb = norm(ob_ref[...], gb_ref[...]).astype(BF16)
    aw = na.shape[1]
    h_res = x_ref[...] + _dot(na, wo_ref[0:aw, :]) + _dot(nb, wo_ref[aw:, :])
    h = norm(h_res, gm_ref[...]).astype(BF16)
    u = jnp.square(jnp.maximum(_dot(h, wu_ref[...]), 0.0)).astype(BF16)
    acc = h_res + _dot(u, wd_ref[...])
    o_ref[...] = norm(acc, gf_ref[...]) if final else acc


def _post(x, oa, ob, g_a, g_b, w_out, g_mlp, w_up, w_down, g_final, *, final, tm=512):
    B, T, D = x.shape
    n = B * T
    dff = w_up.shape[1]
    tok = lambda w: pl.BlockSpec((tm, w), lambda i: (i, 0))
    const = lambda shape: pl.BlockSpec(shape, lambda i: (0, 0), pipeline_mode=pl.Buffered(1))
    out = pl.pallas_call(
        functools.partial(_post_kernel, final=final),
        grid=(n // tm,),
        in_specs=[tok(D), tok(A_WIDTH), tok(B_WIDTH), const((1, A_WIDTH)), const((1, B_WIDTH)),
                  const((A_WIDTH + B_WIDTH, D)), const((1, D)), const((D, dff)), const((dff, D)), const((1, D))],
        out_specs=tok(D),
        out_shape=jax.ShapeDtypeStruct((n, D), F32),
        compiler_params=pltpu.CompilerParams(dimension_semantics=("parallel",), vmem_limit_bytes=VMEM_LIMIT),
        name="post",
    )(x.reshape(n, D), oa.reshape(n, A_WIDTH), ob.reshape(n, B_WIDTH), g_a.reshape(1, -1), g_b.reshape(1, -1),
      w_out.astype(BF16), g_mlp.reshape(1, D), w_up.astype(BF16), w_down.astype(BF16), g_final.reshape(1, D))
    return out.reshape(B, T, D)


def kernel(x, norm_mix, w_in, cmp_pe_k, cmp_w1_k, cmp_w2_k, cmp_pe_v, cmp_w1_v, cmp_w2_v, g_out_a, g_out_b,
           w_out, norm_mlp, w_up, w_down, norm_final):
    B, T, D = x.shape
    depth = w_in.shape[0]
    tables = _rope_tables(np.arange(T))
    cmp_tables = _rope_tables(np.arange(T // CMP_STRIDE) * CMP_STRIDE + CMP_LEN - 1)
    h_res = x
    for l in range(depth):
        qa, ka, va, qb, ks, kw, vst, vwt, kcvc, gt, wo_b, wu_b, wd_b, w1k_b, w1v_b = _in_proj(
            h_res, norm_mix[l], w_in[l], tables, (w_out[l], w_up[l], w_down[l], cmp_w1_k[l], cmp_w1_v[l]))
        kc, vct = _compress(kcvc, cmp_pe_k[l], w1k_b, cmp_w2_k[l], cmp_pe_v[l], w1v_b, cmp_w2_v[l], cmp_tables)
        oa = _mixer_a(qa, ka, va)
        ob = _nsa(qb, ks, kw, vst, vwt, kc, vct, gt)
        h_res = _post(h_res, oa, ob, g_out_a[l], g_out_b[l], wo_b, norm_mlp[l], wu_b, wd_b, norm_final,
                      final=(l == depth - 1))
    return h_res
```

```python
import functools

import jax
import jax.numpy as jnp
import numpy as np
from jax import lax
from jax.experimental import pallas as pl
from jax.experimental.pallas import tpu as pltpu

F32 = jnp.float32
BF16 = jnp.bfloat16

HEAD_DIM = 64
ROT_DIM = HEAD_DIM // 4
ROPE_THETA = 500000.0
EPS = 1e-6
NEG = -1e30
Q_SCALE = HEAD_DIM ** -0.5 * 1.4426950408889634
LANES = 128

A_HEADS = 8
A_PATTERNS = ((128, 1), (512, 4), (2048, 16))
A_BLOCK = 128
A_GROUP = 16
A_PERM_ROWS = 256

B_HEADS = 8
B_KV_HEADS = 2
B_GROUP = B_HEADS // B_KV_HEADS
CMP_LEN = 32
CMP_STRIDE = 16
CMP_HIDDEN = 256
CMP_POS_PER_DOT = 2
SEL_BLOCK = 64
SEL_SHIFT = 6
SEL_TOPK = 8
WIN = 512
N_BRANCH = 3

A_WIDTH = A_HEADS * HEAD_DIM
B_WIDTH = B_HEADS * HEAD_DIM
KV_WIDTH = B_KV_HEADS * HEAD_DIM

NSA_TQ = 256
NSA_CK = 256
NSA_SPLIT = 2
NSA_VROWS = 80
GATE_ROWS = 32
SEL_LANE0 = HEAD_DIM

N_LATER = 5

VMEM_LIMIT = 56 * 1024 * 1024


def _dot(a, b):
    return jnp.dot(a, b, preferred_element_type=F32)


def _dot_nt(a, b):
    return lax.dot_general(a, b, (((1,), (1,)), ((), ())), preferred_element_type=F32)


def _rope_rows(y, cos, sin_a, sin_b):
    outs = []
    for c in range(y.shape[1] // LANES):
        yc = y[:, c * LANES:(c + 1) * LANES]
        outs.append(yc * cos + pltpu.roll(yc, LANES - ROT_DIM // 2, 1) * sin_a
                    + pltpu.roll(yc, ROT_DIM // 2, 1) * sin_b)
    return outs[0] if len(outs) == 1 else jnp.concatenate(outs, axis=1)


def _in_proj_kernel(x_ref, g_ref, wq_ref, wkk_ref, wt_ref, cos_ref, sa_ref, sb_ref, *refs, tm):
    (later_f32, (qa_ref, ka_ref, va_ref, qb_ref, ks_ref, kw_ref, vst_ref, vwt_ref, kcvc_ref, gt_ref), later_bf16) = (
        refs[:N_LATER], refs[N_LATER:-N_LATER], refs[-N_LATER:])
    for src, dst in zip(later_f32, later_bf16):
        dst[...] = src[...].astype(BF16)
    tt = pl.program_id(1)
    x = x_ref[...]
    ms = jnp.mean(x * x, axis=-1, keepdims=True)
    h = (x * lax.rsqrt(ms + EPS) * g_ref[...]).astype(BF16)
    cos, sa, sb = cos_ref[...], sa_ref[...], sb_ref[...]
    scale = Q_SCALE

    def proj(c0, c1):
        return _dot(h, wq_ref[:, c0:c1])

    o = 0
    qa_ref[...] = (_rope_rows(proj(o, o + A_WIDTH), cos, sa, sb) * scale).astype(BF16)
    o += A_WIDTH
    ka_ref[...] = _rope_rows(proj(o, o + A_WIDTH), cos, sa, sb).astype(BF16)
    o += A_WIDTH
    va_ref[...] = proj(o, o + A_WIDTH).astype(BF16)
    o += A_WIDTH
    lane = lax.broadcasted_iota(jnp.int32, (tm, LANES), 1)
    row = lax.broadcasted_iota(jnp.int32, (tm, LANES), 0)
    lo = lane < HEAD_DIM
    qb = _rope_rows(proj(o, o + B_WIDTH), cos, sa, sb) * scale
    for hh in range(B_HEADS):
        ch = qb[:, (hh // 2) * LANES:(hh // 2 + 1) * LANES]
        if hh % 2:
            ch = pltpu.roll(ch, HEAD_DIM, 1)
        qb_ref[0, hh // B_GROUP, hh % B_GROUP] = jnp.where(lo, ch, 0.0).astype(BF16)
    o += B_WIDTH
    kcvc_ref[...] = proj(o, o + 2 * KV_WIDTH)
    o += 2 * KV_WIDTH
    ksw = _rope_rows(_dot(h, wkk_ref[...]), cos, sa, sb)

    blk = (tt * tm + row) >> SEL_SHIFT
    onehot = jnp.where(lane - SEL_LANE0 == blk, 1.0, 0.0)
    for kind, ref in ((0, ks_ref), (1, kw_ref)):
        kk = ksw[:, kind * LANES:(kind + 1) * LANES]
        tail = onehot if kind == 0 else 0.0
        ref[0, 0] = jnp.where(lo, kk, tail).astype(BF16)
        ref[0, 1] = jnp.where(lo, pltpu.roll(kk, HEAD_DIM, 1), tail).astype(BF16)

    tr = _dot_nt(wt_ref[...], h)
    ones_row = jnp.where(lax.broadcasted_iota(jnp.int32, (NSA_VROWS - HEAD_DIM, tm), 0) == 0, 1.0, 0.0)
    for kind, ref in ((0, vst_ref), (1, vwt_ref)):
        rows = [tr[kind * LANES + gg * HEAD_DIM:kind * LANES + (gg + 1) * HEAD_DIM] for gg in range(B_KV_HEADS)]
        slab = jnp.concatenate([rows[0], ones_row, rows[1], ones_row], axis=0).astype(BF16)
        for c in range(tm // NSA_CK):
            ref[0, c] = slab[:, c * NSA_CK:(c + 1) * NSA_CK]
    gt_ref[0, 0] = jax.nn.sigmoid(tr[2 * LANES:2 * LANES + GATE_ROWS, :])


def _rope_tables(pos):
    half = ROT_DIM // 2
    inv = ROPE_THETA ** (-np.arange(0, ROT_DIM, 2, dtype=np.float64) / ROT_DIM)
    ang = np.asarray(pos, np.float64)[:, None] * inv[None, :]
    cos, sin = np.cos(ang), np.sin(ang)
    n = len(pos)
    ones = np.ones((n, HEAD_DIM - ROT_DIM))
    zeros = np.zeros((n, HEAD_DIM - ROT_DIM))
    zh = np.zeros((n, half))
    c_head = np.concatenate([cos, cos, ones], axis=1)
    a_head = np.concatenate([-sin, zh, zeros], axis=1)
    b_head = np.concatenate([zh, sin, zeros], axis=1)
    rep = LANES // HEAD_DIM
    return tuple(jnp.asarray(np.tile(t, (1, rep)).astype(np.float32)) for t in (c_head, a_head, b_head))


def _in_proj(x, norm_g, w_in, tables, later, *, tm=1024):
    B, T, D = x.shape
    nt = T // tm
    assert len(later) == N_LATER and all(w.shape[0] % (16 * B * nt) == 0 for w in later)
    slab = lambda w: pl.BlockSpec((w.shape[0] // (B * nt), w.shape[1]), lambda b, t: (b * nt + t, 0))
    offs = [0]
    for n in (A_WIDTH, A_WIDTH, A_WIDTH, B_WIDTH, KV_WIDTH, KV_WIDTH, KV_WIDTH, KV_WIDTH, KV_WIDTH, KV_WIDTH,
              B_HEADS * N_BRANCH):
        offs.append(offs[-1] + n)
    col = lambda i: w_in[:, offs[i]:offs[i + 1]]
    wq = w_in[:, :offs[6]].astype(BF16)
    wkk = jnp.concatenate([col(6), col(8)], axis=1).astype(BF16)
    gpad = jnp.zeros((D, GATE_ROWS - B_HEADS * N_BRANCH), w_in.dtype)
    wt = jnp.concatenate([col(7), col(9), col(10), gpad], axis=1).T.astype(BF16)
    cos, sa, sb = tables
    nq = wq.shape[1]
    nr = wt.shape[0]
    tok = lambda w: pl.BlockSpec((None, tm, w), lambda b, t: (b, t, 0))
    const = lambda shape: pl.BlockSpec(shape, lambda b, t: (0,) * len(shape))
    tab = pl.BlockSpec((tm, LANES), lambda b, t: (t, 0))
    out_shapes = (
        jax.ShapeDtypeStruct((B, T, A_WIDTH), BF16),
        jax.ShapeDtypeStruct((B, T, A_WIDTH), BF16),
        jax.ShapeDtypeStruct((B, T, A_WIDTH), BF16),
        jax.ShapeDtypeStruct((B, B_KV_HEADS, B_GROUP, T, LANES), BF16),
        jax.ShapeDtypeStruct((B, B_KV_HEADS, T, LANES), BF16),
        jax.ShapeDtypeStruct((B, B_KV_HEADS, T, LANES), BF16),
        jax.ShapeDtypeStruct((B, T // NSA_CK, B_KV_HEADS * NSA_VROWS, NSA_CK), BF16),
        jax.ShapeDtypeStruct((B, T // NSA_CK, B_KV_HEADS * NSA_VROWS, NSA_CK), BF16),
        jax.ShapeDtypeStruct((B, T, 2 * KV_WIDTH), F32),
        jax.ShapeDtypeStruct((B, nt, GATE_ROWS, tm), F32),
    )
    frame = pl.BlockSpec((1, B_KV_HEADS, tm, LANES), lambda b, t: (b, 0, t, 0))
    vt = pl.BlockSpec((1, tm // NSA_CK, B_KV_HEADS * NSA_VROWS, NSA_CK), lambda b, t: (b, t, 0, 0))
    qframe = pl.BlockSpec((1, B_KV_HEADS, B_GROUP, tm, LANES), lambda b, t: (b, 0, 0, t, 0))
    out_specs = (tok(A_WIDTH), tok(A_WIDTH), tok(A_WIDTH), qframe, frame, frame, vt, vt,
                 tok(2 * KV_WIDTH), pl.BlockSpec((1, 1, GATE_ROWS, tm), lambda b, t: (b, t, 0, 0)))
    return pl.pallas_call(
        functools.partial(_in_proj_kernel, tm=tm),
        grid=(B, nt),
        in_specs=[tok(D), const((1, D)), const((D, nq)), const((D, 2 * KV_WIDTH)), const((nr, D)), tab, tab, tab]
        + [slab(w) for w in later],
        out_specs=out_specs + tuple(slab(w) for w in later),
        out_shape=out_shapes + tuple(jax.ShapeDtypeStruct(w.shape, BF16) for w in later),
        compiler_params=pltpu.CompilerParams(dimension_semantics=("parallel", "parallel"),
                                             vmem_limit_bytes=VMEM_LIMIT),
        name="in_proj",
    )(x, norm_g.reshape(1, D), wq, wkk, wt, cos, sa, sb, *later)


def _compress_kernel(ak_ref, av_ref, w1k_ref, w1v_ref, pek_ref, pev_ref, w2k_ref, w2vt_ref, cos_ref, sa_ref, sb_ref,
                     kc_ref, vct_ref, *, nc_pad):
    half = CMP_LEN // 2
    hid_w = B_KV_HEADS * CMP_HIDDEN

    zeros = jnp.zeros((HEAD_DIM, CMP_HIDDEN), BF16)

    def both_groups(w):
        return jnp.concatenate([jnp.concatenate([w, zeros], axis=1), jnp.concatenate([zeros, w], axis=1)], axis=0)

    def hidden(a_ref, w1_ref, pe_ref):
        acc_u = jnp.zeros((nc_pad, hid_w), F32)
        acc_v = jnp.zeros((nc_pad, hid_w), F32)
        for p in range(0, half, CMP_POS_PER_DOT):
            ks = range(CMP_POS_PER_DOT)
            aps = [a_ref[0, pl.ds(p + k, nc_pad, stride=CMP_STRIDE), :] for k in ks]

            def partial_hidden(base):
                lhs = jnp.concatenate([(aps[k] + pe_ref[base + p + k:base + p + k + 1, :]).astype(BF16) for k in ks],
                                      axis=1)
                return _dot(lhs, jnp.concatenate([both_groups(w1_ref[base + p + k]) for k in ks], axis=0))

            acc_u = acc_u + partial_hidden(0)
            acc_v = acc_v + partial_hidden(half)
        return jax.nn.gelu(acc_u + pltpu.roll(acc_v, nc_pad - 1, 0))

    hk = hidden(ak_ref, w1k_ref, pek_ref).astype(BF16)
    hv = hidden(av_ref, w1v_ref, pev_ref).astype(BF16)
    for g in range(B_KV_HEADS):
        hg = hk[:, g * CMP_HIDDEN:(g + 1) * CMP_HIDDEN]
        kc = _dot(hg, w2k_ref[...])
        kc_ref[0, g] = _rope_rows(kc, cos_ref[...], sa_ref[...], sb_ref[...]).astype(BF16)
        vg = hv[:, g * CMP_HIDDEN:(g + 1) * CMP_HIDDEN]
        vct_ref[0, g] = _dot_nt(w2vt_ref[...], vg).astype(BF16)


def _compress(kcvc, pe_k, w1_k, w2_k, pe_v, w1_v, w2_v, cmp_tables):
    B, T, _ = kcvc.shape
    nc_pad = T // CMP_STRIDE
    w2k = jnp.concatenate([w2_k, jnp.zeros_like(w2_k)], axis=1).astype(BF16)
    w2vt = w2_v.T.astype(BF16)
    pek = jnp.tile(pe_k, (1, B_KV_HEADS))
    pev = jnp.tile(pe_v, (1, B_KV_HEADS))
    per_pos = lambda w1: w1.reshape(CMP_LEN, HEAD_DIM, CMP_HIDDEN).astype(BF16)
    const = lambda shape: pl.BlockSpec(shape, lambda b: (0,) * len(shape))
    cos, sa, sb = cmp_tables
    return pl.pallas_call(
        functools.partial(_compress_kernel, nc_pad=nc_pad),
        grid=(B,),
        in_specs=[pl.BlockSpec((1, T, KV_WIDTH), lambda b: (b, 0, 0)), pl.BlockSpec((1, T, KV_WIDTH), lambda b: (b, 0, 1)),
                  const((CMP_LEN, HEAD_DIM, CMP_HIDDEN)), const((CMP_LEN, HEAD_DIM, CMP_HIDDEN)),
                  const((CMP_LEN, LANES)), const((CMP_LEN, LANES)),
                  const((CMP_HIDDEN, LANES)), const((HEAD_DIM, CMP_HIDDEN)),
                  const((nc_pad, LANES)), const((nc_pad, LANES)), const((nc_pad, LANES))],
        out_specs=(pl.BlockSpec((1, B_KV_HEADS, nc_pad, LANES), lambda b: (b, 0, 0, 0)),
                   pl.BlockSpec((1, B_KV_HEADS, HEAD_DIM, nc_pad), lambda b: (b, 0, 0, 0))),
        out_shape=(jax.ShapeDtypeStruct((B, B_KV_HEADS, nc_pad, LANES), BF16),
                   jax.ShapeDtypeStruct((B, B_KV_HEADS, HEAD_DIM, nc_pad), BF16)),
        compiler_params=pltpu.CompilerParams(dimension_semantics=("parallel",), vmem_limit_bytes=VMEM_LIMIT),
        name="compress",
    )(kcvc, kcvc, per_pos(w1_k), per_pos(w1_v), pek, pev, w2k, w2vt, cos, sa, sb)


def _mixer_a_kernel(q_ref, k_ref, v_ref, perm_ref, bias_ref, o_ref, qd0, qd1, kd, vd0, vd1, u_s, m_s, l_s, *, seq):
    blk = A_BLOCK
    nres = seq // blk
    npat = len(A_PATTERNS)
    order = sorted(range(npat), key=lambda p: -A_PATTERNS[p][1])
    slot = {p: n for n, p in enumerate(order[:-1])}
    lane = lax.broadcasted_iota(jnp.int32, (blk, LANES), 1)
    lo = lane < HEAD_DIM

    chunk = perm_ref.shape[0]
    per = chunk // nres
    lo_p = lax.broadcasted_iota(jnp.int32, (per, LANES), 1) < HEAD_DIM

    def deinterleave(c):
        src = pl.ds(c * chunk, chunk)
        q, k, v = (_dot(perm_ref[...], ref[0, src, :]) for ref in (q_ref, k_ref, v_ref))
        for r in range(nres):
            rows = pl.ds(r * blk + c * per, per)
            qr, vr = q[r * per:(r + 1) * per], v[r * per:(r + 1) * per]
            qd0[rows, :] = jnp.where(lo_p, qr, 0.0)
            qd1[rows, :] = jnp.where(lo_p, 0.0, qr)
            kd[rows, :] = k[r * per:(r + 1) * per]
            vd0[rows, :] = jnp.where(lo_p, vr, 1.0)
            vd1[rows, :] = jnp.where(lo_p, 1.0, vr)

    def pieces(dil, rd, row_off, rows):
        return [pl.ds((rd + dil * jj) * blk + row_off, rows) for jj in range(nres // dil)]

    def gather(ref, idx, lead=()):
        parts = [ref[lead + (i, slice(None))] for i in idx]
        return parts[0] if len(parts) == 1 else jnp.concatenate(parts, axis=0)

    def attend(pi, blocks):
        dil = A_PATTERNS[pi][1]
        pr = blk // (nres // dil)
        q_idxs, vbs, scores = [], [], []
        for rd, n, first in blocks:
            q_idx = pieces(dil, rd, n * pr, pr)
            k_idx = q_idx if first else pieces(dil, rd, (n - 1) * pr, 2 * pr)
            bias = bias_ref[pi, :, 0:blk] if first else bias_ref[pi, :, blk:3 * blk]
            kb = gather(kd, k_idx).astype(BF16)
            q_idxs.append(q_idx)
            for qd, vd in ((qd0, vd0), (qd1, vd1)):
                vbs.append(gather(vd, k_idx).astype(BF16))
                scores.append(_dot_nt(gather(qd, q_idx).astype(BF16), kb) + bias)
        es, ms = [], []
        for s in scores:
            m = jnp.max(s, axis=-1, keepdims=True)
            ms.append(m)
            es.append(jnp.exp2((s - m).astype(BF16)))
        pvs = [_dot(e, vb) for e, vb in zip(es, vbs)]
        for b, q_idx in enumerate(q_idxs):
            u = jnp.where(lo, pvs[2 * b], pvs[2 * b + 1])
            l_swapped = jnp.where(lo, pvs[2 * b + 1], pvs[2 * b])
            m = jnp.where(lo, ms[2 * b], ms[2 * b + 1])
            if pi != order[-1]:
                for jj, idx in enumerate(q_idx):
                    u_s[slot[pi], idx, :] = u[jj * pr:(jj + 1) * pr]
                    m_s[slot[pi], idx, :] = m[jj * pr:(jj + 1) * pr]
                    l_s[slot[pi], idx, :] = l_swapped[jj * pr:(jj + 1) * pr]
                continue
            parts = [(u, m, l_swapped)] + [(gather(u_s, q_idx, (sl,)), gather(m_s, q_idx, (sl,)),
                                            gather(l_s, q_idx, (sl,))) for sl in slot.values()]
            m_all = functools.reduce(jnp.maximum, [mm for _, mm, _ in parts])
            num = jnp.zeros((blk, LANES), F32)
            den = jnp.zeros((blk, LANES), F32)
            for uu, mm, ll in parts:
                a = jnp.exp2(mm - m_all)
                num = num + a * uu
                den = den + a * pltpu.roll(ll, HEAD_DIM, 1)
            out = num / den
            n = blocks[b][1]
            for jj in range(nres):
                o_ref[0, pl.ds(n * blk + jj, pr, stride=nres), :] = out[jj * pr:(jj + 1) * pr]

    for c in range(seq // chunk):
        deinterleave(c)
    for pi in order:
        dil = A_PATTERNS[pi][1]
        nb = seq // dil // blk
        blocks = [(rd, n, n == 0) for rd in range(dil) for n in range(nb)]
        for g0 in range(0, len(blocks), A_GROUP):
            attend(pi, blocks[g0:g0 + A_GROUP])


def _mixer_a_bias(nres):
    blk = A_BLOCK

    def sub_pos(i, fold, rows):
        return fold * (i % rows) + i // rows

    out = np.zeros((len(A_PATTERNS), blk, 3 * blk), np.float32)
    for pi, (window, dil) in enumerate(A_PATTERNS):
        n_back = window // dil
        fold = nres // dil
        pr = blk // fold
        sq = sub_pos(np.arange(blk), fold, pr)[:, None]
        d_first = sq - sub_pos(np.arange(blk), fold, pr)[None, :]
        d_band = sq + blk - sub_pos(np.arange(2 * blk), fold, 2 * pr)[None, :]
        dist = np.concatenate([d_first, d_band], axis=1)
        out[pi] = np.where((dist >= 0) & (dist <= n_back), 0.0, NEG)
    return jnp.asarray(out)


def _mixer_a(qa, ka, va):
    B, T, W = qa.shape
    npair = W // LANES
    spec = pl.BlockSpec((1, T, LANES), lambda b, p: (b, 0, p))
    npat = len(A_PATTERNS)
    nres = T // A_BLOCK
    dils = sorted(d for _, d in A_PATTERNS)
    assert all(nres % d == 0 and T % (d * A_BLOCK) == 0 for d in dils) and dils[0] == 1 and dils[-1] == nres
    assert T % A_PERM_ROWS == 0 and A_PERM_ROWS % nres == 0
    dst = np.arange(A_PERM_ROWS)
    perm = np.zeros((A_PERM_ROWS, A_PERM_ROWS), np.float32)
    perm[dst, (dst % (A_PERM_ROWS // nres)) * nres + dst // (A_PERM_ROWS // nres)] = 1.0
    return pl.pallas_call(
        functools.partial(_mixer_a_kernel, seq=T),
        grid=(B, npair),
        in_specs=[spec, spec, spec, pl.BlockSpec((A_PERM_ROWS, A_PERM_ROWS), lambda b, p: (0, 0)),
                  pl.BlockSpec((npat, A_BLOCK, 3 * A_BLOCK), lambda b, p: (0, 0, 0))],
        out_specs=spec,
        out_shape=jax.ShapeDtypeStruct((B, T, W), F32),
        scratch_shapes=[pltpu.VMEM((T, LANES), F32)] * 5 + [pltpu.VMEM((npat - 1, T, LANES), F32)] * 3,
        compiler_params=pltpu.CompilerParams(dimension_semantics=("parallel", "parallel"),
                                             vmem_limit_bytes=VMEM_LIMIT),
        name="mixer_a",
    )(qa, ka, va, jnp.asarray(perm, BF16), _mixer_a_bias(nres))


def _nsa_kernel(q_ref, ks_ref, kw_ref, vst_ref, vwt_ref, kc_ref, vct_ref, gt_ref, ovt_ref, bias_ref, cbias_ref, o_ref,
                *, n_sel_blocks, nq):
    tq, ck = NSA_TQ, NSA_CK
    nrow = B_GROUP * tq
    gw = B_GROUP * HEAD_DIM
    step = pl.program_id(1)
    per_part = nq // NSA_SPLIT
    lanes = [(h, g) for h in range(NSA_SPLIT) for g in range(B_KV_HEADS)]
    qi = [step + h * per_part for h in range(NSA_SPLIT)]
    col_i = lax.broadcasted_iota(jnp.int32, (1, nrow), 1) & (tq - 1)

    def chunk(ref, g, c):
        return ref[0, g, pl.ds(pl.multiple_of(c * ck, ck), ck), :]

    def vt_chunk(ref, g, c):
        return ref[0, c, g * NSA_VROWS:(g + 1) * NSA_VROWS, :]

    def softmax_pv(s, vt):
        m = jnp.max(s, axis=0, keepdims=True)
        return m, _dot(vt, jnp.exp2(s - m).astype(BF16))

    def merged(parts):
        m_new = functools.reduce(jnp.maximum, [m for m, _ in parts])
        acc = sum(jnp.exp2(m - m_new) * a for m, a in parts)
        return m_new, acc

    def normalised(state):
        return state[1][0:HEAD_DIM] * (1.0 / state[1][HEAD_DIM:HEAD_DIM + 1])

    q_b = {(h, g): jnp.concatenate([q_ref[0, g, r, h] for r in range(B_GROUP)], axis=0) for h, g in lanes}

    nwin = WIN // ck
    win_chunks = [[jnp.maximum(qi[h] - back, 0) for back in range(nwin, -1, -1)] for h in range(NSA_SPLIT)]
    s_win = {}
    for h, g in lanes:
        s = _dot_nt(jnp.concatenate([chunk(kw_ref, g, c) for c in win_chunks[h]], axis=0), q_b[h, g])
        exists = lambda back: True if h * per_part >= back else qi[h] >= back
        masked = lambda piece, back: piece if exists(back) is True else jnp.where(exists(back), piece, NEG)
        pieces = [masked(s[0:ck] + bias_ref[1], nwin)]
        for n in range(1, nwin):
            pieces.append(masked(s[n * ck:(n + 1) * ck], nwin - n))
        pieces.append(s[nwin * ck:] + bias_ref[0])
        s_win[h, g] = pieces

    s_cmp = {(h, g): _dot_nt(kc_ref[0, g], q_b[h, g]) + cbias_ref[h, 0] for h, g in lanes}
    p_cmp, o_cmp = {}, {}
    for h, g in lanes:
        m = jnp.max(s_cmp[h, g], axis=0, keepdims=True)
        e = jnp.exp2(s_cmp[h, g] - m)
        den = jnp.sum(e, axis=0, keepdims=True)
        sees_block = qi[h] * tq + col_i >= CMP_LEN - 1
        p_cmp[h, g] = e * jnp.where(sees_block, 1.0 / jnp.maximum(den, 1e-30), 0.0)
        o_cmp[h, g] = _dot(vct_ref[0, g], p_cmp[h, g].astype(BF16))

    imp = {}
    for h, g in lanes:
        psum = p_cmp[h, g][:, 0:tq]
        for r in range(1, B_GROUP):
            psum = psum + p_cmp[h, g][:, r * tq:(r + 1) * tq]
        p_hi = psum.astype(BF16)
        p_lo = (psum - p_hi.astype(F32)).astype(BF16)
        imp[h, g] = _dot(ovt_ref[...], p_hi) + _dot(ovt_ref[...], p_lo)
    j = lax.broadcasted_iota(jnp.int32, (n_sel_blocks, tq), 0)
    j_f = j.astype(F32)
    low = -3e38
    q_aug = {}
    for h, g in lanes:
        cur = (qi[h] * tq + lax.broadcasted_iota(jnp.int32, (n_sel_blocks, tq), 1)) >> SEL_SHIFT
        forced = (j == 0) | (j == cur) | (j == cur - 1)
        score = jnp.where(forced, imp[h, g] + 2.0, jnp.where(j > cur, -1.0, imp[h, g]))
        sel = jnp.zeros((n_sel_blocks, tq), jnp.bool_)
        for _ in range(min(SEL_TOPK, n_sel_blocks)):
            mx = jnp.max(score, axis=0, keepdims=True)
            first = jnp.min(jnp.where(score == mx, j_f, 4.0 * LANES), axis=0, keepdims=True)
            hit = j_f == first
            sel = sel | hit
            score = jnp.where(hit, low, score)
        selneg = jnp.concatenate([jnp.zeros((SEL_LANE0, tq), F32), jnp.where(sel, 0.0, NEG),
                                  jnp.zeros((LANES - SEL_LANE0 - n_sel_blocks, tq), F32)], axis=0).T
        q_aug[h, g] = q_b[h, g] + jnp.concatenate([selneg.astype(BF16)] * B_GROUP, axis=0)

    s_diag = {(h, g): _dot_nt(chunk(ks_ref, g, qi[h]), q_aug[h, g]) + bias_ref[0] for h, g in lanes}
    o_win = {(h, g): normalised(merged([softmax_pv(s_win[h, g][n], vt_chunk(vwt_ref, g, c))
                                        for n, c in enumerate(win_chunks[h])])) for h, g in lanes}
    states = tuple(softmax_pv(s_diag[h, g], vt_chunk(vst_ref, g, qi[h])) for h, g in lanes)

    def earlier(work, st):
        s = [_dot_nt(chunk(ks_ref, lanes[n][1], c), q_aug[lanes[n]]) for n, c in work]
        parts = [[st[n]] for n in range(len(lanes))]
        for (n, c), sc in zip(work, s):
            parts[n].append(softmax_pv(sc, vt_chunk(vst_ref, lanes[n][1], c)))
        return tuple(merged(p) if len(p) > 1 else p[0] for p in parts)

    states = lax.fori_loop(0, step, lambda c, st: earlier([(n, c) for n in range(len(lanes))], st), states)
    for h in range(1, NSA_SPLIT):
        mine = [n for n, (hh, _) in enumerate(lanes) if hh >= h]
        states = earlier([(n, step + (h - 1) * per_part + k) for n in mine for k in range(per_part)], states)

    for n, (h, g) in enumerate(lanes):
        o_sel = normalised(states[n])
        gt = gt_ref[0, h, g * B_GROUP * N_BRANCH:(g + 1) * B_GROUP * N_BRANCH, :]
        outs = []
        for r in range(B_GROUP):
            cs = slice(r * tq, (r + 1) * tq)
            outs.append(gt[r * N_BRANCH:r * N_BRANCH + 1, :] * o_cmp[h, g][:, cs]
                        + gt[r * N_BRANCH + 1:r * N_BRANCH + 2, :] * o_sel[:, cs]
                        + gt[r * N_BRANCH + 2:r * N_BRANCH + 3, :] * o_win[h, g][:, cs])
        for c in range(B_GROUP // 2):
            pair = jnp.concatenate([outs[2 * c], outs[2 * c + 1]], axis=0)
            o_ref[0, h, :, g * gw + c * LANES:g * gw + (c + 1) * LANES] = pair.T


def _overlap_t(nc_pad, ns):
    nc = nc_pad - 1
    c0 = np.arange(nc_pad) * CMP_STRIDE
    s0 = np.arange(ns) * SEL_BLOCK
    ov = np.minimum(c0[None, :] + CMP_LEN, s0[:, None] + SEL_BLOCK) - np.maximum(c0[None, :], s0[:, None])
    ov = np.clip(ov, 0, None).astype(np.float32) / CMP_LEN
    ov = np.where(np.arange(nc_pad)[None, :] < nc, ov, 0.0)
    return jnp.asarray(ov, BF16)


def _nsa(qb, ks, kw, vst, vwt, kc, vct, gt):
    B, _, _, T, _ = qb.shape
    W = B_WIDTH
    nq = T // NSA_TQ
    ns = T // SEL_BLOCK
    nc_pad = kc.shape[2]
    nrow = B_GROUP * NSA_TQ
    part = T // NSA_SPLIT
    assert NSA_TQ == NSA_CK and WIN % NSA_CK == 0 and SEL_LANE0 + ns <= LANES and (nq // NSA_SPLIT) % 2 == 0
    assert gt.shape == (B, NSA_SPLIT, GATE_ROWS, part)
    kspec = pl.BlockSpec((1, B_KV_HEADS, T, LANES), lambda b, i: (b, 0, 0, 0))
    vspec = pl.BlockSpec((1, T // NSA_CK, B_KV_HEADS * NSA_VROWS, NSA_CK), lambda b, i: (b, 0, 0, 0))
    key = np.arange(NSA_CK)[:, None]
    qry = (np.arange(nrow) % NSA_TQ)[None, :]
    bias = jnp.asarray(np.stack([np.where(key <= qry, 0.0, NEG), np.where(key > qry, 0.0, NEG)]).astype(np.float32))
    cmp_end = (np.arange(nc_pad) * CMP_STRIDE + CMP_LEN - 1)[None, :, None]
    t_query = (np.arange(nq) * NSA_TQ)[:, None, None] + qry[None]
    cbias = jnp.asarray(np.where((cmp_end <= t_query) & (np.arange(nc_pad) < nc_pad - 1)[None, :, None], 0.0, NEG)
                        .astype(np.float32)).reshape(NSA_SPLIT, nq // NSA_SPLIT, nc_pad, nrow)
    out = pl.pallas_call(
        functools.partial(_nsa_kernel, n_sel_blocks=ns, nq=nq),
        grid=(B, nq // NSA_SPLIT),
        in_specs=[pl.BlockSpec((1, B_KV_HEADS, B_GROUP, NSA_SPLIT, NSA_TQ, LANES), lambda b, i: (b, 0, 0, 0, i, 0)),
                  kspec, kspec, vspec, vspec,
                  pl.BlockSpec((1, B_KV_HEADS, nc_pad, LANES), lambda b, i: (b, 0, 0, 0)),
                  pl.BlockSpec((1, B_KV_HEADS, HEAD_DIM, nc_pad), lambda b, i: (b, 0, 0, 0)),
                  pl.BlockSpec((1, NSA_SPLIT, GATE_ROWS, NSA_TQ), lambda b, i: (b, 0, 0, i)),
                  pl.BlockSpec((ns, nc_pad), lambda b, i: (0, 0)),
                  pl.BlockSpec((2, NSA_CK, nrow), lambda b, i: (0, 0, 0)),
                  pl.BlockSpec((NSA_SPLIT, 1, nc_pad, nrow), lambda b, i: (0, i, 0, 0))],
        out_specs=pl.BlockSpec((1, NSA_SPLIT, NSA_TQ, W), lambda b, i: (b, 0, i, 0)),
        out_shape=jax.ShapeDtypeStruct((B, NSA_SPLIT, part, W), F32),
        compiler_params=pltpu.CompilerParams(dimension_semantics=("parallel", "arbitrary"),
                                             vmem_limit_bytes=VMEM_LIMIT),
        name="nsa",
    )(qb.reshape(B, B_KV_HEADS, B_GROUP, NSA_SPLIT, part, LANES), ks, kw, vst, vwt, kc, vct,
      gt, _overlap_t(nc_pad, ns), bias, cbias)
    return out.reshape(B, T, W)


def _post_kernel(x_ref, oa_ref, ob_ref, ga_ref, gb_ref, wo_ref, gm_ref, wu_ref, wd_ref, gf_ref, o_ref, *, final):
    def norm(v, g):
        return v * lax.rsqrt(jnp.mean(v * v, axis=-1, keepdims=True) + EPS) * g

    na = norm(oa_ref[...], ga_ref[...]).astype(BF16)
    nb = norm(ob_ref[...], gb_ref[...]).astype(BF16)
    aw = na.shape[1]
    h_res = x_ref[...] + _dot(na, wo_ref[0:aw, :]) + _dot(nb, wo_ref[aw:, :])
    h = norm(h_res, gm_ref[...]).astype(BF16)
    u = jnp.square(jnp.maximum(_dot(h, wu_ref[...]), 0.0)).astype(BF16)
    acc = h_res + _dot(u, wd_ref[...])
    o_ref[...] = norm(acc, gf_ref[...]) if final else acc


def _post(x, oa, ob, g_a, g_b, w_out, g_mlp, w_up, w_down, g_final, *, final, tm=512):
    B, T, D = x.shape
    n = B * T
    dff = w_up.shape[1]
    tok = lambda w: pl.BlockSpec((tm, w), lambda i: (i, 0))
    const = lambda shape: pl.BlockSpec(shape, lambda i: (0, 0), pipeline_mode=pl.Buffered(1))
    out = pl.pallas_call(
        functools.partial(_post_kernel, final=final),
        grid=(n // tm,),
        in_specs=[tok(D), tok(A_WIDTH), tok(B_WIDTH), const((1, A_WIDTH)), const((1, B_WIDTH)),
                  const((A_WIDTH + B_WIDTH, D)), const((1, D)), const((D, dff)), const((dff, D)), const((1, D))],
        out_specs=tok(D),
        out_shape=jax.ShapeDtypeStruct((n, D), F32),
        compiler_params=pltpu.CompilerParams(dimension_semantics=("parallel",), vmem_limit_bytes=VMEM_LIMIT),
        name="post",
    )(x.reshape(n, D), oa.reshape(n, A_WIDTH), ob.reshape(n, B_WIDTH), g_a.reshape(1, -1), g_b.reshape(1, -1),
      w_out.astype(BF16), g_mlp.reshape(1, D), w_up.astype(BF16), w_down.astype(BF16), g_final.reshape(1, D))
    return out.reshape(B, T, D)


def kernel(x, norm_mix, w_in, cmp_pe_k, cmp_w1_k, cmp_w2_k, cmp_pe_v, cmp_w1_v, cmp_w2_v, g_out_a, g_out_b,
           w_out, norm_mlp, w_up, w_down, norm_final):
    B, T, D = x.shape
    depth = w_in.shape[0]
    tables = _rope_tables(np.arange(T))
    cmp_tables = _rope_tables(np.arange(T // CMP_STRIDE) * CMP_STRIDE + CMP_LEN - 1)
    h_res = x
    for l in range(depth):
        qa, ka, va, qb, ks, kw, vst, vwt, kcvc, gt, wo_b, wu_b, wd_b, w1k_b, w1v_b = _in_proj(
            h_res, norm_mix[l], w_in[l], tables, (w_out[l], w_up[l], w_down[l], cmp_w1_k[l], cmp_w1_v[l]))
        kc, vct = _compress(kcvc, cmp_pe_k[l], w1k_b, cmp_w2_k[l], cmp_pe_v[l], w1v_b, cmp_w2_v[l], cmp_tables)
        oa = _mixer_a(qa, ka, va)
        ob = _nsa(qb, ks, kw, vst, vwt, kc, vct, gt)
        h_res = _post(h_res, oa, ob, g_out_a[l], g_out_b[l], wo_b, norm_mlp[l], wu_b, wd_b, norm_final,
                      final=(l == depth - 1))
    return h_res
```

```python
import functools

import jax
import jax.numpy as jnp
import numpy as np
from jax import lax
from jax.experimental import pallas as pl
from jax.experimental.pallas import tpu as pltpu

F32 = jnp.float32
BF16 = jnp.bfloat16

HEAD_DIM = 64
ROT_DIM = HEAD_DIM // 4
ROPE_THETA = 500000.0
EPS = 1e-6
NEG = -1e30
Q_SCALE = HEAD_DIM ** -0.5 * 1.4426950408889634
LANES = 128

A_HEADS = 8
A_PATTERNS = ((128, 1), (512, 4), (2048, 16))
A_BLOCK = 128
A_GROUP = 16

B_HEADS = 8
B_KV_HEADS = 2
B_GROUP = B_HEADS // B_KV_HEADS
CMP_LEN = 32
CMP_STRIDE = 16
CMP_HIDDEN = 256
CMP_POS_PER_DOT = 2
SEL_BLOCK = 64
SEL_SHIFT = 6
SEL_TOPK = 8
WIN = 512
N_BRANCH = 3

A_WIDTH = A_HEADS * HEAD_DIM
B_WIDTH = B_HEADS * HEAD_DIM
KV_WIDTH = B_KV_HEADS * HEAD_DIM

NSA_TQ = 256
NSA_CK = 256
NSA_SPLIT = 2
NSA_VROWS = 80
GATE_ROWS = 32
SEL_LANE0 = HEAD_DIM

N_LATER = 5
X_SLOTS = 3

VMEM_LIMIT = 56 * 1024 * 1024


def _dot(a, b):
    return jnp.dot(a, b, preferred_element_type=F32)


def _dot_nt(a, b):
    return lax.dot_general(a, b, (((1,), (1,)), ((), ())), preferred_element_type=F32)


def _rope_rows(y, cos, sin_a, sin_b):
    outs = []
    for c in range(y.shape[1] // LANES):
        yc = y[:, c * LANES:(c + 1) * LANES]
        outs.append(yc * cos + pltpu.roll(yc, LANES - ROT_DIM // 2, 1) * sin_a
                    + pltpu.roll(yc, ROT_DIM // 2, 1) * sin_b)
    return outs[0] if len(outs) == 1 else jnp.concatenate(outs, axis=1)


def _in_proj_kernel(x_hbm, g_ref, wq_ref, wkk_ref, wt_ref, cos_ref, sa_ref, sb_ref, *refs, tm, nt, nsteps):
    refs, (xbuf, xsem) = refs[:-2], refs[-2:]
    (later_f32, (qa_ref, ka_ref, va_ref, qb_ref, ks_ref, kw_ref, vst_ref, vwt_ref, kcvc_ref, gt_ref), later_bf16) = (
        refs[:N_LATER], refs[N_LATER:-N_LATER], refs[-N_LATER:])
    tt = pl.program_id(1)
    step = pl.program_id(0) * nt + tt

    def x_copy(s):
        return pltpu.make_async_copy(x_hbm.at[s // nt, pl.ds((s % nt) * tm, tm), :], xbuf.at[s % X_SLOTS],
                                     xsem.at[s % X_SLOTS])

    @pl.when(step == 0)
    def _():
        for s in range(X_SLOTS - 1):
            x_copy(s).start()

    @pl.when(step + X_SLOTS - 1 < nsteps)
    def _():
        x_copy(step + X_SLOTS - 1).start()

    for src, dst in zip(later_f32, later_bf16):
        dst[...] = src[...].astype(BF16)
    x_copy(step).wait()
    x = xbuf[step % X_SLOTS]
    ms = jnp.mean(x * x, axis=-1, keepdims=True)
    h = (x * lax.rsqrt(ms + EPS) * g_ref[...]).astype(BF16)
    cos, sa, sb = cos_ref[...], sa_ref[...], sb_ref[...]
    scale = Q_SCALE

    def proj(c0, c1):
        return _dot(h, wq_ref[:, c0:c1])

    o = 0
    qa_ref[...] = _rope_rows(proj(o, o + A_WIDTH), cos, sa, sb) * scale
    o += A_WIDTH
    ka_ref[...] = _rope_rows(proj(o, o + A_WIDTH), cos, sa, sb)
    o += A_WIDTH
    va_ref[...] = proj(o, o + A_WIDTH)
    o += A_WIDTH
    lane = lax.broadcasted_iota(jnp.int32, (tm, LANES), 1)
    row = lax.broadcasted_iota(jnp.int32, (tm, LANES), 0)
    lo = lane < HEAD_DIM
    qb = _rope_rows(proj(o, o + B_WIDTH), cos, sa, sb) * scale
    for hh in range(B_HEADS):
        ch = qb[:, (hh // 2) * LANES:(hh // 2 + 1) * LANES]
        if hh % 2:
            ch = pltpu.roll(ch, HEAD_DIM, 1)
        qb_ref[0, hh // B_GROUP, hh % B_GROUP] = jnp.where(lo, ch, 0.0).astype(BF16)
    o += B_WIDTH
    kcvc_ref[...] = proj(o, o + 2 * KV_WIDTH)
    o += 2 * KV_WIDTH
    ksw = _rope_rows(_dot(h, wkk_ref[...]), cos, sa, sb)

    blk = (tt * tm + row) >> SEL_SHIFT
    onehot = jnp.where(lane - SEL_LANE0 == blk, 1.0, 0.0)
    for kind, ref in ((0, ks_ref), (1, kw_ref)):
        kk = ksw[:, kind * LANES:(kind + 1) * LANES]
        tail = onehot if kind == 0 else 0.0
        ref[0, 0] = jnp.where(lo, kk, tail).astype(BF16)
        ref[0, 1] = jnp.where(lo, pltpu.roll(kk, HEAD_DIM, 1), tail).astype(BF16)

    tr = _dot_nt(wt_ref[...], h)
    ones_row = jnp.where(lax.broadcasted_iota(jnp.int32, (NSA_VROWS - HEAD_DIM, tm), 0) == 0, 1.0, 0.0)
    for kind, ref in ((0, vst_ref), (1, vwt_ref)):
        rows = [tr[kind * LANES + gg * HEAD_DIM:kind * LANES + (gg + 1) * HEAD_DIM] for gg in range(B_KV_HEADS)]
        slab = jnp.concatenate([rows[0], ones_row, rows[1], ones_row], axis=0).astype(BF16)
        for c in range(tm // NSA_CK):
            ref[0, c] = slab[:, c * NSA_CK:(c + 1) * NSA_CK]
    gt_ref[0, 0] = jax.nn.sigmoid(tr[2 * LANES:2 * LANES + GATE_ROWS, :])


def _rope_tables(pos):
    half = ROT_DIM // 2
    inv = ROPE_THETA ** (-np.arange(0, ROT_DIM, 2, dtype=np.float64) / ROT_DIM)
    ang = np.asarray(pos, np.float64)[:, None] * inv[None, :]
    cos, sin = np.cos(ang), np.sin(ang)
    n = len(pos)
    ones = np.ones((n, HEAD_DIM - ROT_DIM))
    zeros = np.zeros((n, HEAD_DIM - ROT_DIM))
    zh = np.zeros((n, half))
    c_head = np.concatenate([cos, cos, ones], axis=1)
    a_head = np.concatenate([-sin, zh, zeros], axis=1)
    b_head = np.concatenate([zh, sin, zeros], axis=1)
    rep = LANES // HEAD_DIM
    return tuple(jnp.asarray(np.tile(t, (1, rep)).astype(np.float32)) for t in (c_head, a_head, b_head))


def _in_proj(x, norm_g, w_in, tables, later, *, tm=1024):
    B, T, D = x.shape
    nt = T // tm
    assert len(later) == N_LATER and all(w.shape[0] % (16 * B * nt) == 0 for w in later)
    slab = lambda w: pl.BlockSpec((w.shape[0] // (B * nt), w.shape[1]), lambda b, t: (b * nt + t, 0))
    offs = [0]
    for n in (A_WIDTH, A_WIDTH, A_WIDTH, B_WIDTH, KV_WIDTH, KV_WIDTH, KV_WIDTH, KV_WIDTH, KV_WIDTH, KV_WIDTH,
              B_HEADS * N_BRANCH):
        offs.append(offs[-1] + n)
    col = lambda i: w_in[:, offs[i]:offs[i + 1]]
    wq = w_in[:, :offs[6]].astype(BF16)
    wkk = jnp.concatenate([col(6), col(8)], axis=1).astype(BF16)
    gpad = jnp.zeros((D, GATE_ROWS - B_HEADS * N_BRANCH), w_in.dtype)
    wt = jnp.concatenate([col(7), col(9), col(10), gpad], axis=1).T.astype(BF16)
    cos, sa, sb = tables
    nq = wq.shape[1]
    nr = wt.shape[0]
    tok = lambda w: pl.BlockSpec((None, tm, w), lambda b, t: (b, t, 0))
    const = lambda shape: pl.BlockSpec(shape, lambda b, t: (0,) * len(shape))
    tab = pl.BlockSpec((tm, LANES), lambda b, t: (t, 0))
    out_shapes = (
        jax.ShapeDtypeStruct((B, T, A_WIDTH), F32),
        jax.ShapeDtypeStruct((B, T, A_WIDTH), F32),
        jax.ShapeDtypeStruct((B, T, A_WIDTH), F32),
        jax.ShapeDtypeStruct((B, B_KV_HEADS, B_GROUP, T, LANES), BF16),
        jax.ShapeDtypeStruct((B, B_KV_HEADS, T, LANES), BF16),
        jax.ShapeDtypeStruct((B, B_KV_HEADS, T, LANES), BF16),
        jax.ShapeDtypeStruct((B, T // NSA_CK, B_KV_HEADS * NSA_VROWS, NSA_CK), BF16),
        jax.ShapeDtypeStruct((B, T // NSA_CK, B_KV_HEADS * NSA_VROWS, NSA_CK), BF16),
        jax.ShapeDtypeStruct((B, T, 2 * KV_WIDTH), F32),
        jax.ShapeDtypeStruct((B, nt, GATE_ROWS, tm), F32),
    )
    frame = pl.BlockSpec((1, B_KV_HEADS, tm, LANES), lambda b, t: (b, 0, t, 0))
    vt = pl.BlockSpec((1, tm // NSA_CK, B_KV_HEADS * NSA_VROWS, NSA_CK), lambda b, t: (b, t, 0, 0))
    qframe = pl.BlockSpec((1, B_KV_HEADS, B_GROUP, tm, LANES), lambda b, t: (b, 0, 0, t, 0))
    out_specs = (tok(A_WIDTH), tok(A_WIDTH), tok(A_WIDTH), qframe, frame, frame, vt, vt,
                 tok(2 * KV_WIDTH), pl.BlockSpec((1, 1, GATE_ROWS, tm), lambda b, t: (b, t, 0, 0)))
    assert B * nt >= X_SLOTS - 1
    return pl.pallas_call(
        functools.partial(_in_proj_kernel, tm=tm, nt=nt, nsteps=B * nt),
        grid=(B, nt),
        in_specs=[pl.BlockSpec(memory_space=pl.ANY), const((1, D)), const((D, nq)), const((D, 2 * KV_WIDTH)),
                  const((nr, D)), tab, tab, tab]
        + [slab(w) for w in later],
        out_specs=out_specs + tuple(slab(w) for w in later),
        out_shape=out_shapes + tuple(jax.ShapeDtypeStruct(w.shape, BF16) for w in later),
        scratch_shapes=[pltpu.VMEM((X_SLOTS, tm, D), F32), pltpu.SemaphoreType.DMA((X_SLOTS,))],
        compiler_params=pltpu.CompilerParams(dimension_semantics=("arbitrary", "arbitrary"),
                                             vmem_limit_bytes=VMEM_LIMIT),
        name="in_proj",
    )(x, norm_g.reshape(1, D), wq, wkk, wt, cos, sa, sb, *later)


def _compress_kernel(ak_ref, av_ref, w1k_ref, w1v_ref, pek_ref, pev_ref, w2k_ref, w2vt_ref, cos_ref, sa_ref, sb_ref,
                     kc_ref, vct_ref, *, nc_pad):
    half = CMP_LEN // 2
    hid_w = B_KV_HEADS * CMP_HIDDEN

    zeros = jnp.zeros((HEAD_DIM, CMP_HIDDEN), BF16)

    def both_groups(w):
        return jnp.concatenate([jnp.concatenate([w, zeros], axis=1), jnp.concatenate([zeros, w], axis=1)], axis=0)

    def hidden(a_ref, w1_ref, pe_ref):
        acc_u = jnp.zeros((nc_pad, hid_w), F32)
        acc_v = jnp.zeros((nc_pad, hid_w), F32)
        for p in range(0, half, CMP_POS_PER_DOT):
            ks = range(CMP_POS_PER_DOT)
            aps = [a_ref[0, pl.ds(p + k, nc_pad, stride=CMP_STRIDE), :] for k in ks]

            def partial_hidden(base):
                lhs = jnp.concatenate([(aps[k] + pe_ref[base + p + k:base + p + k + 1, :]).astype(BF16) for k in ks],
                                      axis=1)
                return _dot(lhs, jnp.concatenate([both_groups(w1_ref[base + p + k]) for k in ks], axis=0))

            acc_u = acc_u + partial_hidden(0)
            acc_v = acc_v + partial_hidden(half)
        return jax.nn.gelu(acc_u + pltpu.roll(acc_v, nc_pad - 1, 0))

    hk = hidden(ak_ref, w1k_ref, pek_ref).astype(BF16)
    hv = hidden(av_ref, w1v_ref, pev_ref).astype(BF16)
    for g in range(B_KV_HEADS):
        hg = hk[:, g * CMP_HIDDEN:(g + 1) * CMP_HIDDEN]
        kc = _dot(hg, w2k_ref[...])
        kc_ref[0, g] = _rope_rows(kc, cos_ref[...], sa_ref[...], sb_ref[...]).astype(BF16)
        vg = hv[:, g * CMP_HIDDEN:(g + 1) * CMP_HIDDEN]
        vct_ref[0, g] = _dot_nt(w2vt_ref[...], vg).astype(BF16)


def _compress(kcvc, pe_k, w1_k, w2_k, pe_v, w1_v, w2_v, cmp_tables):
    B, T, _ = kcvc.shape
    nc_pad = T // CMP_STRIDE
    w2k = jnp.concatenate([w2_k, jnp.zeros_like(w2_k)], axis=1).astype(BF16)
    w2vt = w2_v.T.astype(BF16)
    pek = jnp.tile(pe_k, (1, B_KV_HEADS))
    pev = jnp.tile(pe_v, (1, B_KV_HEADS))
    per_pos = lambda w1: w1.reshape(CMP_LEN, HEAD_DIM, CMP_HIDDEN).astype(BF16)
    const = lambda shape: pl.BlockSpec(shape, lambda b: (0,) * len(shape))
    cos, sa, sb = cmp_tables
    return pl.pallas_call(
        functools.partial(_compress_kernel, nc_pad=nc_pad),
        grid=(B,),
        in_specs=[pl.BlockSpec((1, T, KV_WIDTH), lambda b: (b, 0, 0)), pl.BlockSpec((1, T, KV_WIDTH), lambda b: (b, 0, 1)),
                  const((CMP_LEN, HEAD_DIM, CMP_HIDDEN)), const((CMP_LEN, HEAD_DIM, CMP_HIDDEN)),
                  const((CMP_LEN, LANES)), const((CMP_LEN, LANES)),
                  const((CMP_HIDDEN, LANES)), const((HEAD_DIM, CMP_HIDDEN)),
                  const((nc_pad, LANES)), const((nc_pad, LANES)), const((nc_pad, LANES))],
        out_specs=(pl.BlockSpec((1, B_KV_HEADS, nc_pad, LANES), lambda b: (b, 0, 0, 0)),
                   pl.BlockSpec((1, B_KV_HEADS, HEAD_DIM, nc_pad), lambda b: (b, 0, 0, 0))),
        out_shape=(jax.ShapeDtypeStruct((B, B_KV_HEADS, nc_pad, LANES), BF16),
                   jax.ShapeDtypeStruct((B, B_KV_HEADS, HEAD_DIM, nc_pad), BF16)),
        compiler_params=pltpu.CompilerParams(dimension_semantics=("parallel",), vmem_limit_bytes=VMEM_LIMIT),
        name="compress",
    )(kcvc, kcvc, per_pos(w1_k), per_pos(w1_v), pek, pev, w2k, w2vt, cos, sa, sb)


def _mixer_a_kernel(q_ref, k_ref, v_ref, bias_ref, o_ref, qd0, qd1, kd, vd0, vd1, u_s, m_s, l_s, *, seq):
    blk = A_BLOCK
    nres = seq // blk
    npat = len(A_PATTERNS)
    order = sorted(range(npat), key=lambda p: -A_PATTERNS[p][1])
    slot = {p: n for n, p in enumerate(order[:-1])}
    lane = lax.broadcasted_iota(jnp.int32, (blk, LANES), 1)
    lo = lane < HEAD_DIM

    def deinterleave(r):
        rows = pl.ds(r * blk, blk)
        q, v = q_ref[0, pl.ds(r, blk, stride=nres), :], v_ref[0, pl.ds(r, blk, stride=nres), :]
        qd0[rows, :] = jnp.where(lo, q, 0.0)
        qd1[rows, :] = jnp.where(lo, 0.0, q)
        kd[rows, :] = k_ref[0, pl.ds(r, blk, stride=nres), :]
        vd0[rows, :] = jnp.where(lo, v, 1.0)
        vd1[rows, :] = jnp.where(lo, 1.0, v)

    def pieces(dil, rd, row_off, rows):
        return [pl.ds((rd + dil * jj) * blk + row_off, rows) for jj in range(nres // dil)]

    def gather(ref, idx, lead=()):
        parts = [ref[lead + (i, slice(None))] for i in idx]
        return parts[0] if len(parts) == 1 else jnp.concatenate(parts, axis=0)

    def attend(pi, blocks):
        dil = A_PATTERNS[pi][1]
        pr = blk // (nres // dil)
        q_idxs, vbs, scores = [], [], []
        for rd, n, first in blocks:
            q_idx = pieces(dil, rd, n * pr, pr)
            k_idx = q_idx if first else pieces(dil, rd, (n - 1) * pr, 2 * pr)
            bias = bias_ref[pi, :, 0:blk] if first else bias_ref[pi, :, blk:3 * blk]
            kb = gather(kd, k_idx).astype(BF16)
            q_idxs.append(q_idx)
            for qd, vd in ((qd0, vd0), (qd1, vd1)):
                vbs.append(gather(vd, k_idx).astype(BF16))
                scores.append(_dot_nt(gather(qd, q_idx).astype(BF16), kb) + bias)
        es, ms = [], []
        for s in scores:
            m = jnp.max(s, axis=-1, keepdims=True)
            ms.append(m)
            es.append(jnp.exp2((s - m).astype(BF16)))
        pvs = [_dot(e, vb) for e, vb in zip(es, vbs)]
        for b, q_idx in enumerate(q_idxs):
            u = jnp.where(lo, pvs[2 * b], pvs[2 * b + 1])
            l_swapped = jnp.where(lo, pvs[2 * b + 1], pvs[2 * b])
            m = jnp.where(lo, ms[2 * b], ms[2 * b + 1])
            if pi != order[-1]:
                for jj, idx in enumerate(q_idx):
                    u_s[slot[pi], idx, :] = u[jj * pr:(jj + 1) * pr]
                    m_s[slot[pi], idx, :] = m[jj * pr:(jj + 1) * pr]
                    l_s[slot[pi], idx, :] = l_swapped[jj * pr:(jj + 1) * pr]
                continue
            parts = [(u, m, l_swapped)] + [(gather(u_s, q_idx, (sl,)), gather(m_s, q_idx, (sl,)),
                                            gather(l_s, q_idx, (sl,))) for sl in slot.values()]
            m_all = functools.reduce(jnp.maximum, [mm for _, mm, _ in parts])
            num = jnp.zeros((blk, LANES), F32)
            den = jnp.zeros((blk, LANES), F32)
            for uu, mm, ll in parts:
                a = jnp.exp2(mm - m_all)
                num = num + a * uu
                den = den + a * pltpu.roll(ll, HEAD_DIM, 1)
            out = num / den
            n = blocks[b][1]
            for jj in range(nres):
                o_ref[0, pl.ds(n * blk + jj, pr, stride=nres), :] = out[jj * pr:(jj + 1) * pr]

    for pi in order:
        dil = A_PATTERNS[pi][1]
        nb = seq // dil // blk
        blocks = [(rd, n, n == 0) for rd in range(dil) for n in range(nb)]
        for g0 in range(0, len(blocks), A_GROUP):
            if pi == order[0]:
                for rd, _, _ in blocks[g0:g0 + A_GROUP]:
                    deinterleave(rd)
            attend(pi, blocks[g0:g0 + A_GROUP])


def _mixer_a_bias(nres):
    blk = A_BLOCK

    def sub_pos(i, fold, rows):
        return fold * (i % rows) + i // rows

    out = np.zeros((len(A_PATTERNS), blk, 3 * blk), np.float32)
    for pi, (window, dil) in enumerate(A_PATTERNS):
        n_back = window // dil
        fold = nres // dil
        pr = blk // fold
        sq = sub_pos(np.arange(blk), fold, pr)[:, None]
        d_first = sq - sub_pos(np.arange(blk), fold, pr)[None, :]
        d_band = sq + blk - sub_pos(np.arange(2 * blk), fold, 2 * pr)[None, :]
        dist = np.concatenate([d_first, d_band], axis=1)
        out[pi] = np.where((dist >= 0) & (dist <= n_back), 0.0, NEG)
    return jnp.asarray(out)


def _mixer_a(qa, ka, va):
    B, T, W = qa.shape
    npair = W // LANES
    spec = pl.BlockSpec((1, T, LANES), lambda b, p: (b, 0, p))
    npat = len(A_PATTERNS)
    nres = T // A_BLOCK
    dils = sorted(d for _, d in A_PATTERNS)
    assert all(nres % d == 0 and T % (d * A_BLOCK) == 0 for d in dils) and dils[0] == 1 and dils[-1] == nres
    return pl.pallas_call(
        functools.partial(_mixer_a_kernel, seq=T),
        grid=(B, npair),
        in_specs=[spec, spec, spec, pl.BlockSpec((npat, A_BLOCK, 3 * A_BLOCK), lambda b, p: (0, 0, 0))],
        out_specs=spec,
        out_shape=jax.ShapeDtypeStruct((B, T, W), F32),
        scratch_shapes=[pltpu.VMEM((T, LANES), F32)] * 5 + [pltpu.VMEM((npat - 1, T, LANES), F32)] * 3,
        compiler_params=pltpu.CompilerParams(dimension_semantics=("parallel", "parallel"),
                                             vmem_limit_bytes=VMEM_LIMIT),
        name="mixer_a",
    )(qa, ka, va, _mixer_a_bias(nres))


def _nsa_kernel(q_ref, ks_ref, kw_ref, vst_ref, vwt_ref, kc_ref, vct_ref, gt_ref, ovt_ref, bias_ref, cbias_ref, o_ref,
                *, n_sel_blocks, nq):
    tq, ck = NSA_TQ, NSA_CK
    nrow = B_GROUP * tq
    gw = B_GROUP * HEAD_DIM
    step = pl.program_id(1)
    per_part = nq // NSA_SPLIT
    lanes = [(h, g) for h in range(NSA_SPLIT) for g in range(B_KV_HEADS)]
    qi = [step + h * per_part for h in range(NSA_SPLIT)]
    col_i = lax.broadcasted_iota(jnp.int32, (1, nrow), 1) & (tq - 1)

    def chunk(ref, g, c):
        return ref[0, g, pl.ds(pl.multiple_of(c * ck, ck), ck), :]

    def vt_chunk(ref, g, c):
        return ref[0, c, g * NSA_VROWS:(g + 1) * NSA_VROWS, :]

    def softmax_pv(s, vt):
        m = jnp.max(s, axis=0, keepdims=True)
        return m, _dot(vt, jnp.exp2(s - m).astype(BF16))

    def merged(parts):
        m_new = functools.reduce(jnp.maximum, [m for m, _ in parts])
        acc = sum(jnp.exp2(m - m_new) * a for m, a in parts)
        return m_new, acc

    def normalised(state):
        return state[1][0:HEAD_DIM] * (1.0 / state[1][HEAD_DIM:HEAD_DIM + 1])

    q_b = {(h, g): jnp.concatenate([q_ref[0, g, r, h] for r in range(B_GROUP)], axis=0) for h, g in lanes}

    nwin = WIN // ck
    win_chunks = [[jnp.maximum(qi[h] - back, 0) for back in range(nwin, -1, -1)] for h in range(NSA_SPLIT)]
    s_win = {}
    for h, g in lanes:
        s = _dot_nt(jnp.concatenate([chunk(kw_ref, g, c) for c in win_chunks[h]], axis=0), q_b[h, g])
        exists = lambda back: True if h * per_part >= back else qi[h] >= back
        masked = lambda piece, back: piece if exists(back) is True else jnp.where(exists(back), piece, NEG)
        pieces = [masked(s[0:ck] + bias_ref[1], nwin)]
        for n in range(1, nwin):
            pieces.append(masked(s[n * ck:(n + 1) * ck], nwin - n))
        pieces.append(s[nwin * ck:] + bias_ref[0])
        s_win[h, g] = pieces

    s_cmp = {(h, g): _dot_nt(kc_ref[0, g], q_b[h, g]) + cbias_ref[h, 0] for h, g in lanes}
    p_cmp, o_cmp = {}, {}
    for h, g in lanes:
        m = jnp.max(s_cmp[h, g], axis=0, keepdims=True)
        e = jnp.exp2(s_cmp[h, g] - m)
        den = jnp.sum(e, axis=0, keepdims=True)
        sees_block = qi[h] * tq + col_i >= CMP_LEN - 1
        p_cmp[h, g] = e * jnp.where(sees_block, 1.0 / jnp.maximum(den, 1e-30), 0.0)
        o_cmp[h, g] = _dot(vct_ref[0, g], p_cmp[h, g].astype(BF16))

    imp = {}
    for h, g in lanes:
        psum = p_cmp[h, g][:, 0:tq]
        for r in range(1, B_GROUP):
            psum = psum + p_cmp[h, g][:, r * tq:(r + 1) * tq]
        p_hi = psum.astype(BF16)
        p_lo = (psum - p_hi.astype(F32)).astype(BF16)
        imp[h, g] = _dot(ovt_ref[...], p_hi) + _dot(ovt_ref[...], p_lo)
    j = lax.broadcasted_iota(jnp.int32, (n_sel_blocks, tq), 0)
    j_f = j.astype(F32)
    low = -3e38
    q_aug = {}
    for h, g in lanes:
        cur = (qi[h] * tq + lax.broadcasted_iota(jnp.int32, (n_sel_blocks, tq), 1)) >> SEL_SHIFT
        forced = (j == 0) | (j == cur) | (j == cur - 1)
        score = jnp.where(forced, imp[h, g] + 2.0, jnp.where(j > cur, -1.0, imp[h, g]))
        sel = jnp.zeros((n_sel_blocks, tq), jnp.bool_)
        for _ in range(min(SEL_TOPK, n_sel_blocks)):
            mx = jnp.max(score, axis=0, keepdims=True)
            first = jnp.min(jnp.where(score == mx, j_f, 4.0 * LANES), axis=0, keepdims=True)
            hit = j_f == first
            sel = sel | hit
            score = jnp.where(hit, low, score)
        selneg = jnp.concatenate([jnp.zeros((SEL_LANE0, tq), F32), jnp.where(sel, 0.0, NEG),
                                  jnp.zeros((LANES - SEL_LANE0 - n_sel_blocks, tq), F32)], axis=0).T
        q_aug[h, g] = q_b[h, g] + jnp.concatenate([selneg.astype(BF16)] * B_GROUP, axis=0)

    s_diag = {(h, g): _dot_nt(chunk(ks_ref, g, qi[h]), q_aug[h, g]) + bias_ref[0] for h, g in lanes}
    o_win = {(h, g): normalised(merged([softmax_pv(s_win[h, g][n], vt_chunk(vwt_ref, g, c))
                                        for n, c in enumerate(win_chunks[h])])) for h, g in lanes}
    states = tuple(softmax_pv(s_diag[h, g], vt_chunk(vst_ref, g, qi[h])) for h, g in lanes)

    def earlier(work, st):
        s = [_dot_nt(chunk(ks_ref, lanes[n][1], c), q_aug[lanes[n]]) for n, c in work]
        parts = [[st[n]] for n in range(len(lanes))]
        for (n, c), sc in zip(work, s):
            parts[n].append(softmax_pv(sc, vt_chunk(vst_ref, lanes[n][1], c)))
        return tuple(merged(p) if len(p) > 1 else p[0] for p in parts)

    states = lax.fori_loop(0, step, lambda c, st: earlier([(n, c) for n in range(len(lanes))], st), states)
    for h in range(1, NSA_SPLIT):
        mine = [n for n, (hh, _) in enumerate(lanes) if hh >= h]
        states = earlier([(n, step + (h - 1) * per_part + k) for n in mine for k in range(per_part)], states)

    for n, (h, g) in enumerate(lanes):
        o_sel = normalised(states[n])
        gt = gt_ref[0, h, g * B_GROUP * N_BRANCH:(g + 1) * B_GROUP * N_BRANCH, :]
        outs = []
        for r in range(B_GROUP):
            cs = slice(r * tq, (r + 1) * tq)
            outs.append(gt[r * N_BRANCH:r * N_BRANCH + 1, :] * o_cmp[h, g][:, cs]
                        + gt[r * N_BRANCH + 1:r * N_BRANCH + 2, :] * o_sel[:, cs]
                        + gt[r * N_BRANCH + 2:r * N_BRANCH + 3, :] * o_win[h, g][:, cs])
        for c in range(B_GROUP // 2):
            pair = jnp.concatenate([outs[2 * c], outs[2 * c + 1]], axis=0)
            o_ref[0, h, :, g * gw + c * LANES:g * gw + (c + 1) * LANES] = pair.T


def _overlap_t(nc_pad, ns):
    nc = nc_pad - 1
    c0 = np.arange(nc_pad) * CMP_STRIDE
    s0 = np.arange(ns) * SEL_BLOCK
    ov = np.minimum(c0[None, :] + CMP_LEN, s0[:, None] + SEL_BLOCK) - np.maximum(c0[None, :], s0[:, None])
    ov = np.clip(ov, 0, None).astype(np.float32) / CMP_LEN
    ov = np.where(np.arange(nc_pad)[None, :] < nc, ov, 0.0)
    return jnp.asarray(ov, BF16)


def _nsa(qb, ks, kw, vst, vwt, kc, vct, gt):
    B, _, _, T, _ = qb.shape
    W = B_WIDTH
    nq = T // NSA_TQ
    ns = T // SEL_BLOCK
    nc_pad = kc.shape[2]
    nrow = B_GROUP * NSA_TQ
    part = T // NSA_SPLIT
    assert NSA_TQ == NSA_CK and WIN % NSA_CK == 0 and SEL_LANE0 + ns <= LANES and (nq // NSA_SPLIT) % 2 == 0
    assert gt.shape == (B, NSA_SPLIT, GATE_ROWS, part)
    kspec = pl.BlockSpec((1, B_KV_HEADS, T, LANES), lambda b, i: (b, 0, 0, 0))
    vspec = pl.BlockSpec((1, T // NSA_CK, B_KV_HEADS * NSA_VROWS, NSA_CK), lambda b, i: (b, 0, 0, 0))
    key = np.arange(NSA_CK)[:, None]
    qry = (np.arange(nrow) % NSA_TQ)[None, :]
    bias = jnp.asarray(np.stack([np.where(key <= qry, 0.0, NEG), np.where(key > qry, 0.0, NEG)]).astype(np.float32))
    cmp_end = (np.arange(nc_pad) * CMP_STRIDE + CMP_LEN - 1)[None, :, None]
    t_query = (np.arange(nq) * NSA_TQ)[:, None, None] + qry[None]
    cbias = jnp.asarray(np.where((cmp_end <= t_query) & (np.arange(nc_pad) < nc_pad - 1)[None, :, None], 0.0, NEG)
                        .astype(np.float32)).reshape(NSA_SPLIT, nq // NSA_SPLIT, nc_pad, nrow)
    out = pl.pallas_call(
        functools.partial(_nsa_kernel, n_sel_blocks=ns, nq=nq),
        grid=(B, nq // NSA_SPLIT),
        in_specs=[pl.BlockSpec((1, B_KV_HEADS, B_GROUP, NSA_SPLIT, NSA_TQ, LANES), lambda b, i: (b, 0, 0, 0, i, 0)),
                  kspec, kspec, vspec, vspec,
                  pl.BlockSpec((1, B_KV_HEADS, nc_pad, LANES), lambda b, i: (b, 0, 0, 0)),
                  pl.BlockSpec((1, B_KV_HEADS, HEAD_DIM, nc_pad), lambda b, i: (b, 0, 0, 0)),
                  pl.BlockSpec((1, NSA_SPLIT, GATE_ROWS, NSA_TQ), lambda b, i: (b, 0, 0, i)),
                  pl.BlockSpec((ns, nc_pad), lambda b, i: (0, 0)),
                  pl.BlockSpec((2, NSA_CK, nrow), lambda b, i: (0, 0, 0)),
                  pl.BlockSpec((NSA_SPLIT, 1, nc_pad, nrow), lambda b, i: (0, i, 0, 0))],
        out_specs=pl.BlockSpec((1, NSA_SPLIT, NSA_TQ, W), lambda b, i: (b, 0, i, 0)),
        out_shape=jax.ShapeDtypeStruct((B, NSA_SPLIT, part, W), F32),
        compiler_params=pltpu.CompilerParams(dimension_semantics=("parallel", "arbitrary"),
                                             vmem_limit_bytes=VMEM_LIMIT),
        name="nsa",
    )(qb.reshape(B, B_KV_HEADS, B_GROUP, NSA_SPLIT, part, LANES), ks, kw, vst, vwt, kc, vct,
      gt, _overlap_t(nc_pad, ns), bias, cbias)
    return out.reshape(B, T, W)


def _post_kernel(x_ref, oa_ref, ob_ref, ga_ref, gb_ref, wo_ref, gm_ref, wu_ref, wd_ref, gf_ref, o_ref, *, final):
    def norm(v, g):
        return v * lax.rsqrt(jnp.mean(v * v, axis=-1, keepdims=True) + EPS) * g

    na = norm(oa_ref[...], ga_ref[...]).astype(BF16)
    nb = norm(ob_ref[...], gb_ref[...]).astype(BF16)
    aw = na.shape[1]
    h_res = x_ref[...] + _dot(na, wo_ref[0:aw, :]) + _dot(nb, wo_ref[aw:, :])
    h = norm(h_res, gm_ref[...]).astype(BF16)
    u = jnp.square(jnp.maximum(_dot(h, wu_ref[...]), 0.0)).astype(BF16)
    acc = h_res + _dot(u, wd_ref[...])
    o_ref[...] = norm(acc, gf_ref[...]) if final else acc


def _post(x, oa, ob, g_a, g_b, w_out, g_mlp, w_up, w_down, g_final, *, final, tm=512):
    B, T, D = x.shape
    n = B * T
    dff = w_up.shape[1]
    tok = lambda w: pl.BlockSpec((tm, w), lambda i: (i, 0))
    const = lambda shape: pl.BlockSpec(shape, lambda i: (0, 0), pipeline_mode=pl.Buffered(1))
    out = pl.pallas_call(
        functools.partial(_post_kernel, final=final),
        grid=(n // tm,),
        in_specs=[tok(D), tok(A_WIDTH), tok(B_WIDTH), const((1, A_WIDTH)), const((1, B_WIDTH)),
                  const((A_WIDTH + B_WIDTH, D)), const((1, D)), const((D, dff)), const((dff, D)), const((1, D))],
        out_specs=tok(D),
        out_shape=jax.ShapeDtypeStruct((n, D), F32),
        compiler_params=pltpu.CompilerParams(dimension_semantics=("parallel",), vmem_limit_bytes=VMEM_LIMIT),
        name="post",
    )(x.reshape(n, D), oa.reshape(n, A_WIDTH), ob.reshape(n, B_WIDTH), g_a.reshape(1, -1), g_b.reshape(1, -1),
      w_out.astype(BF16), g_mlp.reshape(1, D), w_up.astype(BF16), w_down.astype(BF16), g_final.reshape(1, D))
    return out.reshape(B, T, D)


def kernel(x, norm_mix, w_in, cmp_pe_k, cmp_w1_k, cmp_w2_k, cmp_pe_v, cmp_w1_v, cmp_w2_v, g_out_a, g_out_b,
           w_out, norm_mlp, w_up, w_down, norm_final):
    B, T, D = x.shape
    depth = w_in.shape[0]
    tables = _rope_tables(np.arange(T))
    cmp_tables = _rope_tables(np.arange(T // CMP_STRIDE) * CMP_STRIDE + CMP_LEN - 1)
    h_res = x
    for l in range(depth):
        qa, ka, va, qb, ks, kw, vst, vwt, kcvc, gt, wo_b, wu_b, wd_b, w1k_b, w1v_b = _in_proj(
            h_res, norm_mix[l], w_in[l], tables, (w_out[l], w_up[l], w_down[l], cmp_w1_k[l], cmp_w1_v[l]))
        kc, vct = _compress(kcvc, cmp_pe_k[l], w1k_b, cmp_w2_k[l], cmp_pe_v[l], w1v_b, cmp_w2_v[l], cmp_tables)
        oa = _mixer_a(qa, ka, va)
        ob = _nsa(qb, ks, kw, vst, vwt, kc, vct, gt)
        h_res = _post(h_res, oa, ob, g_out_a[l], g_out_b[l], wo_b, norm_mlp[l], wu_b, wd_b, norm_final,
                      final=(l == depth - 1))
    return h_res
```

```python
import functools

import jax
import jax.numpy as jnp
import numpy as np
from jax import lax
from jax.experimental import pallas as pl
from jax.experimental.pallas import tpu as pltpu

F32 = jnp.float32
BF16 = jnp.bfloat16

HEAD_DIM = 64
ROT_DIM = HEAD_DIM // 4
ROPE_THETA = 500000.0
EPS = 1e-6
NEG = -1e30
Q_SCALE = HEAD_DIM ** -0.5 * 1.4426950408889634
LANES = 128

A_HEADS = 8
A_PATTERNS = ((128, 1), (512, 4), (2048, 16))
A_BLOCK = 128
A_GROUP = 16

B_HEADS = 8
B_KV_HEADS = 2
B_GROUP = B_HEADS // B_KV_HEADS
CMP_LEN = 32
CMP_STRIDE = 16
CMP_HIDDEN = 256
CMP_POS_PER_DOT = 2
SEL_BLOCK = 64
SEL_SHIFT = 6
SEL_TOPK = 8
WIN = 512
N_BRANCH = 3

A_WIDTH = A_HEADS * HEAD_DIM
B_WIDTH = B_HEADS * HEAD_DIM
KV_WIDTH = B_KV_HEADS * HEAD_DIM

NSA_TQ = 256
NSA_CK = 256
NSA_SPLIT = 2
NSA_VROWS = 80
GATE_ROWS = 32
SEL_LANE0 = HEAD_DIM

N_LATER = 5

VMEM_LIMIT = 56 * 1024 * 1024


def _dot(a, b):
    return jnp.dot(a, b, preferred_element_type=F32)


def _dot_nt(a, b):
    return lax.dot_general(a, b, (((1,), (1,)), ((), ())), preferred_element_type=F32)


def _rope_rows(y, cos, sin_a, sin_b):
    outs = []
    for c in range(y.shape[1] // LANES):
        yc = y[:, c * LANES:(c + 1) * LANES]
        outs.append(yc * cos + pltpu.roll(yc, LANES - ROT_DIM // 2, 1) * sin_a
                    + pltpu.roll(yc, ROT_DIM // 2, 1) * sin_b)
    return outs[0] if len(outs) == 1 else jnp.concatenate(outs, axis=1)


def _in_proj_kernel(x_ref, g_ref, wq_ref, wkk_ref, wt_ref, cos_ref, sa_ref, sb_ref, *refs, tm):
    (later_f32, (qkv_ref, qb_ref, ks_ref, kw_ref, vst_ref, vwt_ref, kcvc_ref, gt_ref), later_bf16) = (
        refs[:N_LATER], refs[N_LATER:-N_LATER], refs[-N_LATER:])
    for src, dst in zip(later_f32, later_bf16):
        dst[...] = src[...].astype(BF16)
    tt = pl.program_id(1)
    x = x_ref[...]
    ms = jnp.mean(x * x, axis=-1, keepdims=True)
    h = (x * lax.rsqrt(ms + EPS) * g_ref[...]).astype(BF16)
    cos, sa, sb = cos_ref[...], sa_ref[...], sb_ref[...]
    scale = Q_SCALE

    def proj(c0, c1):
        return _dot(h, wq_ref[:, c0:c1])

    o = 0
    qkv_ref[:, o:o + A_WIDTH] = _rope_rows(proj(o, o + A_WIDTH), cos, sa, sb) * scale
    o += A_WIDTH
    qkv_ref[:, o:o + A_WIDTH] = _rope_rows(proj(o, o + A_WIDTH), cos, sa, sb)
    o += A_WIDTH
    qkv_ref[:, o:o + A_WIDTH] = proj(o, o + A_WIDTH)
    o += A_WIDTH
    lane = lax.broadcasted_iota(jnp.int32, (tm, LANES), 1)
    row = lax.broadcasted_iota(jnp.int32, (tm, LANES), 0)
    lo = lane < HEAD_DIM
    qb = _rope_rows(proj(o, o + B_WIDTH), cos, sa, sb) * scale
    for hh in range(B_HEADS):
        ch = qb[:, (hh // 2) * LANES:(hh // 2 + 1) * LANES]
        if hh % 2:
            ch = pltpu.roll(ch, HEAD_DIM, 1)
        qb_ref[0, hh // B_GROUP, hh % B_GROUP] = jnp.where(lo, ch, 0.0).astype(BF16)
    o += B_WIDTH
    kcvc_ref[...] = proj(o, o + 2 * KV_WIDTH)
    o += 2 * KV_WIDTH
    ksw = _rope_rows(_dot(h, wkk_ref[...]), cos, sa, sb)

    blk = (tt * tm + row) >> SEL_SHIFT
    onehot = jnp.where(lane - SEL_LANE0 == blk, 1.0, 0.0)
    for kind, ref in ((0, ks_ref), (1, kw_ref)):
        kk = ksw[:, kind * LANES:(kind + 1) * LANES]
        tail = onehot if kind == 0 else 0.0
        ref[0, 0] = jnp.where(lo, kk, tail).astype(BF16)
        ref[0, 1] = jnp.where(lo, pltpu.roll(kk, HEAD_DIM, 1), tail).astype(BF16)

    tr = _dot_nt(wt_ref[...], h)
    ones_row = jnp.where(lax.broadcasted_iota(jnp.int32, (NSA_VROWS - HEAD_DIM, tm), 0) == 0, 1.0, 0.0)
    for kind, ref in ((0, vst_ref), (1, vwt_ref)):
        rows = [tr[kind * LANES + gg * HEAD_DIM:kind * LANES + (gg + 1) * HEAD_DIM] for gg in range(B_KV_HEADS)]
        slab = jnp.concatenate([rows[0], ones_row, rows[1], ones_row], axis=0).astype(BF16)
        for c in range(tm // NSA_CK):
            ref[0, c] = slab[:, c * NSA_CK:(c + 1) * NSA_CK]
    gt_ref[0, 0] = jax.nn.sigmoid(tr[2 * LANES:2 * LANES + GATE_ROWS, :])


def _rope_tables(pos):
    half = ROT_DIM // 2
    inv = ROPE_THETA ** (-np.arange(0, ROT_DIM, 2, dtype=np.float64) / ROT_DIM)
    ang = np.asarray(pos, np.float64)[:, None] * inv[None, :]
    cos, sin = np.cos(ang), np.sin(ang)
    n = len(pos)
    ones = np.ones((n, HEAD_DIM - ROT_DIM))
    zeros = np.zeros((n, HEAD_DIM - ROT_DIM))
    zh = np.zeros((n, half))
    c_head = np.concatenate([cos, cos, ones], axis=1)
    a_head = np.concatenate([-sin, zh, zeros], axis=1)
    b_head = np.concatenate([zh, sin, zeros], axis=1)
    rep = LANES // HEAD_DIM
    return tuple(jnp.asarray(np.tile(t, (1, rep)).astype(np.float32)) for t in (c_head, a_head, b_head))


def _in_proj(x, norm_g, w_in, tables, later, *, tm=1024):
    B, T, D = x.shape
    nt = T // tm
    assert len(later) == N_LATER and all(w.shape[0] % (16 * B * nt) == 0 for w in later)
    slab = lambda w: pl.BlockSpec((w.shape[0] // (B * nt), w.shape[1]), lambda b, t: (b * nt + t, 0))
    offs = [0]
    for n in (A_WIDTH, A_WIDTH, A_WIDTH, B_WIDTH, KV_WIDTH, KV_WIDTH, KV_WIDTH, KV_WIDTH, KV_WIDTH, KV_WIDTH,
              B_HEADS * N_BRANCH):
        offs.append(offs[-1] + n)
    col = lambda i: w_in[:, offs[i]:offs[i + 1]]
    wq = w_in[:, :offs[6]].astype(BF16)
    wkk = jnp.concatenate([col(6), col(8)], axis=1).astype(BF16)
    gpad = jnp.zeros((D, GATE_ROWS - B_HEADS * N_BRANCH), w_in.dtype)
    wt = jnp.concatenate([col(7), col(9), col(10), gpad], axis=1).T.astype(BF16)
    cos, sa, sb = tables
    nq = wq.shape[1]
    nr = wt.shape[0]
    tok = lambda w: pl.BlockSpec((None, tm, w), lambda b, t: (b, t, 0))
    const = lambda shape: pl.BlockSpec(shape, lambda b, t: (0,) * len(shape))
    tab = pl.BlockSpec((tm, LANES), lambda b, t: (t, 0))
    out_shapes = (
        jax.ShapeDtypeStruct((B, T, 3 * A_WIDTH), F32),
        jax.ShapeDtypeStruct((B, B_KV_HEADS, B_GROUP, T, LANES), BF16),
        jax.ShapeDtypeStruct((B, B_KV_HEADS, T, LANES), BF16),
        jax.ShapeDtypeStruct((B, B_KV_HEADS, T, LANES), BF16),
        jax.ShapeDtypeStruct((B, T // NSA_CK, B_KV_HEADS * NSA_VROWS, NSA_CK), BF16),
        jax.ShapeDtypeStruct((B, T // NSA_CK, B_KV_HEADS * NSA_VROWS, NSA_CK), BF16),
        jax.ShapeDtypeStruct((B, T, 2 * KV_WIDTH), F32),
        jax.ShapeDtypeStruct((B, nt, GATE_ROWS, tm), F32),
    )
    frame = pl.BlockSpec((1, B_KV_HEADS, tm, LANES), lambda b, t: (b, 0, t, 0))
    vt = pl.BlockSpec((1, tm // NSA_CK, B_KV_HEADS * NSA_VROWS, NSA_CK), lambda b, t: (b, t, 0, 0))
    qframe = pl.BlockSpec((1, B_KV_HEADS, B_GROUP, tm, LANES), lambda b, t: (b, 0, 0, t, 0))
    out_specs = (tok(3 * A_WIDTH), qframe, frame, frame, vt, vt,
                 tok(2 * KV_WIDTH), pl.BlockSpec((1, 1, GATE_ROWS, tm), lambda b, t: (b, t, 0, 0)))
    return pl.pallas_call(
        functools.partial(_in_proj_kernel, tm=tm),
        grid=(B, nt),
        in_specs=[tok(D), const((1, D)), const((D, nq)), const((D, 2 * KV_WIDTH)), const((nr, D)), tab, tab, tab]
        + [slab(w) for w in later],
        out_specs=out_specs + tuple(slab(w) for w in later),
        out_shape=out_shapes + tuple(jax.ShapeDtypeStruct(w.shape, BF16) for w in later),
        compiler_params=pltpu.CompilerParams(dimension_semantics=("parallel", "parallel"),
                                             vmem_limit_bytes=VMEM_LIMIT),
        name="in_proj",
    )(x, norm_g.reshape(1, D), wq, wkk, wt, cos, sa, sb, *later)


def _compress_kernel(ak_ref, av_ref, w1k_ref, w1v_ref, pek_ref, pev_ref, w2k_ref, w2vt_ref, cos_ref, sa_ref, sb_ref,
                     kc_ref, vct_ref, *, nc_pad):
    half = CMP_LEN // 2
    hid_w = B_KV_HEADS * CMP_HIDDEN

    zeros = jnp.zeros((HEAD_DIM, CMP_HIDDEN), BF16)

    def both_groups(w):
        return jnp.concatenate([jnp.concatenate([w, zeros], axis=1), jnp.concatenate([zeros, w], axis=1)], axis=0)

    def hidden(a_ref, w1_ref, pe_ref):
        acc_u = jnp.zeros((nc_pad, hid_w), F32)
        acc_v = jnp.zeros((nc_pad, hid_w), F32)
        for p in range(0, half, CMP_POS_PER_DOT):
            ks = range(CMP_POS_PER_DOT)
            aps = [a_ref[0, pl.ds(p + k, nc_pad, stride=CMP_STRIDE), :] for k in ks]

            def partial_hidden(base):
                lhs = jnp.concatenate([(aps[k] + pe_ref[base + p + k:base + p + k + 1, :]).astype(BF16) for k in ks],
                                      axis=1)
                return _dot(lhs, jnp.concatenate([both_groups(w1_ref[base + p + k]) for k in ks], axis=0))

            acc_u = acc_u + partial_hidden(0)
            acc_v = acc_v + partial_hidden(half)
        return jax.nn.gelu(acc_u + pltpu.roll(acc_v, nc_pad - 1, 0))

    hk = hidden(ak_ref, w1k_ref, pek_ref).astype(BF16)
    hv = hidden(av_ref, w1v_ref, pev_ref).astype(BF16)
    for g in range(B_KV_HEADS):
        hg = hk[:, g * CMP_HIDDEN:(g + 1) * CMP_HIDDEN]
        kc = _dot(hg, w2k_ref[...])
        kc_ref[0, g] = _rope_rows(kc, cos_ref[...], sa_ref[...], sb_ref[...]).astype(BF16)
        vg = hv[:, g * CMP_HIDDEN:(g + 1) * CMP_HIDDEN]
        vct_ref[0, g] = _dot_nt(w2vt_ref[...], vg).astype(BF16)


def _compress(kcvc, pe_k, w1_k, w2_k, pe_v, w1_v, w2_v, cmp_tables):
    B, T, _ = kcvc.shape
    nc_pad = T // CMP_STRIDE
    w2k = jnp.concatenate([w2_k, jnp.zeros_like(w2_k)], axis=1).astype(BF16)
    w2vt = w2_v.T.astype(BF16)
    pek = jnp.tile(pe_k, (1, B_KV_HEADS))
    pev = jnp.tile(pe_v, (1, B_KV_HEADS))
    per_pos = lambda w1: w1.reshape(CMP_LEN, HEAD_DIM, CMP_HIDDEN).astype(BF16)
    const = lambda shape: pl.BlockSpec(shape, lambda b: (0,) * len(shape))
    cos, sa, sb = cmp_tables
    return pl.pallas_call(
        functools.partial(_compress_kernel, nc_pad=nc_pad),
        grid=(B,),
        in_specs=[pl.BlockSpec((1, T, KV_WIDTH), lambda b: (b, 0, 0)), pl.BlockSpec((1, T, KV_WIDTH), lambda b: (b, 0, 1)),
                  const((CMP_LEN, HEAD_DIM, CMP_HIDDEN)), const((CMP_LEN, HEAD_DIM, CMP_HIDDEN)),
                  const((CMP_LEN, LANES)), const((CMP_LEN, LANES)),
                  const((CMP_HIDDEN, LANES)), const((HEAD_DIM, CMP_HIDDEN)),
                  const((nc_pad, LANES)), const((nc_pad, LANES)), const((nc_pad, LANES))],
        out_specs=(pl.BlockSpec((1, B_KV_HEADS, nc_pad, LANES), lambda b: (b, 0, 0, 0)),
                   pl.BlockSpec((1, B_KV_HEADS, HEAD_DIM, nc_pad), lambda b: (b, 0, 0, 0))),
        out_shape=(jax.ShapeDtypeStruct((B, B_KV_HEADS, nc_pad, LANES), BF16),
                   jax.ShapeDtypeStruct((B, B_KV_HEADS, HEAD_DIM, nc_pad), BF16)),
        compiler_params=pltpu.CompilerParams(dimension_semantics=("parallel",), vmem_limit_bytes=VMEM_LIMIT),
        name="compress",
    )(kcvc, kcvc, per_pos(w1_k), per_pos(w1_v), pek, pev, w2k, w2vt, cos, sa, sb)


def _mixer_a_kernel(q_ref, k_ref, v_ref, bias_ref, o_ref, qd0, qd1, kd, vd0, vd1, u_s, m_s, l_s, *, seq):
    blk = A_BLOCK
    nres = seq // blk
    npat = len(A_PATTERNS)
    order = sorted(range(npat), key=lambda p: -A_PATTERNS[p][1])
    slot = {p: n for n, p in enumerate(order[:-1])}
    lane = lax.broadcasted_iota(jnp.int32, (blk, LANES), 1)
    lo = lane < HEAD_DIM

    def deinterleave(r):
        rows = pl.ds(r * blk, blk)
        q, v = q_ref[0, pl.ds(r, blk, stride=nres), :], v_ref[0, pl.ds(r, blk, stride=nres), :]
        qd0[rows, :] = jnp.where(lo, q, 0.0)
        qd1[rows, :] = jnp.where(lo, 0.0, q)
        kd[rows, :] = k_ref[0, pl.ds(r, blk, stride=nres), :]
        vd0[rows, :] = jnp.where(lo, v, 1.0)
        vd1[rows, :] = jnp.where(lo, 1.0, v)

    def pieces(dil, rd, row_off, rows):
        return [pl.ds((rd + dil * jj) * blk + row_off, rows) for jj in range(nres // dil)]

    def gather(ref, idx, lead=()):
        parts = [ref[lead + (i, slice(None))] for i in idx]
        return parts[0] if len(parts) == 1 else jnp.concatenate(parts, axis=0)

    def attend(pi, blocks):
        dil = A_PATTERNS[pi][1]
        pr = blk // (nres // dil)
        q_idxs, vbs, scores = [], [], []
        for rd, n, first in blocks:
            q_idx = pieces(dil, rd, n * pr, pr)
            k_idx = q_idx if first else pieces(dil, rd, (n - 1) * pr, 2 * pr)
            bias = bias_ref[pi, :, 0:blk] if first else bias_ref[pi, :, blk:3 * blk]
            kb = gather(kd, k_idx).astype(BF16)
            q_idxs.append(q_idx)
            for qd, vd in ((qd0, vd0), (qd1, vd1)):
                vbs.append(gather(vd, k_idx).astype(BF16))
                scores.append(_dot_nt(gather(qd, q_idx).astype(BF16), kb) + bias)
        es, ms = [], []
        for s in scores:
            m = jnp.max(s, axis=-1, keepdims=True)
            ms.append(m)
            es.append(jnp.exp2((s - m).astype(BF16)))
        pvs = [_dot(e, vb) for e, vb in zip(es, vbs)]
        for b, q_idx in enumerate(q_idxs):
            u = jnp.where(lo, pvs[2 * b], pvs[2 * b + 1])
            l_swapped = jnp.where(lo, pvs[2 * b + 1], pvs[2 * b])
            m = jnp.where(lo, ms[2 * b], ms[2 * b + 1])
            if pi != order[-1]:
                for jj, idx in enumerate(q_idx):
                    u_s[slot[pi], idx, :] = u[jj * pr:(jj + 1) * pr]
                    m_s[slot[pi], idx, :] = m[jj * pr:(jj + 1) * pr]
                    l_s[slot[pi], idx, :] = l_swapped[jj * pr:(jj + 1) * pr]
                continue
            parts = [(u, m, l_swapped)] + [(gather(u_s, q_idx, (sl,)), gather(m_s, q_idx, (sl,)),
                                            gather(l_s, q_idx, (sl,))) for sl in slot.values()]
            m_all = functools.reduce(jnp.maximum, [mm for _, mm, _ in parts])
            num = jnp.zeros((blk, LANES), F32)
            den = jnp.zeros((blk, LANES), F32)
            for uu, mm, ll in parts:
                a = jnp.exp2(mm - m_all)
                num = num + a * uu
                den = den + a * pltpu.roll(ll, HEAD_DIM, 1)
            out = num / den
            n = blocks[b][1]
            for jj in range(nres):
                o_ref[0, pl.ds(n * blk + jj, pr, stride=nres), :] = out[jj * pr:(jj + 1) * pr]

    for pi in order:
        dil = A_PATTERNS[pi][1]
        nb = seq // dil // blk
        blocks = [(rd, n, n == 0) for rd in range(dil) for n in range(nb)]
        for g0 in range(0, len(blocks), A_GROUP):
            if pi == order[0]:
                for rd, _, _ in blocks[g0:g0 + A_GROUP]:
                    deinterleave(rd)
            attend(pi, blocks[g0:g0 + A_GROUP])


def _mixer_a_bias(nres):
    blk = A_BLOCK

    def sub_pos(i, fold, rows):
        return fold * (i % rows) + i // rows

    out = np.zeros((len(A_PATTERNS), blk, 3 * blk), np.float32)
    for pi, (window, dil) in enumerate(A_PATTERNS):
        n_back = window // dil
        fold = nres // dil
        pr = blk // fold
        sq = sub_pos(np.arange(blk), fold, pr)[:, None]
        d_first = sq - sub_pos(np.arange(blk), fold, pr)[None, :]
        d_band = sq + blk - sub_pos(np.arange(2 * blk), fold, 2 * pr)[None, :]
        dist = np.concatenate([d_first, d_band], axis=1)
        out[pi] = np.where((dist >= 0) & (dist <= n_back), 0.0, NEG)
    return jnp.asarray(out)


def _mixer_a(qkv):
    B, T, W3 = qkv.shape
    W = W3 // 3
    npair = W // LANES
    spec = pl.BlockSpec((1, T, LANES), lambda b, p: (b, 0, p))
    part = lambda n: pl.BlockSpec((1, T, LANES), lambda b, p: (b, 0, n * npair + p))
    npat = len(A_PATTERNS)
    nres = T // A_BLOCK
    dils = sorted(d for _, d in A_PATTERNS)
    assert all(nres % d == 0 and T % (d * A_BLOCK) == 0 for d in dils) and dils[0] == 1 and dils[-1] == nres
    return pl.pallas_call(
        functools.partial(_mixer_a_kernel, seq=T),
        grid=(B, npair),
        in_specs=[part(0), part(1), part(2), pl.BlockSpec((npat, A_BLOCK, 3 * A_BLOCK), lambda b, p: (0, 0, 0))],
        out_specs=spec,
        out_shape=jax.ShapeDtypeStruct((B, T, W), F32),
        scratch_shapes=[pltpu.VMEM((T, LANES), F32)] * 5 + [pltpu.VMEM((npat - 1, T, LANES), F32)] * 3,
        compiler_params=pltpu.CompilerParams(dimension_semantics=("parallel", "parallel"),
                                             vmem_limit_bytes=VMEM_LIMIT),
        name="mixer_a",
    )(qkv, qkv, qkv, _mixer_a_bias(nres))


def _nsa_kernel(q_ref, ks_ref, kw_ref, vst_ref, vwt_ref, kc_ref, vct_ref, gt_ref, ovt_ref, bias_ref, cbias_ref, o_ref,
                *, n_sel_blocks, nq):
    tq, ck = NSA_TQ, NSA_CK
    nrow = B_GROUP * tq
    gw = B_GROUP * HEAD_DIM
    step = pl.program_id(1)
    per_part = nq // NSA_SPLIT
    lanes = [(h, g) for h in range(NSA_SPLIT) for g in range(B_KV_HEADS)]
    qi = [step + h * per_part for h in range(NSA_SPLIT)]
    col_i = lax.broadcasted_iota(jnp.int32, (1, nrow), 1) & (tq - 1)

    def chunk(ref, g, c):
        return ref[0, g, pl.ds(pl.multiple_of(c * ck, ck), ck), :]

    def vt_chunk(ref, g, c):
        return ref[0, c, g * NSA_VROWS:(g + 1) * NSA_VROWS, :]

    def softmax_pv(s, vt):
        m = jnp.max(s, axis=0, keepdims=True)
        return m, _dot(vt, jnp.exp2(s - m).astype(BF16))

    def merged(parts):
        m_new = functools.reduce(jnp.maximum, [m for m, _ in parts])
        acc = sum(jnp.exp2(m - m_new) * a for m, a in parts)
        return m_new, acc

    def normalised(state):
        return state[1][0:HEAD_DIM] * (1.0 / state[1][HEAD_DIM:HEAD_DIM + 1])

    q_b = {(h, g): jnp.concatenate([q_ref[0, g, r, h] for r in range(B_GROUP)], axis=0) for h, g in lanes}

    nwin = WIN // ck
    win_chunks = [[jnp.maximum(qi[h] - back, 0) for back in range(nwin, -1, -1)] for h in range(NSA_SPLIT)]
    s_win = {}
    for h, g in lanes:
        s = _dot_nt(jnp.concatenate([chunk(kw_ref, g, c) for c in win_chunks[h]], axis=0), q_b[h, g])
        exists = lambda back: True if h * per_part >= back else qi[h] >= back
        masked = lambda piece, back: piece if exists(back) is True else jnp.where(exists(back), piece, NEG)
        pieces = [masked(s[0:ck] + bias_ref[1], nwin)]
        for n in range(1, nwin):
            pieces.append(masked(s[n * ck:(n + 1) * ck], nwin - n))
        pieces.append(s[nwin * ck:] + bias_ref[0])
        s_win[h, g] = pieces

    s_cmp = {(h, g): _dot_nt(kc_ref[0, g], q_b[h, g]) + cbias_ref[h, 0] for h, g in lanes}
    p_cmp, o_cmp = {}, {}
    for h, g in lanes:
        m = jnp.max(s_cmp[h, g], axis=0, keepdims=True)
        e = jnp.exp2(s_cmp[h, g] - m)
        den = jnp.sum(e, axis=0, keepdims=True)
        sees_block = qi[h] * tq + col_i >= CMP_LEN - 1
        p_cmp[h, g] = e * jnp.where(sees_block, 1.0 / jnp.maximum(den, 1e-30), 0.0)
        o_cmp[h, g] = _dot(vct_ref[0, g], p_cmp[h, g].astype(BF16))

    imp = {}
    for h, g in lanes:
        psum = p_cmp[h, g][:, 0:tq]
        for r in range(1, B_GROUP):
            psum = psum + p_cmp[h, g][:, r * tq:(r + 1) * tq]
        p_hi = psum.astype(BF16)
        p_lo = (psum - p_hi.astype(F32)).astype(BF16)
        imp[h, g] = _dot(ovt_ref[...], p_hi) + _dot(ovt_ref[...], p_lo)
    j = lax.broadcasted_iota(jnp.int32, (n_sel_blocks, tq), 0)
    j_f = j.astype(F32)
    low = -3e38
    q_aug = {}
    for h, g in lanes:
        cur = (qi[h] * tq + lax.broadcasted_iota(jnp.int32, (n_sel_blocks, tq), 1)) >> SEL_SHIFT
        forced = (j == 0) | (j == cur) | (j == cur - 1)
        score = jnp.where(forced, imp[h, g] + 2.0, jnp.where(j > cur, -1.0, imp[h, g]))
        sel = jnp.zeros((n_sel_blocks, tq), jnp.bool_)
        for _ in range(min(SEL_TOPK, n_sel_blocks)):
            mx = jnp.max(score, axis=0, keepdims=True)
            first = jnp.min(jnp.where(score == mx, j_f, 4.0 * LANES), axis=0, keepdims=True)
            hit = j_f == first
            sel = sel | hit
            score = jnp.where(hit, low, score)
        selneg = jnp.concatenate([jnp.zeros((SEL_LANE0, tq), F32), jnp.where(sel, 0.0, NEG),
                                  jnp.zeros((LANES - SEL_LANE0 - n_sel_blocks, tq), F32)], axis=0).T
        q_aug[h, g] = q_b[h, g] + jnp.concatenate([selneg.astype(BF16)] * B_GROUP, axis=0)

    s_diag = {(h, g): _dot_nt(chunk(ks_ref, g, qi[h]), q_aug[h, g]) + bias_ref[0] for h, g in lanes}
    o_win = {(h, g): normalised(merged([softmax_pv(s_win[h, g][n], vt_chunk(vwt_ref, g, c))
                                        for n, c in enumerate(win_chunks[h])])) for h, g in lanes}
    states = tuple(softmax_pv(s_diag[h, g], vt_chunk(vst_ref, g, qi[h])) for h, g in lanes)

    def earlier(work, st):
        s = [_dot_nt(chunk(ks_ref, lanes[n][1], c), q_aug[lanes[n]]) for n, c in work]
        parts = [[st[n]] for n in range(len(lanes))]
        for (n, c), sc in zip(work, s):
            parts[n].append(softmax_pv(sc, vt_chunk(vst_ref, lanes[n][1], c)))
        return tuple(merged(p) if len(p) > 1 else p[0] for p in parts)

    states = lax.fori_loop(0, step, lambda c, st: earlier([(n, c) for n in range(len(lanes))], st), states)
    for h in range(1, NSA_SPLIT):
        mine = [n for n, (hh, _) in enumerate(lanes) if hh >= h]
        states = earlier([(n, step + (h - 1) * per_part + k) for n in mine for k in range(per_part)], states)

    for n, (h, g) in enumerate(lanes):
        o_sel = normalised(states[n])
        gt = gt_ref[0, h, g * B_GROUP * N_BRANCH:(g + 1) * B_GROUP * N_BRANCH, :]
        outs = []
        for r in range(B_GROUP):
            cs = slice(r * tq, (r + 1) * tq)
            outs.append(gt[r * N_BRANCH:r * N_BRANCH + 1, :] * o_cmp[h, g][:, cs]
                        + gt[r * N_BRANCH + 1:r * N_BRANCH + 2, :] * o_sel[:, cs]
                        + gt[r * N_BRANCH + 2:r * N_BRANCH + 3, :] * o_win[h, g][:, cs])
        for c in range(B_GROUP // 2):
            pair = jnp.concatenate([outs[2 * c], outs[2 * c + 1]], axis=0)
            o_ref[0, h, :, g * gw + c * LANES:g * gw + (c + 1) * LANES] = pair.T


def _overlap_t(nc_pad, ns):
    nc = nc_pad - 1
    c0 = np.arange(nc_pad) * CMP_STRIDE
    s0 = np.arange(ns) * SEL_BLOCK
    ov = np.minimum(c0[None, :] + CMP_LEN, s0[:, None] + SEL_BLOCK) - np.maximum(c0[None, :], s0[:, None])
    ov = np.clip(ov, 0, None).astype(np.float32) / CMP_LEN
    ov = np.where(np.arange(nc_pad)[None, :] < nc, ov, 0.0)
    return jnp.asarray(ov, BF16)


def _nsa(qb, ks, kw, vst, vwt, kc, vct, gt):
    B, _, _, T, _ = qb.shape
    W = B_WIDTH
    nq = T // NSA_TQ
    ns = T // SEL_BLOCK
    nc_pad = kc.shape[2]
    nrow = B_GROUP * NSA_TQ
    part = T // NSA_SPLIT
    assert NSA_TQ == NSA_CK and WIN % NSA_CK == 0 and SEL_LANE0 + ns <= LANES and (nq // NSA_SPLIT) % 2 == 0
    assert gt.shape == (B, NSA_SPLIT, GATE_ROWS, part)
    kspec = pl.BlockSpec((1, B_KV_HEADS, T, LANES), lambda b, i: (b, 0, 0, 0))
    vspec = pl.BlockSpec((1, T // NSA_CK, B_KV_HEADS * NSA_VROWS, NSA_CK), lambda b, i: (b, 0, 0, 0))
    key = np.arange(NSA_CK)[:, None]
    qry = (np.arange(nrow) % NSA_TQ)[None, :]
    bias = jnp.asarray(np.stack([np.where(key <= qry, 0.0, NEG), np.where(key > qry, 0.0, NEG)]).astype(np.float32))
    cmp_end = (np.arange(nc_pad) * CMP_STRIDE + CMP_LEN - 1)[None, :, None]
    t_query = (np.arange(nq) * NSA_TQ)[:, None, None] + qry[None]
    cbias = jnp.asarray(np.where((cmp_end <= t_query) & (np.arange(nc_pad) < nc_pad - 1)[None, :, None], 0.0, NEG)
                        .astype(np.float32)).reshape(NSA_SPLIT, nq // NSA_SPLIT, nc_pad, nrow)
    out = pl.pallas_call(
        functools.partial(_nsa_kernel, n_sel_blocks=ns, nq=nq),
        grid=(B, nq // NSA_SPLIT),
        in_specs=[pl.BlockSpec((1, B_KV_HEADS, B_GROUP, NSA_SPLIT, NSA_TQ, LANES), lambda b, i: (b, 0, 0, 0, i, 0)),
                  kspec, kspec, vspec, vspec,
                  pl.BlockSpec((1, B_KV_HEADS, nc_pad, LANES), lambda b, i: (b, 0, 0, 0)),
                  pl.BlockSpec((1, B_KV_HEADS, HEAD_DIM, nc_pad), lambda b, i: (b, 0, 0, 0)),
                  pl.BlockSpec((1, NSA_SPLIT, GATE_ROWS, NSA_TQ), lambda b, i: (b, 0, 0, i)),
                  pl.BlockSpec((ns, nc_pad), lambda b, i: (0, 0)),
                  pl.BlockSpec((2, NSA_CK, nrow), lambda b, i: (0, 0, 0)),
                  pl.BlockSpec((NSA_SPLIT, 1, nc_pad, nrow), lambda b, i: (0, i, 0, 0))],
        out_specs=pl.BlockSpec((1, NSA_SPLIT, NSA_TQ, W), lambda b, i: (b, 0, i, 0)),
        out_shape=jax.ShapeDtypeStruct((B, NSA_SPLIT, part, W), F32),
        compiler_params=pltpu.CompilerParams(dimension_semantics=("parallel", "arbitrary"),
                                             vmem_limit_bytes=VMEM_LIMIT),
        name="nsa",
    )(qb.reshape(B, B_KV_HEADS, B_GROUP, NSA_SPLIT, part, LANES), ks, kw, vst, vwt, kc, vct,
      gt, _overlap_t(nc_pad, ns), bias, cbias)
    return out.reshape(B, T, W)


def _post_kernel(x_ref, oa_ref, ob_ref, ga_ref, gb_ref, wo_ref, gm_ref, wu_ref, wd_ref, gf_ref, o_ref, *, final):
    def norm(v, g):
        return v * lax.rsqrt(jnp.mean(v * v, axis=-1, keepdims=True) + EPS) * g

    na = norm(oa_ref[...], ga_ref[...]).astype(BF16)
    nb = norm(ob_ref[...], gb_ref[...]).astype(BF16)
    aw = na.shape[1]
    h_res = x_ref[...] + _dot(na, wo_ref[0:aw, :]) + _dot(nb, wo_ref[aw:, :])
    h = norm(h_res, gm_ref[...]).astype(BF16)
    u = jnp.square(jnp.maximum(_dot(h, wu_ref[...]), 0.0)).astype(BF16)
    acc = h_res + _dot(u, wd_ref[...])
    o_ref[...] = norm(acc, gf_ref[...]) if final else acc


def _post(x, oa, ob, g_a, g_b, w_out, g_mlp, w_up, w_down, g_final, *, final, tm=512):
    B, T, D = x.shape
    n = B * T
    dff = w_up.shape[1]
    tok = lambda w: pl.BlockSpec((tm, w), lambda i: (i, 0))
    const = lambda shape: pl.BlockSpec(shape, lambda i: (0, 0), pipeline_mode=pl.Buffered(1))
    out = pl.pallas_call(
        functools.partial(_post_kernel, final=final),
        grid=(n // tm,),
        in_specs=[tok(D), tok(A_WIDTH), tok(B_WIDTH), const((1, A_WIDTH)), const((1, B_WIDTH)),
                  const((A_WIDTH + B_WIDTH, D)), const((1, D)), const((D, dff)), const((dff, D)), const((1, D))],
        out_specs=tok(D),
        out_shape=jax.ShapeDtypeStruct((n, D), F32),
        compiler_params=pltpu.CompilerParams(dimension_semantics=("parallel",), vmem_limit_bytes=VMEM_LIMIT),
        name="post",
    )(x.reshape(n, D), oa.reshape(n, A_WIDTH), ob.reshape(n, B_WIDTH), g_a.reshape(1, -1), g_b.reshape(1, -1),
      w_out.astype(BF16), g_mlp.reshape(1, D), w_up.astype(BF16), w_down.astype(BF16), g_final.reshape(1, D))
    return out.reshape(B, T, D)


def kernel(x, norm_mix, w_in, cmp_pe_k, cmp_w1_k, cmp_w2_k, cmp_pe_v, cmp_w1_v, cmp_w2_v, g_out_a, g_out_b,
           w_out, norm_mlp, w_up, w_down, norm_final):
    B, T, D = x.shape
    depth = w_in.shape[0]
    tables = _rope_tables(np.arange(T))
    cmp_tables = _rope_tables(np.arange(T // CMP_STRIDE) * CMP_STRIDE + CMP_LEN - 1)
    h_res = x
    for l in range(depth):
        qkv, qb, ks, kw, vst, vwt, kcvc, gt, wo_b, wu_b, wd_b, w1k_b, w1v_b = _in_proj(
            h_res, norm_mix[l], w_in[l], tables, (w_out[l], w_up[l], w_down[l], cmp_w1_k[l], cmp_w1_v[l]))
        kc, vct = _compress(kcvc, cmp_pe_k[l], w1k_b, cmp_w2_k[l], cmp_pe_v[l], w1v_b, cmp_w2_v[l], cmp_tables)
        oa = _mixer_a(qkv)
        ob = _nsa(qb, ks, kw, vst, vwt, kc, vct, gt)
        h_res = _post(h_res, oa, ob, g_out_a[l], g_out_b[l], wo_b, norm_mlp[l], wu_b, wd_b, norm_final,
                      final=(l == depth - 1))
    return h_res
```

```python
import functools

import jax
import jax.numpy as jnp
import numpy as np
from jax import lax
from jax.experimental import pallas as pl
from jax.experimental.pallas import tpu as pltpu

F32 = jnp.float32
BF16 = jnp.bfloat16

HEAD_DIM = 64
ROT_DIM = HEAD_DIM // 4
ROPE_THETA = 500000.0
EPS = 1e-6
NEG = -1e30
Q_SCALE = HEAD_DIM ** -0.5 * 1.4426950408889634
LANES = 128

A_HEADS = 8
A_PATTERNS = ((128, 1), (512, 4), (2048, 16))
A_BLOCK = 128
A_GROUP = 16

B_HEADS = 8
B_KV_HEADS = 2
B_GROUP = B_HEADS // B_KV_HEADS
CMP_LEN = 32
CMP_STRIDE = 16
CMP_HIDDEN = 256
CMP_POS_PER_DOT = 2
SEL_BLOCK = 64
SEL_SHIFT = 6
SEL_TOPK = 8
WIN = 512
N_BRANCH = 3

A_WIDTH = A_HEADS * HEAD_DIM
B_WIDTH = B_HEADS * HEAD_DIM
KV_WIDTH = B_KV_HEADS * HEAD_DIM

NSA_TQ = 256
NSA_CK = 256
NSA_SPLIT = 2
NSA_VROWS = 80
GATE_ROWS = 32
SEL_LANE0 = HEAD_DIM

N_LATER = 5

VMEM_LIMIT = 56 * 1024 * 1024


def _dot(a, b):
    return jnp.dot(a, b, preferred_element_type=F32)


def _dot_nt(a, b):
    return lax.dot_general(a, b, (((1,), (1,)), ((), ())), preferred_element_type=F32)


def _rope_rows(y, cos, sin_a, sin_b):
    outs = []
    for c in range(y.shape[1] // LANES):
        yc = y[:, c * LANES:(c + 1) * LANES]
        outs.append(yc * cos + pltpu.roll(yc, LANES - ROT_DIM // 2, 1) * sin_a
                    + pltpu.roll(yc, ROT_DIM // 2, 1) * sin_b)
    return outs[0] if len(outs) == 1 else jnp.concatenate(outs, axis=1)


def _in_proj_kernel(x_ref, g_ref, wq_ref, wkk_ref, wt_ref, cos_ref, sa_ref, sb_ref, *refs, tm):
    (later_f32, (qa_ref, ka_ref, va_ref, qb_ref, ks_ref, kw_ref, vst_ref, vwt_ref, kcvc_ref, gt_ref), later_bf16) = (
        refs[:N_LATER], refs[N_LATER:-N_LATER], refs[-N_LATER:])
    for src, dst in zip(later_f32, later_bf16):
        dst[...] = src[...].astype(BF16)
    tt = pl.program_id(1)
    x = x_ref[...]
    ms = jnp.mean(x * x, axis=-1, keepdims=True)
    h = (x * lax.rsqrt(ms + EPS) * g_ref[...]).astype(BF16)
    cos, sa, sb = cos_ref[...], sa_ref[...], sb_ref[...]
    scale = Q_SCALE

    def proj(c0, c1):
        return _dot(h, wq_ref[:, c0:c1])

    o = 0
    qa_ref[...] = _rope_rows(proj(o, o + A_WIDTH), cos, sa, sb) * scale
    o += A_WIDTH
    ka_ref[...] = _rope_rows(proj(o, o + A_WIDTH), cos, sa, sb)
    o += A_WIDTH
    va_ref[...] = proj(o, o + A_WIDTH)
    o += A_WIDTH
    lane = lax.broadcasted_iota(jnp.int32, (tm, LANES), 1)
    row = lax.broadcasted_iota(jnp.int32, (tm, LANES), 0)
    lo = lane < HEAD_DIM
    qb = _rope_rows(proj(o, o + B_WIDTH), cos, sa, sb) * scale
    for hh in range(B_HEADS):
        ch = qb[:, (hh // 2) * LANES:(hh // 2 + 1) * LANES]
        if hh % 2:
            ch = pltpu.roll(ch, HEAD_DIM, 1)
        qb_ref[0, hh // B_GROUP, hh % B_GROUP] = jnp.where(lo, ch, 0.0).astype(BF16)
    o += B_WIDTH
    kcvc_ref[...] = proj(o, o + 2 * KV_WIDTH)
    o += 2 * KV_WIDTH
    ksw = _rope_rows(_dot(h, wkk_ref[...]), cos, sa, sb)

    blk = (tt * tm + row) >> SEL_SHIFT
    onehot = jnp.where(lane - SEL_LANE0 == blk, 1.0, 0.0)
    for kind, ref in ((0, ks_ref), (1, kw_ref)):
        kk = ksw[:, kind * LANES:(kind + 1) * LANES]
        tail = onehot if kind == 0 else 0.0
        ref[0, 0] = jnp.where(lo, kk, tail).astype(BF16)
        ref[0, 1] = jnp.where(lo, pltpu.roll(kk, HEAD_DIM, 1), tail).astype(BF16)

    tr = lax.dot_general(wt_ref[...], h, (((0,), (1,)), ((), ())), preferred_element_type=F32)
    ones_row = jnp.where(lax.broadcasted_iota(jnp.int32, (NSA_VROWS - HEAD_DIM, tm), 0) == 0, 1.0, 0.0)
    for kind, ref in ((0, vst_ref), (1, vwt_ref)):
        rows = [tr[kind * LANES + gg * HEAD_DIM:kind * LANES + (gg + 1) * HEAD_DIM] for gg in range(B_KV_HEADS)]
        slab = jnp.concatenate([rows[0], ones_row, rows[1], ones_row], axis=0).astype(BF16)
        for c in range(tm // NSA_CK):
            ref[0, c] = slab[:, c * NSA_CK:(c + 1) * NSA_CK]
    gt_ref[0, 0] = jax.nn.sigmoid(tr[2 * LANES:2 * LANES + GATE_ROWS, :])


def _rope_tables(pos):
    half = ROT_DIM // 2
    inv = ROPE_THETA ** (-np.arange(0, ROT_DIM, 2, dtype=np.float64) / ROT_DIM)
    ang = np.asarray(pos, np.float64)[:, None] * inv[None, :]
    cos, sin = np.cos(ang), np.sin(ang)
    n = len(pos)
    ones = np.ones((n, HEAD_DIM - ROT_DIM))
    zeros = np.zeros((n, HEAD_DIM - ROT_DIM))
    zh = np.zeros((n, half))
    c_head = np.concatenate([cos, cos, ones], axis=1)
    a_head = np.concatenate([-sin, zh, zeros], axis=1)
    b_head = np.concatenate([zh, sin, zeros], axis=1)
    rep = LANES // HEAD_DIM
    return tuple(jnp.asarray(np.tile(t, (1, rep)).astype(np.float32)) for t in (c_head, a_head, b_head))


def _in_proj(x, norm_g, w_in, tables, later, *, tm=1024):
    B, T, D = x.shape
    nt = T // tm
    assert len(later) == N_LATER and all(w.shape[0] % (16 * B * nt) == 0 for w in later)
    slab = lambda w: pl.BlockSpec((w.shape[0] // (B * nt), w.shape[1]), lambda b, t: (b * nt + t, 0))
    offs = [0]
    for n in (A_WIDTH, A_WIDTH, A_WIDTH, B_WIDTH, KV_WIDTH, KV_WIDTH, KV_WIDTH, KV_WIDTH, KV_WIDTH, KV_WIDTH,
              B_HEADS * N_BRANCH):
        offs.append(offs[-1] + n)
    col = lambda i: w_in[:, offs[i]:offs[i + 1]]
    wq = w_in[:, :offs[6]].astype(BF16)
    wkk = jnp.concatenate([col(6), col(8)], axis=1).astype(BF16)
    gpad = jnp.zeros((D, GATE_ROWS - B_HEADS * N_BRANCH), w_in.dtype)
    wt = jnp.concatenate([col(7), col(9), col(10), gpad], axis=1).astype(BF16)
    cos, sa, sb = tables
    nq = wq.shape[1]
    nr = wt.shape[1]
    tok = lambda w: pl.BlockSpec((None, tm, w), lambda b, t: (b, t, 0))
    const = lambda shape: pl.BlockSpec(shape, lambda b, t: (0,) * len(shape))
    tab = pl.BlockSpec((tm, LANES), lambda b, t: (t, 0))
    out_shapes = (
        jax.ShapeDtypeStruct((B, T, A_WIDTH), F32),
        jax.ShapeDtypeStruct((B, T, A_WIDTH), F32),
        jax.ShapeDtypeStruct((B, T, A_WIDTH), F32),
        jax.ShapeDtypeStruct((B, B_KV_HEADS, B_GROUP, T, LANES), BF16),
        jax.ShapeDtypeStruct((B, B_KV_HEADS, T, LANES), BF16),
        jax.ShapeDtypeStruct((B, B_KV_HEADS, T, LANES), BF16),
        jax.ShapeDtypeStruct((B, T // NSA_CK, B_KV_HEADS * NSA_VROWS, NSA_CK), BF16),
        jax.ShapeDtypeStruct((B, T // NSA_CK, B_KV_HEADS * NSA_VROWS, NSA_CK), BF16),
        jax.ShapeDtypeStruct((B, T, 2 * KV_WIDTH), F32),
        jax.ShapeDtypeStruct((B, nt, GATE_ROWS, tm), F32),
    )
    frame = pl.BlockSpec((1, B_KV_HEADS, tm, LANES), lambda b, t: (b, 0, t, 0))
    vt = pl.BlockSpec((1, tm // NSA_CK, B_KV_HEADS * NSA_VROWS, NSA_CK), lambda b, t: (b, t, 0, 0))
    qframe = pl.BlockSpec((1, B_KV_HEADS, B_GROUP, tm, LANES), lambda b, t: (b, 0, 0, t, 0))
    out_specs = (tok(A_WIDTH), tok(A_WIDTH), tok(A_WIDTH), qframe, frame, frame, vt, vt,
                 tok(2 * KV_WIDTH), pl.BlockSpec((1, 1, GATE_ROWS, tm), lambda b, t: (b, t, 0, 0)))
    return pl.pallas_call(
        functools.partial(_in_proj_kernel, tm=tm),
        grid=(B, nt),
        in_specs=[tok(D), const((1, D)), const((D, nq)), const((D, 2 * KV_WIDTH)), const((D, nr)), tab, tab, tab]
        + [slab(w) for w in later],
        out_specs=out_specs + tuple(slab(w) for w in later),
        out_shape=out_shapes + tuple(jax.ShapeDtypeStruct(w.shape, BF16) for w in later),
        compiler_params=pltpu.CompilerParams(dimension_semantics=("parallel", "parallel"),
                                             vmem_limit_bytes=VMEM_LIMIT),
        name="in_proj",
    )(x, norm_g.reshape(1, D), wq, wkk, wt, cos, sa, sb, *later)


def _compress_kernel(ak_ref, av_ref, w1k_ref, w1v_ref, pek_ref, pev_ref, w2k_ref, w2vt_ref, cos_ref, sa_ref, sb_ref,
                     kc_ref, vct_ref, *, nc_pad):
    half = CMP_LEN // 2
    hid_w = B_KV_HEADS * CMP_HIDDEN

    zeros = jnp.zeros((HEAD_DIM, CMP_HIDDEN), BF16)

    def both_groups(w):
        return jnp.concatenate([jnp.concatenate([w, zeros], axis=1), jnp.concatenate([zeros, w], axis=1)], axis=0)

    def hidden(a_ref, w1_ref, pe_ref):
        acc_u = jnp.zeros((nc_pad, hid_w), F32)
        acc_v = jnp.zeros((nc_pad, hid_w), F32)
        for p in range(0, half, CMP_POS_PER_DOT):
            ks = range(CMP_POS_PER_DOT)
            aps = [a_ref[0, pl.ds(p + k, nc_pad, stride=CMP_STRIDE), :] for k in ks]

            def partial_hidden(base):
                lhs = jnp.concatenate([(aps[k] + pe_ref[base + p + k:base + p + k + 1, :]).astype(BF16) for k in ks],
                                      axis=1)
                return _dot(lhs, jnp.concatenate([both_groups(w1_ref[base + p + k]) for k in ks], axis=0))

            acc_u = acc_u + partial_hidden(0)
            acc_v = acc_v + partial_hidden(half)
        return jax.nn.gelu(acc_u + pltpu.roll(acc_v, nc_pad - 1, 0))

    hk = hidden(ak_ref, w1k_ref, pek_ref).astype(BF16)
    hv = hidden(av_ref, w1v_ref, pev_ref).astype(BF16)
    for g in range(B_KV_HEADS):
        hg = hk[:, g * CMP_HIDDEN:(g + 1) * CMP_HIDDEN]
        kc = _dot(hg, w2k_ref[...])
        kc_ref[0, g] = _rope_rows(kc, cos_ref[...], sa_ref[...], sb_ref[...]).astype(BF16)
        vg = hv[:, g * CMP_HIDDEN:(g + 1) * CMP_HIDDEN]
        vct_ref[0, g] = _dot_nt(w2vt_ref[...], vg).astype(BF16)


def _compress(kcvc, pe_k, w1_k, w2_k, pe_v, w1_v, w2_v, cmp_tables):
    B, T, _ = kcvc.shape
    nc_pad = T // CMP_STRIDE
    w2k = jnp.concatenate([w2_k, jnp.zeros_like(w2_k)], axis=1).astype(BF16)
    w2vt = w2_v.T.astype(BF16)
    pek = jnp.tile(pe_k, (1, B_KV_HEADS))
    pev = jnp.tile(pe_v, (1, B_KV_HEADS))
    per_pos = lambda w1: w1.reshape(CMP_LEN, HEAD_DIM, CMP_HIDDEN).astype(BF16)
    const = lambda shape: pl.BlockSpec(shape, lambda b: (0,) * len(shape))
    cos, sa, sb = cmp_tables
    return pl.pallas_call(
        functools.partial(_compress_kernel, nc_pad=nc_pad),
        grid=(B,),
        in_specs=[pl.BlockSpec((1, T, KV_WIDTH), lambda b: (b, 0, 0)), pl.BlockSpec((1, T, KV_WIDTH), lambda b: (b, 0, 1)),
                  const((CMP_LEN, HEAD_DIM, CMP_HIDDEN)), const((CMP_LEN, HEAD_DIM, CMP_HIDDEN)),
                  const((CMP_LEN, LANES)), const((CMP_LEN, LANES)),
                  const((CMP_HIDDEN, LANES)), const((HEAD_DIM, CMP_HIDDEN)),
                  const((nc_pad, LANES)), const((nc_pad, LANES)), const((nc_pad, LANES))],
        out_specs=(pl.BlockSpec((1, B_KV_HEADS, nc_pad, LANES), lambda b: (b, 0, 0, 0)),
                   pl.BlockSpec((1, B_KV_HEADS, HEAD_DIM, nc_pad), lambda b: (b, 0, 0, 0))),
        out_shape=(jax.ShapeDtypeStruct((B, B_KV_HEADS, nc_pad, LANES), BF16),
                   jax.ShapeDtypeStruct((B, B_KV_HEADS, HEAD_DIM, nc_pad), BF16)),
        compiler_params=pltpu.CompilerParams(dimension_semantics=("parallel",), vmem_limit_bytes=VMEM_LIMIT),
        name="compress",
    )(kcvc, kcvc, per_pos(w1_k), per_pos(w1_v), pek, pev, w2k, w2vt, cos, sa, sb)


def _mixer_a_kernel(q_ref, k_ref, v_ref, bias_ref, o_ref, qd0, qd1, kd, vd0, vd1, u_s, m_s, l_s, *, seq):
    blk = A_BLOCK
    nres = seq // blk
    npat = len(A_PATTERNS)
    order = sorted(range(npat), key=lambda p: -A_PATTERNS[p][1])
    slot = {p: n for n, p in enumerate(order[:-1])}
    lane = lax.broadcasted_iota(jnp.int32, (blk, LANES), 1)
    lo = lane < HEAD_DIM

    def deinterleave(r):
        rows = pl.ds(r * blk, blk)
        q, v = q_ref[0, pl.ds(r, blk, stride=nres), :], v_ref[0, pl.ds(r, blk, stride=nres), :]
        qd0[rows, :] = jnp.where(lo, q, 0.0)
        qd1[rows, :] = jnp.where(lo, 0.0, q)
        kd[rows, :] = k_ref[0, pl.ds(r, blk, stride=nres), :]
        vd0[rows, :] = jnp.where(lo, v, 1.0)
        vd1[rows, :] = jnp.where(lo, 1.0, v)

    def pieces(dil, rd, row_off, rows):
        return [pl.ds((rd + dil * jj) * blk + row_off, rows) for jj in range(nres // dil)]

    def gather(ref, idx, lead=()):
        parts = [ref[lead + (i, slice(None))] for i in idx]
        return parts[0] if len(parts) == 1 else jnp.concatenate(parts, axis=0)

    def attend(pi, blocks):
        dil = A_PATTERNS[pi][1]
        pr = blk // (nres // dil)
        q_idxs, vbs, scores = [], [], []
        for rd, n, first in blocks:
            q_idx = pieces(dil, rd, n * pr, pr)
            k_idx = q_idx if first else pieces(dil, rd, (n - 1) * pr, 2 * pr)
            bias = bias_ref[pi, :, 0:blk] if first else bias_ref[pi, :, blk:3 * blk]
            kb = gather(kd, k_idx).astype(BF16)
            q_idxs.append(q_idx)
            for qd, vd in ((qd0, vd0), (qd1, vd1)):
                vbs.append(gather(vd, k_idx).astype(BF16))
                scores.append(_dot_nt(gather(qd, q_idx).astype(BF16), kb) + bias)
        es, ms = [], []
        for s in scores:
            m = jnp.max(s, axis=-1, keepdims=True)
            ms.append(m)
            es.append(jnp.exp2((s - m).astype(BF16)))
        pvs = [_dot(e, vb) for e, vb in zip(es, vbs)]
        for b, q_idx in enumerate(q_idxs):
            u = jnp.where(lo, pvs[2 * b], pvs[2 * b + 1])
            l_swapped = jnp.where(lo, pvs[2 * b + 1], pvs[2 * b])
            m = jnp.where(lo, ms[2 * b], ms[2 * b + 1])
            if pi != order[-1]:
                for jj, idx in enumerate(q_idx):
                    u_s[slot[pi], idx, :] = u[jj * pr:(jj + 1) * pr]
                    m_s[slot[pi], idx, :] = m[jj * pr:(jj + 1) * pr]
                    l_s[slot[pi], idx, :] = l_swapped[jj * pr:(jj + 1) * pr]
                continue
            parts = [(u, m, l_swapped)] + [(gather(u_s, q_idx, (sl,)), gather(m_s, q_idx, (sl,)),
                                            gather(l_s, q_idx, (sl,))) for sl in slot.values()]
            m_all = functools.reduce(jnp.maximum, [mm for _, mm, _ in parts])
            num = jnp.zeros((blk, LANES), F32)
            den = jnp.zeros((blk, LANES), F32)
            for uu, mm, ll in parts:
                a = jnp.exp2(mm - m_all)
                num = num + a * uu
                den = den + a * pltpu.roll(ll, HEAD_DIM, 1)
            out = num / den
            n = blocks[b][1]
            for jj in range(nres):
                o_ref[0, pl.ds(n * blk + jj, pr, stride=nres), :] = out[jj * pr:(jj + 1) * pr]

    for pi in order:
        dil = A_PATTERNS[pi][1]
        nb = seq // dil // blk
        blocks = [(rd, n, n == 0) for rd in range(dil) for n in range(nb)]
        for g0 in range(0, len(blocks), A_GROUP):
            if pi == order[0]:
                for rd, _, _ in blocks[g0:g0 + A_GROUP]:
                    deinterleave(rd)
            attend(pi, blocks[g0:g0 + A_GROUP])


def _mixer_a_bias(nres):
    blk = A_BLOCK

    def sub_pos(i, fold, rows):
        return fold * (i % rows) + i // rows

    out = np.zeros((len(A_PATTERNS), blk, 3 * blk), np.float32)
    for pi, (window, dil) in enumerate(A_PATTERNS):
        n_back = window // dil
        fold = nres // dil
        pr = blk // fold
        sq = sub_pos(np.arange(blk), fold, pr)[:, None]
        d_first = sq - sub_pos(np.arange(blk), fold, pr)[None, :]
        d_band = sq + blk - sub_pos(np.arange(2 * blk), fold, 2 * pr)[None, :]
        dist = np.concatenate([d_first, d_band], axis=1)
        out[pi] = np.where((dist >= 0) & (dist <= n_back), 0.0, NEG)
    return jnp.asarray(out)


def _mixer_a(qa, ka, va):
    B, T, W = qa.shape
    npair = W // LANES
    spec = pl.BlockSpec((1, T, LANES), lambda b, p: (b, 0, p))
    npat = len(A_PATTERNS)
    nres = T // A_BLOCK
    dils = sorted(d for _, d in A_PATTERNS)
    assert all(nres % d == 0 and T % (d * A_BLOCK) == 0 for d in dils) and dils[0] == 1 and dils[-1] == nres
    return pl.pallas_call(
        functools.partial(_mixer_a_kernel, seq=T),
        grid=(B, npair),
        in_specs=[spec, spec, spec, pl.BlockSpec((npat, A_BLOCK, 3 * A_BLOCK), lambda b, p: (0, 0, 0))],
        out_specs=spec,
        out_shape=jax.ShapeDtypeStruct((B, T, W), F32),
        scratch_shapes=[pltpu.VMEM((T, LANES), F32)] * 5 + [pltpu.VMEM((npat - 1, T, LANES), F32)] * 3,
        compiler_params=pltpu.CompilerParams(dimension_semantics=("parallel", "parallel"),
                                             vmem_limit_bytes=VMEM_LIMIT),
        name="mixer_a",
    )(qa, ka, va, _mixer_a_bias(nres))


def _nsa_kernel(q_ref, ks_ref, kw_ref, vst_ref, vwt_ref, kc_ref, vct_ref, gt_ref, ovt_ref, bias_ref, cbias_ref, o_ref,
                *, n_sel_blocks, nq):
    tq, ck = NSA_TQ, NSA_CK
    nrow = B_GROUP * tq
    gw = B_GROUP * HEAD_DIM
    step = pl.program_id(1)
    per_part = nq // NSA_SPLIT
    lanes = [(h, g) for h in range(NSA_SPLIT) for g in range(B_KV_HEADS)]
    qi = [step + h * per_part for h in range(NSA_SPLIT)]
    col_i = lax.broadcasted_iota(jnp.int32, (1, nrow), 1) & (tq - 1)

    def chunk(ref, g, c):
        return ref[0, g, pl.ds(pl.multiple_of(c * ck, ck), ck), :]

    def vt_chunk(ref, g, c):
        return ref[0, c, g * NSA_VROWS:(g + 1) * NSA_VROWS, :]

    def softmax_pv(s, vt):
        m = jnp.max(s, axis=0, keepdims=True)
        return m, _dot(vt, jnp.exp2(s - m).astype(BF16))

    def merged(parts):
        m_new = functools.reduce(jnp.maximum, [m for m, _ in parts])
        acc = sum(jnp.exp2(m - m_new) * a for m, a in parts)
        return m_new, acc

    def normalised(state):
        return state[1][0:HEAD_DIM] * (1.0 / state[1][HEAD_DIM:HEAD_DIM + 1])

    q_b = {(h, g): jnp.concatenate([q_ref[0, g, r, h] for r in range(B_GROUP)], axis=0) for h, g in lanes}

    nwin = WIN // ck
    win_chunks = [[jnp.maximum(qi[h] - back, 0) for back in range(nwin, -1, -1)] for h in range(NSA_SPLIT)]
    s_win = {}
    for h, g in lanes:
        s = _dot_nt(jnp.concatenate([chunk(kw_ref, g, c) for c in win_chunks[h]], axis=0), q_b[h, g])
        exists = lambda back: True if h * per_part >= back else qi[h] >= back
        masked = lambda piece, back: piece if exists(back) is True else jnp.where(exists(back), piece, NEG)
        pieces = [masked(s[0:ck] + bias_ref[1], nwin)]
        for n in range(1, nwin):
            pieces.append(masked(s[n * ck:(n + 1) * ck], nwin - n))
        pieces.append(s[nwin * ck:] + bias_ref[0])
        s_win[h, g] = pieces

    s_cmp = {(h, g): _dot_nt(kc_ref[0, g], q_b[h, g]) + cbias_ref[h, 0] for h, g in lanes}
    p_cmp, o_cmp = {}, {}
    for h, g in lanes:
        m = jnp.max(s_cmp[h, g], axis=0, keepdims=True)
        e = jnp.exp2(s_cmp[h, g] - m)
        den = jnp.sum(e, axis=0, keepdims=True)
        sees_block = qi[h] * tq + col_i >= CMP_LEN - 1
        p_cmp[h, g] = e * jnp.where(sees_block, 1.0 / jnp.maximum(den, 1e-30), 0.0)
        o_cmp[h, g] = _dot(vct_ref[0, g], p_cmp[h, g].astype(BF16))

    imp = {}
    for h, g in lanes:
        psum = p_cmp[h, g][:, 0:tq]
        for r in range(1, B_GROUP):
            psum = psum + p_cmp[h, g][:, r * tq:(r + 1) * tq]
        p_hi = psum.astype(BF16)
        p_lo = (psum - p_hi.astype(F32)).astype(BF16)
        imp[h, g] = _dot(ovt_ref[...], p_hi) + _dot(ovt_ref[...], p_lo)
    j = lax.broadcasted_iota(jnp.int32, (n_sel_blocks, tq), 0)
    j_f = j.astype(F32)
    low = -3e38
    q_aug = {}
    for h, g in lanes:
        cur = (qi[h] * tq + lax.broadcasted_iota(jnp.int32, (n_sel_blocks, tq), 1)) >> SEL_SHIFT
        forced = (j == 0) | (j == cur) | (j == cur - 1)
        score = jnp.where(forced, imp[h, g] + 2.0, jnp.where(j > cur, -1.0, imp[h, g]))
        sel = jnp.zeros((n_sel_blocks, tq), jnp.bool_)
        for _ in range(min(SEL_TOPK, n_sel_blocks)):
            mx = jnp.max(score, axis=0, keepdims=True)
            first = jnp.min(jnp.where(score == mx, j_f, 4.0 * LANES), axis=0, keepdims=True)
            hit = j_f == first
            sel = sel | hit
            score = jnp.where(hit, low, score)
        selneg = jnp.concatenate([jnp.zeros((SEL_LANE0, tq), F32), jnp.where(sel, 0.0, NEG),
                                  jnp.zeros((LANES - SEL_LANE0 - n_sel_blocks, tq), F32)], axis=0).T
        q_aug[h, g] = q_b[h, g] + jnp.concatenate([selneg.astype(BF16)] * B_GROUP, axis=0)

    s_diag = {(h, g): _dot_nt(chunk(ks_ref, g, qi[h]), q_aug[h, g]) + bias_ref[0] for h, g in lanes}
    o_win = {(h, g): normalised(merged([softmax_pv(s_win[h, g][n], vt_chunk(vwt_ref, g, c))
                                        for n, c in enumerate(win_chunks[h])])) for h, g in lanes}
    states = tuple(softmax_pv(s_diag[h, g], vt_chunk(vst_ref, g, qi[h])) for h, g in lanes)

    def earlier(work, st):
        s = [_dot_nt(chunk(ks_ref, lanes[n][1], c), q_aug[lanes[n]]) for n, c in work]
        parts = [[st[n]] for n in range(len(lanes))]
        for (n, c), sc in zip(work, s):
            parts[n].append(softmax_pv(sc, vt_chunk(vst_ref, lanes[n][1], c)))
        return tuple(merged(p) if len(p) > 1 else p[0] for p in parts)

    states = lax.fori_loop(0, step, lambda c, st: earlier([(n, c) for n in range(len(lanes))], st), states)
    for h in range(1, NSA_SPLIT):
        mine = [n for n, (hh, _) in enumerate(lanes) if hh >= h]
        states = earlier([(n, step + (h - 1) * per_part + k) for n in mine for k in range(per_part)], states)

    for n, (h, g) in enumerate(lanes):
        o_sel = normalised(states[n])
        gt = gt_ref[0, h, g * B_GROUP * N_BRANCH:(g + 1) * B_GROUP * N_BRANCH, :]
        outs = []
        for r in range(B_GROUP):
            cs = slice(r * tq, (r + 1) * tq)
            outs.append(gt[r * N_BRANCH:r * N_BRANCH + 1, :] * o_cmp[h, g][:, cs]
                        + gt[r * N_BRANCH + 1:r * N_BRANCH + 2, :] * o_sel[:, cs]
                        + gt[r * N_BRANCH + 2:r * N_BRANCH + 3, :] * o_win[h, g][:, cs])
        for c in range(B_GROUP // 2):
            pair = jnp.concatenate([outs[2 * c], outs[2 * c + 1]], axis=0)
            o_ref[0, h, :, g * gw + c * LANES:g * gw + (c + 1) * LANES] = pair.T


def _overlap_t(nc_pad, ns):
    nc = nc_pad - 1
    c0 = np.arange(nc_pad) * CMP_STRIDE
    s0 = np.arange(ns) * SEL_BLOCK
    ov = np.minimum(c0[None, :] + CMP_LEN, s0[:, None] + SEL_BLOCK) - np.maximum(c0[None, :], s0[:, None])
    ov = np.clip(ov, 0, None).astype(np.float32) / CMP_LEN
    ov = np.where(np.arange(nc_pad)[None, :] < nc, ov, 0.0)
    return jnp.asarray(ov, BF16)


def _nsa(qb, ks, kw, vst, vwt, kc, vct, gt):
    B, _, _, T, _ = qb.shape
    W = B_WIDTH
    nq = T // NSA_TQ
    ns = T // SEL_BLOCK
    nc_pad = kc.shape[2]
    nrow = B_GROUP * NSA_TQ
    part = T // NSA_SPLIT
    assert NSA_TQ == NSA_CK and WIN % NSA_CK == 0 and SEL_LANE0 + ns <= LANES and (nq // NSA_SPLIT) % 2 == 0
    assert gt.shape == (B, NSA_SPLIT, GATE_ROWS, part)
    kspec = pl.BlockSpec((1, B_KV_HEADS, T, LANES), lambda b, i: (b, 0, 0, 0))
    vspec = pl.BlockSpec((1, T // NSA_CK, B_KV_HEADS * NSA_VROWS, NSA_CK), lambda b, i: (b, 0, 0, 0))
    key = np.arange(NSA_CK)[:, None]
    qry = (np.arange(nrow) % NSA_TQ)[None, :]
    bias = jnp.asarray(np.stack([np.where(key <= qry, 0.0, NEG), np.where(key > qry, 0.0, NEG)]).astype(np.float32))
    cmp_end = (np.arange(nc_pad) * CMP_STRIDE + CMP_LEN - 1)[None, :, None]
    t_query = (np.arange(nq) * NSA_TQ)[:, None, None] + qry[None]
    cbias = jnp.asarray(np.where((cmp_end <= t_query) & (np.arange(nc_pad) < nc_pad - 1)[None, :, None], 0.0, NEG)
                        .astype(np.float32)).reshape(NSA_SPLIT, nq // NSA_SPLIT, nc_pad, nrow)
    out = pl.pallas_call(
        functools.partial(_nsa_kernel, n_sel_blocks=ns, nq=nq),
        grid=(B, nq // NSA_SPLIT),
        in_specs=[pl.BlockSpec((1, B_KV_HEADS, B_GROUP, NSA_SPLIT, NSA_TQ, LANES), lambda b, i: (b, 0, 0, 0, i, 0)),
                  kspec, kspec, vspec, vspec,
                  pl.BlockSpec((1, B_KV_HEADS, nc_pad, LANES), lambda b, i: (b, 0, 0, 0)),
                  pl.BlockSpec((1, B_KV_HEADS, HEAD_DIM, nc_pad), lambda b, i: (b, 0, 0, 0)),
                  pl.BlockSpec((1, NSA_SPLIT, GATE_ROWS, NSA_TQ), lambda b, i: (b, 0, 0, i)),
                  pl.BlockSpec((ns, nc_pad), lambda b, i: (0, 0)),
                  pl.BlockSpec((2, NSA_CK, nrow), lambda b, i: (0, 0, 0)),
                  pl.BlockSpec((NSA_SPLIT, 1, nc_pad, nrow), lambda b, i: (0, i, 0, 0))],
        out_specs=pl.BlockSpec((1, NSA_SPLIT, NSA_TQ, W), lambda b, i: (b, 0, i, 0)),
        out_shape=jax.ShapeDtypeStruct((B, NSA_SPLIT, part, W), F32),
        compiler_params=pltpu.CompilerParams(dimension_semantics=("parallel", "arbitrary"),
                                             vmem_limit_bytes=VMEM_LIMIT),
        name="nsa",
    )(qb.reshape(B, B_KV_HEADS, B_GROUP, NSA_SPLIT, part, LANES), ks, kw, vst, vwt, kc, vct,
      gt, _overlap_t(nc_pad, ns), bias, cbias)
    return out.reshape(B, T, W)


def _post_kernel(x_ref, oa_ref, ob_ref, ga_ref, gb_ref, wo_ref, gm_ref, wu_ref, wd_ref, gf_ref, o_ref, *, final):
    def norm(v, g):
        return v * lax.rsqrt(jnp.mean(v * v, axis=-1, keepdims=True) + EPS) * g

    na = norm(oa_ref[...], ga_ref[...]).astype(BF16)
    nb = norm(ob_ref[...], gb_ref[...]).astype(BF16)
    aw = na.shape[1]
    h_res = x_ref[...] + _dot(na, wo_ref[0:aw, :]) + _dot(nb, wo_ref[aw:, :])
    h = norm(h_res, gm_ref[...]).astype(BF16)
    u = jnp.square(jnp.maximum(_dot(h, wu_ref[...]), 0.0)).astype(BF16)
    acc = h_res + _dot(u, wd_ref[...])
    o_ref[...] = norm(acc, gf_ref[...]) if final else acc


def _post(x, oa, ob, g_a, g_b, w_out, g_mlp, w_up, w_down, g_final, *, final, tm=512):
    B, T, D = x.shape
    n = B * T
    dff = w_up.shape[1]
    tok = lambda w: pl.BlockSpec((tm, w), lambda i: (i, 0))
    const = lambda shape: pl.BlockSpec(shape, lambda i: (0, 0), pipeline_mode=pl.Buffered(1))
    out = pl.pallas_call(
        functools.partial(_post_kernel, final=final),
        grid=(n // tm,),
        in_specs=[tok(D), tok(A_WIDTH), tok(B_WIDTH), const((1, A_WIDTH)), const((1, B_WIDTH)),
                  const((A_WIDTH + B_WIDTH, D)), const((1, D)), const((D, dff)), const((dff, D)), const((1, D))],
        out_specs=tok(D),
        out_shape=jax.ShapeDtypeStruct((n, D), F32),
        compiler_params=pltpu.CompilerParams(dimension_semantics=("parallel",), vmem_limit_bytes=VMEM_LIMIT),
        name="post",
    )(x.reshape(n, D), oa.reshape(n, A_WIDTH), ob.reshape(n, B_WIDTH), g_a.reshape(1, -1), g_b.reshape(1, -1),
      w_out.astype(BF16), g_mlp.reshape(1, D), w_up.astype(BF16), w_down.astype(BF16), g_final.reshape(1, D))
    return out.reshape(B, T, D)


def kernel(x, norm_mix, w_in, cmp_pe_k, cmp_w1_k, cmp_w2_k, cmp_pe_v, cmp_w1_v, cmp_w2_v, g_out_a, g_out_b,
           w_out, norm_mlp, w_up, w_down, norm_final):
    B, T, D = x.shape
    depth = w_in.shape[0]
    tables = _rope_tables(np.arange(T))
    cmp_tables = _rope_tables(np.arange(T // CMP_STRIDE) * CMP_STRIDE + CMP_LEN - 1)
    h_res = x
    for l in range(depth):
        qa, ka, va, qb, ks, kw, vst, vwt, kcvc, gt, wo_b, wu_b, wd_b, w1k_b, w1v_b = _in_proj(
            h_res, norm_mix[l], w_in[l], tables, (w_out[l], w_up[l], w_down[l], cmp_w1_k[l], cmp_w1_v[l]))
        kc, vct = _compress(kcvc, cmp_pe_k[l], w1k_b, cmp_w2_k[l], cmp_pe_v[l], w1v_b, cmp_w2_v[l], cmp_tables)
        oa = _mixer_a(qa, ka, va)
        ob = _nsa(qb, ks, kw, vst, vwt, kc, vct, gt)
        h_res = _post(h_res, oa, ob, g_out_a[l], g_out_b[l], wo_b, norm_mlp[l], wu_b, wd_b, norm_final,
                      final=(l == depth - 1))
    return h_res
```
